```python
import math
import jax, jax.numpy as jnp
from jax import lax
import numpy as np

D_MODEL = 1024
BATCH = 2
SEQ = 8192
DEPTH = 1

N_MEM = 256
MAX_POS_OFFSET = 1024
BLOCK = 128
RMS_EPS = 1e-6
NEG_INF = -1e30

MLA_HEADS = 8
MLA_NOPE = 64
MLA_ROPE = 32
MLA_V = 64
MLA_QK_DIM = MLA_NOPE + MLA_ROPE
MLA_Q_RANK = 384
MLA_KV_RANK = 256
ROPE_THETA = 10000.0
MLA_WIDTH = MLA_HEADS * MLA_V

DIL_PAIRS = ((128, 1), (512, 4), (2048, 16))
DIL_GROUPS = 3
DIL_HEADS_PER_GROUP = 4
DIL_HEADS = DIL_GROUPS * DIL_HEADS_PER_GROUP
DIL_HEAD_DIM = 128
DIL_WIDTH = DIL_HEADS_PER_GROUP * DIL_HEAD_DIM

MEM_HEADS = 4
MEM_HEAD_DIM = 128
MEM_WIDTH = MEM_HEADS * MEM_HEAD_DIM

N_BRANCH = 3

D_FF = 2816
CONV_WIDTH = 3

OFF_Q = MLA_Q_RANK
OFF_KV = OFF_Q + MLA_KV_RANK
OFF_KR = OFF_KV + MLA_ROPE
OFF_DIL = OFF_KR + 3 * DIL_HEADS * DIL_HEAD_DIM
OFF_MEMQ = OFF_DIL + MEM_WIDTH
D_IN = OFF_MEMQ + N_BRANCH * D_MODEL

kernel_name = "hybrid_mla_dilated_memory_convffn"


def _rms_norm(x, g):
    xf = x.astype(jnp.float32)
    y = xf * lax.rsqrt(jnp.mean(xf * xf, axis=-1, keepdims=True) + RMS_EPS)
    return (y * g.astype(jnp.float32)).astype(x.dtype)


def _rope(t, positions):
    half = t.shape[-1] // 2
    inv_freq = ROPE_THETA ** (-jnp.arange(half, dtype=jnp.float32) / half)
    ang = positions.astype(jnp.float32)[:, :, None, None] * inv_freq
    cos, sin = jnp.cos(ang), jnp.sin(ang)
    t1 = t[..., :half].astype(jnp.float32)
    t2 = t[..., half:].astype(jnp.float32)
    return jnp.concatenate([t1 * cos - t2 * sin, t2 * cos + t1 * sin], axis=-1).astype(t.dtype)


def _alibi_slopes(n):
    return jnp.exp2(-8.0 * jnp.arange(1, n + 1, dtype=jnp.float32) / n)


def _mla(c_q, c_kv, k_rope, positions, q_norm, w_uq, kv_norm, w_ukv):
    B, S, _ = c_q.shape
    q = (_rms_norm(c_q, q_norm) @ w_uq).reshape(B, S, MLA_HEADS, MLA_QK_DIM)
    q = jnp.concatenate([q[..., :MLA_NOPE], _rope(q[..., MLA_NOPE:], positions)], axis=-1)
    kv = (_rms_norm(c_kv, kv_norm) @ w_ukv).reshape(B, S, MLA_HEADS, MLA_NOPE + MLA_V)
    k_pe = jnp.broadcast_to(_rope(k_rope[:, :, None, :], positions), (B, S, MLA_HEADS, MLA_ROPE))
    k = jnp.concatenate([kv[..., :MLA_NOPE], k_pe], axis=-1)
    v = kv[..., MLA_NOPE:]
    n_blk = S // BLOCK
    q_blocks = (q * MLA_QK_DIM ** -0.5).reshape(B, n_blk, BLOCK, MLA_HEADS, MLA_QK_DIM).transpose(1, 0, 2, 3, 4)
    key_idx = jnp.arange(S)

    def attend(args):
        q_blk, blk = args
        s = jnp.einsum('bqhd,bkhd->bhqk', q_blk, k).astype(jnp.float32)
        q_idx = blk * BLOCK + jnp.arange(BLOCK)
        s = jnp.where(key_idx[None, :] <= q_idx[:, None], s, NEG_INF)
        p = jax.nn.softmax(s, axis=-1).astype(v.dtype)
        return jnp.einsum('bhqk,bkhd->bqhd', p, v)

    o = lax.map(attend, (q_blocks, jnp.arange(n_blk)))
    return o.transpose(1, 0, 2, 3, 4).reshape(B, S, MLA_WIDTH)


def _dilated_group(q, k, v, window, dilation, slopes):
    B, S, H, dh = q.shape
    span = window // dilation
    L = S // dilation
    n_blk = -(-L // BLOCK)
    Lp = n_blk * BLOCK

    def to_blocks(t):
        t = t.reshape(B, L, dilation, H, dh).transpose(0, 2, 3, 1, 4)
        t = jnp.pad(t, ((0, 0), (0, 0), (0, 0), (0, Lp - L), (0, 0)))
        return t.reshape(B, dilation, H, n_blk, BLOCK, dh)

    def band(t):
        prev = jnp.pad(t, ((0, 0), (0, 0), (0, 0), (1, 0), (0, 0), (0, 0)))[:, :, :, :-1]
        return jnp.concatenate([prev, t], axis=4)

    qb = to_blocks(q) * dh ** -0.5
    kb = band(to_blocks(k))
    vb = band(to_blocks(v))
    s = jnp.einsum('bdhnqe,bdhnke->bdhnqk', qb, kb).astype(jnp.float32)
    dist = jnp.arange(BLOCK)[:, None] + BLOCK - jnp.arange(2 * BLOCK)[None, :]
    key_sub = jnp.arange(n_blk)[:, None, None] * BLOCK - BLOCK + jnp.arange(2 * BLOCK)[None, None, :]
    valid = (dist >= 0) & (dist <= span) & (key_sub >= 0)
    alibi = -slopes.astype(jnp.float32)[:, None, None, None] * (dist * dilation).astype(jnp.float32)
    s = jnp.where(valid, s + alibi, NEG_INF)
    m = jnp.max(s, axis=-1, keepdims=True)
    e = jnp.exp(s - m)
    den = jnp.sum(e, axis=-1, keepdims=True)
    o = jnp.einsum('bdhnqk,bdhnke->bdhnqe', (e / den).astype(v.dtype), vb)
    lse = (m + jnp.log(den))[..., 0]
    o = o.reshape(B, dilation, H, Lp, dh)[:, :, :, :L].transpose(0, 3, 1, 2, 4).reshape(B, S, H, dh)
    lse = lse.reshape(B, dilation, H, Lp)[..., :L].transpose(0, 3, 1, 2).reshape(B, S, H)
    return o, lse


def _dilated_mixture(dil_qkv):
    B, S, _ = dil_qkv.shape
    qkv = dil_qkv.reshape(B, S, 3, DIL_GROUPS, DIL_HEADS_PER_GROUP, DIL_HEAD_DIM)
    slopes = _alibi_slopes(DIL_HEADS).reshape(DIL_HEADS_PER_GROUP, DIL_GROUPS).T
    outs, lses = [], []
    for g, (window, dilation) in enumerate(DIL_PAIRS):
        o, lse = _dilated_group(qkv[:, :, 0, g], qkv[:, :, 1, g], qkv[:, :, 2, g], window, dilation, slopes[g])
        outs.append(o)
        lses.append(lse)
    w = jax.nn.softmax(jnp.stack(lses, axis=0), axis=0)
    o_stack = jnp.stack(outs, axis=0)
    o = jnp.sum(w[..., None].astype(o_stack.dtype) * o_stack, axis=0)
    return o.reshape(B, S, DIL_WIDTH)


def _mem_attention(q, mem, g_mem, w_mem_kv):
    B, S, _ = q.shape
    M = mem.shape[1]
    kv = (_rms_norm(mem, g_mem) @ w_mem_kv).reshape(B, M, 2, MEM_HEADS, MEM_HEAD_DIM)
    qh = q.reshape(B, S, MEM_HEADS, MEM_HEAD_DIM) * MEM_HEAD_DIM ** -0.5
    s = jnp.einsum('bshd,bmhd->bhsm', qh, kv[:, :, 0]).astype(jnp.float32)
    p = jax.nn.softmax(s, axis=-1).astype(q.dtype)
    return jnp.einsum('bhsm,bmhd->bshd', p, kv[:, :, 1]).reshape(B, S, MEM_WIDTH)


def _conv_ffn(h, w_up, conv_w, conv_b, w_down):
    S = h.shape[1]
    u = h @ w_up
    u_pad = jnp.pad(u, ((0, 0), (CONV_WIDTH - 1, 0), (0, 0)))
    z = conv_b + conv_w[0] * u_pad[:, 0:S]
    for j in range(1, CONV_WIDTH):
        z = z + conv_w[j] * u_pad[:, j:j + S]
    gate, val = z[..., :D_FF], z[..., D_FF:]
    return (jax.nn.silu(gate) * val) @ w_down


def _layer(x, mem, positions, g_pre_mix, w_in, b_gate, mla_q_norm, w_uq, mla_kv_norm, w_ukv, g_mem, w_mem_kv,
           w_br_mla, w_br_dil, w_br_mem, w_o, g_post_mix, g_pre_ffn, w_ffn_up, conv_w, conv_b, w_ffn_down, g_post_ffn):
    B, S, _ = x.shape
    h = _rms_norm(x, g_pre_mix)
    proj = h @ w_in
    y_mla = _mla(proj[..., :OFF_Q], proj[..., OFF_Q:OFF_KV], proj[..., OFF_KV:OFF_KR], positions,
                 mla_q_norm, w_uq, mla_kv_norm, w_ukv)
    y_dil = _dilated_mixture(proj[..., OFF_KR:OFF_DIL])
    y_mem = _mem_attention(proj[..., OFF_DIL:OFF_MEMQ], mem, g_mem, w_mem_kv)
    gates = jax.nn.sigmoid((proj[..., OFF_MEMQ:] + b_gate).astype(jnp.float32)).astype(x.dtype)
    gates = gates.reshape(B, S, N_BRANCH, D_MODEL)
    merged = (gates[:, :, 0] * (y_mla @ w_br_mla)
              + gates[:, :, 1] * (y_dil @ w_br_dil)
              + gates[:, :, 2] * (y_mem @ w_br_mem))
    x = x + _rms_norm(merged @ w_o, g_post_mix)
    h2 = _rms_norm(x, g_pre_ffn)
    x = x + _rms_norm(_conv_ffn(h2, w_ffn_up, conv_w, conv_b, w_ffn_down), g_post_ffn)
    return x


def setup_inputs(seed: int = 0) -> dict:
    key = jax.random.key(seed)
    ks = jax.random.split(key, 24)
    f32 = jnp.float32

    def dense(k, fan_in, fan_out):
        return jax.random.normal(k, (DEPTH, fan_in, fan_out), f32) * fan_in ** -0.5

    def gain(k, n):
        return 1.0 + 0.05 * jax.random.normal(k, (DEPTH, n), f32)

    x = jax.random.normal(ks[0], (BATCH, SEQ, D_MODEL), f32)
    mem = jax.random.normal(ks[1], (BATCH, N_MEM, D_MODEL), f32)
    offset = jax.random.randint(ks[2], (BATCH, 1), 0, MAX_POS_OFFSET, dtype=jnp.int32)
    positions = (offset + jnp.arange(SEQ, dtype=jnp.int32)[None, :]).astype(jnp.int32)
    return {
        "x": x,
        "mem": mem,
        "positions": positions,
        "g_pre_mix": gain(ks[3], D_MODEL),
        "w_in": dense(ks[4], D_MODEL, D_IN),
        "b_gate": 0.1 * jax.random.normal(ks[5], (DEPTH, N_BRANCH * D_MODEL), f32),
        "mla_q_norm": gain(ks[6], MLA_Q_RANK),
        "w_uq": dense(ks[7], MLA_Q_RANK, MLA_HEADS * MLA_QK_DIM),
        "mla_kv_norm": gain(ks[8], MLA_KV_RANK),
        "w_ukv": dense(ks[9], MLA_KV_RANK, MLA_HEADS * (MLA_NOPE + MLA_V)),
        "g_mem": gain(ks[10], D_MODEL),
        "w_mem_kv": dense(ks[11], D_MODEL, 2 * MEM_WIDTH),
        "w_br_mla": dense(ks[12], MLA_WIDTH, D_MODEL),
        "w_br_dil": dense(ks[13], DIL_WIDTH, D_MODEL),
        "w_br_mem": dense(ks[14], MEM_WIDTH, D_MODEL),
        "w_o": dense(ks[15], D_MODEL, D_MODEL),
        "g_post_mix": gain(ks[16], D_MODEL),
        "g_pre_ffn": gain(ks[17], D_MODEL),
        "w_ffn_up": dense(ks[18], D_MODEL, 2 * D_FF),
        "conv_w": jax.random.normal(ks[19], (DEPTH, CONV_WIDTH, 2 * D_FF), f32) * CONV_WIDTH ** -0.5,
        "conv_b": 0.01 * jax.random.normal(ks[20], (DEPTH, 2 * D_FF), f32),
        "w_ffn_down": dense(ks[21], D_FF, D_MODEL),
        "g_post_ffn": gain(ks[22], D_MODEL),
    }


def reference(x, mem, positions, g_pre_mix, w_in, b_gate, mla_q_norm, w_uq, mla_kv_norm, w_ukv, g_mem, w_mem_kv,
              w_br_mla, w_br_dil, w_br_mem, w_o, g_post_mix, g_pre_ffn, w_ffn_up, conv_w, conv_b, w_ffn_down,
              g_post_ffn):
    for l in range(DEPTH):
        x = _layer(x, mem, positions, g_pre_mix[l], w_in[l], b_gate[l], mla_q_norm[l], w_uq[l], mla_kv_norm[l],
                   w_ukv[l], g_mem[l], w_mem_kv[l], w_br_mla[l], w_br_dil[l], w_br_mem[l], w_o[l], g_post_mix[l],
                   g_pre_ffn[l], w_ffn_up[l], conv_w[l], conv_b[l], w_ffn_down[l], g_post_ffn[l])
    return x
```

```python
import functools

import jax
import jax.numpy as jnp
from jax import lax
from jax.experimental import pallas as pl
from jax.experimental.pallas import tpu as pltpu

F32 = jnp.float32
BF16 = jnp.bfloat16

RMS_EPS = 1e-6
NEG_INF = -1e30
LANES = 128

BLOCK = 128
MLA_HEADS = 8
MLA_NOPE = 64
MLA_ROPE = 32
MLA_V = 64
MLA_QK_DIM = MLA_NOPE + MLA_ROPE
MLA_Q_RANK = 384
MLA_KV_RANK = 256
ROPE_THETA = 10000.0
ROPE_HALF = MLA_ROPE // 2

DIL_PAIRS = ((128, 1), (512, 4), (2048, 16))
DIL_GROUPS = 3
DIL_HPG = 4
DIL_HEADS = DIL_GROUPS * DIL_HPG
DIL_HEAD_DIM = 128
DIL_WIDTH = DIL_HPG * DIL_HEAD_DIM

MEM_HEADS = 4
MEM_HEAD_DIM = 128
MEM_WIDTH = MEM_HEADS * MEM_HEAD_DIM

N_BRANCH = 3
CONV_WIDTH = 3

OFF_Q = MLA_Q_RANK
OFF_KV = OFF_Q + MLA_KV_RANK
OFF_KR = OFF_KV + MLA_ROPE
OFF_DIL = OFF_KR + 3 * DIL_HEADS * DIL_HEAD_DIM
OFF_MEMQ = OFF_DIL + MEM_WIDTH

PREP_TM = 512
DILPROJ_TM = 2048
DILPROJ_TN = 512
MLA_TQ = 512
DIL_TB = 512
MERGE_TM = 512
FFN_TM = 1024
FFN_TF = 256
FFN_HALO = 16

VMEM_LIMIT = 56 * 1024 * 1024


def _rms(xf, g):
    return xf * lax.rsqrt(jnp.mean(xf * xf, axis=-1, keepdims=True) + RMS_EPS) * g


def _dot(a, b):
    return jnp.dot(a, b, preferred_element_type=F32)


def _dot_nt(a, b):
    return lax.dot_general(a, b, (((1,), (1,)), ((), ())), preferred_element_type=F32)


def _const_spec(shape):
    nd = len(shape)
    return pl.BlockSpec(shape, lambda *_: (0,) * nd)


def _prep_kernel(x_ref, pos_ref, invf_ref, g_ref, wa_ref, qn_ref, wqm_ref, wqs_ref, kvn_ref, wk_ref, wv_ref,
                 h_ref, q_ref, k_ref, v_ref):
    tm = x_ref.shape[0]
    h = _rms(x_ref[...], g_ref[...]).astype(BF16)
    h_ref[...] = h
    p = _dot(h, wa_ref[...])
    cq = _rms(p[:, :OFF_Q], qn_ref[...]).astype(BF16)
    ckv = _rms(p[:, OFF_Q:OFF_KV], kvn_ref[...]).astype(BF16)

    ang = invf_ref[...] * pos_ref[0].astype(F32)
    c16 = jnp.cos(ang)
    s16 = jnp.sin(ang)
    ones = jnp.ones((MLA_NOPE, tm), F32)
    zeros_lo = jnp.zeros((MLA_NOPE, tm), F32)
    zeros_hi = jnp.zeros((LANES - MLA_QK_DIM, tm), F32)
    cos_t = jnp.concatenate([ones, c16, c16, zeros_hi], axis=0).T
    sin_t = jnp.concatenate([zeros_lo, s16, s16, zeros_hi], axis=0).T

    qm = _dot(cq, wqm_ref[...])
    qs = _dot(cq, wqs_ref[...])
    kn = _dot(ckv, wk_ref[...])
    kpe = p[:, OFF_KV:OFF_KV + LANES] * cos_t + p[:, OFF_KV + LANES:OFF_KV + 2 * LANES] * sin_t
    scale = MLA_QK_DIM ** -0.5
    for hd in range(MLA_HEADS):
        sl = slice(hd * LANES, (hd + 1) * LANES)
        q_ref[:, sl] = ((qm[:, sl] * cos_t + qs[:, sl] * sin_t) * scale).astype(BF16)
        k_ref[:, sl] = (kn[:, sl] + kpe).astype(BF16)
    v_ref[...] = _dot(ckv, wv_ref[...]).astype(BF16)


def _prep(x2, pos_rows, invf, g_pre, w_a, q_norm, w_qm, w_qs, kv_norm, w_k, w_v):
    t, d = x2.shape
    tm = PREP_TM
    hw = MLA_HEADS * LANES
    row = lambda i: (i, 0)
    out_bf = jax.ShapeDtypeStruct((t, hw), BF16)
    return pl.pallas_call(
        _prep_kernel,
        grid=(t // tm,),
        in_specs=[
            pl.BlockSpec((tm, d), row),
            pl.BlockSpec((1, 1, tm), lambda i: (i, 0, 0)),
            _const_spec(invf.shape),
            _const_spec(g_pre.shape),
            _const_spec(w_a.shape),
            _const_spec(q_norm.shape),
            _const_spec(w_qm.shape),
            _const_spec(w_qs.shape),
            _const_spec(kv_norm.shape),
            _const_spec(w_k.shape),
            _const_spec(w_v.shape),
        ],
        out_specs=[pl.BlockSpec((tm, d), row), pl.BlockSpec((tm, hw), row), pl.BlockSpec((tm, hw), row),
                   pl.BlockSpec((tm, hw), row)],
        out_shape=[jax.ShapeDtypeStruct((t, d), BF16), out_bf, out_bf, out_bf],
        compiler_params=pltpu.CompilerParams(dimension_semantics=("arbitrary",), vmem_limit_bytes=VMEM_LIMIT),
        name="prep",
    )(x2, pos_rows, invf, g_pre, w_a, q_norm, w_qm, w_qs, kv_norm, w_k, w_v)


def _dilproj_kernel(h_ref, w_ref, o_ref, acc_ref, *, dil):
    acc = _dot(h_ref[...], w_ref[...])
    if dil == 1:
        o_ref[0] = acc.astype(BF16)
    else:
        chunks, tm, _ = acc_ref.shape
        rows = tm // dil
        for c in range(chunks):
            acc_ref[c] = acc[:, c * LANES:(c + 1) * LANES]
        for r in range(dil):
            for c in range(chunks):
                o_ref[r, :, c * LANES:(c + 1) * LANES] = acc_ref[c, pl.ds(r, rows, stride=dil), :].astype(BF16)


def _dilproj(h2, w, batch, seq, dil):
    t, d = h2.shape
    n = w.shape[1]
    tm, tn = DILPROJ_TM, DILPROJ_TN
    tiles_per_seq = seq // tm
    return pl.pallas_call(
        functools.partial(_dilproj_kernel, dil=dil),
        grid=(t // tm, n // tn),
        in_specs=[pl.BlockSpec((tm, d), lambda i, j: (i, 0)), pl.BlockSpec((d, tn), lambda i, j: (0, j))],
        out_specs=pl.BlockSpec((None, dil, tm // dil, tn),
                               lambda i, j: (i // tiles_per_seq, 0, i % tiles_per_seq, j)),
        out_shape=jax.ShapeDtypeStruct((batch, dil, seq // dil, n), BF16),
        scratch_shapes=[pltpu.VMEM((tn // LANES, tm, LANES), F32)],
        compiler_params=pltpu.CompilerParams(dimension_semantics=("arbitrary", "arbitrary"),
                                             vmem_limit_bytes=VMEM_LIMIT),
        name=f"dilproj{dil}",
    )(h2, w)


def _mla_kernel(q_ref, k_ref, v_ref, o_ref, m_ref, l_ref, acc_ref):
    tq = q_ref.shape[0]
    qi = pl.program_id(2)
    q = q_ref[...]
    m_ref[...] = jnp.full(m_ref.shape, NEG_INF, F32)
    l_ref[...] = jnp.zeros(l_ref.shape, F32)
    acc_ref[...] = jnp.zeros(acc_ref.shape, F32)

    def step(start, masked):
        k = k_ref[pl.ds(start, tq), :]
        v = v_ref[pl.ds(start, tq), :]
        s = _dot_nt(q, k)
        if masked:
            row = lax.broadcasted_iota(jnp.int32, s.shape, 0)
            col = lax.broadcasted_iota(jnp.int32, s.shape, 1)
            s = jnp.where(col <= row, s, NEG_INF)
        m_prev = m_ref[...]
        m_new = jnp.maximum(m_prev, jnp.max(s, axis=-1, keepdims=True))
        alpha = jnp.exp(m_prev - m_new)
        p = jnp.exp(s - m_new)
        l_ref[...] = alpha * l_ref[...] + jnp.sum(p, axis=-1, keepdims=True)
        acc_ref[...] = alpha * acc_ref[...] + _dot(p.astype(BF16), v)
        m_ref[...] = m_new

    def body(c, carry):
        step(pl.multiple_of(c * tq, tq), False)
        return carry

    lax.fori_loop(0, qi, body, 0)
    step(pl.multiple_of(qi * tq, tq), True)
    o_ref[...] = (acc_ref[...] / l_ref[...]).astype(BF16)


def _mla(q3, k3, v3):
    b, s, hw = q3.shape
    tq = MLA_TQ
    return pl.pallas_call(
        _mla_kernel,
        grid=(b, MLA_HEADS, s // tq),
        in_specs=[
            pl.BlockSpec((None, tq, LANES), lambda bi, h, i: (bi, i, h)),
            pl.BlockSpec((None, s, LANES), lambda bi, h, i: (bi, 0, h)),
            pl.BlockSpec((None, s, LANES), lambda bi, h, i: (bi, 0, h)),
        ],
        out_specs=pl.BlockSpec((None, tq, LANES), lambda bi, h, i: (bi, i, h)),
        out_shape=jax.ShapeDtypeStruct((b, s, hw), BF16),
        scratch_shapes=[pltpu.VMEM((tq, 1), F32), pltpu.VMEM((tq, 1), F32), pltpu.VMEM((tq, LANES), F32)],
        compiler_params=pltpu.CompilerParams(dimension_semantics=("arbitrary", "arbitrary", "arbitrary"),
                                             vmem_limit_bytes=VMEM_LIMIT),
        name="mla",
    )(q3, k3, v3)


def _dilattn_kernel(q_ref, kc_ref, vc_ref, kp_ref, vp_ref, o_ref, lse_ref, *, dil, group):
    n = pl.program_id(2)
    tb = q_ref.shape[0]
    row = lax.broadcasted_iota(jnp.int32, (BLOCK, BLOCK), 0)
    col = lax.broadcasted_iota(jnp.int32, (BLOCK, BLOCK), 1)
    rel = (row - col).astype(F32)
    cur_ok = col <= row
    prev_tri = col >= row
    lane = lax.broadcasted_iota(jnp.int32, (BLOCK, LANES), 1)
    scale = DIL_HEAD_DIM ** -0.5
    for t in range(tb // BLOCK):
        rs = slice(t * BLOCK, (t + 1) * BLOCK)
        if t == 0:
            prev_ok = jnp.logical_and(prev_tri, n > 0)
        else:
            prev_ok = prev_tri
        lse_tile = jnp.zeros((BLOCK, LANES), F32)
        for hh in range(DIL_HPG):
            cs = slice(hh * DIL_HEAD_DIM, (hh + 1) * DIL_HEAD_DIM)
            slope = float(2.0 ** (-8.0 * (hh * DIL_GROUPS + group + 1) / DIL_HEADS)) * dil
            q = q_ref[rs, cs]
            if t == 0:
                kp, vp = kp_ref[:, cs], vp_ref[:, cs]
            else:
                ps = slice((t - 1) * BLOCK, t * BLOCK)
                kp, vp = kc_ref[ps, cs], vc_ref[ps, cs]
            kc, vc = kc_ref[rs, cs], vc_ref[rs, cs]
            s_cur = jnp.where(cur_ok, _dot_nt(q, kc) * scale - slope * rel, NEG_INF)
            s_prev = jnp.where(prev_ok, _dot_nt(q, kp) * scale - slope * (rel + float(BLOCK)), NEG_INF)
            m = jnp.maximum(jnp.max(s_cur, axis=-1, keepdims=True), jnp.max(s_prev, axis=-1, keepdims=True))
            e_cur = jnp.exp(s_cur - m)
            e_prev = jnp.exp(s_prev - m)
            den = jnp.sum(e_cur, axis=-1, keepdims=True) + jnp.sum(e_prev, axis=-1, keepdims=True)
            o = (_dot(e_cur.astype(BF16), vc) + _dot(e_prev.astype(BF16), vp)) / den
            o_ref[rs, cs] = o.astype(BF16)
            lse_tile = jnp.where(lane == hh, m + jnp.log(den), lse_tile)
        lse_ref[rs, :] = lse_tile


def _dilattn(qkv, dil, group):
    b, d, l, _ = qkv.shape
    tb = DIL_TB
    bpt = tb // BLOCK
    w = DIL_WIDTH
    cur = lambda c: pl.BlockSpec((None, None, tb, w), lambda bi, r, n: (bi, r, n, c))
    prev = lambda c: pl.BlockSpec((None, None, BLOCK, w), lambda bi, r, n: (bi, r, jnp.maximum(n * bpt - 1, 0), c))
    return pl.pallas_call(
        functools.partial(_dilattn_kernel, dil=dil, group=group),
        grid=(b, d, l // tb),
        in_specs=[cur(0), cur(1), cur(2), prev(1), prev(2)],
        out_specs=[pl.BlockSpec((None, None, tb, w), lambda bi, r, n: (bi, r, n, 0)),
                   pl.BlockSpec((None, None, tb, LANES), lambda bi, r, n: (bi, r, n, 0))],
        out_shape=[jax.ShapeDtypeStruct((b, d, l, w), BF16), jax.ShapeDtypeStruct((b, d, l, LANES), F32)],
        compiler_params=pltpu.CompilerParams(dimension_semantics=("arbitrary", "arbitrary", "arbitrary"),
                                             vmem_limit_bytes=VMEM_LIMIT),
        name=f"dilattn{dil}",
    )(qkv, qkv, qkv, qkv, qkv)


def _memkv_kernel(mem_ref, g_ref, w_ref, o_ref):
    o_ref[...] = _dot(_rms(mem_ref[...], g_ref[...]).astype(BF16), w_ref[...]).astype(BF16)


def _memkv(mem2, g_mem, w):
    return pl.pallas_call(
        _memkv_kernel,
        out_shape=jax.ShapeDtypeStruct((mem2.shape[0], w.shape[1]), BF16),
        compiler_params=pltpu.CompilerParams(vmem_limit_bytes=VMEM_LIMIT),
        name="memkv",
    )(mem2, g_mem, w)


def _sigmoid(z):
    return 1.0 / (1.0 + jnp.exp(-z))


def _merge_kernel(x_ref, h_ref, wmq_ref, wg_ref, bg_ref, kvm_ref, ymla_ref, wbm_ref,
                  o0_ref, o1_ref, o2_ref, l0_ref, l1_ref, l2_ref, wbd_ref, wbmem_ref, wo_ref, gpm_ref, gpf_ref,
                  x1_ref, h2_ref, nat1_ref, nat2_ref, lse1_ref, lse2_ref):
    tm, dm = x_ref.shape
    h = h_ref[...]

    for src, lsrc, dst, ldst in ((o1_ref, l1_ref, nat1_ref, lse1_ref), (o2_ref, l2_ref, nat2_ref, lse2_ref)):
        d = src.shape[0]
        rows = src.shape[1]
        for r in range(d):
            for hh in range(DIL_HPG):
                cs = slice(hh * DIL_HEAD_DIM, (hh + 1) * DIL_HEAD_DIM)
                dst[hh, pl.ds(r, rows, stride=d), :] = src[r, :, cs].astype(F32)
            ldst[pl.ds(r, rows, stride=d), :] = lsrc[r]
    lg = (l0_ref[0], lse1_ref[...], lse2_ref[...])
    heads = []
    for hh in range(DIL_HPG):
        cs = slice(hh * DIL_HEAD_DIM, (hh + 1) * DIL_HEAD_DIM)
        og = (o0_ref[0, :, cs].astype(F32), nat1_ref[hh], nat2_ref[hh])
        ls = [l[:, hh:hh + 1] for l in lg]
        mx = jnp.maximum(jnp.maximum(ls[0], ls[1]), ls[2])
        ws = [jnp.exp(l - mx) for l in ls]
        num = ws[0] * og[0] + ws[1] * og[1] + ws[2] * og[2]
        heads.append((num / (ws[0] + ws[1] + ws[2])).astype(BF16))
    y_dil = jnp.concatenate(heads, axis=-1)

    memq = _dot(h, wmq_ref[...])
    mheads = []
    for hh in range(MEM_HEADS):
        cs = slice(hh * MEM_HEAD_DIM, (hh + 1) * MEM_HEAD_DIM)
        q = (memq[:, cs] * MEM_HEAD_DIM ** -0.5).astype(BF16)
        s = _dot_nt(q, kvm_ref[:, cs])
        e = jnp.exp(s - jnp.max(s, axis=-1, keepdims=True))
        o = _dot(e.astype(BF16), kvm_ref[:, MEM_WIDTH + hh * MEM_HEAD_DIM:MEM_WIDTH + (hh + 1) * MEM_HEAD_DIM])
        mheads.append((o / jnp.sum(e, axis=-1, keepdims=True)).astype(BF16))
    y_mem = jnp.concatenate(mheads, axis=-1)

    merged = jnp.zeros((tm, dm), F32)
    for br, (y, w_ref) in enumerate(((ymla_ref[...], wbm_ref), (y_dil, wbd_ref), (y_mem, wbmem_ref))):
        cs = slice(br * dm, (br + 1) * dm)
        gate = _sigmoid(_dot(h, wg_ref[:, cs]) + bg_ref[:, cs])
        merged = merged + gate * _dot(y, w_ref[...])
    mixed = _dot(merged.astype(BF16), wo_ref[...])
    x1 = x_ref[...] + _rms(mixed, gpm_ref[...])
    x1_ref[...] = x1
    h2_ref[...] = _rms(x1, gpf_ref[...]).astype(BF16)


def _merge(x2, h2, w_mq, w_g, b_g, kvm, y_mla, w_bm, o_dil, lse_dil, w_bd, w_bmem, w_o, g_pm, g_pf, batch, seq):
    t, dm = x2.shape
    tm = MERGE_TM
    tps = seq // tm
    n_mem = kvm.shape[0] // batch
    row = lambda i: (i, 0)
    single = pl.Buffered(1)
    const = lambda a: pl.BlockSpec(a.shape, lambda i: (0,) * a.ndim, pipeline_mode=single)

    def dil_spec(a):
        d, width = a.shape[1], a.shape[3]
        return pl.BlockSpec((None, d, tm // d, width), lambda i: (i // tps, 0, i % tps, 0))

    return pl.pallas_call(
        _merge_kernel,
        grid=(t // tm,),
        in_specs=[
            pl.BlockSpec((tm, dm), row), pl.BlockSpec((tm, dm), row), const(w_mq), const(w_g), const(b_g),
            pl.BlockSpec((n_mem, kvm.shape[1]), lambda i: (i // tps, 0)),
            pl.BlockSpec((tm, y_mla.shape[1]), row), const(w_bm),
            dil_spec(o_dil[0]), dil_spec(o_dil[1]), dil_spec(o_dil[2]),
            dil_spec(lse_dil[0]), dil_spec(lse_dil[1]), dil_spec(lse_dil[2]),
            const(w_bd), const(w_bmem), const(w_o), const(g_pm), const(g_pf),
        ],
        out_specs=[pl.BlockSpec((tm, dm), row), pl.BlockSpec((tm, dm), row)],
        out_shape=[jax.ShapeDtypeStruct((t, dm), F32), jax.ShapeDtypeStruct((t, dm), BF16)],
        scratch_shapes=[pltpu.VMEM((DIL_HPG, tm, DIL_HEAD_DIM), F32), pltpu.VMEM((DIL_HPG, tm, DIL_HEAD_DIM), F32),
                        pltpu.VMEM((tm, LANES), F32), pltpu.VMEM((tm, LANES), F32)],
        compiler_params=pltpu.CompilerParams(dimension_semantics=("arbitrary",), vmem_limit_bytes=VMEM_LIMIT),
        name="merge",
    )(x2, h2, w_mq, w_g, b_g, kvm, y_mla, w_bm, o_dil[0], o_dil[1], o_dil[2], lse_dil[0], lse_dil[1], lse_dil[2],
      w_bd, w_bmem, w_o, g_pm, g_pf)


def _ffn_kernel(x1_ref, h2_ref, halo_ref, wg_ref, wv_ref, cwg_ref, cwv_ref, cbg_ref, cbv_ref, wd_ref, gpost_ref,
                out_ref, hcat_ref, ug_ref, uv_ref, acc_ref, *, tiles_per_seq):
    i = pl.program_id(0)
    f = pl.program_id(1)
    tm = x1_ref.shape[0]
    halo = FFN_HALO

    @pl.when(f == 0)
    def _():
        first = (i % tiles_per_seq) == 0
        hcat_ref[:halo, :] = jnp.where(first, jnp.zeros_like(halo_ref[...]), halo_ref[...])
        hcat_ref[halo:, :] = h2_ref[...]
        acc_ref[...] = jnp.zeros(acc_ref.shape, F32)

    hc = hcat_ref[...]
    ug_ref[...] = _dot(hc, wg_ref[...])
    uv_ref[...] = _dot(hc, wv_ref[...])

    def conv(u_ref, cw_ref, cb_ref):
        z = cb_ref[...] + cw_ref[0:1, :] * u_ref[halo - 2:halo - 2 + tm, :]
        z = z + cw_ref[1:2, :] * u_ref[halo - 1:halo - 1 + tm, :]
        return z + cw_ref[2:3, :] * u_ref[halo:halo + tm, :]

    gate = conv(ug_ref, cwg_ref, cbg_ref)
    val = conv(uv_ref, cwv_ref, cbv_ref)
    a = (gate * _sigmoid(gate) * val).astype(BF16)
    acc_ref[...] += _dot(a, wd_ref[...])

    @pl.when(f == pl.num_programs(1) - 1)
    def _():
        out_ref[...] = x1_ref[...] + _rms(acc_ref[...], gpost_ref[...])


def _ffn(x1, h2, w_up, conv_w, conv_b, w_down, g_post, seq):
    t, dm = x1.shape
    dff = w_down.shape[0]
    tm, tf, halo = FFN_TM, FFN_TF, FFN_HALO
    nf = dff // tf
    tps = seq // tm
    row = lambda i, f: (i, 0)
    return pl.pallas_call(
        functools.partial(_ffn_kernel, tiles_per_seq=tps),
        grid=(t // tm, nf),
        in_specs=[
            pl.BlockSpec((tm, dm), row),
            pl.BlockSpec((tm, dm), row),
            pl.BlockSpec((halo, dm), lambda i, f: (jnp.maximum(i * (tm // halo) - 1, 0), 0)),
            pl.BlockSpec((dm, tf), lambda i, f: (0, f)),
            pl.BlockSpec((dm, tf), lambda i, f: (0, nf + f)),
            pl.BlockSpec((CONV_WIDTH, tf), lambda i, f: (0, f)),
            pl.BlockSpec((CONV_WIDTH, tf), lambda i, f: (0, nf + f)),
            pl.BlockSpec((1, tf), lambda i, f: (0, f)),
            pl.BlockSpec((1, tf), lambda i, f: (0, nf + f)),
            pl.BlockSpec((tf, dm), lambda i, f: (f, 0)),
            pl.BlockSpec((1, dm), lambda i, f: (0, 0)),
        ],
        out_specs=pl.BlockSpec((tm, dm), row),
        out_shape=jax.ShapeDtypeStruct((t, dm), F32),
        scratch_shapes=[pltpu.VMEM((tm + halo, dm), BF16), pltpu.VMEM((tm + halo, tf), F32),
                        pltpu.VMEM((tm + halo, tf), F32), pltpu.VMEM((tm, dm), F32)],
        compiler_params=pltpu.CompilerParams(dimension_semantics=("arbitrary", "arbitrary"),
                                             vmem_limit_bytes=VMEM_LIMIT),
        name="ffn",
    )(x1, h2, h2, w_up, w_up, conv_w, conv_w, conv_b, conv_b, w_down, g_post)


def _rot_half_cols(w):
    return jnp.concatenate([-w[..., ROPE_HALF:], w[..., :ROPE_HALF]], axis=-1)


def _prep_weights(w_in, w_uq, w_ukv):
    dm = w_in.shape[0]
    pad_hi = LANES - MLA_QK_DIM
    kr = w_in[:, OFF_KV:OFF_KR]
    z_lo = jnp.zeros((dm, MLA_NOPE), F32)
    z_hi = jnp.zeros((dm, pad_hi), F32)
    w_a = jnp.concatenate([w_in[:, :OFF_KV], z_lo, kr, z_hi, z_lo, _rot_half_cols(kr), z_hi], axis=1)

    uq = w_uq.reshape(MLA_Q_RANK, MLA_HEADS, MLA_QK_DIM)
    zq_lo = jnp.zeros((MLA_Q_RANK, MLA_HEADS, MLA_NOPE), F32)
    zq_hi = jnp.zeros((MLA_Q_RANK, MLA_HEADS, pad_hi), F32)
    w_qm = jnp.concatenate([uq, zq_hi], axis=-1).reshape(MLA_Q_RANK, MLA_HEADS * LANES)
    w_qs = jnp.concatenate([zq_lo, _rot_half_cols(uq[..., MLA_NOPE:]), zq_hi], axis=-1)
    w_qs = w_qs.reshape(MLA_Q_RANK, MLA_HEADS * LANES)

    ukv = w_ukv.reshape(MLA_KV_RANK, MLA_HEADS, MLA_NOPE + MLA_V)
    zk = jnp.zeros((MLA_KV_RANK, MLA_HEADS, LANES - MLA_NOPE), F32)
    zv = jnp.zeros((MLA_KV_RANK, MLA_HEADS, LANES - MLA_V), F32)
    w_k = jnp.concatenate([ukv[..., :MLA_NOPE], zk], axis=-1).reshape(MLA_KV_RANK, MLA_HEADS * LANES)
    w_v = jnp.concatenate([ukv[..., MLA_NOPE:], zv], axis=-1).reshape(MLA_KV_RANK, MLA_HEADS * LANES)
    return tuple(a.astype(BF16) for a in (w_a, w_qm, w_qs, w_k, w_v))


def _layer(x, mem, positions, g_pre_mix, w_in, b_gate, mla_q_norm, w_uq, mla_kv_norm, w_ukv, g_mem, w_mem_kv,
           w_br_mla, w_br_dil, w_br_mem, w_o, g_post_mix, g_pre_ffn, w_ffn_up, conv_w, conv_b, w_ffn_down,
           g_post_ffn):
    batch, seq, dm = x.shape
    t = batch * seq
    x2 = x.reshape(t, dm)
    r2 = lambda v: v.reshape(1, -1)

    w_a, w_qm, w_qs, w_k, w_v = _prep_weights(w_in, w_uq, w_ukv)
    invf = (ROPE_THETA ** (-jnp.arange(ROPE_HALF, dtype=F32) / ROPE_HALF)).reshape(ROPE_HALF, 1)
    pos_rows = positions.reshape(t // PREP_TM, 1, PREP_TM)

    h2d, q, k, v = _prep(x2, pos_rows, invf, r2(g_pre_mix), w_a, r2(mla_q_norm), w_qm, w_qs, r2(mla_kv_norm),
                         w_k, w_v)
    hw = MLA_HEADS * LANES
    y_mla = _mla(q.reshape(batch, seq, hw), k.reshape(batch, seq, hw), v.reshape(batch, seq, hw)).reshape(t, hw)

    w_dil = w_in[:, OFF_KR:OFF_DIL].reshape(dm, 3, DIL_GROUPS, DIL_WIDTH)
    o_dil, lse_dil = [], []
    for g, (_, dil) in enumerate(DIL_PAIRS):
        w_g = w_dil[:, :, g, :].reshape(dm, 3 * DIL_WIDTH).astype(BF16)
        qkv = _dilproj(h2d, w_g, batch, seq, dil)
        o, lse = _dilattn(qkv, dil, g)
        o_dil.append(o)
        lse_dil.append(lse)

    kvm = _memkv(mem.reshape(-1, dm), r2(g_mem), w_mem_kv.astype(BF16))

    w_bm = w_br_mla.reshape(MLA_HEADS, MLA_V, dm)
    w_bm = jnp.concatenate([w_bm, jnp.zeros((MLA_HEADS, LANES - MLA_V, dm), F32)], axis=1).reshape(hw, dm)
    x1, h2 = _merge(x2, h2d, w_in[:, OFF_DIL:OFF_MEMQ].astype(BF16), w_in[:, OFF_MEMQ:].astype(BF16),
                    r2(b_gate), kvm, y_mla, w_bm.astype(BF16), o_dil, lse_dil, w_br_dil.astype(BF16),
                    w_br_mem.astype(BF16), w_o.astype(BF16), r2(g_post_mix), r2(g_pre_ffn), batch, seq)

    out = _ffn(x1, h2, w_ffn_up.astype(BF16), conv_w, r2(conv_b), w_ffn_down.astype(BF16), r2(g_post_ffn), seq)
    return out.reshape(batch, seq, dm)


def kernel(x, mem, positions, g_pre_mix, w_in, b_gate, mla_q_norm, w_uq, mla_kv_norm, w_ukv, g_mem, w_mem_kv,
           w_br_mla, w_br_dil, w_br_mem, w_o, g_post_mix, g_pre_ffn, w_ffn_up, conv_w, conv_b, w_ffn_down,
           g_post_ffn):
    for l in range(w_in.shape[0]):
        x = _layer(x, mem, positions, g_pre_mix[l], w_in[l], b_gate[l], mla_q_norm[l], w_uq[l], mla_kv_norm[l],
                   w_ukv[l], g_mem[l], w_mem_kv[l], w_br_mla[l], w_br_dil[l], w_br_mem[l], w_o[l], g_post_mix[l],
                   g_pre_ffn[l], w_ffn_up[l], conv_w[l], conv_b[l], w_ffn_down[l], g_post_ffn[l])
    return x
```

```python
import functools

import jax
import jax.numpy as jnp
from jax import lax
from jax.experimental import pallas as pl
from jax.experimental.pallas import tpu as pltpu

F32 = jnp.float32
BF16 = jnp.bfloat16

RMS_EPS = 1e-6
LOG2E = 1.4426950408889634
NEG_INF = -1e30
LANES = 128

BLOCK = 128
MLA_HEADS = 8
MLA_NOPE = 64
MLA_ROPE = 32
MLA_V = 64
MLA_QK_DIM = MLA_NOPE + MLA_ROPE
MLA_Q_RANK = 384
MLA_KV_RANK = 256
ROPE_THETA = 10000.0
ROPE_HALF = MLA_ROPE // 2

DIL_PAIRS = ((128, 1), (512, 4), (2048, 16))
DIL_GROUPS = 3
DIL_HPG = 4
DIL_HEADS = DIL_GROUPS * DIL_HPG
DIL_HEAD_DIM = 128
DIL_WIDTH = DIL_HPG * DIL_HEAD_DIM

MEM_HEADS = 4
MEM_HEAD_DIM = 128
MEM_WIDTH = MEM_HEADS * MEM_HEAD_DIM

N_BRANCH = 3
CONV_WIDTH = 3

OFF_Q = MLA_Q_RANK
OFF_KV = OFF_Q + MLA_KV_RANK
OFF_KR = OFF_KV + MLA_ROPE
OFF_DIL = OFF_KR + 3 * DIL_HEADS * DIL_HEAD_DIM
OFF_MEMQ = OFF_DIL + MEM_WIDTH

PREP_TM = 512
DILPROJ_TM = 2048
DILPROJ_TN = 512
MLA_TQ = 512
MLA_HPS = 2
assert PREP_TM == MLA_TQ
DIL_TB = 512
MERGE_TM = 512
FFN_TM = 1024
FFN_TF = 256
FFN_HALO = 16

VMEM_LIMIT = 56 * 1024 * 1024


def _rms(xf, g):
    return xf * lax.rsqrt(jnp.mean(xf * xf, axis=-1, keepdims=True) + RMS_EPS) * g


def _dot(a, b):
    return jnp.dot(a, b, preferred_element_type=F32)


def _dot_nt(a, b):
    return lax.dot_general(a, b, (((1,), (1,)), ((), ())), preferred_element_type=F32)


def _const_spec(shape):
    nd = len(shape)
    return pl.BlockSpec(shape, lambda *_: (0,) * nd)


def _prep_kernel(x_ref, pos_ref, invf_ref, g_ref, wa_ref, qn_ref, wqmt_ref, wqst_ref, kvn_ref, wk_ref, wvt_ref,
                 h_ref, qt_ref, k_ref, vt_ref):
    tm = x_ref.shape[0]
    h = _rms(x_ref[...], g_ref[...]).astype(BF16)
    h_ref[...] = h
    p = _dot(h, wa_ref[...])
    cq = _rms(p[:, :OFF_Q], qn_ref[...]).astype(BF16)
    ckv = _rms(p[:, OFF_Q:OFF_KV], kvn_ref[...]).astype(BF16)

    ang = invf_ref[...] * pos_ref[0].astype(F32)
    c16 = jnp.cos(ang)
    s16 = jnp.sin(ang)
    ones = jnp.ones((MLA_NOPE, tm), F32)
    zeros_lo = jnp.zeros((MLA_NOPE, tm), F32)
    zeros_hi = jnp.zeros((LANES - MLA_QK_DIM, tm), F32)
    cos_c = jnp.concatenate([ones, c16, c16, zeros_hi], axis=0)
    sin_c = jnp.concatenate([zeros_lo, s16, s16, zeros_hi], axis=0)

    qmt = _dot_nt(wqmt_ref[...], cq)
    qst = _dot_nt(wqst_ref[...], cq)
    qscale = MLA_QK_DIM ** -0.5 * LOG2E
    for hd in range(MLA_HEADS):
        sl = slice(hd * LANES, (hd + 1) * LANES)
        qt_ref[sl, :] = ((qmt[sl, :] * cos_c + qst[sl, :] * sin_c) * qscale).astype(BF16)

    cos_r = cos_c.T
    sin_r = sin_c.T
    kn = _dot(ckv, wk_ref[...])
    kpe = p[:, OFF_KV:OFF_KV + LANES] * cos_r + p[:, OFF_KV + LANES:OFF_KV + 2 * LANES] * sin_r
    for hd in range(MLA_HEADS):
        sl = slice(hd * LANES, (hd + 1) * LANES)
        k_ref[:, sl] = (kn[:, sl] + kpe).astype(BF16)
    vt_ref[0] = _dot_nt(wvt_ref[...], ckv).astype(BF16)


def _prep(x2, pos_rows, invf, g_pre, w_a, q_norm, w_qmt, w_qst, kv_norm, w_k, w_vt, batch, seq):
    t, d = x2.shape
    tm = PREP_TM
    tps = seq // tm
    hw = MLA_HEADS * LANES
    vw = MLA_HEADS * MLA_V
    row = lambda i: (i, 0)
    return pl.pallas_call(
        _prep_kernel,
        grid=(t // tm,),
        in_specs=[
            pl.BlockSpec((tm, d), row),
            pl.BlockSpec((1, 1, tm), lambda i: (i, 0, 0)),
            _const_spec(invf.shape),
            _const_spec(g_pre.shape),
            _const_spec(w_a.shape),
            _const_spec(q_norm.shape),
            _const_spec(w_qmt.shape),
            _const_spec(w_qst.shape),
            _const_spec(kv_norm.shape),
            _const_spec(w_k.shape),
            _const_spec(w_vt.shape),
        ],
        out_specs=[pl.BlockSpec((tm, d), row),
                   pl.BlockSpec((None, hw, tm), lambda i: (i // tps, 0, i % tps)),
                   pl.BlockSpec((tm, hw), row),
                   pl.BlockSpec((None, 1, vw, tm), lambda i: (i // tps, i % tps, 0, 0))],
        out_shape=[jax.ShapeDtypeStruct((t, d), BF16),
                   jax.ShapeDtypeStruct((batch, hw, seq), BF16),
                   jax.ShapeDtypeStruct((t, hw), BF16),
                   jax.ShapeDtypeStruct((batch, tps, vw, tm), BF16)],
        compiler_params=pltpu.CompilerParams(dimension_semantics=("arbitrary",), vmem_limit_bytes=VMEM_LIMIT),
        name="prep",
    )(x2, pos_rows, invf, g_pre, w_a, q_norm, w_qmt, w_qst, kv_norm, w_k, w_vt)


def _dilproj_kernel(h_ref, w_ref, o_ref, acc_ref, *, dil):
    acc = _dot(h_ref[...], w_ref[...])
    if dil == 1:
        o_ref[0] = acc.astype(BF16)
    else:
        chunks, tm, _ = acc_ref.shape
        rows = tm // dil
        for c in range(chunks):
            acc_ref[c] = acc[:, c * LANES:(c + 1) * LANES]
        for r in range(dil):
            for c in range(chunks):
                o_ref[r, :, c * LANES:(c + 1) * LANES] = acc_ref[c, pl.ds(r, rows, stride=dil), :].astype(BF16)


def _dilproj(h2, w, batch, seq, dil):
    t, d = h2.shape
    n = w.shape[1]
    tm, tn = DILPROJ_TM, DILPROJ_TN
    tiles_per_seq = seq // tm
    return pl.pallas_call(
        functools.partial(_dilproj_kernel, dil=dil),
        grid=(t // tm, n // tn),
        in_specs=[pl.BlockSpec((tm, d), lambda i, j: (i, 0)), pl.BlockSpec((d, tn), lambda i, j: (0, j))],
        out_specs=pl.BlockSpec((None, dil, tm // dil, tn),
                               lambda i, j: (i // tiles_per_seq, 0, i % tiles_per_seq, j)),
        out_shape=jax.ShapeDtypeStruct((batch, dil, seq // dil, n), BF16),
        scratch_shapes=[pltpu.VMEM((tn // LANES, tm, LANES), F32)],
        compiler_params=pltpu.CompilerParams(dimension_semantics=("arbitrary", "arbitrary"),
                                             vmem_limit_bytes=VMEM_LIMIT),
        name=f"dilproj{dil}",
    )(h2, w)


def _mla_kernel(qt_ref, k_ref, vt_ref, o_ref, m_ref, l_ref, acc_ref):
    tq = qt_ref.shape[1]
    qi = pl.program_id(2)
    m_ref[...] = jnp.full(m_ref.shape, NEG_INF, F32)
    l_ref[...] = jnp.zeros(l_ref.shape, F32)
    acc_ref[...] = jnp.zeros(acc_ref.shape, F32)

    def step(c, masked):
        start = pl.multiple_of(c * tq, tq)
        for a in range(MLA_HPS):
            k = k_ref[pl.ds(start, tq), a * LANES:(a + 1) * LANES]
            st = _dot(k, qt_ref[a * LANES:(a + 1) * LANES, :])
            if masked:
                key = lax.broadcasted_iota(jnp.int32, st.shape, 0)
                qry = lax.broadcasted_iota(jnp.int32, st.shape, 1)
                st = jnp.where(key <= qry, st, NEG_INF)
            m_prev = m_ref[a]
            m_new = jnp.maximum(m_prev, jnp.max(st, axis=0, keepdims=True))
            alpha = jnp.exp2(m_prev - m_new)
            p = jnp.exp2(st - m_new)
            l_ref[a] = alpha * l_ref[a] + jnp.sum(p, axis=0, keepdims=True)
            vt = vt_ref[c, a * MLA_V:(a + 1) * MLA_V, :]
            acc_ref[a] = alpha * acc_ref[a] + _dot(vt, p.astype(BF16))
            m_ref[a] = m_new

    def body(c, carry):
        step(c, False)
        return carry

    lax.fori_loop(0, qi, body, 0)
    step(qi, True)
    out_t = jnp.concatenate([acc_ref[a] / l_ref[a] for a in range(MLA_HPS)], axis=0)
    o_ref[...] = out_t.T.astype(BF16)


def _mla(qt, k3, vt):
    b, s, _ = k3.shape
    tq = MLA_TQ
    hps = MLA_HPS
    return pl.pallas_call(
        _mla_kernel,
        grid=(b, MLA_HEADS // hps, s // tq),
        in_specs=[
            pl.BlockSpec((None, hps * LANES, tq), lambda bi, h, i: (bi, h, i)),
            pl.BlockSpec((None, s, hps * LANES), lambda bi, h, i: (bi, 0, h)),
            pl.BlockSpec((None, s // tq, hps * MLA_V, tq), lambda bi, h, i: (bi, 0, h, 0)),
        ],
        out_specs=pl.BlockSpec((None, tq, hps * MLA_V), lambda bi, h, i: (bi, i, h)),
        out_shape=jax.ShapeDtypeStruct((b, s, MLA_HEADS * MLA_V), BF16),
        scratch_shapes=[pltpu.VMEM((hps, 1, tq), F32), pltpu.VMEM((hps, 1, tq), F32),
                        pltpu.VMEM((hps, MLA_V, tq), F32)],
        compiler_params=pltpu.CompilerParams(dimension_semantics=("arbitrary", "arbitrary", "arbitrary"),
                                             vmem_limit_bytes=VMEM_LIMIT),
        name="mla",
    )(qt, k3, vt)


def _dilattn_kernel(q_ref, kc_ref, vc_ref, kp_ref, vp_ref, o_ref, lse_ref, *, dil, group):
    n = pl.program_id(2)
    tb = q_ref.shape[0]
    row = lax.broadcasted_iota(jnp.int32, (BLOCK, BLOCK), 0)
    col = lax.broadcasted_iota(jnp.int32, (BLOCK, BLOCK), 1)
    rel = (row - col).astype(F32)
    cur_ok = col <= row
    prev_tri = col >= row
    lane = lax.broadcasted_iota(jnp.int32, (BLOCK, LANES), 1)
    scale = DIL_HEAD_DIM ** -0.5
    for t in range(tb // BLOCK):
        rs = slice(t * BLOCK, (t + 1) * BLOCK)
        if t == 0:
            prev_ok = jnp.logical_and(prev_tri, n > 0)
        else:
            prev_ok = prev_tri
        lse_tile = jnp.zeros((BLOCK, LANES), F32)
        for hh in range(DIL_HPG):
            cs = slice(hh * DIL_HEAD_DIM, (hh + 1) * DIL_HEAD_DIM)
            slope = float(2.0 ** (-8.0 * (hh * DIL_GROUPS + group + 1) / DIL_HEADS)) * dil
            q = q_ref[rs, cs]
            if t == 0:
                kp, vp = kp_ref[:, cs], vp_ref[:, cs]
            else:
                ps = slice((t - 1) * BLOCK, t * BLOCK)
                kp, vp = kc_ref[ps, cs], vc_ref[ps, cs]
            kc, vc = kc_ref[rs, cs], vc_ref[rs, cs]
            s_cur = jnp.where(cur_ok, _dot_nt(q, kc) * scale - slope * rel, NEG_INF)
            s_prev = jnp.where(prev_ok, _dot_nt(q, kp) * scale - slope * (rel + float(BLOCK)), NEG_INF)
            m = jnp.maximum(jnp.max(s_cur, axis=-1, keepdims=True), jnp.max(s_prev, axis=-1, keepdims=True))
            e_cur = jnp.exp(s_cur - m)
            e_prev = jnp.exp(s_prev - m)
            den = jnp.sum(e_cur, axis=-1, keepdims=True) + jnp.sum(e_prev, axis=-1, keepdims=True)
            o = (_dot(e_cur.astype(BF16), vc) + _dot(e_prev.astype(BF16), vp)) / den
            o_ref[rs, cs] = o.astype(BF16)
            lse_tile = jnp.where(lane == hh, m + jnp.log(den), lse_tile)
        lse_ref[rs, :] = lse_tile


def _dilattn(qkv, dil, group):
    b, d, l, _ = qkv.shape
    tb = DIL_TB
    bpt = tb // BLOCK
    w = DIL_WIDTH
    cur = lambda c: pl.BlockSpec((None, None, tb, w), lambda bi, r, n: (bi, r, n, c))
    prev = lambda c: pl.BlockSpec((None, None, BLOCK, w), lambda bi, r, n: (bi, r, jnp.maximum(n * bpt - 1, 0), c))
    return pl.pallas_call(
        functools.partial(_dilattn_kernel, dil=dil, group=group),
        grid=(b, d, l // tb),
        in_specs=[cur(0), cur(1), cur(2), prev(1), prev(2)],
        out_specs=[pl.BlockSpec((None, None, tb, w), lambda bi, r, n: (bi, r, n, 0)),
                   pl.BlockSpec((None, None, tb, LANES), lambda bi, r, n: (bi, r, n, 0))],
        out_shape=[jax.ShapeDtypeStruct((b, d, l, w), BF16), jax.ShapeDtypeStruct((b, d, l, LANES), F32)],
        compiler_params=pltpu.CompilerParams(dimension_semantics=("arbitrary", "arbitrary", "arbitrary"),
                                             vmem_limit_bytes=VMEM_LIMIT),
        name=f"dilattn{dil}",
    )(qkv, qkv, qkv, qkv, qkv)


def _memkv_kernel(mem_ref, g_ref, w_ref, o_ref):
    o_ref[...] = _dot(_rms(mem_ref[...], g_ref[...]).astype(BF16), w_ref[...]).astype(BF16)


def _memkv(mem2, g_mem, w):
    return pl.pallas_call(
        _memkv_kernel,
        out_shape=jax.ShapeDtypeStruct((mem2.shape[0], w.shape[1]), BF16),
        compiler_params=pltpu.CompilerParams(vmem_limit_bytes=VMEM_LIMIT),
        name="memkv",
    )(mem2, g_mem, w)


def _sigmoid(z):
    return 1.0 / (1.0 + jnp.exp(-z))


def _merge_kernel(x_ref, h_ref, wmq_ref, wg_ref, bg_ref, kvm_ref, ymla_ref, wbm_ref,
                  o0_ref, o1_ref, o2_ref, l0_ref, l1_ref, l2_ref, wbd_ref, wbmem_ref, wo_ref, gpm_ref, gpf_ref,
                  x1_ref, h2_ref, nat1_ref, nat2_ref, lse1_ref, lse2_ref):
    tm, dm = x_ref.shape
    h = h_ref[...]

    for src, lsrc, dst, ldst in ((o1_ref, l1_ref, nat1_ref, lse1_ref), (o2_ref, l2_ref, nat2_ref, lse2_ref)):
        d = src.shape[0]
        rows = src.shape[1]
        for r in range(d):
            for hh in range(DIL_HPG):
                cs = slice(hh * DIL_HEAD_DIM, (hh + 1) * DIL_HEAD_DIM)
                dst[hh, pl.ds(r, rows, stride=d), :] = src[r, :, cs].astype(F32)
            ldst[pl.ds(r, rows, stride=d), :] = lsrc[r]
    lg = (l0_ref[0], lse1_ref[...], lse2_ref[...])
    heads = []
    for hh in range(DIL_HPG):
        cs = slice(hh * DIL_HEAD_DIM, (hh + 1) * DIL_HEAD_DIM)
        og = (o0_ref[0, :, cs].astype(F32), nat1_ref[hh], nat2_ref[hh])
        ls = [l[:, hh:hh + 1] for l in lg]
        mx = jnp.maximum(jnp.maximum(ls[0], ls[1]), ls[2])
        ws = [jnp.exp(l - mx) for l in ls]
        num = ws[0] * og[0] + ws[1] * og[1] + ws[2] * og[2]
        heads.append((num / (ws[0] + ws[1] + ws[2])).astype(BF16))
    y_dil = jnp.concatenate(heads, axis=-1)

    memq = _dot(h, wmq_ref[...])
    mheads = []
    for hh in range(MEM_HEADS):
        cs = slice(hh * MEM_HEAD_DIM, (hh + 1) * MEM_HEAD_DIM)
        q = (memq[:, cs] * MEM_HEAD_DIM ** -0.5).astype(BF16)
        s = _dot_nt(q, kvm_ref[:, cs])
        e = jnp.exp(s - jnp.max(s, axis=-1, keepdims=True))
        o = _dot(e.astype(BF16), kvm_ref[:, MEM_WIDTH + hh * MEM_HEAD_DIM:MEM_WIDTH + (hh + 1) * MEM_HEAD_DIM])
        mheads.append((o / jnp.sum(e, axis=-1, keepdims=True)).astype(BF16))
    y_mem = jnp.concatenate(mheads, axis=-1)

    merged = jnp.zeros((tm, dm), F32)
    for br, (y, w_ref) in enumerate(((ymla_ref[...], wbm_ref), (y_dil, wbd_ref), (y_mem, wbmem_ref))):
        cs = slice(br * dm, (br + 1) * dm)
        gate = _sigmoid(_dot(h, wg_ref[:, cs]) + bg_ref[:, cs])
        merged = merged + gate * _dot(y, w_ref[...])
    mixed = _dot(merged.astype(BF16), wo_ref[...])
    x1 = x_ref[...] + _rms(mixed, gpm_ref[...])
    x1_ref[...] = x1
    h2_ref[...] = _rms(x1, gpf_ref[...]).astype(BF16)


def _merge(x2, h2, w_mq, w_g, b_g, kvm, y_mla, w_bm, o_dil, lse_dil, w_bd, w_bmem, w_o, g_pm, g_pf, batch, seq):
    t, dm = x2.shape
    tm = MERGE_TM
    tps = seq // tm
    n_mem = kvm.shape[0] // batch
    row = lambda i: (i, 0)
    single = pl.Buffered(1)
    const = lambda a: pl.BlockSpec(a.shape, lambda i: (0,) * a.ndim, pipeline_mode=single)

    def dil_spec(a):
        d, width = a.shape[1], a.shape[3]
        return pl.BlockSpec((None, d, tm // d, width), lambda i: (i // tps, 0, i % tps, 0))

    return pl.pallas_call(
        _merge_kernel,
        grid=(t // tm,),
        in_specs=[
            pl.BlockSpec((tm, dm), row), pl.BlockSpec((tm, dm), row), const(w_mq), const(w_g), const(b_g),
            pl.BlockSpec((n_mem, kvm.shape[1]), lambda i: (i // tps, 0)),
            pl.BlockSpec((tm, y_mla.shape[1]), row), const(w_bm),
            dil_spec(o_dil[0]), dil_spec(o_dil[1]), dil_spec(o_dil[2]),
            dil_spec(lse_dil[0]), dil_spec(lse_dil[1]), dil_spec(lse_dil[2]),
            const(w_bd), const(w_bmem), const(w_o), const(g_pm), const(g_pf),
        ],
        out_specs=[pl.BlockSpec((tm, dm), row), pl.BlockSpec((tm, dm), row)],
        out_shape=[jax.ShapeDtypeStruct((t, dm), F32), jax.ShapeDtypeStruct((t, dm), BF16)],
        scratch_shapes=[pltpu.VMEM((DIL_HPG, tm, DIL_HEAD_DIM), F32), pltpu.VMEM((DIL_HPG, tm, DIL_HEAD_DIM), F32),
                        pltpu.VMEM((tm, LANES), F32), pltpu.VMEM((tm, LANES), F32)],
        compiler_params=pltpu.CompilerParams(dimension_semantics=("arbitrary",), vmem_limit_bytes=VMEM_LIMIT),
        name="merge",
    )(x2, h2, w_mq, w_g, b_g, kvm, y_mla, w_bm, o_dil[0], o_dil[1], o_dil[2], lse_dil[0], lse_dil[1], lse_dil[2],
      w_bd, w_bmem, w_o, g_pm, g_pf)


def _ffn_kernel(x1_ref, h2_ref, halo_ref, wg_ref, wv_ref, cwg_ref, cwv_ref, cbg_ref, cbv_ref, wd_ref, gpost_ref,
                out_ref, hcat_ref, ug_ref, uv_ref, acc_ref, *, tiles_per_seq):
    i = pl.program_id(0)
    f = pl.program_id(1)
    tm = x1_ref.shape[0]
    halo = FFN_HALO

    @pl.when(f == 0)
    def _():
        first = (i % tiles_per_seq) == 0
        hcat_ref[:halo, :] = jnp.where(first, jnp.zeros_like(halo_ref[...]), halo_ref[...])
        hcat_ref[halo:, :] = h2_ref[...]
        acc_ref[...] = jnp.zeros(acc_ref.shape, F32)

    hc = hcat_ref[...]
    ug_ref[...] = _dot(hc, wg_ref[...])
    uv_ref[...] = _dot(hc, wv_ref[...])

    def conv(u_ref, cw_ref, cb_ref):
        z = cb_ref[...] + cw_ref[0:1, :] * u_ref[halo - 2:halo - 2 + tm, :]
        z = z + cw_ref[1:2, :] * u_ref[halo - 1:halo - 1 + tm, :]
        return z + cw_ref[2:3, :] * u_ref[halo:halo + tm, :]

    gate = conv(ug_ref, cwg_ref, cbg_ref)
    val = conv(uv_ref, cwv_ref, cbv_ref)
    a = (gate * _sigmoid(gate) * val).astype(BF16)
    acc_ref[...] += _dot(a, wd_ref[...])

    @pl.when(f == pl.num_programs(1) - 1)
    def _():
        out_ref[...] = x1_ref[...] + _rms(acc_ref[...], gpost_ref[...])


def _ffn(x1, h2, w_up, conv_w, conv_b, w_down, g_post, seq):
    t, dm = x1.shape
    dff = w_down.shape[0]
    tm, tf, halo = FFN_TM, FFN_TF, FFN_HALO
    nf = dff // tf
    tps = seq // tm
    row = lambda i, f: (i, 0)
    return pl.pallas_call(
        functools.partial(_ffn_kernel, tiles_per_seq=tps),
        grid=(t // tm, nf),
        in_specs=[
            pl.BlockSpec((tm, dm), row),
            pl.BlockSpec((tm, dm), row),
            pl.BlockSpec((halo, dm), lambda i, f: (jnp.maximum(i * (tm // halo) - 1, 0), 0)),
            pl.BlockSpec((dm, tf), lambda i, f: (0, f)),
            pl.BlockSpec((dm, tf), lambda i, f: (0, nf + f)),
            pl.BlockSpec((CONV_WIDTH, tf), lambda i, f: (0, f)),
            pl.BlockSpec((CONV_WIDTH, tf), lambda i, f: (0, nf + f)),
            pl.BlockSpec((1, tf), lambda i, f: (0, f)),
            pl.BlockSpec((1, tf), lambda i, f: (0, nf + f)),
            pl.BlockSpec((tf, dm), lambda i, f: (f, 0)),
            pl.BlockSpec((1, dm), lambda i, f: (0, 0)),
        ],
        out_specs=pl.BlockSpec((tm, dm), row),
        out_shape=jax.ShapeDtypeStruct((t, dm), F32),
        scratch_shapes=[pltpu.VMEM((tm + halo, dm), BF16), pltpu.VMEM((tm + halo, tf), F32),
                        pltpu.VMEM((tm + halo, tf), F32), pltpu.VMEM((tm, dm), F32)],
        compiler_params=pltpu.CompilerParams(dimension_semantics=("arbitrary", "arbitrary"),
                                             vmem_limit_bytes=VMEM_LIMIT),
        name="ffn",
    )(x1, h2, h2, w_up, w_up, conv_w, conv_w, conv_b, conv_b, w_down, g_post)


def _rot_half_cols(w):
    return jnp.concatenate([-w[..., ROPE_HALF:], w[..., :ROPE_HALF]], axis=-1)


def _prep_weights(w_in, w_uq, w_ukv):
    dm = w_in.shape[0]
    pad_hi = LANES - MLA_QK_DIM
    kr = w_in[:, OFF_KV:OFF_KR]
    z_lo = jnp.zeros((dm, MLA_NOPE), F32)
    z_hi = jnp.zeros((dm, pad_hi), F32)
    w_a = jnp.concatenate([w_in[:, :OFF_KV], z_lo, kr, z_hi, z_lo, _rot_half_cols(kr), z_hi], axis=1)

    uq = w_uq.reshape(MLA_Q_RANK, MLA_HEADS, MLA_QK_DIM)
    zq_lo = jnp.zeros((MLA_Q_RANK, MLA_HEADS, MLA_NOPE), F32)
    zq_hi = jnp.zeros((MLA_Q_RANK, MLA_HEADS, pad_hi), F32)
    w_qm = jnp.concatenate([uq, zq_hi], axis=-1).reshape(MLA_Q_RANK, MLA_HEADS * LANES)
    w_qs = jnp.concatenate([zq_lo, _rot_half_cols(uq[..., MLA_NOPE:]), zq_hi], axis=-1)
    w_qs = w_qs.reshape(MLA_Q_RANK, MLA_HEADS * LANES)

    ukv = w_ukv.reshape(MLA_KV_RANK, MLA_HEADS, MLA_NOPE + MLA_V)
    zk = jnp.zeros((MLA_KV_RANK, MLA_HEADS, LANES - MLA_NOPE), F32)
    w_k = jnp.concatenate([ukv[..., :MLA_NOPE], zk], axis=-1).reshape(MLA_KV_RANK, MLA_HEADS * LANES)
    w_v = ukv[..., MLA_NOPE:].reshape(MLA_KV_RANK, MLA_HEADS * MLA_V)
    return tuple(a.astype(BF16) for a in (w_a, w_qm.T, w_qs.T, w_k, w_v.T))


def _layer(x, mem, positions, g_pre_mix, w_in, b_gate, mla_q_norm, w_uq, mla_kv_norm, w_ukv, g_mem, w_mem_kv,
           w_br_mla, w_br_dil, w_br_mem, w_o, g_post_mix, g_pre_ffn, w_ffn_up, conv_w, conv_b, w_ffn_down,
           g_post_ffn):
    batch, seq, dm = x.shape
    t = batch * seq
    x2 = x.reshape(t, dm)
    r2 = lambda v: v.reshape(1, -1)

    w_a, w_qmt, w_qst, w_k, w_vt = _prep_weights(w_in, w_uq, w_ukv)
    invf = (ROPE_THETA ** (-jnp.arange(ROPE_HALF, dtype=F32) / ROPE_HALF)).reshape(ROPE_HALF, 1)
    pos_rows = positions.reshape(t // PREP_TM, 1, PREP_TM)

    h2d, qt, k, vt = _prep(x2, pos_rows, invf, r2(g_pre_mix), w_a, r2(mla_q_norm), w_qmt, w_qst,
                           r2(mla_kv_norm), w_k, w_vt, batch, seq)
    y_mla = _mla(qt, k.reshape(batch, seq, MLA_HEADS * LANES), vt).reshape(t, MLA_HEADS * MLA_V)

    w_dil = w_in[:, OFF_KR:OFF_DIL].reshape(dm, 3, DIL_GROUPS, DIL_WIDTH)
    o_dil, lse_dil = [], []
    for g, (_, dil) in enumerate(DIL_PAIRS):
        w_g = w_dil[:, :, g, :].reshape(dm, 3 * DIL_WIDTH).astype(BF16)
        qkv = _dilproj(h2d, w_g, batch, seq, dil)
        o, lse = _dilattn(qkv, dil, g)
        o_dil.append(o)
        lse_dil.append(lse)

    kvm = _memkv(mem.reshape(-1, dm), r2(g_mem), w_mem_kv.astype(BF16))

    x1, h2 = _merge(x2, h2d, w_in[:, OFF_DIL:OFF_MEMQ].astype(BF16), w_in[:, OFF_MEMQ:].astype(BF16),
                    r2(b_gate), kvm, y_mla, w_br_mla.astype(BF16), o_dil, lse_dil, w_br_dil.astype(BF16),
                    w_br_mem.astype(BF16), w_o.astype(BF16), r2(g_post_mix), r2(g_pre_ffn), batch, seq)

    out = _ffn(x1, h2, w_ffn_up.astype(BF16), conv_w, r2(conv_b), w_ffn_down.astype(BF16), r2(g_post_ffn), seq)
    return out.reshape(batch, seq, dm)


def kernel(x, mem, positions, g_pre_mix, w_in, b_gate, mla_q_norm, w_uq, mla_kv_norm, w_ukv, g_mem, w_mem_kv,
           w_br_mla, w_br_dil, w_br_mem, w_o, g_post_mix, g_pre_ffn, w_ffn_up, conv_w, conv_b, w_ffn_down,
           g_post_ffn):
    for l in range(w_in.shape[0]):
        x = _layer(x, mem, positions, g_pre_mix[l], w_in[l], b_gate[l], mla_q_norm[l], w_uq[l], mla_kv_norm[l],
                   w_ukv[l], g_mem[l], w_mem_kv[l], w_br_mla[l], w_br_dil[l], w_br_mem[l], w_o[l], g_post_mix[l],
                   g_pre_ffn[l], w_ffn_up[l], conv_w[l], conv_b[l], w_ffn_down[l], g_post_ffn[l])
    return x
```

```python
import functools

import jax
import jax.numpy as jnp
from jax import lax
from jax.experimental import pallas as pl
from jax.experimental.pallas import tpu as pltpu

F32 = jnp.float32
BF16 = jnp.bfloat16

RMS_EPS = 1e-6
LOG2E = 1.4426950408889634
NEG_INF = -1e30
LANES = 128

BLOCK = 128
MLA_HEADS = 8
MLA_NOPE = 64
MLA_ROPE = 32
MLA_V = 64
MLA_QK_DIM = MLA_NOPE + MLA_ROPE
MLA_Q_RANK = 384
MLA_KV_RANK = 256
ROPE_THETA = 10000.0
ROPE_HALF = MLA_ROPE // 2

DIL_PAIRS = ((128, 1), (512, 4), (2048, 16))
DIL_GROUPS = 3
DIL_HPG = 4
DIL_HEADS = DIL_GROUPS * DIL_HPG
DIL_HEAD_DIM = 128
DIL_WIDTH = DIL_HPG * DIL_HEAD_DIM

MEM_HEADS = 4
MEM_HEAD_DIM = 128
MEM_WIDTH = MEM_HEADS * MEM_HEAD_DIM

N_BRANCH = 3
CONV_WIDTH = 3

OFF_Q = MLA_Q_RANK
OFF_KV = OFF_Q + MLA_KV_RANK
OFF_KR = OFF_KV + MLA_ROPE
OFF_DIL = OFF_KR + 3 * DIL_HEADS * DIL_HEAD_DIM
OFF_MEMQ = OFF_DIL + MEM_WIDTH

PREP_TM = 512
DILPROJ_TM = 2048
DILPROJ_TN = 512
MLA_TQ = 512
MLA_HPS = 2
MLA_QSTRIP = 256
MLA_VROWS = MLA_V + 16
assert PREP_TM == MLA_TQ
DIL_TB = 512
MERGE_TM = 512
FFN_TM = 1024
FFN_TF = 256
FFN_HALO = 16

VMEM_LIMIT = 56 * 1024 * 1024


def _rms(xf, g):
    return xf * lax.rsqrt(jnp.mean(xf * xf, axis=-1, keepdims=True) + RMS_EPS) * g


def _dot(a, b):
    return jnp.dot(a, b, preferred_element_type=F32)


def _dot_nt(a, b):
    return lax.dot_general(a, b, (((1,), (1,)), ((), ())), preferred_element_type=F32)


def _const_spec(shape):
    nd = len(shape)
    return pl.BlockSpec(shape, lambda *_: (0,) * nd)


def _prep_kernel(x_ref, pos_ref, invf_ref, g_ref, wa_ref, qn_ref, wqmt_ref, wqst_ref, kvn_ref, wk_ref, wvt_ref,
                 h_ref, qt_ref, k_ref, vt_ref):
    tm = x_ref.shape[0]
    h = _rms(x_ref[...], g_ref[...]).astype(BF16)
    h_ref[...] = h
    p = _dot(h, wa_ref[...])
    cq = _rms(p[:, :OFF_Q], qn_ref[...]).astype(BF16)
    ckv = _rms(p[:, OFF_Q:OFF_KV], kvn_ref[...]).astype(BF16)

    ang = invf_ref[...] * pos_ref[0].astype(F32)
    c16 = jnp.cos(ang)
    s16 = jnp.sin(ang)
    ones = jnp.ones((MLA_NOPE, tm), F32)
    zeros_lo = jnp.zeros((MLA_NOPE, tm), F32)
    zeros_hi = jnp.zeros((LANES - MLA_QK_DIM, tm), F32)
    cos_c = jnp.concatenate([ones, c16, c16, zeros_hi], axis=0)
    sin_c = jnp.concatenate([zeros_lo, s16, s16, zeros_hi], axis=0)

    qmt = _dot_nt(wqmt_ref[...], cq)
    qst = _dot_nt(wqst_ref[...], cq)
    qscale = MLA_QK_DIM ** -0.5 * LOG2E
    for hd in range(MLA_HEADS):
        sl = slice(hd * LANES, (hd + 1) * LANES)
        qt_ref[sl, :] = ((qmt[sl, :] * cos_c + qst[sl, :] * sin_c) * qscale).astype(BF16)

    cos_r = cos_c.T
    sin_r = sin_c.T
    kn = _dot(ckv, wk_ref[...])
    kpe = p[:, OFF_KV:OFF_KV + LANES] * cos_r + p[:, OFF_KV + LANES:OFF_KV + 2 * LANES] * sin_r
    for hd in range(MLA_HEADS):
        sl = slice(hd * LANES, (hd + 1) * LANES)
        k_ref[:, sl] = (kn[:, sl] + kpe).astype(BF16)
    vt = _dot_nt(wvt_ref[...], ckv).astype(BF16)
    ones_rows = jnp.ones((MLA_VROWS - MLA_V, tm), BF16)
    for hd in range(MLA_HEADS):
        vt_ref[0, hd * MLA_VROWS:hd * MLA_VROWS + MLA_V, :] = vt[hd * MLA_V:(hd + 1) * MLA_V, :]
        vt_ref[0, hd * MLA_VROWS + MLA_V:(hd + 1) * MLA_VROWS, :] = ones_rows


def _prep(x2, pos_rows, invf, g_pre, w_a, q_norm, w_qmt, w_qst, kv_norm, w_k, w_vt, batch, seq):
    t, d = x2.shape
    tm = PREP_TM
    tps = seq // tm
    hw = MLA_HEADS * LANES
    vw = MLA_HEADS * MLA_VROWS
    row = lambda i: (i, 0)
    return pl.pallas_call(
        _prep_kernel,
        grid=(t // tm,),
        in_specs=[
            pl.BlockSpec((tm, d), row),
            pl.BlockSpec((1, 1, tm), lambda i: (i, 0, 0)),
            _const_spec(invf.shape),
            _const_spec(g_pre.shape),
            _const_spec(w_a.shape),
            _const_spec(q_norm.shape),
            _const_spec(w_qmt.shape),
            _const_spec(w_qst.shape),
            _const_spec(kv_norm.shape),
            _const_spec(w_k.shape),
            _const_spec(w_vt.shape),
        ],
        out_specs=[pl.BlockSpec((tm, d), row),
                   pl.BlockSpec((None, hw, tm), lambda i: (i // tps, 0, i % tps)),
                   pl.BlockSpec((tm, hw), row),
                   pl.BlockSpec((None, 1, vw, tm), lambda i: (i // tps, i % tps, 0, 0))],
        out_shape=[jax.ShapeDtypeStruct((t, d), BF16),
                   jax.ShapeDtypeStruct((batch, hw, seq), BF16),
                   jax.ShapeDtypeStruct((t, hw), BF16),
                   jax.ShapeDtypeStruct((batch, tps, vw, tm), BF16)],
        compiler_params=pltpu.CompilerParams(dimension_semantics=("arbitrary",), vmem_limit_bytes=VMEM_LIMIT),
        name="prep",
    )(x2, pos_rows, invf, g_pre, w_a, q_norm, w_qmt, w_qst, kv_norm, w_k, w_vt)


def _dilproj_kernel(h_ref, w_ref, o_ref, acc_ref, *, dil):
    acc = _dot(h_ref[...], w_ref[...])
    if dil == 1:
        o_ref[0] = acc.astype(BF16)
    else:
        chunks, tm, _ = acc_ref.shape
        rows = tm // dil
        for c in range(chunks):
            acc_ref[c] = acc[:, c * LANES:(c + 1) * LANES]
        for r in range(dil):
            for c in range(chunks):
                o_ref[r, :, c * LANES:(c + 1) * LANES] = acc_ref[c, pl.ds(r, rows, stride=dil), :].astype(BF16)


def _dilproj(h2, w, batch, seq, dil):
    t, d = h2.shape
    n = w.shape[1]
    tm, tn = DILPROJ_TM, DILPROJ_TN
    tiles_per_seq = seq // tm
    return pl.pallas_call(
        functools.partial(_dilproj_kernel, dil=dil),
        grid=(t // tm, n // tn),
        in_specs=[pl.BlockSpec((tm, d), lambda i, j: (i, 0)), pl.BlockSpec((d, tn), lambda i, j: (0, j))],
        out_specs=pl.BlockSpec((None, dil, tm // dil, tn),
                               lambda i, j: (i // tiles_per_seq, 0, i % tiles_per_seq, j)),
        out_shape=jax.ShapeDtypeStruct((batch, dil, seq // dil, n), BF16),
        scratch_shapes=[pltpu.VMEM((tn // LANES, tm, LANES), F32)],
        compiler_params=pltpu.CompilerParams(dimension_semantics=("arbitrary", "arbitrary"),
                                             vmem_limit_bytes=VMEM_LIMIT),
        name=f"dilproj{dil}",
    )(h2, w)


def _mla_kernel(qt_ref, k_ref, vt_ref, o_ref, m_ref, acc_ref, sta_ref, stb_ref):
    tq = qt_ref.shape[1]
    qi = pl.program_id(2)
    m_ref[...] = jnp.full(m_ref.shape, NEG_INF, F32)
    acc_ref[...] = jnp.zeros(acc_ref.shape, F32)
    chains = [(a, hq) for a in range(MLA_HPS) for hq in range(tq // MLA_QSTRIP)]

    def scores(c, i):
        a, hq = chains[i]
        k = k_ref[pl.ds(pl.multiple_of(c * tq, tq), tq), a * LANES:(a + 1) * LANES]
        return _dot(k, qt_ref[a * LANES:(a + 1) * LANES, hq * MLA_QSTRIP:(hq + 1) * MLA_QSTRIP])

    def step(c, src_ref, dst_ref, masked):
        m_all = m_ref[...]
        acc_all = acc_ref[...]
        m_new, acc_new = {}, {}
        for i, (a, hq) in enumerate(chains):
            qs = slice(hq * MLA_QSTRIP, (hq + 1) * MLA_QSTRIP)
            if dst_ref is not None:
                dst_ref[i] = scores(c + 1, i)
            st = src_ref[i]
            if masked:
                key = lax.broadcasted_iota(jnp.int32, st.shape, 0)
                qry = lax.broadcasted_iota(jnp.int32, st.shape, 1) + hq * MLA_QSTRIP
                st = jnp.where(key <= qry, st, NEG_INF)
            m_prev = m_all[a, :, qs]
            m_cur = jnp.maximum(m_prev, jnp.max(st, axis=0, keepdims=True))
            alpha = jnp.exp2(m_prev - m_cur)
            p = jnp.exp2(st - m_cur).astype(BF16)
            vt = vt_ref[c, a * MLA_VROWS:(a + 1) * MLA_VROWS, :]
            acc_new[a, hq] = alpha * acc_all[a, :, qs] + _dot(vt, p)
            m_new[a, hq] = m_cur
        for a in range(MLA_HPS):
            strips = range(tq // MLA_QSTRIP)
            m_ref[a] = jnp.concatenate([m_new[a, hq] for hq in strips], axis=-1)
            acc_ref[a] = jnp.concatenate([acc_new[a, hq] for hq in strips], axis=-1)

    for i in range(len(chains)):
        sta_ref[i] = scores(0, i)

    def pair(j, carry):
        step(2 * j, sta_ref, stb_ref, False)
        step(2 * j + 1, stb_ref, sta_ref, False)
        return carry

    lax.fori_loop(0, qi // 2, pair, 0)

    @pl.when(qi % 2 == 0)
    def _():
        step(qi, sta_ref, None, True)

    @pl.when(qi % 2 == 1)
    def _():
        step(qi - 1, sta_ref, stb_ref, False)
        step(qi, stb_ref, None, True)

    out_t = jnp.concatenate([acc_ref[a, :MLA_V, :] / acc_ref[a, MLA_V:MLA_V + 1, :] for a in range(MLA_HPS)],
                            axis=0)
    o_ref[...] = out_t.T.astype(BF16)


def _mla(qt, k3, vt):
    b, s, _ = k3.shape
    tq = MLA_TQ
    hps = MLA_HPS
    return pl.pallas_call(
        _mla_kernel,
        grid=(b, MLA_HEADS // hps, s // tq),
        in_specs=[
            pl.BlockSpec((None, hps * LANES, tq), lambda bi, h, i: (bi, h, i)),
            pl.BlockSpec((None, s, hps * LANES), lambda bi, h, i: (bi, 0, h)),
            pl.BlockSpec((None, s // tq, hps * MLA_VROWS, tq), lambda bi, h, i: (bi, 0, h, 0)),
        ],
        out_specs=pl.BlockSpec((None, tq, hps * MLA_V), lambda bi, h, i: (bi, i, h)),
        out_shape=jax.ShapeDtypeStruct((b, s, MLA_HEADS * MLA_V), BF16),
        scratch_shapes=[pltpu.VMEM((hps, 1, tq), F32), pltpu.VMEM((hps, MLA_VROWS, tq), F32),
                        pltpu.VMEM((hps * (tq // MLA_QSTRIP), tq, MLA_QSTRIP), F32),
                        pltpu.VMEM((hps * (tq // MLA_QSTRIP), tq, MLA_QSTRIP), F32)],
        compiler_params=pltpu.CompilerParams(dimension_semantics=("arbitrary", "arbitrary", "arbitrary"),
                                             vmem_limit_bytes=VMEM_LIMIT),
        name="mla",
    )(qt, k3, vt)


def _dilattn_kernel(q_ref, kc_ref, vc_ref, kp_ref, vp_ref, o_ref, lse_ref, *, dil, group):
    n = pl.program_id(2)
    tb = q_ref.shape[0]
    row = lax.broadcasted_iota(jnp.int32, (BLOCK, BLOCK), 0)
    col = lax.broadcasted_iota(jnp.int32, (BLOCK, BLOCK), 1)
    rel = (row - col).astype(F32)
    cur_ok = col <= row
    prev_tri = col >= row
    lane = lax.broadcasted_iota(jnp.int32, (BLOCK, LANES), 1)
    scale = DIL_HEAD_DIM ** -0.5
    for t in range(tb // BLOCK):
        rs = slice(t * BLOCK, (t + 1) * BLOCK)
        if t == 0:
            prev_ok = jnp.logical_and(prev_tri, n > 0)
        else:
            prev_ok = prev_tri
        lse_tile = jnp.zeros((BLOCK, LANES), F32)
        for hh in range(DIL_HPG):
            cs = slice(hh * DIL_HEAD_DIM, (hh + 1) * DIL_HEAD_DIM)
            slope = float(2.0 ** (-8.0 * (hh * DIL_GROUPS + group + 1) / DIL_HEADS)) * dil
            q = q_ref[rs, cs]
            if t == 0:
                kp, vp = kp_ref[:, cs], vp_ref[:, cs]
            else:
                ps = slice((t - 1) * BLOCK, t * BLOCK)
                kp, vp = kc_ref[ps, cs], vc_ref[ps, cs]
            kc, vc = kc_ref[rs, cs], vc_ref[rs, cs]
            s_cur = jnp.where(cur_ok, _dot_nt(q, kc) * scale - slope * rel, NEG_INF)
            s_prev = jnp.where(prev_ok, _dot_nt(q, kp) * scale - slope * (rel + float(BLOCK)), NEG_INF)
            m = jnp.maximum(jnp.max(s_cur, axis=-1, keepdims=True), jnp.max(s_prev, axis=-1, keepdims=True))
            e_cur = jnp.exp(s_cur - m)
            e_prev = jnp.exp(s_prev - m)
            den = jnp.sum(e_cur, axis=-1, keepdims=True) + jnp.sum(e_prev, axis=-1, keepdims=True)
            o = (_dot(e_cur.astype(BF16), vc) + _dot(e_prev.astype(BF16), vp)) / den
            o_ref[rs, cs] = o.astype(BF16)
            lse_tile = jnp.where(lane == hh, m + jnp.log(den), lse_tile)
        lse_ref[rs, :] = lse_tile


def _dilattn(qkv, dil, group):
    b, d, l, _ = qkv.shape
    tb = DIL_TB
    bpt = tb // BLOCK
    w = DIL_WIDTH
    cur = lambda c: pl.BlockSpec((None, None, tb, w), lambda bi, r, n: (bi, r, n, c))
    prev = lambda c: pl.BlockSpec((None, None, BLOCK, w), lambda bi, r, n: (bi, r, jnp.maximum(n * bpt - 1, 0), c))
    return pl.pallas_call(
        functools.partial(_dilattn_kernel, dil=dil, group=group),
        grid=(b, d, l // tb),
        in_specs=[cur(0), cur(1), cur(2), prev(1), prev(2)],
        out_specs=[pl.BlockSpec((None, None, tb, w), lambda bi, r, n: (bi, r, n, 0)),
                   pl.BlockSpec((None, None, tb, LANES), lambda bi, r, n: (bi, r, n, 0))],
        out_shape=[jax.ShapeDtypeStruct((b, d, l, w), BF16), jax.ShapeDtypeStruct((b, d, l, LANES), F32)],
        compiler_params=pltpu.CompilerParams(dimension_semantics=("arbitrary", "arbitrary", "arbitrary"),
                                             vmem_limit_bytes=VMEM_LIMIT),
        name=f"dilattn{dil}",
    )(qkv, qkv, qkv, qkv, qkv)


def _memkv_kernel(mem_ref, g_ref, w_ref, o_ref):
    o_ref[...] = _dot(_rms(mem_ref[...], g_ref[...]).astype(BF16), w_ref[...]).astype(BF16)


def _memkv(mem2, g_mem, w):
    return pl.pallas_call(
        _memkv_kernel,
        out_shape=jax.ShapeDtypeStruct((mem2.shape[0], w.shape[1]), BF16),
        compiler_params=pltpu.CompilerParams(vmem_limit_bytes=VMEM_LIMIT),
        name="memkv",
    )(mem2, g_mem, w)


def _sigmoid(z):
    return 1.0 / (1.0 + jnp.exp(-z))


def _merge_kernel(x_ref, h_ref, wmq_ref, wg_ref, bg_ref, kvm_ref, ymla_ref, wbm_ref,
                  o0_ref, o1_ref, o2_ref, l0_ref, l1_ref, l2_ref, wbd_ref, wbmem_ref, wo_ref, gpm_ref, gpf_ref,
                  x1_ref, h2_ref, nat1_ref, nat2_ref, lse1_ref, lse2_ref):
    tm, dm = x_ref.shape
    h = h_ref[...]

    for src, lsrc, dst, ldst in ((o1_ref, l1_ref, nat1_ref, lse1_ref), (o2_ref, l2_ref, nat2_ref, lse2_ref)):
        d = src.shape[0]
        rows = src.shape[1]
        for r in range(d):
            for hh in range(DIL_HPG):
                cs = slice(hh * DIL_HEAD_DIM, (hh + 1) * DIL_HEAD_DIM)
                dst[hh, pl.ds(r, rows, stride=d), :] = src[r, :, cs].astype(F32)
            ldst[pl.ds(r, rows, stride=d), :] = lsrc[r]
    lg = (l0_ref[0], lse1_ref[...], lse2_ref[...])
    heads = []
    for hh in range(DIL_HPG):
        cs = slice(hh * DIL_HEAD_DIM, (hh + 1) * DIL_HEAD_DIM)
        og = (o0_ref[0, :, cs].astype(F32), nat1_ref[hh], nat2_ref[hh])
        ls = [l[:, hh:hh + 1] for l in lg]
        mx = jnp.maximum(jnp.maximum(ls[0], ls[1]), ls[2])
        ws = [jnp.exp(l - mx) for l in ls]
        num = ws[0] * og[0] + ws[1] * og[1] + ws[2] * og[2]
        heads.append((num / (ws[0] + ws[1] + ws[2])).astype(BF16))
    y_dil = jnp.concatenate(heads, axis=-1)

    memq = _dot(h, wmq_ref[...])
    mheads = []
    for hh in range(MEM_HEADS):
        cs = slice(hh * MEM_HEAD_DIM, (hh + 1) * MEM_HEAD_DIM)
        q = (memq[:, cs] * MEM_HEAD_DIM ** -0.5).astype(BF16)
        s = _dot_nt(q, kvm_ref[:, cs])
        e = jnp.exp(s - jnp.max(s, axis=-1, keepdims=True))
        o = _dot(e.astype(BF16), kvm_ref[:, MEM_WIDTH + hh * MEM_HEAD_DIM:MEM_WIDTH + (hh + 1) * MEM_HEAD_DIM])
        mheads.append((o / jnp.sum(e, axis=-1, keepdims=True)).astype(BF16))
    y_mem = jnp.concatenate(mheads, axis=-1)

    merged = jnp.zeros((tm, dm), F32)
    for br, (y, w_ref) in enumerate(((ymla_ref[...], wbm_ref), (y_dil, wbd_ref), (y_mem, wbmem_ref))):
        cs = slice(br * dm, (br + 1) * dm)
        gate = _sigmoid(_dot(h, wg_ref[:, cs]) + bg_ref[:, cs])
        merged = merged + gate * _dot(y, w_ref[...])
    mixed = _dot(merged.astype(BF16), wo_ref[...])
    x1 = x_ref[...] + _rms(mixed, gpm_ref[...])
    x1_ref[...] = x1
    h2_ref[...] = _rms(x1, gpf_ref[...]).astype(BF16)


def _merge(x2, h2, w_mq, w_g, b_g, kvm, y_mla, w_bm, o_dil, lse_dil, w_bd, w_bmem, w_o, g_pm, g_pf, batch, seq):
    t, dm = x2.shape
    tm = MERGE_TM
    tps = seq // tm
    n_mem = kvm.shape[0] // batch
    row = lambda i: (i, 0)
    single = pl.Buffered(1)
    const = lambda a: pl.BlockSpec(a.shape, lambda i: (0,) * a.ndim, pipeline_mode=single)

    def dil_spec(a):
        d, width = a.shape[1], a.shape[3]
        return pl.BlockSpec((None, d, tm // d, width), lambda i: (i // tps, 0, i % tps, 0))

    return pl.pallas_call(
        _merge_kernel,
        grid=(t // tm,),
        in_specs=[
            pl.BlockSpec((tm, dm), row), pl.BlockSpec((tm, dm), row), const(w_mq), const(w_g), const(b_g),
            pl.BlockSpec((n_mem, kvm.shape[1]), lambda i: (i // tps, 0)),
            pl.BlockSpec((tm, y_mla.shape[1]), row), const(w_bm),
            dil_spec(o_dil[0]), dil_spec(o_dil[1]), dil_spec(o_dil[2]),
            dil_spec(lse_dil[0]), dil_spec(lse_dil[1]), dil_spec(lse_dil[2]),
            const(w_bd), const(w_bmem), const(w_o), const(g_pm), const(g_pf),
        ],
        out_specs=[pl.BlockSpec((tm, dm), row), pl.BlockSpec((tm, dm), row)],
        out_shape=[jax.ShapeDtypeStruct((t, dm), F32), jax.ShapeDtypeStruct((t, dm), BF16)],
        scratch_shapes=[pltpu.VMEM((DIL_HPG, tm, DIL_HEAD_DIM), F32), pltpu.VMEM((DIL_HPG, tm, DIL_HEAD_DIM), F32),
                        pltpu.VMEM((tm, LANES), F32), pltpu.VMEM((tm, LANES), F32)],
        compiler_params=pltpu.CompilerParams(dimension_semantics=("arbitrary",), vmem_limit_bytes=VMEM_LIMIT),
        name="merge",
    )(x2, h2, w_mq, w_g, b_g, kvm, y_mla, w_bm, o_dil[0], o_dil[1], o_dil[2], lse_dil[0], lse_dil[1], lse_dil[2],
      w_bd, w_bmem, w_o, g_pm, g_pf)


def _ffn_kernel(x1_ref, h2_ref, halo_ref, wg_ref, wv_ref, cwg_ref, cwv_ref, cbg_ref, cbv_ref, wd_ref, gpost_ref,
                out_ref, hcat_ref, ug_ref, uv_ref, acc_ref, *, tiles_per_seq):
    i = pl.program_id(0)
    f = pl.program_id(1)
    tm = x1_ref.shape[0]
    halo = FFN_HALO

    @pl.when(f == 0)
    def _():
        first = (i % tiles_per_seq) == 0
        hcat_ref[:halo, :] = jnp.where(first, jnp.zeros_like(halo_ref[...]), halo_ref[...])
        hcat_ref[halo:, :] = h2_ref[...]
        acc_ref[...] = jnp.zeros(acc_ref.shape, F32)

    hc = hcat_ref[...]
    ug_ref[...] = _dot(hc, wg_ref[...])
    uv_ref[...] = _dot(hc, wv_ref[...])

    def conv(u_ref, cw_ref, cb_ref):
        z = cb_ref[...] + cw_ref[0:1, :] * u_ref[halo - 2:halo - 2 + tm, :]
        z = z + cw_ref[1:2, :] * u_ref[halo - 1:halo - 1 + tm, :]
        return z + cw_ref[2:3, :] * u_ref[halo:halo + tm, :]

    gate = conv(ug_ref, cwg_ref, cbg_ref)
    val = conv(uv_ref, cwv_ref, cbv_ref)
    a = (gate * _sigmoid(gate) * val).astype(BF16)
    acc_ref[...] += _dot(a, wd_ref[...])

    @pl.when(f == pl.num_programs(1) - 1)
    def _():
        out_ref[...] = x1_ref[...] + _rms(acc_ref[...], gpost_ref[...])


def _ffn(x1, h2, w_up, conv_w, conv_b, w_down, g_post, seq):
    t, dm = x1.shape
    dff = w_down.shape[0]
    tm, tf, halo = FFN_TM, FFN_TF, FFN_HALO
    nf = dff // tf
    tps = seq // tm
    row = lambda i, f: (i, 0)
    return pl.pallas_call(
        functools.partial(_ffn_kernel, tiles_per_seq=tps),
        grid=(t // tm, nf),
        in_specs=[
            pl.BlockSpec((tm, dm), row),
            pl.BlockSpec((tm, dm), row),
            pl.BlockSpec((halo, dm), lambda i, f: (jnp.maximum(i * (tm // halo) - 1, 0), 0)),
            pl.BlockSpec((dm, tf), lambda i, f: (0, f)),
            pl.BlockSpec((dm, tf), lambda i, f: (0, nf + f)),
            pl.BlockSpec((CONV_WIDTH, tf), lambda i, f: (0, f)),
            pl.BlockSpec((CONV_WIDTH, tf), lambda i, f: (0, nf + f)),
            pl.BlockSpec((1, tf), lambda i, f: (0, f)),
            pl.BlockSpec((1, tf), lambda i, f: (0, nf + f)),
            pl.BlockSpec((tf, dm), lambda i, f: (f, 0)),
            pl.BlockSpec((1, dm), lambda i, f: (0, 0)),
        ],
        out_specs=pl.BlockSpec((tm, dm), row),
        out_shape=jax.ShapeDtypeStruct((t, dm), F32),
        scratch_shapes=[pltpu.VMEM((tm + halo, dm), BF16), pltpu.VMEM((tm + halo, tf), F32),
                        pltpu.VMEM((tm + halo, tf), F32), pltpu.VMEM((tm, dm), F32)],
        compiler_params=pltpu.CompilerParams(dimension_semantics=("arbitrary", "arbitrary"),
                                             vmem_limit_bytes=VMEM_LIMIT),
        name="ffn",
    )(x1, h2, h2, w_up, w_up, conv_w, conv_w, conv_b, conv_b, w_down, g_post)


def _rot_half_cols(w):
    return jnp.concatenate([-w[..., ROPE_HALF:], w[..., :ROPE_HALF]], axis=-1)


def _prep_weights(w_in, w_uq, w_ukv):
    dm = w_in.shape[0]
    pad_hi = LANES - MLA_QK_DIM
    kr = w_in[:, OFF_KV:OFF_KR]
    z_lo = jnp.zeros((dm, MLA_NOPE), F32)
    z_hi = jnp.zeros((dm, pad_hi), F32)
    w_a = jnp.concatenate([w_in[:, :OFF_KV], z_lo, kr, z_hi, z_lo, _rot_half_cols(kr), z_hi], axis=1)

    uq = w_uq.reshape(MLA_Q_RANK, MLA_HEADS, MLA_QK_DIM)
    zq_lo = jnp.zeros((MLA_Q_RANK, MLA_HEADS, MLA_NOPE), F32)
    zq_hi = jnp.zeros((MLA_Q_RANK, MLA_HEADS, pad_hi), F32)
    w_qm = jnp.concatenate([uq, zq_hi], axis=-1).reshape(MLA_Q_RANK, MLA_HEADS * LANES)
    w_qs = jnp.concatenate([zq_lo, _rot_half_cols(uq[..., MLA_NOPE:]), zq_hi], axis=-1)
    w_qs = w_qs.reshape(MLA_Q_RANK, MLA_HEADS * LANES)

    ukv = w_ukv.reshape(MLA_KV_RANK, MLA_HEADS, MLA_NOPE + MLA_V)
    zk = jnp.zeros((MLA_KV_RANK, MLA_HEADS, LANES - MLA_NOPE), F32)
    w_k = jnp.concatenate([ukv[..., :MLA_NOPE], zk], axis=-1).reshape(MLA_KV_RANK, MLA_HEADS * LANES)
    w_v = ukv[..., MLA_NOPE:].reshape(MLA_KV_RANK, MLA_HEADS * MLA_V)
    return tuple(a.astype(BF16) for a in (w_a, w_qm.T, w_qs.T, w_k, w_v.T))


def _layer(x, mem, positions, g_pre_mix, w_in, b_gate, mla_q_norm, w_uq, mla_kv_norm, w_ukv, g_mem, w_mem_kv,
           w_br_mla, w_br_dil, w_br_mem, w_o, g_post_mix, g_pre_ffn, w_ffn_up, conv_w, conv_b, w_ffn_down,
           g_post_ffn):
    batch, seq, dm = x.shape
    t = batch * seq
    x2 = x.reshape(t, dm)
    r2 = lambda v: v.reshape(1, -1)

    w_a, w_qmt, w_qst, w_k, w_vt = _prep_weights(w_in, w_uq, w_ukv)
    invf = (ROPE_THETA ** (-jnp.arange(ROPE_HALF, dtype=F32) / ROPE_HALF)).reshape(ROPE_HALF, 1)
    pos_rows = positions.reshape(t // PREP_TM, 1, PREP_TM)

    h2d, qt, k, vt = _prep(x2, pos_rows, invf, r2(g_pre_mix), w_a, r2(mla_q_norm), w_qmt, w_qst,
                           r2(mla_kv_norm), w_k, w_vt, batch, seq)
    y_mla = _mla(qt, k.reshape(batch, seq, MLA_HEADS * LANES), vt).reshape(t, MLA_HEADS * MLA_V)

    w_dil = w_in[:, OFF_KR:OFF_DIL].reshape(dm, 3, DIL_GROUPS, DIL_WIDTH)
    o_dil, lse_dil = [], []
    for g, (_, dil) in enumerate(DIL_PAIRS):
        w_g = w_dil[:, :, g, :].reshape(dm, 3 * DIL_WIDTH).astype(BF16)
        qkv = _dilproj(h2d, w_g, batch, seq, dil)
        o, lse = _dilattn(qkv, dil, g)
        o_dil.append(o)
        lse_dil.append(lse)

    kvm = _memkv(mem.reshape(-1, dm), r2(g_mem), w_mem_kv.astype(BF16))

    x1, h2 = _merge(x2, h2d, w_in[:, OFF_DIL:OFF_MEMQ].astype(BF16), w_in[:, OFF_MEMQ:].astype(BF16),
                    r2(b_gate), kvm, y_mla, w_br_mla.astype(BF16), o_dil, lse_dil, w_br_dil.astype(BF16),
                    w_br_mem.astype(BF16), w_o.astype(BF16), r2(g_post_mix), r2(g_pre_ffn), batch, seq)

    out = _ffn(x1, h2, w_ffn_up.astype(BF16), conv_w, r2(conv_b), w_ffn_down.astype(BF16), r2(g_post_ffn), seq)
    return out.reshape(batch, seq, dm)


def kernel(x, mem, positions, g_pre_mix, w_in, b_gate, mla_q_norm, w_uq, mla_kv_norm, w_ukv, g_mem, w_mem_kv,
           w_br_mla, w_br_dil, w_br_mem, w_o, g_post_mix, g_pre_ffn, w_ffn_up, conv_w, conv_b, w_ffn_down,
           g_post_ffn):
    for l in range(w_in.shape[0]):
        x = _layer(x, mem, positions, g_pre_mix[l], w_in[l], b_gate[l], mla_q_norm[l], w_uq[l], mla_kv_norm[l],
                   w_ukv[l], g_mem[l], w_mem_kv[l], w_br_mla[l], w_br_dil[l], w_br_mem[l], w_o[l], g_post_mix[l],
                   g_pre_ffn[l], w_ffn_up[l], conv_w[l], conv_b[l], w_ffn_down[l], g_post_ffn[l])
    return x
```

```python
import functools

import jax
import jax.numpy as jnp
from jax import lax
from jax.experimental import pallas as pl
from jax.experimental.pallas import tpu as pltpu

F32 = jnp.float32
BF16 = jnp.bfloat16

RMS_EPS = 1e-6
LOG2E = 1.4426950408889634
NEG_INF = -1e30
LANES = 128

BLOCK = 128
MLA_HEADS = 8
MLA_NOPE = 64
MLA_ROPE = 32
MLA_V = 64
MLA_QK_DIM = MLA_NOPE + MLA_ROPE
MLA_Q_RANK = 384
MLA_KV_RANK = 256
ROPE_THETA = 10000.0
ROPE_HALF = MLA_ROPE // 2

DIL_PAIRS = ((128, 1), (512, 4), (2048, 16))
DIL_GROUPS = 3
DIL_HPG = 4
DIL_HEADS = DIL_GROUPS * DIL_HPG
DIL_HEAD_DIM = 128
DIL_WIDTH = DIL_HPG * DIL_HEAD_DIM

MEM_HEADS = 4
MEM_HEAD_DIM = 128
MEM_WIDTH = MEM_HEADS * MEM_HEAD_DIM

N_BRANCH = 3
CONV_WIDTH = 3

OFF_Q = MLA_Q_RANK
OFF_KV = OFF_Q + MLA_KV_RANK
OFF_KR = OFF_KV + MLA_ROPE
OFF_DIL = OFF_KR + 3 * DIL_HEADS * DIL_HEAD_DIM
OFF_MEMQ = OFF_DIL + MEM_WIDTH

PREP_TM = 512
DILPROJ_TM = 2048
DILPROJ_TN = 512
MLA_TQ = 512
MLA_HPS = 2
MLA_QSTRIP = 256
MLA_VROWS = MLA_V + 16
assert PREP_TM == MLA_TQ
DIL_TB = 512
DIL_AHEAD = 4
MERGE_TM = 512
FFN_TM = 1024
FFN_TF = 256
FFN_HALO = 16

VMEM_LIMIT = 56 * 1024 * 1024


def _rms(xf, g):
    return xf * lax.rsqrt(jnp.mean(xf * xf, axis=-1, keepdims=True) + RMS_EPS) * g


def _dot(a, b):
    return jnp.dot(a, b, preferred_element_type=F32)


def _dot_nt(a, b):
    return lax.dot_general(a, b, (((1,), (1,)), ((), ())), preferred_element_type=F32)


def _const_spec(shape):
    nd = len(shape)
    return pl.BlockSpec(shape, lambda *_: (0,) * nd)


def _prep_kernel(x_ref, pos_ref, invf_ref, g_ref, wa_ref, qn_ref, wqmt_ref, wqst_ref, kvn_ref, wk_ref, wvt_ref,
                 h_ref, qt_ref, k_ref, vt_ref):
    tm = x_ref.shape[0]
    h = _rms(x_ref[...], g_ref[...]).astype(BF16)
    h_ref[...] = h
    p = _dot(h, wa_ref[...])
    cq = _rms(p[:, :OFF_Q], qn_ref[...]).astype(BF16)
    ckv = _rms(p[:, OFF_Q:OFF_KV], kvn_ref[...]).astype(BF16)

    ang = invf_ref[...] * pos_ref[0].astype(F32)
    c16 = jnp.cos(ang)
    s16 = jnp.sin(ang)
    ones = jnp.ones((MLA_NOPE, tm), F32)
    zeros_lo = jnp.zeros((MLA_NOPE, tm), F32)
    zeros_hi = jnp.zeros((LANES - MLA_QK_DIM, tm), F32)
    cos_c = jnp.concatenate([ones, c16, c16, zeros_hi], axis=0)
    sin_c = jnp.concatenate([zeros_lo, s16, s16, zeros_hi], axis=0)

    qmt = _dot_nt(wqmt_ref[...], cq)
    qst = _dot_nt(wqst_ref[...], cq)
    qscale = MLA_QK_DIM ** -0.5 * LOG2E
    for hd in range(MLA_HEADS):
        sl = slice(hd * LANES, (hd + 1) * LANES)
        qt_ref[sl, :] = ((qmt[sl, :] * cos_c + qst[sl, :] * sin_c) * qscale).astype(BF16)

    cos_r = cos_c.T
    sin_r = sin_c.T
    kn = _dot(ckv, wk_ref[...])
    kpe = p[:, OFF_KV:OFF_KV + LANES] * cos_r + p[:, OFF_KV + LANES:OFF_KV + 2 * LANES] * sin_r
    for hd in range(MLA_HEADS):
        sl = slice(hd * LANES, (hd + 1) * LANES)
        k_ref[:, sl] = (kn[:, sl] + kpe).astype(BF16)
    vt = _dot_nt(wvt_ref[...], ckv).astype(BF16)
    ones_rows = jnp.ones((MLA_VROWS - MLA_V, tm), BF16)
    for hd in range(MLA_HEADS):
        vt_ref[0, hd * MLA_VROWS:hd * MLA_VROWS + MLA_V, :] = vt[hd * MLA_V:(hd + 1) * MLA_V, :]
        vt_ref[0, hd * MLA_VROWS + MLA_V:(hd + 1) * MLA_VROWS, :] = ones_rows


def _prep(x2, pos_rows, invf, g_pre, w_a, q_norm, w_qmt, w_qst, kv_norm, w_k, w_vt, batch, seq):
    t, d = x2.shape
    tm = PREP_TM
    tps = seq // tm
    hw = MLA_HEADS * LANES
    vw = MLA_HEADS * MLA_VROWS
    row = lambda i: (i, 0)
    return pl.pallas_call(
        _prep_kernel,
        grid=(t // tm,),
        in_specs=[
            pl.BlockSpec((tm, d), row),
            pl.BlockSpec((1, 1, tm), lambda i: (i, 0, 0)),
            _const_spec(invf.shape),
            _const_spec(g_pre.shape),
            _const_spec(w_a.shape),
            _const_spec(q_norm.shape),
            _const_spec(w_qmt.shape),
            _const_spec(w_qst.shape),
            _const_spec(kv_norm.shape),
            _const_spec(w_k.shape),
            _const_spec(w_vt.shape),
        ],
        out_specs=[pl.BlockSpec((tm, d), row),
                   pl.BlockSpec((None, hw, tm), lambda i: (i // tps, 0, i % tps)),
                   pl.BlockSpec((tm, hw), row),
                   pl.BlockSpec((None, 1, vw, tm), lambda i: (i // tps, i % tps, 0, 0))],
        out_shape=[jax.ShapeDtypeStruct((t, d), BF16),
                   jax.ShapeDtypeStruct((batch, hw, seq), BF16),
                   jax.ShapeDtypeStruct((t, hw), BF16),
                   jax.ShapeDtypeStruct((batch, tps, vw, tm), BF16)],
        compiler_params=pltpu.CompilerParams(dimension_semantics=("arbitrary",), vmem_limit_bytes=VMEM_LIMIT),
        name="prep",
    )(x2, pos_rows, invf, g_pre, w_a, q_norm, w_qmt, w_qst, kv_norm, w_k, w_vt)


def _dilproj_kernel(h_ref, w_ref, o_ref, acc_ref, *, dil):
    acc = _dot(h_ref[...], w_ref[...])
    if dil == 1:
        o_ref[0] = acc.astype(BF16)
    else:
        chunks, tm, _ = acc_ref.shape
        rows = tm // dil
        for c in range(chunks):
            acc_ref[c] = acc[:, c * LANES:(c + 1) * LANES]
        for r in range(dil):
            for c in range(chunks):
                o_ref[r, :, c * LANES:(c + 1) * LANES] = acc_ref[c, pl.ds(r, rows, stride=dil), :].astype(BF16)


def _dilproj(h2, w, batch, seq, dil):
    t, d = h2.shape
    n = w.shape[1]
    tm, tn = DILPROJ_TM, DILPROJ_TN
    tiles_per_seq = seq // tm
    return pl.pallas_call(
        functools.partial(_dilproj_kernel, dil=dil),
        grid=(t // tm, n // tn),
        in_specs=[pl.BlockSpec((tm, d), lambda i, j: (i, 0)), pl.BlockSpec((d, tn), lambda i, j: (0, j))],
        out_specs=pl.BlockSpec((None, dil, tm // dil, tn),
                               lambda i, j: (i // tiles_per_seq, 0, i % tiles_per_seq, j)),
        out_shape=jax.ShapeDtypeStruct((batch, dil, seq // dil, n), BF16),
        scratch_shapes=[pltpu.VMEM((tn // LANES, tm, LANES), F32)],
        compiler_params=pltpu.CompilerParams(dimension_semantics=("arbitrary", "arbitrary"),
                                             vmem_limit_bytes=VMEM_LIMIT),
        name=f"dilproj{dil}",
    )(h2, w)


def _mla_kernel(qt_ref, k_ref, vt_ref, o_ref, m_ref, acc_ref, sta_ref, stb_ref):
    tq = qt_ref.shape[1]
    qi = pl.program_id(2)
    m_ref[...] = jnp.full(m_ref.shape, NEG_INF, F32)
    acc_ref[...] = jnp.zeros(acc_ref.shape, F32)
    chains = [(a, hq) for a in range(MLA_HPS) for hq in range(tq // MLA_QSTRIP)]

    def scores(c, i):
        a, hq = chains[i]
        k = k_ref[pl.ds(pl.multiple_of(c * tq, tq), tq), a * LANES:(a + 1) * LANES]
        return _dot(k, qt_ref[a * LANES:(a + 1) * LANES, hq * MLA_QSTRIP:(hq + 1) * MLA_QSTRIP])

    def step(c, src_ref, dst_ref, masked):
        m_all = m_ref[...]
        acc_all = acc_ref[...]
        m_new, acc_new = {}, {}
        for i, (a, hq) in enumerate(chains):
            qs = slice(hq * MLA_QSTRIP, (hq + 1) * MLA_QSTRIP)
            if dst_ref is not None:
                dst_ref[i] = scores(c + 1, i)
            st = src_ref[i]
            if masked:
                key = lax.broadcasted_iota(jnp.int32, st.shape, 0)
                qry = lax.broadcasted_iota(jnp.int32, st.shape, 1) + hq * MLA_QSTRIP
                st = jnp.where(key <= qry, st, NEG_INF)
            m_prev = m_all[a, :, qs]
            m_cur = jnp.maximum(m_prev, jnp.max(st, axis=0, keepdims=True))
            alpha = jnp.exp2(m_prev - m_cur)
            p = jnp.exp2(st - m_cur).astype(BF16)
            vt = vt_ref[c, a * MLA_VROWS:(a + 1) * MLA_VROWS, :]
            acc_new[a, hq] = alpha * acc_all[a, :, qs] + _dot(vt, p)
            m_new[a, hq] = m_cur
        for a in range(MLA_HPS):
            strips = range(tq // MLA_QSTRIP)
            m_ref[a] = jnp.concatenate([m_new[a, hq] for hq in strips], axis=-1)
            acc_ref[a] = jnp.concatenate([acc_new[a, hq] for hq in strips], axis=-1)

    for i in range(len(chains)):
        sta_ref[i] = scores(0, i)

    def pair(j, carry):
        step(2 * j, sta_ref, stb_ref, False)
        step(2 * j + 1, stb_ref, sta_ref, False)
        return carry

    lax.fori_loop(0, qi // 2, pair, 0)

    @pl.when(qi % 2 == 0)
    def _():
        step(qi, sta_ref, None, True)

    @pl.when(qi % 2 == 1)
    def _():
        step(qi - 1, sta_ref, stb_ref, False)
        step(qi, stb_ref, None, True)

    out_t = jnp.concatenate([acc_ref[a, :MLA_V, :] / acc_ref[a, MLA_V:MLA_V + 1, :] for a in range(MLA_HPS)],
                            axis=0)
    o_ref[...] = out_t.T.astype(BF16)


def _mla(qt, k3, vt):
    b, s, _ = k3.shape
    tq = MLA_TQ
    hps = MLA_HPS
    return pl.pallas_call(
        _mla_kernel,
        grid=(b, MLA_HEADS // hps, s // tq),
        in_specs=[
            pl.BlockSpec((None, hps * LANES, tq), lambda bi, h, i: (bi, h, i)),
            pl.BlockSpec((None, s, hps * LANES), lambda bi, h, i: (bi, 0, h)),
            pl.BlockSpec((None, s // tq, hps * MLA_VROWS, tq), lambda bi, h, i: (bi, 0, h, 0)),
        ],
        out_specs=pl.BlockSpec((None, tq, hps * MLA_V), lambda bi, h, i: (bi, i, h)),
        out_shape=jax.ShapeDtypeStruct((b, s, MLA_HEADS * MLA_V), BF16),
        scratch_shapes=[pltpu.VMEM((hps, 1, tq), F32), pltpu.VMEM((hps, MLA_VROWS, tq), F32),
                        pltpu.VMEM((hps * (tq // MLA_QSTRIP), tq, MLA_QSTRIP), F32),
                        pltpu.VMEM((hps * (tq // MLA_QSTRIP), tq, MLA_QSTRIP), F32)],
        compiler_params=pltpu.CompilerParams(dimension_semantics=("arbitrary", "arbitrary", "arbitrary"),
                                             vmem_limit_bytes=VMEM_LIMIT),
        name="mla",
    )(qt, k3, vt)


def _dilattn_kernel(q_ref, kc_ref, vc_ref, kp_ref, vp_ref, o_ref, lse_ref, kx_ref, vx_ref, *, dil, group):
    n = pl.program_id(2)
    tb = q_ref.shape[0]
    nblk = tb // BLOCK
    kx_ref[:BLOCK, :] = kp_ref[...]
    kx_ref[BLOCK:, :] = kc_ref[...]
    vx_ref[:BLOCK, :] = vp_ref[...]
    vx_ref[BLOCK:, :] = vc_ref[...]

    qry = lax.broadcasted_iota(jnp.int32, (BLOCK, 2 * BLOCK), 0)
    key = lax.broadcasted_iota(jnp.int32, (BLOCK, 2 * BLOCK), 1)
    dist = qry + BLOCK - key
    in_window = jnp.logical_and(dist >= 0, dist <= BLOCK)
    first_ok = jnp.logical_and(in_window, jnp.logical_or(key >= BLOCK, n > 0))
    distf = (dist * dil).astype(F32)
    lane = lax.broadcasted_iota(jnp.int32, (BLOCK, LANES), 1)
    scale = DIL_HEAD_DIM ** -0.5
    bias, bias_first = [], []
    for hh in range(DIL_HPG):
        slope = float(2.0 ** (-8.0 * (hh * DIL_GROUPS + group + 1) / DIL_HEADS))
        bias.append(jnp.where(in_window, -slope * distf, NEG_INF))
        bias_first.append(jnp.where(first_ok, -slope * distf, NEG_INF))

    units = [(t, hh) for t in range(nblk) for hh in range(DIL_HPG)]

    def scores(u):
        t, hh = units[u]
        cs = slice(hh * DIL_HEAD_DIM, (hh + 1) * DIL_HEAD_DIM)
        return _dot_nt(q_ref[t * BLOCK:(t + 1) * BLOCK, cs], kx_ref[t * BLOCK:(t + 2) * BLOCK, cs])

    pending = [scores(u) for u in range(min(DIL_AHEAD, len(units)))]
    lse_tile = None
    for u, (t, hh) in enumerate(units):
        if u + DIL_AHEAD < len(units):
            pending.append(scores(u + DIL_AHEAD))
        rs = slice(t * BLOCK, (t + 1) * BLOCK)
        cs = slice(hh * DIL_HEAD_DIM, (hh + 1) * DIL_HEAD_DIM)
        s = pending[u] * scale + (bias_first[hh] if t == 0 else bias[hh])
        m = jnp.max(s, axis=-1, keepdims=True)
        e = jnp.exp(s - m)
        den = jnp.sum(e, axis=-1, keepdims=True)
        o = _dot(e.astype(BF16), vx_ref[t * BLOCK:(t + 2) * BLOCK, cs]) / den
        o_ref[rs, cs] = o.astype(BF16)
        lse = m + jnp.log(den)
        lse_tile = jnp.where(lane == hh, lse, jnp.zeros((BLOCK, LANES), F32) if hh == 0 else lse_tile)
        if hh == DIL_HPG - 1:
            lse_ref[rs, :] = lse_tile


def _dilattn(qkv, dil, group):
    b, d, l, _ = qkv.shape
    tb = DIL_TB
    bpt = tb // BLOCK
    w = DIL_WIDTH
    cur = lambda c: pl.BlockSpec((None, None, tb, w), lambda bi, r, n: (bi, r, n, c))
    prev = lambda c: pl.BlockSpec((None, None, BLOCK, w), lambda bi, r, n: (bi, r, jnp.maximum(n * bpt - 1, 0), c))
    return pl.pallas_call(
        functools.partial(_dilattn_kernel, dil=dil, group=group),
        grid=(b, d, l // tb),
        in_specs=[cur(0), cur(1), cur(2), prev(1), prev(2)],
        out_specs=[pl.BlockSpec((None, None, tb, w), lambda bi, r, n: (bi, r, n, 0)),
                   pl.BlockSpec((None, None, tb, LANES), lambda bi, r, n: (bi, r, n, 0))],
        out_shape=[jax.ShapeDtypeStruct((b, d, l, w), BF16), jax.ShapeDtypeStruct((b, d, l, LANES), F32)],
        scratch_shapes=[pltpu.VMEM((tb + BLOCK, w), BF16), pltpu.VMEM((tb + BLOCK, w), BF16)],
        compiler_params=pltpu.CompilerParams(dimension_semantics=("arbitrary", "arbitrary", "arbitrary"),
                                             vmem_limit_bytes=VMEM_LIMIT),
        name=f"dilattn{dil}",
    )(qkv, qkv, qkv, qkv, qkv)


def _memkv_kernel(mem_ref, g_ref, w_ref, o_ref):
    o_ref[...] = _dot(_rms(mem_ref[...], g_ref[...]).astype(BF16), w_ref[...]).astype(BF16)


def _memkv(mem2, g_mem, w):
    return pl.pallas_call(
        _memkv_kernel,
        out_shape=jax.ShapeDtypeStruct((mem2.shape[0], w.shape[1]), BF16),
        compiler_params=pltpu.CompilerParams(vmem_limit_bytes=VMEM_LIMIT),
        name="memkv",
    )(mem2, g_mem, w)


def _sigmoid(z):
    return 1.0 / (1.0 + jnp.exp(-z))


def _merge_kernel(x_ref, h_ref, wmq_ref, wg_ref, bg_ref, kvm_ref, ymla_ref, wbm_ref,
                  o0_ref, o1_ref, o2_ref, l0_ref, l1_ref, l2_ref, wbd_ref, wbmem_ref, wo_ref, gpm_ref, gpf_ref,
                  x1_ref, h2_ref, nat1_ref, nat2_ref, lse1_ref, lse2_ref):
    tm, dm = x_ref.shape
    h = h_ref[...]

    for src, lsrc, dst, ldst in ((o1_ref, l1_ref, nat1_ref, lse1_ref), (o2_ref, l2_ref, nat2_ref, lse2_ref)):
        d = src.shape[0]
        rows = src.shape[1]
        for r in range(d):
            for hh in range(DIL_HPG):
                cs = slice(hh * DIL_HEAD_DIM, (hh + 1) * DIL_HEAD_DIM)
                dst[hh, pl.ds(r, rows, stride=d), :] = src[r, :, cs].astype(F32)
            ldst[pl.ds(r, rows, stride=d), :] = lsrc[r]
    lg = (l0_ref[0], lse1_ref[...], lse2_ref[...])
    heads = []
    for hh in range(DIL_HPG):
        cs = slice(hh * DIL_HEAD_DIM, (hh + 1) * DIL_HEAD_DIM)
        og = (o0_ref[0, :, cs].astype(F32), nat1_ref[hh], nat2_ref[hh])
        ls = [l[:, hh:hh + 1] for l in lg]
        mx = jnp.maximum(jnp.maximum(ls[0], ls[1]), ls[2])
        ws = [jnp.exp(l - mx) for l in ls]
        num = ws[0] * og[0] + ws[1] * og[1] + ws[2] * og[2]
        heads.append((num / (ws[0] + ws[1] + ws[2])).astype(BF16))
    y_dil = jnp.concatenate(heads, axis=-1)

    memq = _dot(h, wmq_ref[...])
    mheads = []
    for hh in range(MEM_HEADS):
        cs = slice(hh * MEM_HEAD_DIM, (hh + 1) * MEM_HEAD_DIM)
        q = (memq[:, cs] * MEM_HEAD_DIM ** -0.5).astype(BF16)
        s = _dot_nt(q, kvm_ref[:, cs])
        e = jnp.exp(s - jnp.max(s, axis=-1, keepdims=True))
        o = _dot(e.astype(BF16), kvm_ref[:, MEM_WIDTH + hh * MEM_HEAD_DIM:MEM_WIDTH + (hh + 1) * MEM_HEAD_DIM])
        mheads.append((o / jnp.sum(e, axis=-1, keepdims=True)).astype(BF16))
    y_mem = jnp.concatenate(mheads, axis=-1)

    merged = jnp.zeros((tm, dm), F32)
    for br, (y, w_ref) in enumerate(((ymla_ref[...], wbm_ref), (y_dil, wbd_ref), (y_mem, wbmem_ref))):
        cs = slice(br * dm, (br + 1) * dm)
        gate = _sigmoid(_dot(h, wg_ref[:, cs]) + bg_ref[:, cs])
        merged = merged + gate * _dot(y, w_ref[...])
    mixed = _dot(merged.astype(BF16), wo_ref[...])
    x1 = x_ref[...] + _rms(mixed, gpm_ref[...])
    x1_ref[...] = x1
    h2_ref[...] = _rms(x1, gpf_ref[...]).astype(BF16)


def _merge(x2, h2, w_mq, w_g, b_g, kvm, y_mla, w_bm, o_dil, lse_dil, w_bd, w_bmem, w_o, g_pm, g_pf, batch, seq):
    t, dm = x2.shape
    tm = MERGE_TM
    tps = seq // tm
    n_mem = kvm.shape[0] // batch
    row = lambda i: (i, 0)
    single = pl.Buffered(1)
    const = lambda a: pl.BlockSpec(a.shape, lambda i: (0,) * a.ndim, pipeline_mode=single)

    def dil_spec(a):
        d, width = a.shape[1], a.shape[3]
        return pl.BlockSpec((None, d, tm // d, width), lambda i: (i // tps, 0, i % tps, 0))

    return pl.pallas_call(
        _merge_kernel,
        grid=(t // tm,),
        in_specs=[
            pl.BlockSpec((tm, dm), row), pl.BlockSpec((tm, dm), row), const(w_mq), const(w_g), const(b_g),
            pl.BlockSpec((n_mem, kvm.shape[1]), lambda i: (i // tps, 0)),
            pl.BlockSpec((tm, y_mla.shape[1]), row), const(w_bm),
            dil_spec(o_dil[0]), dil_spec(o_dil[1]), dil_spec(o_dil[2]),
            dil_spec(lse_dil[0]), dil_spec(lse_dil[1]), dil_spec(lse_dil[2]),
            const(w_bd), const(w_bmem), const(w_o), const(g_pm), const(g_pf),
        ],
        out_specs=[pl.BlockSpec((tm, dm), row), pl.BlockSpec((tm, dm), row)],
        out_shape=[jax.ShapeDtypeStruct((t, dm), F32), jax.ShapeDtypeStruct((t, dm), BF16)],
        scratch_shapes=[pltpu.VMEM((DIL_HPG, tm, DIL_HEAD_DIM), F32), pltpu.VMEM((DIL_HPG, tm, DIL_HEAD_DIM), F32),
                        pltpu.VMEM((tm, LANES), F32), pltpu.VMEM((tm, LANES), F32)],
        compiler_params=pltpu.CompilerParams(dimension_semantics=("arbitrary",), vmem_limit_bytes=VMEM_LIMIT),
        name="merge",
    )(x2, h2, w_mq, w_g, b_g, kvm, y_mla, w_bm, o_dil[0], o_dil[1], o_dil[2], lse_dil[0], lse_dil[1], lse_dil[2],
      w_bd, w_bmem, w_o, g_pm, g_pf)


def _ffn_kernel(x1_ref, h2_ref, halo_ref, wg_ref, wv_ref, cwg_ref, cwv_ref, cbg_ref, cbv_ref, wd_ref, gpost_ref,
                out_ref, hcat_ref, ug_ref, uv_ref, acc_ref, *, tiles_per_seq):
    i = pl.program_id(0)
    f = pl.program_id(1)
    tm = x1_ref.shape[0]
    halo = FFN_HALO

    @pl.when(f == 0)
    def _():
        first = (i % tiles_per_seq) == 0
        hcat_ref[:halo, :] = jnp.where(first, jnp.zeros_like(halo_ref[...]), halo_ref[...])
        hcat_ref[halo:, :] = h2_ref[...]
        acc_ref[...] = jnp.zeros(acc_ref.shape, F32)

    hc = hcat_ref[...]
    ug_ref[...] = _dot(hc, wg_ref[...])
    uv_ref[...] = _dot(hc, wv_ref[...])

    def conv(u_ref, cw_ref, cb_ref):
        z = cb_ref[...] + cw_ref[0:1, :] * u_ref[halo - 2:halo - 2 + tm, :]
        z = z + cw_ref[1:2, :] * u_ref[halo - 1:halo - 1 + tm, :]
        return z + cw_ref[2:3, :] * u_ref[halo:halo + tm, :]

    gate = conv(ug_ref, cwg_ref, cbg_ref)
    val = conv(uv_ref, cwv_ref, cbv_ref)
    a = (gate * _sigmoid(gate) * val).astype(BF16)
    acc_ref[...] += _dot(a, wd_ref[...])

    @pl.when(f == pl.num_programs(1) - 1)
    def _():
        out_ref[...] = x1_ref[...] + _rms(acc_ref[...], gpost_ref[...])


def _ffn(x1, h2, w_up, conv_w, conv_b, w_down, g_post, seq):
    t, dm = x1.shape
    dff = w_down.shape[0]
    tm, tf, halo = FFN_TM, FFN_TF, FFN_HALO
    nf = dff // tf
    tps = seq // tm
    row = lambda i, f: (i, 0)
    return pl.pallas_call(
        functools.partial(_ffn_kernel, tiles_per_seq=tps),
        grid=(t // tm, nf),
        in_specs=[
            pl.BlockSpec((tm, dm), row),
            pl.BlockSpec((tm, dm), row),
            pl.BlockSpec((halo, dm), lambda i, f: (jnp.maximum(i * (tm // halo) - 1, 0), 0)),
            pl.BlockSpec((dm, tf), lambda i, f: (0, f)),
            pl.BlockSpec((dm, tf), lambda i, f: (0, nf + f)),
            pl.BlockSpec((CONV_WIDTH, tf), lambda i, f: (0, f)),
            pl.BlockSpec((CONV_WIDTH, tf), lambda i, f: (0, nf + f)),
            pl.BlockSpec((1, tf), lambda i, f: (0, f)),
            pl.BlockSpec((1, tf), lambda i, f: (0, nf + f)),
            pl.BlockSpec((tf, dm), lambda i, f: (f, 0)),
            pl.BlockSpec((1, dm), lambda i, f: (0, 0)),
        ],
        out_specs=pl.BlockSpec((tm, dm), row),
        out_shape=jax.ShapeDtypeStruct((t, dm), F32),
        scratch_shapes=[pltpu.VMEM((tm + halo, dm), BF16), pltpu.VMEM((tm + halo, tf), F32),
                        pltpu.VMEM((tm + halo, tf), F32), pltpu.VMEM((tm, dm), F32)],
        compiler_params=pltpu.CompilerParams(dimension_semantics=("arbitrary", "arbitrary"),
                                             vmem_limit_bytes=VMEM_LIMIT),
        name="ffn",
    )(x1, h2, h2, w_up, w_up, conv_w, conv_w, conv_b, conv_b, w_down, g_post)


def _rot_half_cols(w):
    return jnp.concatenate([-w[..., ROPE_HALF:], w[..., :ROPE_HALF]], axis=-1)


def _prep_weights(w_in, w_uq, w_ukv):
    dm = w_in.shape[0]
    pad_hi = LANES - MLA_QK_DIM
    kr = w_in[:, OFF_KV:OFF_KR]
    z_lo = jnp.zeros((dm, MLA_NOPE), F32)
    z_hi = jnp.zeros((dm, pad_hi), F32)
    w_a = jnp.concatenate([w_in[:, :OFF_KV], z_lo, kr, z_hi, z_lo, _rot_half_cols(kr), z_hi], axis=1)

    uq = w_uq.reshape(MLA_Q_RANK, MLA_HEADS, MLA_QK_DIM)
    zq_lo = jnp.zeros((MLA_Q_RANK, MLA_HEADS, MLA_NOPE), F32)
    zq_hi = jnp.zeros((MLA_Q_RANK, MLA_HEADS, pad_hi), F32)
    w_qm = jnp.concatenate([uq, zq_hi], axis=-1).reshape(MLA_Q_RANK, MLA_HEADS * LANES)
    w_qs = jnp.concatenate([zq_lo, _rot_half_cols(uq[..., MLA_NOPE:]), zq_hi], axis=-1)
    w_qs = w_qs.reshape(MLA_Q_RANK, MLA_HEADS * LANES)

    ukv = w_ukv.reshape(MLA_KV_RANK, MLA_HEADS, MLA_NOPE + MLA_V)
    zk = jnp.zeros((MLA_KV_RANK, MLA_HEADS, LANES - MLA_NOPE), F32)
    w_k = jnp.concatenate([ukv[..., :MLA_NOPE], zk], axis=-1).reshape(MLA_KV_RANK, MLA_HEADS * LANES)
    w_v = ukv[..., MLA_NOPE:].reshape(MLA_KV_RANK, MLA_HEADS * MLA_V)
    return tuple(a.astype(BF16) for a in (w_a, w_qm.T, w_qs.T, w_k, w_v.T))


def _layer(x, mem, positions, g_pre_mix, w_in, b_gate, mla_q_norm, w_uq, mla_kv_norm, w_ukv, g_mem, w_mem_kv,
           w_br_mla, w_br_dil, w_br_mem, w_o, g_post_mix, g_pre_ffn, w_ffn_up, conv_w, conv_b, w_ffn_down,
           g_post_ffn):
    batch, seq, dm = x.shape
    t = batch * seq
    x2 = x.reshape(t, dm)
    r2 = lambda v: v.reshape(1, -1)

    w_a, w_qmt, w_qst, w_k, w_vt = _prep_weights(w_in, w_uq, w_ukv)
    invf = (ROPE_THETA ** (-jnp.arange(ROPE_HALF, dtype=F32) / ROPE_HALF)).reshape(ROPE_HALF, 1)
    pos_rows = positions.reshape(t // PREP_TM, 1, PREP_TM)

    h2d, qt, k, vt = _prep(x2, pos_rows, invf, r2(g_pre_mix), w_a, r2(mla_q_norm), w_qmt, w_qst,
                           r2(mla_kv_norm), w_k, w_vt, batch, seq)
    y_mla = _mla(qt, k.reshape(batch, seq, MLA_HEADS * LANES), vt).reshape(t, MLA_HEADS * MLA_V)

    w_dil = w_in[:, OFF_KR:OFF_DIL].reshape(dm, 3, DIL_GROUPS, DIL_WIDTH)
    o_dil, lse_dil = [], []
    for g, (_, dil) in enumerate(DIL_PAIRS):
        w_g = w_dil[:, :, g, :].reshape(dm, 3 * DIL_WIDTH).astype(BF16)
        qkv = _dilproj(h2d, w_g, batch, seq, dil)
        o, lse = _dilattn(qkv, dil, g)
        o_dil.append(o)
        lse_dil.append(lse)

    kvm = _memkv(mem.reshape(-1, dm), r2(g_mem), w_mem_kv.astype(BF16))

    x1, h2 = _merge(x2, h2d, w_in[:, OFF_DIL:OFF_MEMQ].astype(BF16), w_in[:, OFF_MEMQ:].astype(BF16),
                    r2(b_gate), kvm, y_mla, w_br_mla.astype(BF16), o_dil, lse_dil, w_br_dil.astype(BF16),
                    w_br_mem.astype(BF16), w_o.astype(BF16), r2(g_post_mix), r2(g_pre_ffn), batch, seq)

    out = _ffn(x1, h2, w_ffn_up.astype(BF16), conv_w, r2(conv_b), w_ffn_down.astype(BF16), r2(g_post_ffn), seq)
    return out.reshape(batch, seq, dm)


def kernel(x, mem, positions, g_pre_mix, w_in, b_gate, mla_q_norm, w_uq, mla_kv_norm, w_ukv, g_mem, w_mem_kv,
           w_br_mla, w_br_dil, w_br_mem, w_o, g_post_mix, g_pre_ffn, w_ffn_up, conv_w, conv_b, w_ffn_down,
           g_post_ffn):
    for l in range(w_in.shape[0]):
        x = _layer(x, mem, positions, g_pre_mix[l], w_in[l], b_gate[l], mla_q_norm[l], w_uq[l], mla_kv_norm[l],
                   w_ukv[l], g_mem[l], w_mem_kv[l], w_br_mla[l], w_br_dil[l], w_br_mem[l], w_o[l], g_post_mix[l],
                   g_pre_ffn[l], w_ffn_up[l], conv_w[l], conv_b[l], w_ffn_down[l], g_post_ffn[l])
    return x
```

```python
import functools

import jax
import jax.numpy as jnp
from jax import lax
from jax.experimental import pallas as pl
from jax.experimental.pallas import tpu as pltpu

F32 = jnp.float32
BF16 = jnp.bfloat16

RMS_EPS = 1e-6
LOG2E = 1.4426950408889634
NEG_INF = -1e30
LANES = 128

BLOCK = 128
MLA_HEADS = 8
MLA_NOPE = 64
MLA_ROPE = 32
MLA_V = 64
MLA_QK_DIM = MLA_NOPE + MLA_ROPE
MLA_Q_RANK = 384
MLA_KV_RANK = 256
ROPE_THETA = 10000.0
ROPE_HALF = MLA_ROPE // 2

DIL_PAIRS = ((128, 1), (512, 4), (2048, 16))
DIL_GROUPS = 3
DIL_HPG = 4
DIL_HEADS = DIL_GROUPS * DIL_HPG
DIL_HEAD_DIM = 128
DIL_WIDTH = DIL_HPG * DIL_HEAD_DIM

MEM_HEADS = 4
MEM_HEAD_DIM = 128
MEM_WIDTH = MEM_HEADS * MEM_HEAD_DIM

N_BRANCH = 3
CONV_WIDTH = 3

OFF_Q = MLA_Q_RANK
OFF_KV = OFF_Q + MLA_KV_RANK
OFF_KR = OFF_KV + MLA_ROPE
OFF_DIL = OFF_KR + 3 * DIL_HEADS * DIL_HEAD_DIM
OFF_MEMQ = OFF_DIL + MEM_WIDTH

PREP_TM = 512
DILPROJ_TM = 2048
DILPROJ_TN = 512
MLA_TQ = 512
MLA_HPS = 2
MLA_QSTRIP = 256
MLA_VROWS = MLA_V + 16
assert PREP_TM == MLA_TQ
DIL_TB = 512
DIL_AHEAD = 4
MERGE_TM = 512
FFN_TM = 512
FFN_TF = 256
FFN_HALO = 16

VMEM_LIMIT = 56 * 1024 * 1024


def _rms(xf, g):
    return xf * lax.rsqrt(jnp.mean(xf * xf, axis=-1, keepdims=True) + RMS_EPS) * g


def _dot(a, b):
    return jnp.dot(a, b, preferred_element_type=F32)


def _dot_nt(a, b):
    return lax.dot_general(a, b, (((1,), (1,)), ((), ())), preferred_element_type=F32)


def _const_spec(shape):
    nd = len(shape)
    return pl.BlockSpec(shape, lambda *_: (0,) * nd)


def _prep_kernel(x_ref, pos_ref, invf_ref, g_ref, wa_ref, qn_ref, wqmt_ref, wqst_ref, kvn_ref, wk_ref, wvt_ref,
                 h_ref, qt_ref, k_ref, vt_ref):
    tm = x_ref.shape[0]
    h = _rms(x_ref[...], g_ref[...]).astype(BF16)
    h_ref[...] = h
    p = _dot(h, wa_ref[...])
    cq = _rms(p[:, :OFF_Q], qn_ref[...]).astype(BF16)
    ckv = _rms(p[:, OFF_Q:OFF_KV], kvn_ref[...]).astype(BF16)

    ang = invf_ref[...] * pos_ref[0].astype(F32)
    c16 = jnp.cos(ang)
    s16 = jnp.sin(ang)
    ones = jnp.ones((MLA_NOPE, tm), F32)
    zeros_lo = jnp.zeros((MLA_NOPE, tm), F32)
    zeros_hi = jnp.zeros((LANES - MLA_QK_DIM, tm), F32)
    cos_c = jnp.concatenate([ones, c16, c16, zeros_hi], axis=0)
    sin_c = jnp.concatenate([zeros_lo, s16, s16, zeros_hi], axis=0)

    qmt = _dot_nt(wqmt_ref[...], cq)
    qst = _dot_nt(wqst_ref[...], cq)
    qscale = MLA_QK_DIM ** -0.5 * LOG2E
    for hd in range(MLA_HEADS):
        sl = slice(hd * LANES, (hd + 1) * LANES)
        qt_ref[sl, :] = ((qmt[sl, :] * cos_c + qst[sl, :] * sin_c) * qscale).astype(BF16)

    cos_r = cos_c.T
    sin_r = sin_c.T
    kn = _dot(ckv, wk_ref[...])
    kpe = p[:, OFF_KV:OFF_KV + LANES] * cos_r + p[:, OFF_KV + LANES:OFF_KV + 2 * LANES] * sin_r
    for hd in range(MLA_HEADS):
        sl = slice(hd * LANES, (hd + 1) * LANES)
        k_ref[:, sl] = (kn[:, sl] + kpe).astype(BF16)
    vt = _dot_nt(wvt_ref[...], ckv).astype(BF16)
    ones_rows = jnp.ones((MLA_VROWS - MLA_V, tm), BF16)
    for hd in range(MLA_HEADS):
        vt_ref[0, hd * MLA_VROWS:hd * MLA_VROWS + MLA_V, :] = vt[hd * MLA_V:(hd + 1) * MLA_V, :]
        vt_ref[0, hd * MLA_VROWS + MLA_V:(hd + 1) * MLA_VROWS, :] = ones_rows


def _prep(x2, pos_rows, invf, g_pre, w_a, q_norm, w_qmt, w_qst, kv_norm, w_k, w_vt, batch, seq):
    t, d = x2.shape
    tm = PREP_TM
    tps = seq // tm
    hw = MLA_HEADS * LANES
    vw = MLA_HEADS * MLA_VROWS
    row = lambda i: (i, 0)
    return pl.pallas_call(
        _prep_kernel,
        grid=(t // tm,),
        in_specs=[
            pl.BlockSpec((tm, d), row),
            pl.BlockSpec((1, 1, tm), lambda i: (i, 0, 0)),
            _const_spec(invf.shape),
            _const_spec(g_pre.shape),
            _const_spec(w_a.shape),
            _const_spec(q_norm.shape),
            _const_spec(w_qmt.shape),
            _const_spec(w_qst.shape),
            _const_spec(kv_norm.shape),
            _const_spec(w_k.shape),
            _const_spec(w_vt.shape),
        ],
        out_specs=[pl.BlockSpec((tm, d), row),
                   pl.BlockSpec((None, hw, tm), lambda i: (i // tps, 0, i % tps)),
                   pl.BlockSpec((tm, hw), row),
                   pl.BlockSpec((None, 1, vw, tm), lambda i: (i // tps, i % tps, 0, 0))],
        out_shape=[jax.ShapeDtypeStruct((t, d), BF16),
                   jax.ShapeDtypeStruct((batch, hw, seq), BF16),
                   jax.ShapeDtypeStruct((t, hw), BF16),
                   jax.ShapeDtypeStruct((batch, tps, vw, tm), BF16)],
        compiler_params=pltpu.CompilerParams(dimension_semantics=("arbitrary",), vmem_limit_bytes=VMEM_LIMIT),
        name="prep",
    )(x2, pos_rows, invf, g_pre, w_a, q_norm, w_qmt, w_qst, kv_norm, w_k, w_vt)


def _dilproj_kernel(h_ref, w_ref, o_ref, acc_ref, *, dil):
    acc = _dot(h_ref[...], w_ref[...])
    if dil == 1:
        o_ref[0] = acc.astype(BF16)
    else:
        chunks, tm, _ = acc_ref.shape
        rows = tm // dil
        for c in range(chunks):
            acc_ref[c] = acc[:, c * LANES:(c + 1) * LANES]
        for r in range(dil):
            for c in range(chunks):
                o_ref[r, :, c * LANES:(c + 1) * LANES] = acc_ref[c, pl.ds(r, rows, stride=dil), :].astype(BF16)


def _dilproj(h2, w, batch, seq, dil):
    t, d = h2.shape
    n = w.shape[1]
    tm, tn = DILPROJ_TM, DILPROJ_TN
    tiles_per_seq = seq // tm
    return pl.pallas_call(
        functools.partial(_dilproj_kernel, dil=dil),
        grid=(t // tm, n // tn),
        in_specs=[pl.BlockSpec((tm, d), lambda i, j: (i, 0)), pl.BlockSpec((d, tn), lambda i, j: (0, j))],
        out_specs=pl.BlockSpec((None, dil, tm // dil, tn),
                               lambda i, j: (i // tiles_per_seq, 0, i % tiles_per_seq, j)),
        out_shape=jax.ShapeDtypeStruct((batch, dil, seq // dil, n), BF16),
        scratch_shapes=[pltpu.VMEM((tn // LANES, tm, LANES), F32)],
        compiler_params=pltpu.CompilerParams(dimension_semantics=("arbitrary", "arbitrary"),
                                             vmem_limit_bytes=VMEM_LIMIT),
        name=f"dilproj{dil}",
    )(h2, w)


def _mla_kernel(qt_ref, k_ref, vt_ref, o_ref, m_ref, acc_ref, sta_ref, stb_ref):
    tq = qt_ref.shape[1]
    qi = pl.program_id(2)
    m_ref[...] = jnp.full(m_ref.shape, NEG_INF, F32)
    acc_ref[...] = jnp.zeros(acc_ref.shape, F32)
    chains = [(a, hq) for a in range(MLA_HPS) for hq in range(tq // MLA_QSTRIP)]

    def scores(c, i):
        a, hq = chains[i]
        k = k_ref[pl.ds(pl.multiple_of(c * tq, tq), tq), a * LANES:(a + 1) * LANES]
        return _dot(k, qt_ref[a * LANES:(a + 1) * LANES, hq * MLA_QSTRIP:(hq + 1) * MLA_QSTRIP])

    def step(c, src_ref, dst_ref, masked):
        m_all = m_ref[...]
        acc_all = acc_ref[...]
        m_new, acc_new = {}, {}
        for i, (a, hq) in enumerate(chains):
            qs = slice(hq * MLA_QSTRIP, (hq + 1) * MLA_QSTRIP)
            if dst_ref is not None:
                dst_ref[i] = scores(c + 1, i)
            st = src_ref[i]
            if masked:
                key = lax.broadcasted_iota(jnp.int32, st.shape, 0)
                qry = lax.broadcasted_iota(jnp.int32, st.shape, 1) + hq * MLA_QSTRIP
                st = jnp.where(key <= qry, st, NEG_INF)
            m_prev = m_all[a, :, qs]
            m_cur = jnp.maximum(m_prev, jnp.max(st, axis=0, keepdims=True))
            alpha = jnp.exp2(m_prev - m_cur)
            p = jnp.exp2(st - m_cur).astype(BF16)
            vt = vt_ref[c, a * MLA_VROWS:(a + 1) * MLA_VROWS, :]
            acc_new[a, hq] = alpha * acc_all[a, :, qs] + _dot(vt, p)
            m_new[a, hq] = m_cur
        for a in range(MLA_HPS):
            strips = range(tq // MLA_QSTRIP)
            m_ref[a] = jnp.concatenate([m_new[a, hq] for hq in strips], axis=-1)
            acc_ref[a] = jnp.concatenate([acc_new[a, hq] for hq in strips], axis=-1)

    for i in range(len(chains)):
        sta_ref[i] = scores(0, i)

    def pair(j, carry):
        step(2 * j, sta_ref, stb_ref, False)
        step(2 * j + 1, stb_ref, sta_ref, False)
        return carry

    lax.fori_loop(0, qi // 2, pair, 0)

    @pl.when(qi % 2 == 0)
    def _():
        step(qi, sta_ref, None, True)

    @pl.when(qi % 2 == 1)
    def _():
        step(qi - 1, sta_ref, stb_ref, False)
        step(qi, stb_ref, None, True)

    out_t = jnp.concatenate([acc_ref[a, :MLA_V, :] / acc_ref[a, MLA_V:MLA_V + 1, :] for a in range(MLA_HPS)],
                            axis=0)
    o_ref[...] = out_t.T.astype(BF16)


def _mla(qt, k3, vt):
    b, s, _ = k3.shape
    tq = MLA_TQ
    hps = MLA_HPS
    return pl.pallas_call(
        _mla_kernel,
        grid=(b, MLA_HEADS // hps, s // tq),
        in_specs=[
            pl.BlockSpec((None, hps * LANES, tq), lambda bi, h, i: (bi, h, i)),
            pl.BlockSpec((None, s, hps * LANES), lambda bi, h, i: (bi, 0, h)),
            pl.BlockSpec((None, s // tq, hps * MLA_VROWS, tq), lambda bi, h, i: (bi, 0, h, 0)),
        ],
        out_specs=pl.BlockSpec((None, tq, hps * MLA_V), lambda bi, h, i: (bi, i, h)),
        out_shape=jax.ShapeDtypeStruct((b, s, MLA_HEADS * MLA_V), BF16),
        scratch_shapes=[pltpu.VMEM((hps, 1, tq), F32), pltpu.VMEM((hps, MLA_VROWS, tq), F32),
                        pltpu.VMEM((hps * (tq // MLA_QSTRIP), tq, MLA_QSTRIP), F32),
                        pltpu.VMEM((hps * (tq // MLA_QSTRIP), tq, MLA_QSTRIP), F32)],
        compiler_params=pltpu.CompilerParams(dimension_semantics=("arbitrary", "arbitrary", "arbitrary"),
                                             vmem_limit_bytes=VMEM_LIMIT),
        name="mla",
    )(qt, k3, vt)


def _dilattn_kernel(q_ref, kc_ref, vc_ref, kp_ref, vp_ref, o_ref, lse_ref, kx_ref, vx_ref, *, dil, group):
    n = pl.program_id(2)
    tb = q_ref.shape[0]
    nblk = tb // BLOCK
    kx_ref[:BLOCK, :] = kp_ref[...]
    kx_ref[BLOCK:, :] = kc_ref[...]
    vx_ref[:BLOCK, :] = vp_ref[...]
    vx_ref[BLOCK:, :] = vc_ref[...]

    qry = lax.broadcasted_iota(jnp.int32, (BLOCK, 2 * BLOCK), 0)
    key = lax.broadcasted_iota(jnp.int32, (BLOCK, 2 * BLOCK), 1)
    dist = qry + BLOCK - key
    in_window = jnp.logical_and(dist >= 0, dist <= BLOCK)
    first_ok = jnp.logical_and(in_window, jnp.logical_or(key >= BLOCK, n > 0))
    distf = (dist * dil).astype(F32)
    lane = lax.broadcasted_iota(jnp.int32, (BLOCK, LANES), 1)
    scale = DIL_HEAD_DIM ** -0.5
    bias, bias_first = [], []
    for hh in range(DIL_HPG):
        slope = float(2.0 ** (-8.0 * (hh * DIL_GROUPS + group + 1) / DIL_HEADS))
        bias.append(jnp.where(in_window, -slope * distf, NEG_INF))
        bias_first.append(jnp.where(first_ok, -slope * distf, NEG_INF))

    units = [(t, hh) for t in range(nblk) for hh in range(DIL_HPG)]

    def scores(u):
        t, hh = units[u]
        cs = slice(hh * DIL_HEAD_DIM, (hh + 1) * DIL_HEAD_DIM)
        return _dot_nt(q_ref[t * BLOCK:(t + 1) * BLOCK, cs], kx_ref[t * BLOCK:(t + 2) * BLOCK, cs])

    pending = [scores(u) for u in range(min(DIL_AHEAD, len(units)))]
    lse_tile = None
    for u, (t, hh) in enumerate(units):
        if u + DIL_AHEAD < len(units):
            pending.append(scores(u + DIL_AHEAD))
        rs = slice(t * BLOCK, (t + 1) * BLOCK)
        cs = slice(hh * DIL_HEAD_DIM, (hh + 1) * DIL_HEAD_DIM)
        s = pending[u] * scale + (bias_first[hh] if t == 0 else bias[hh])
        m = jnp.max(s, axis=-1, keepdims=True)
        e = jnp.exp(s - m)
        den = jnp.sum(e, axis=-1, keepdims=True)
        o = _dot(e.astype(BF16), vx_ref[t * BLOCK:(t + 2) * BLOCK, cs]) / den
        o_ref[rs, cs] = o.astype(BF16)
        lse = m + jnp.log(den)
        lse_tile = jnp.where(lane == hh, lse, jnp.zeros((BLOCK, LANES), F32) if hh == 0 else lse_tile)
        if hh == DIL_HPG - 1:
            lse_ref[rs, :] = lse_tile


def _dilattn(qkv, dil, group):
    b, d, l, _ = qkv.shape
    tb = DIL_TB
    bpt = tb // BLOCK
    w = DIL_WIDTH
    cur = lambda c: pl.BlockSpec((None, None, tb, w), lambda bi, r, n: (bi, r, n, c))
    prev = lambda c: pl.BlockSpec((None, None, BLOCK, w), lambda bi, r, n: (bi, r, jnp.maximum(n * bpt - 1, 0), c))
    return pl.pallas_call(
        functools.partial(_dilattn_kernel, dil=dil, group=group),
        grid=(b, d, l // tb),
        in_specs=[cur(0), cur(1), cur(2), prev(1), prev(2)],
        out_specs=[pl.BlockSpec((None, None, tb, w), lambda bi, r, n: (bi, r, n, 0)),
                   pl.BlockSpec((None, None, tb, LANES), lambda bi, r, n: (bi, r, n, 0))],
        out_shape=[jax.ShapeDtypeStruct((b, d, l, w), BF16), jax.ShapeDtypeStruct((b, d, l, LANES), F32)],
        scratch_shapes=[pltpu.VMEM((tb + BLOCK, w), BF16), pltpu.VMEM((tb + BLOCK, w), BF16)],
        compiler_params=pltpu.CompilerParams(dimension_semantics=("arbitrary", "arbitrary", "arbitrary"),
                                             vmem_limit_bytes=VMEM_LIMIT),
        name=f"dilattn{dil}",
    )(qkv, qkv, qkv, qkv, qkv)


def _memkv_kernel(mem_ref, g_ref, w_ref, o_ref):
    o_ref[...] = _dot(_rms(mem_ref[...], g_ref[...]).astype(BF16), w_ref[...]).astype(BF16)


def _memkv(mem2, g_mem, w):
    return pl.pallas_call(
        _memkv_kernel,
        out_shape=jax.ShapeDtypeStruct((mem2.shape[0], w.shape[1]), BF16),
        compiler_params=pltpu.CompilerParams(vmem_limit_bytes=VMEM_LIMIT),
        name="memkv",
    )(mem2, g_mem, w)


def _sigmoid(z):
    return 1.0 / (1.0 + jnp.exp(-z))


def _merge_kernel(x_ref, h_ref, wmq_ref, wg_ref, bg_ref, kvm_ref, ymla_ref, wbm_ref,
                  o0_ref, o1_ref, o2_ref, l0_ref, l1_ref, l2_ref, wbd_ref, wbmem_ref, wo_ref, gpm_ref, gpf_ref,
                  x1_ref, h2_ref, nat1_ref, nat2_ref, lse1_ref, lse2_ref):
    tm, dm = x_ref.shape
    h = h_ref[...]

    for src, lsrc, dst, ldst in ((o1_ref, l1_ref, nat1_ref, lse1_ref), (o2_ref, l2_ref, nat2_ref, lse2_ref)):
        d = src.shape[0]
        rows = src.shape[1]
        for r in range(d):
            for hh in range(DIL_HPG):
                cs = slice(hh * DIL_HEAD_DIM, (hh + 1) * DIL_HEAD_DIM)
                dst[hh, pl.ds(r, rows, stride=d), :] = src[r, :, cs].astype(F32)
            ldst[pl.ds(r, rows, stride=d), :] = lsrc[r]
    lg = (l0_ref[0], lse1_ref[...], lse2_ref[...])
    heads = []
    for hh in range(DIL_HPG):
        cs = slice(hh * DIL_HEAD_DIM, (hh + 1) * DIL_HEAD_DIM)
        og = (o0_ref[0, :, cs].astype(F32), nat1_ref[hh], nat2_ref[hh])
        ls = [l[:, hh:hh + 1] for l in lg]
        mx = jnp.maximum(jnp.maximum(ls[0], ls[1]), ls[2])
        ws = [jnp.exp(l - mx) for l in ls]
        num = ws[0] * og[0] + ws[1] * og[1] + ws[2] * og[2]
        heads.append((num / (ws[0] + ws[1] + ws[2])).astype(BF16))
    y_dil = jnp.concatenate(heads, axis=-1)

    memq = _dot(h, wmq_ref[...])
    mheads = []
    for hh in range(MEM_HEADS):
        cs = slice(hh * MEM_HEAD_DIM, (hh + 1) * MEM_HEAD_DIM)
        q = (memq[:, cs] * MEM_HEAD_DIM ** -0.5).astype(BF16)
        s = _dot_nt(q, kvm_ref[:, cs])
        e = jnp.exp(s - jnp.max(s, axis=-1, keepdims=True))
        o = _dot(e.astype(BF16), kvm_ref[:, MEM_WIDTH + hh * MEM_HEAD_DIM:MEM_WIDTH + (hh + 1) * MEM_HEAD_DIM])
        mheads.append((o / jnp.sum(e, axis=-1, keepdims=True)).astype(BF16))
    y_mem = jnp.concatenate(mheads, axis=-1)

    merged = jnp.zeros((tm, dm), F32)
    for br, (y, w_ref) in enumerate(((ymla_ref[...], wbm_ref), (y_dil, wbd_ref), (y_mem, wbmem_ref))):
        cs = slice(br * dm, (br + 1) * dm)
        gate = _sigmoid(_dot(h, wg_ref[:, cs]) + bg_ref[:, cs])
        merged = merged + gate * _dot(y, w_ref[...])
    mixed = _dot(merged.astype(BF16), wo_ref[...])
    x1 = x_ref[...] + _rms(mixed, gpm_ref[...])
    x1_ref[...] = x1
    h2_ref[...] = _rms(x1, gpf_ref[...]).astype(BF16)


def _merge(x2, h2, w_mq, w_g, b_g, kvm, y_mla, w_bm, o_dil, lse_dil, w_bd, w_bmem, w_o, g_pm, g_pf, batch, seq):
    t, dm = x2.shape
    tm = MERGE_TM
    tps = seq // tm
    n_mem = kvm.shape[0] // batch
    row = lambda i: (i, 0)
    single = pl.Buffered(1)
    const = lambda a: pl.BlockSpec(a.shape, lambda i: (0,) * a.ndim, pipeline_mode=single)

    def dil_spec(a):
        d, width = a.shape[1], a.shape[3]
        return pl.BlockSpec((None, d, tm // d, width), lambda i: (i // tps, 0, i % tps, 0))

    return pl.pallas_call(
        _merge_kernel,
        grid=(t // tm,),
        in_specs=[
            pl.BlockSpec((tm, dm), row), pl.BlockSpec((tm, dm), row), const(w_mq), const(w_g), const(b_g),
            pl.BlockSpec((n_mem, kvm.shape[1]), lambda i: (i // tps, 0)),
            pl.BlockSpec((tm, y_mla.shape[1]), row), const(w_bm),
            dil_spec(o_dil[0]), dil_spec(o_dil[1]), dil_spec(o_dil[2]),
            dil_spec(lse_dil[0]), dil_spec(lse_dil[1]), dil_spec(lse_dil[2]),
            const(w_bd), const(w_bmem), const(w_o), const(g_pm), const(g_pf),
        ],
        out_specs=[pl.BlockSpec((tm, dm), row), pl.BlockSpec((tm, dm), row)],
        out_shape=[jax.ShapeDtypeStruct((t, dm), F32), jax.ShapeDtypeStruct((t, dm), BF16)],
        scratch_shapes=[pltpu.VMEM((DIL_HPG, tm, DIL_HEAD_DIM), F32), pltpu.VMEM((DIL_HPG, tm, DIL_HEAD_DIM), F32),
                        pltpu.VMEM((tm, LANES), F32), pltpu.VMEM((tm, LANES), F32)],
        compiler_params=pltpu.CompilerParams(dimension_semantics=("arbitrary",), vmem_limit_bytes=VMEM_LIMIT),
        name="merge",
    )(x2, h2, w_mq, w_g, b_g, kvm, y_mla, w_bm, o_dil[0], o_dil[1], o_dil[2], lse_dil[0], lse_dil[1], lse_dil[2],
      w_bd, w_bmem, w_o, g_pm, g_pf)


def _ffn_kernel(x1_ref, h2_ref, halo_ref, wup_ref, cw_ref, cb_ref, wd_ref, gpost_ref,
                out_ref, hcat_ref, ua_ref, ub_ref, acc_ref, *, tiles_per_seq):
    i = pl.program_id(0)
    tm = x1_ref.shape[0]
    halo, tf = FFN_HALO, FFN_TF
    dff = wd_ref.shape[0]
    nchunk = dff // tf
    lanes_per_chunk = tf // LANES

    first = (i % tiles_per_seq) == 0
    hcat_ref[:halo, :] = jnp.where(first, jnp.zeros_like(halo_ref[...]), halo_ref[...])
    hcat_ref[halo:, :] = h2_ref[...]

    def up(c, u_ref):
        hc = hcat_ref[...]
        for part, off in enumerate((c * tf, dff + c * tf)):
            u = _dot(hc, wup_ref[:, off:off + tf])
            for j in range(lanes_per_chunk):
                u_ref[part * lanes_per_chunk + j] = u[:, j * LANES:(j + 1) * LANES]

    def conv(u_ref, slab, col):
        cols = slice(col, col + LANES)
        z = cb_ref[:, cols] + cw_ref[0:1, cols] * u_ref[slab, halo - 2:halo - 2 + tm, :]
        z = z + cw_ref[1:2, cols] * u_ref[slab, halo - 1:halo - 1 + tm, :]
        return z + cw_ref[2:3, cols] * u_ref[slab, halo:halo + tm, :]

    bufs = (ua_ref, ub_ref)
    up(0, bufs[0])
    for c in range(nchunk):
        cur = bufs[c % 2]
        if c + 1 < nchunk:
            up(c + 1, bufs[(c + 1) % 2])
        acts = []
        for j in range(lanes_per_chunk):
            gate = conv(cur, j, c * tf + j * LANES)
            val = conv(cur, lanes_per_chunk + j, dff + c * tf + j * LANES)
            acts.append((gate * _sigmoid(gate) * val).astype(BF16))
        down = _dot(jnp.concatenate(acts, axis=-1), wd_ref[c * tf:(c + 1) * tf, :])
        if c == 0:
            acc_ref[...] = down
        else:
            acc_ref[...] += down

    out_ref[...] = x1_ref[...] + _rms(acc_ref[...], gpost_ref[...])


def _ffn(x1, h2, w_up, conv_w, conv_b, w_down, g_post, seq):
    t, dm = x1.shape
    tm, tf, halo = FFN_TM, FFN_TF, FFN_HALO
    tps = seq // tm
    row = lambda i: (i, 0)
    const = lambda a: pl.BlockSpec(a.shape, lambda i: (0,) * a.ndim, pipeline_mode=pl.Buffered(1))
    u_scratch = pltpu.VMEM((2 * tf // LANES, tm + halo, LANES), F32)
    return pl.pallas_call(
        functools.partial(_ffn_kernel, tiles_per_seq=tps),
        grid=(t // tm,),
        in_specs=[
            pl.BlockSpec((tm, dm), row),
            pl.BlockSpec((tm, dm), row),
            pl.BlockSpec((halo, dm), lambda i: (jnp.maximum(i * (tm // halo) - 1, 0), 0)),
            const(w_up), const(conv_w), const(conv_b), const(w_down), const(g_post),
        ],
        out_specs=pl.BlockSpec((tm, dm), row),
        out_shape=jax.ShapeDtypeStruct((t, dm), F32),
        scratch_shapes=[pltpu.VMEM((tm + halo, dm), BF16), u_scratch, u_scratch, pltpu.VMEM((tm, dm), F32)],
        compiler_params=pltpu.CompilerParams(dimension_semantics=("arbitrary",), vmem_limit_bytes=VMEM_LIMIT),
        name="ffn",
    )(x1, h2, h2, w_up, conv_w, conv_b, w_down, g_post)


def _rot_half_cols(w):
    return jnp.concatenate([-w[..., ROPE_HALF:], w[..., :ROPE_HALF]], axis=-1)


def _prep_weights(w_in, w_uq, w_ukv):
    dm = w_in.shape[0]
    pad_hi = LANES - MLA_QK_DIM
    kr = w_in[:, OFF_KV:OFF_KR]
    z_lo = jnp.zeros((dm, MLA_NOPE), F32)
    z_hi = jnp.zeros((dm, pad_hi), F32)
    w_a = jnp.concatenate([w_in[:, :OFF_KV], z_lo, kr, z_hi, z_lo, _rot_half_cols(kr), z_hi], axis=1)

    uq = w_uq.reshape(MLA_Q_RANK, MLA_HEADS, MLA_QK_DIM)
    zq_lo = jnp.zeros((MLA_Q_RANK, MLA_HEADS, MLA_NOPE), F32)
    zq_hi = jnp.zeros((MLA_Q_RANK, MLA_HEADS, pad_hi), F32)
    w_qm = jnp.concatenate([uq, zq_hi], axis=-1).reshape(MLA_Q_RANK, MLA_HEADS * LANES)
    w_qs = jnp.concatenate([zq_lo, _rot_half_cols(uq[..., MLA_NOPE:]), zq_hi], axis=-1)
    w_qs = w_qs.reshape(MLA_Q_RANK, MLA_HEADS * LANES)

    ukv = w_ukv.reshape(MLA_KV_RANK, MLA_HEADS, MLA_NOPE + MLA_V)
    zk = jnp.zeros((MLA_KV_RANK, MLA_HEADS, LANES - MLA_NOPE), F32)
    w_k = jnp.concatenate([ukv[..., :MLA_NOPE], zk], axis=-1).reshape(MLA_KV_RANK, MLA_HEADS * LANES)
    w_v = ukv[..., MLA_NOPE:].reshape(MLA_KV_RANK, MLA_HEADS * MLA_V)
    return tuple(a.astype(BF16) for a in (w_a, w_qm.T, w_qs.T, w_k, w_v.T))


def _layer(x, mem, positions, g_pre_mix, w_in, b_gate, mla_q_norm, w_uq, mla_kv_norm, w_ukv, g_mem, w_mem_kv,
           w_br_mla, w_br_dil, w_br_mem, w_o, g_post_mix, g_pre_ffn, w_ffn_up, conv_w, conv_b, w_ffn_down,
           g_post_ffn):
    batch, seq, dm = x.shape
    t = batch * seq
    x2 = x.reshape(t, dm)
    r2 = lambda v: v.reshape(1, -1)

    w_a, w_qmt, w_qst, w_k, w_vt = _prep_weights(w_in, w_uq, w_ukv)
    invf = (ROPE_THETA ** (-jnp.arange(ROPE_HALF, dtype=F32) / ROPE_HALF)).reshape(ROPE_HALF, 1)
    pos_rows = positions.reshape(t // PREP_TM, 1, PREP_TM)

    h2d, qt, k, vt = _prep(x2, pos_rows, invf, r2(g_pre_mix), w_a, r2(mla_q_norm), w_qmt, w_qst,
                           r2(mla_kv_norm), w_k, w_vt, batch, seq)
    y_mla = _mla(qt, k.reshape(batch, seq, MLA_HEADS * LANES), vt).reshape(t, MLA_HEADS * MLA_V)

    w_dil = w_in[:, OFF_KR:OFF_DIL].reshape(dm, 3, DIL_GROUPS, DIL_WIDTH)
    o_dil, lse_dil = [], []
    for g, (_, dil) in enumerate(DIL_PAIRS):
        w_g = w_dil[:, :, g, :].reshape(dm, 3 * DIL_WIDTH).astype(BF16)
        qkv = _dilproj(h2d, w_g, batch, seq, dil)
        o, lse = _dilattn(qkv, dil, g)
        o_dil.append(o)
        lse_dil.append(lse)

    kvm = _memkv(mem.reshape(-1, dm), r2(g_mem), w_mem_kv.astype(BF16))

    x1, h2 = _merge(x2, h2d, w_in[:, OFF_DIL:OFF_MEMQ].astype(BF16), w_in[:, OFF_MEMQ:].astype(BF16),
                    r2(b_gate), kvm, y_mla, w_br_mla.astype(BF16), o_dil, lse_dil, w_br_dil.astype(BF16),
                    w_br_mem.astype(BF16), w_o.astype(BF16), r2(g_post_mix), r2(g_pre_ffn), batch, seq)

    out = _ffn(x1, h2, w_ffn_up.astype(BF16), conv_w, r2(conv_b), w_ffn_down.astype(BF16), r2(g_post_ffn), seq)
    return out.reshape(batch, seq, dm)


def kernel(x, mem, positions, g_pre_mix, w_in, b_gate, mla_q_norm, w_uq, mla_kv_norm, w_ukv, g_mem, w_mem_kv,
           w_br_mla, w_br_dil, w_br_mem, w_o, g_post_mix, g_pre_ffn, w_ffn_up, conv_w, conv_b, w_ffn_down,
           g_post_ffn):
    for l in range(w_in.shape[0]):
        x = _layer(x, mem, positions, g_pre_mix[l], w_in[l], b_gate[l], mla_q_norm[l], w_uq[l], mla_kv_norm[l],
                   w_ukv[l], g_mem[l], w_mem_kv[l], w_br_mla[l], w_br_dil[l], w_br_mem[l], w_o[l], g_post_mix[l],
                   g_pre_ffn[l], w_ffn_up[l], conv_w[l], conv_b[l], w_ffn_down[l], g_post_ffn[l])
    return x
```

```python
import functools

import jax
import jax.numpy as jnp
from jax import lax
from jax.experimental import pallas as pl
from jax.experimental.pallas import tpu as pltpu

F32 = jnp.float32
BF16 = jnp.bfloat16

RMS_EPS = 1e-6
LOG2E = 1.4426950408889634
NEG_INF = -1e30
LANES = 128

BLOCK = 128
MLA_HEADS = 8
MLA_NOPE = 64
MLA_ROPE = 32
MLA_V = 64
MLA_QK_DIM = MLA_NOPE + MLA_ROPE
MLA_Q_RANK = 384
MLA_KV_RANK = 256
ROPE_THETA = 10000.0
ROPE_HALF = MLA_ROPE // 2

DIL_PAIRS = ((128, 1), (512, 4), (2048, 16))
DIL_GROUPS = 3
DIL_HPG = 4
DIL_HEADS = DIL_GROUPS * DIL_HPG
DIL_HEAD_DIM = 128
DIL_WIDTH = DIL_HPG * DIL_HEAD_DIM

MEM_HEADS = 4
MEM_HEAD_DIM = 128
MEM_WIDTH = MEM_HEADS * MEM_HEAD_DIM

N_BRANCH = 3
CONV_WIDTH = 3

OFF_Q = MLA_Q_RANK
OFF_KV = OFF_Q + MLA_KV_RANK
OFF_KR = OFF_KV + MLA_ROPE
OFF_DIL = OFF_KR + 3 * DIL_HEADS * DIL_HEAD_DIM
OFF_MEMQ = OFF_DIL + MEM_WIDTH

PREP_TM = 512
DILPROJ_TM = 2048
DILPROJ_TN = 512
MLA_TQ = 1024
MLA_TK = 512
MLA_HPS = 2
MLA_QSTRIP = 256
MLA_VROWS = MLA_V + 16
assert PREP_TM == MLA_TK
DIL_TB = 512
DIL_AHEAD = 4
MERGE_TM = 512
FFN_TM = 512
FFN_TF = 256
FFN_HALO = 16

VMEM_LIMIT = 56 * 1024 * 1024


def _rms(xf, g):
    return xf * lax.rsqrt(jnp.mean(xf * xf, axis=-1, keepdims=True) + RMS_EPS) * g


def _dot(a, b):
    return jnp.dot(a, b, preferred_element_type=F32)


def _dot_nt(a, b):
    return lax.dot_general(a, b, (((1,), (1,)), ((), ())), preferred_element_type=F32)


def _const_spec(shape):
    nd = len(shape)
    return pl.BlockSpec(shape, lambda *_: (0,) * nd)


def _prep_kernel(x_ref, pos_ref, invf_ref, g_ref, wa_ref, qn_ref, wqmt_ref, wqst_ref, kvn_ref, wk_ref, wvt_ref,
                 h_ref, qt_ref, k_ref, vt_ref):
    tm = x_ref.shape[0]
    h = _rms(x_ref[...], g_ref[...]).astype(BF16)
    h_ref[...] = h
    p = _dot(h, wa_ref[...])
    cq = _rms(p[:, :OFF_Q], qn_ref[...]).astype(BF16)
    ckv = _rms(p[:, OFF_Q:OFF_KV], kvn_ref[...]).astype(BF16)

    ang = invf_ref[...] * pos_ref[0].astype(F32)
    c16 = jnp.cos(ang)
    s16 = jnp.sin(ang)
    ones = jnp.ones((MLA_NOPE, tm), F32)
    zeros_lo = jnp.zeros((MLA_NOPE, tm), F32)
    zeros_hi = jnp.zeros((LANES - MLA_QK_DIM, tm), F32)
    cos_c = jnp.concatenate([ones, c16, c16, zeros_hi], axis=0)
    sin_c = jnp.concatenate([zeros_lo, s16, s16, zeros_hi], axis=0)

    qmt = _dot_nt(wqmt_ref[...], cq)
    qst = _dot_nt(wqst_ref[...], cq)
    qscale = MLA_QK_DIM ** -0.5 * LOG2E
    for hd in range(MLA_HEADS):
        sl = slice(hd * LANES, (hd + 1) * LANES)
        qt_ref[sl, :] = ((qmt[sl, :] * cos_c + qst[sl, :] * sin_c) * qscale).astype(BF16)

    cos_r = cos_c.T
    sin_r = sin_c.T
    kn = _dot(ckv, wk_ref[...])
    kpe = p[:, OFF_KV:OFF_KV + LANES] * cos_r + p[:, OFF_KV + LANES:OFF_KV + 2 * LANES] * sin_r
    for hd in range(MLA_HEADS):
        sl = slice(hd * LANES, (hd + 1) * LANES)
        k_ref[:, sl] = (kn[:, sl] + kpe).astype(BF16)
    vt = _dot_nt(wvt_ref[...], ckv).astype(BF16)
    ones_rows = jnp.ones((MLA_VROWS - MLA_V, tm), BF16)
    for hd in range(MLA_HEADS):
        vt_ref[0, hd * MLA_VROWS:hd * MLA_VROWS + MLA_V, :] = vt[hd * MLA_V:(hd + 1) * MLA_V, :]
        vt_ref[0, hd * MLA_VROWS + MLA_V:(hd + 1) * MLA_VROWS, :] = ones_rows


def _prep(x2, pos_rows, invf, g_pre, w_a, q_norm, w_qmt, w_qst, kv_norm, w_k, w_vt, batch, seq):
    t, d = x2.shape
    tm = PREP_TM
    tps = seq // tm
    hw = MLA_HEADS * LANES
    vw = MLA_HEADS * MLA_VROWS
    row = lambda i: (i, 0)
    return pl.pallas_call(
        _prep_kernel,
        grid=(t // tm,),
        in_specs=[
            pl.BlockSpec((tm, d), row),
            pl.BlockSpec((1, 1, tm), lambda i: (i, 0, 0)),
            _const_spec(invf.shape),
            _const_spec(g_pre.shape),
            _const_spec(w_a.shape),
            _const_spec(q_norm.shape),
            _const_spec(w_qmt.shape),
            _const_spec(w_qst.shape),
            _const_spec(kv_norm.shape),
            _const_spec(w_k.shape),
            _const_spec(w_vt.shape),
        ],
        out_specs=[pl.BlockSpec((tm, d), row),
                   pl.BlockSpec((None, hw, tm), lambda i: (i // tps, 0, i % tps)),
                   pl.BlockSpec((tm, hw), row),
                   pl.BlockSpec((None, 1, vw, tm), lambda i: (i // tps, i % tps, 0, 0))],
        out_shape=[jax.ShapeDtypeStruct((t, d), BF16),
                   jax.ShapeDtypeStruct((batch, hw, seq), BF16),
                   jax.ShapeDtypeStruct((t, hw), BF16),
                   jax.ShapeDtypeStruct((batch, tps, vw, tm), BF16)],
        compiler_params=pltpu.CompilerParams(dimension_semantics=("arbitrary",), vmem_limit_bytes=VMEM_LIMIT),
        name="prep",
    )(x2, pos_rows, invf, g_pre, w_a, q_norm, w_qmt, w_qst, kv_norm, w_k, w_vt)


def _dilproj_kernel(h_ref, w_ref, o_ref, acc_ref, *, dil):
    acc = _dot(h_ref[...], w_ref[...])
    if dil == 1:
        o_ref[0] = acc.astype(BF16)
    else:
        chunks, tm, _ = acc_ref.shape
        rows = tm // dil
        for c in range(chunks):
            acc_ref[c] = acc[:, c * LANES:(c + 1) * LANES]
        for r in range(dil):
            for c in range(chunks):
                o_ref[r, :, c * LANES:(c + 1) * LANES] = acc_ref[c, pl.ds(r, rows, stride=dil), :].astype(BF16)


def _dilproj(h2, w, batch, seq, dil):
    t, d = h2.shape
    n = w.shape[1]
    tm, tn = DILPROJ_TM, DILPROJ_TN
    tiles_per_seq = seq // tm
    return pl.pallas_call(
        functools.partial(_dilproj_kernel, dil=dil),
        grid=(t // tm, n // tn),
        in_specs=[pl.BlockSpec((tm, d), lambda i, j: (i, 0)), pl.BlockSpec((d, tn), lambda i, j: (0, j))],
        out_specs=pl.BlockSpec((None, dil, tm // dil, tn),
                               lambda i, j: (i // tiles_per_seq, 0, i % tiles_per_seq, j)),
        out_shape=jax.ShapeDtypeStruct((batch, dil, seq // dil, n), BF16),
        scratch_shapes=[pltpu.VMEM((tn // LANES, tm, LANES), F32)],
        compiler_params=pltpu.CompilerParams(dimension_semantics=("arbitrary", "arbitrary"),
                                             vmem_limit_bytes=VMEM_LIMIT),
        name=f"dilproj{dil}",
    )(h2, w)


def _mla_kernel(qt_ref, k_ref, vt_ref, o_ref, m_ref, acc_ref, sta_ref, stb_ref):
    tq = qt_ref.shape[1]
    tk = MLA_TK
    qi = pl.program_id(2)
    nstrip = tq // MLA_QSTRIP
    m_ref[...] = jnp.full(m_ref.shape, NEG_INF, F32)
    acc_ref[...] = jnp.zeros(acc_ref.shape, F32)
    chains = [(a, hq) for a in range(MLA_HPS) for hq in range(nstrip)]

    def scores(c, i):
        a, hq = chains[i]
        k = k_ref[pl.ds(pl.multiple_of(c * tk, tk), tk), a * LANES:(a + 1) * LANES]
        return _dot(k, qt_ref[a * LANES:(a + 1) * LANES, hq * MLA_QSTRIP:(hq + 1) * MLA_QSTRIP])

    def step(c, src_ref, dst_ref, key_off=None):
        def live(hq, off):
            return off is None or (hq + 1) * MLA_QSTRIP > off

        next_off = None if key_off is None else key_off + tk
        m_all = m_ref[...]
        acc_all = acc_ref[...]
        m_new, acc_new = {}, {}
        for i, (a, hq) in enumerate(chains):
            qs = slice(hq * MLA_QSTRIP, (hq + 1) * MLA_QSTRIP)
            if dst_ref is not None and live(hq, next_off):
                dst_ref[i] = scores(c + 1, i)
            if not live(hq, key_off):
                m_new[a, hq], acc_new[a, hq] = m_all[a, :, qs], acc_all[a, :, qs]
                continue
            st = src_ref[i]
            if key_off is not None and key_off + tk - 1 > hq * MLA_QSTRIP:
                key = lax.broadcasted_iota(jnp.int32, st.shape, 0) + key_off
                qry = lax.broadcasted_iota(jnp.int32, st.shape, 1) + hq * MLA_QSTRIP
                st = jnp.where(key <= qry, st, NEG_INF)
            m_prev = m_all[a, :, qs]
            m_cur = jnp.maximum(m_prev, jnp.max(st, axis=0, keepdims=True))
            alpha = jnp.exp2(m_prev - m_cur)
            p = jnp.exp2(st - m_cur).astype(BF16)
            vt = vt_ref[c, a * MLA_VROWS:(a + 1) * MLA_VROWS, :]
            acc_new[a, hq] = alpha * acc_all[a, :, qs] + _dot(vt, p)
            m_new[a, hq] = m_cur
        for a in range(MLA_HPS):
            m_ref[a] = jnp.concatenate([m_new[a, hq] for hq in range(nstrip)], axis=-1)
            acc_ref[a] = jnp.concatenate([acc_new[a, hq] for hq in range(nstrip)], axis=-1)

    for i in range(len(chains)):
        sta_ref[i] = scores(0, i)

    cpt = tq // tk
    assert cpt == 2

    def pair(j, carry):
        step(2 * j, sta_ref, stb_ref)
        step(2 * j + 1, stb_ref, sta_ref)
        return carry

    lax.fori_loop(0, qi, pair, 0)
    step(2 * qi, sta_ref, stb_ref, key_off=0)
    step(2 * qi + 1, stb_ref, None, key_off=tk)

    out_t = jnp.concatenate([acc_ref[a, :MLA_V, :] / acc_ref[a, MLA_V:MLA_V + 1, :] for a in range(MLA_HPS)],
                            axis=0)
    o_ref[...] = out_t.T.astype(BF16)


def _mla(qt, k3, vt):
    b, s, _ = k3.shape
    tq, tk = MLA_TQ, MLA_TK
    hps = MLA_HPS
    st_scratch = pltpu.VMEM((hps * (tq // MLA_QSTRIP), tk, MLA_QSTRIP), F32)
    return pl.pallas_call(
        _mla_kernel,
        grid=(b, MLA_HEADS // hps, s // tq),
        in_specs=[
            pl.BlockSpec((None, hps * LANES, tq), lambda bi, h, i: (bi, h, i)),
            pl.BlockSpec((None, s, hps * LANES), lambda bi, h, i: (bi, 0, h)),
            pl.BlockSpec((None, s // tk, hps * MLA_VROWS, tk), lambda bi, h, i: (bi, 0, h, 0)),
        ],
        out_specs=pl.BlockSpec((None, tq, hps * MLA_V), lambda bi, h, i: (bi, i, h)),
        out_shape=jax.ShapeDtypeStruct((b, s, MLA_HEADS * MLA_V), BF16),
        scratch_shapes=[pltpu.VMEM((hps, 1, tq), F32), pltpu.VMEM((hps, MLA_VROWS, tq), F32),
                        st_scratch, st_scratch],
        compiler_params=pltpu.CompilerParams(dimension_semantics=("arbitrary", "arbitrary", "arbitrary"),
                                             vmem_limit_bytes=VMEM_LIMIT),
        name="mla",
    )(qt, k3, vt)


def _dilattn_kernel(q_ref, kc_ref, vc_ref, kp_ref, vp_ref, o_ref, lse_ref, kx_ref, vx_ref, *, dil, group):
    n = pl.program_id(2)
    tb = q_ref.shape[0]
    nblk = tb // BLOCK
    kx_ref[:BLOCK, :] = kp_ref[...]
    kx_ref[BLOCK:, :] = kc_ref[...]
    vx_ref[:BLOCK, :] = vp_ref[...]
    vx_ref[BLOCK:, :] = vc_ref[...]

    qry = lax.broadcasted_iota(jnp.int32, (BLOCK, 2 * BLOCK), 0)
    key = lax.broadcasted_iota(jnp.int32, (BLOCK, 2 * BLOCK), 1)
    dist = qry + BLOCK - key
    in_window = jnp.logical_and(dist >= 0, dist <= BLOCK)
    first_ok = jnp.logical_and(in_window, jnp.logical_or(key >= BLOCK, n > 0))
    distf = (dist * dil).astype(F32)
    lane = lax.broadcasted_iota(jnp.int32, (BLOCK, LANES), 1)
    scale = DIL_HEAD_DIM ** -0.5
    bias, bias_first = [], []
    for hh in range(DIL_HPG):
        slope = float(2.0 ** (-8.0 * (hh * DIL_GROUPS + group + 1) / DIL_HEADS))
        bias.append(jnp.where(in_window, -slope * distf, NEG_INF))
        bias_first.append(jnp.where(first_ok, -slope * distf, NEG_INF))

    units = [(t, hh) for t in range(nblk) for hh in range(DIL_HPG)]

    def scores(u):
        t, hh = units[u]
        cs = slice(hh * DIL_HEAD_DIM, (hh + 1) * DIL_HEAD_DIM)
        return _dot_nt(q_ref[t * BLOCK:(t + 1) * BLOCK, cs], kx_ref[t * BLOCK:(t + 2) * BLOCK, cs])

    pending = [scores(u) for u in range(min(DIL_AHEAD, len(units)))]
    lse_tile = None
    for u, (t, hh) in enumerate(units):
        if u + DIL_AHEAD < len(units):
            pending.append(scores(u + DIL_AHEAD))
        rs = slice(t * BLOCK, (t + 1) * BLOCK)
        cs = slice(hh * DIL_HEAD_DIM, (hh + 1) * DIL_HEAD_DIM)
        s = pending[u] * scale + (bias_first[hh] if t == 0 else bias[hh])
        m = jnp.max(s, axis=-1, keepdims=True)
        e = jnp.exp(s - m)
        den = jnp.sum(e, axis=-1, keepdims=True)
        o = _dot(e.astype(BF16), vx_ref[t * BLOCK:(t + 2) * BLOCK, cs]) / den
        o_ref[rs, cs] = o.astype(BF16)
        lse = m + jnp.log(den)
        lse_tile = jnp.where(lane == hh, lse, jnp.zeros((BLOCK, LANES), F32) if hh == 0 else lse_tile)
        if hh == DIL_HPG - 1:
            lse_ref[rs, :] = lse_tile


def _dilattn(qkv, dil, group):
    b, d, l, _ = qkv.shape
    tb = DIL_TB
    bpt = tb // BLOCK
    w = DIL_WIDTH
    cur = lambda c: pl.BlockSpec((None, None, tb, w), lambda bi, r, n: (bi, r, n, c))
    prev = lambda c: pl.BlockSpec((None, None, BLOCK, w), lambda bi, r, n: (bi, r, jnp.maximum(n * bpt - 1, 0), c))
    return pl.pallas_call(
        functools.partial(_dilattn_kernel, dil=dil, group=group),
        grid=(b, d, l // tb),
        in_specs=[cur(0), cur(1), cur(2), prev(1), prev(2)],
        out_specs=[pl.BlockSpec((None, None, tb, w), lambda bi, r, n: (bi, r, n, 0)),
                   pl.BlockSpec((None, None, tb, LANES), lambda bi, r, n: (bi, r, n, 0))],
        out_shape=[jax.ShapeDtypeStruct((b, d, l, w), BF16), jax.ShapeDtypeStruct((b, d, l, LANES), F32)],
        scratch_shapes=[pltpu.VMEM((tb + BLOCK, w), BF16), pltpu.VMEM((tb + BLOCK, w), BF16)],
        compiler_params=pltpu.CompilerParams(dimension_semantics=("arbitrary", "arbitrary", "arbitrary"),
                                             vmem_limit_bytes=VMEM_LIMIT),
        name=f"dilattn{dil}",
    )(qkv, qkv, qkv, qkv, qkv)


def _memkv_kernel(mem_ref, g_ref, w_ref, o_ref):
    o_ref[...] = _dot(_rms(mem_ref[...], g_ref[...]).astype(BF16), w_ref[...]).astype(BF16)


def _memkv(mem2, g_mem, w):
    return pl.pallas_call(
        _memkv_kernel,
        out_shape=jax.ShapeDtypeStruct((mem2.shape[0], w.shape[1]), BF16),
        compiler_params=pltpu.CompilerParams(vmem_limit_bytes=VMEM_LIMIT),
        name="memkv",
    )(mem2, g_mem, w)


def _sigmoid(z):
    return 1.0 / (1.0 + jnp.exp(-z))


def _merge_kernel(x_ref, h_ref, wmq_ref, wg_ref, bg_ref, kvm_ref, ymla_ref, wbm_ref,
                  o0_ref, o1_ref, o2_ref, l0_ref, l1_ref, l2_ref, wbd_ref, wbmem_ref, wo_ref, gpm_ref, gpf_ref,
                  x1_ref, h2_ref, nat1_ref, nat2_ref, lse1_ref, lse2_ref):
    tm, dm = x_ref.shape
    h = h_ref[...]

    for src, lsrc, dst, ldst in ((o1_ref, l1_ref, nat1_ref, lse1_ref), (o2_ref, l2_ref, nat2_ref, lse2_ref)):
        d = src.shape[0]
        rows = src.shape[1]
        for r in range(d):
            for hh in range(DIL_HPG):
                cs = slice(hh * DIL_HEAD_DIM, (hh + 1) * DIL_HEAD_DIM)
                dst[hh, pl.ds(r, rows, stride=d), :] = src[r, :, cs].astype(F32)
            ldst[pl.ds(r, rows, stride=d), :] = lsrc[r]
    lg = (l0_ref[0], lse1_ref[...], lse2_ref[...])
    heads = []
    for hh in range(DIL_HPG):
        cs = slice(hh * DIL_HEAD_DIM, (hh + 1) * DIL_HEAD_DIM)
        og = (o0_ref[0, :, cs].astype(F32), nat1_ref[hh], nat2_ref[hh])
        ls = [l[:, hh:hh + 1] for l in lg]
        mx = jnp.maximum(jnp.maximum(ls[0], ls[1]), ls[2])
        ws = [jnp.exp(l - mx) for l in ls]
        num = ws[0] * og[0] + ws[1] * og[1] + ws[2] * og[2]
        heads.append((num / (ws[0] + ws[1] + ws[2])).astype(BF16))
    y_dil = jnp.concatenate(heads, axis=-1)

    memq = _dot(h, wmq_ref[...])
    mheads = []
    for hh in range(MEM_HEADS):
        cs = slice(hh * MEM_HEAD_DIM, (hh + 1) * MEM_HEAD_DIM)
        q = (memq[:, cs] * MEM_HEAD_DIM ** -0.5).astype(BF16)
        s = _dot_nt(q, kvm_ref[:, cs])
        e = jnp.exp(s - jnp.max(s, axis=-1, keepdims=True))
        o = _dot(e.astype(BF16), kvm_ref[:, MEM_WIDTH + hh * MEM_HEAD_DIM:MEM_WIDTH + (hh + 1) * MEM_HEAD_DIM])
        mheads.append((o / jnp.sum(e, axis=-1, keepdims=True)).astype(BF16))
    y_mem = jnp.concatenate(mheads, axis=-1)

    merged = jnp.zeros((tm, dm), F32)
    for br, (y, w_ref) in enumerate(((ymla_ref[...], wbm_ref), (y_dil, wbd_ref), (y_mem, wbmem_ref))):
        cs = slice(br * dm, (br + 1) * dm)
        gate = _sigmoid(_dot(h, wg_ref[:, cs]) + bg_ref[:, cs])
        merged = merged + gate * _dot(y, w_ref[...])
    mixed = _dot(merged.astype(BF16), wo_ref[...])
    x1 = x_ref[...] + _rms(mixed, gpm_ref[...])
    x1_ref[...] = x1
    h2_ref[...] = _rms(x1, gpf_ref[...]).astype(BF16)


def _merge(x2, h2, w_mq, w_g, b_g, kvm, y_mla, w_bm, o_dil, lse_dil, w_bd, w_bmem, w_o, g_pm, g_pf, batch, seq):
    t, dm = x2.shape
    tm = MERGE_TM
    tps = seq // tm
    n_mem = kvm.shape[0] // batch
    row = lambda i: (i, 0)
    single = pl.Buffered(1)
    const = lambda a: pl.BlockSpec(a.shape, lambda i: (0,) * a.ndim, pipeline_mode=single)

    def dil_spec(a):
        d, width = a.shape[1], a.shape[3]
        return pl.BlockSpec((None, d, tm // d, width), lambda i: (i // tps, 0, i % tps, 0))

    return pl.pallas_call(
        _merge_kernel,
        grid=(t // tm,),
        in_specs=[
            pl.BlockSpec((tm, dm), row), pl.BlockSpec((tm, dm), row), const(w_mq), const(w_g), const(b_g),
            pl.BlockSpec((n_mem, kvm.shape[1]), lambda i: (i // tps, 0)),
            pl.BlockSpec((tm, y_mla.shape[1]), row), const(w_bm),
            dil_spec(o_dil[0]), dil_spec(o_dil[1]), dil_spec(o_dil[2]),
            dil_spec(lse_dil[0]), dil_spec(lse_dil[1]), dil_spec(lse_dil[2]),
            const(w_bd), const(w_bmem), const(w_o), const(g_pm), const(g_pf),
        ],
        out_specs=[pl.BlockSpec((tm, dm), row), pl.BlockSpec((tm, dm), row)],
        out_shape=[jax.ShapeDtypeStruct((t, dm), F32), jax.ShapeDtypeStruct((t, dm), BF16)],
        scratch_shapes=[pltpu.VMEM((DIL_HPG, tm, DIL_HEAD_DIM), F32), pltpu.VMEM((DIL_HPG, tm, DIL_HEAD_DIM), F32),
                        pltpu.VMEM((tm, LANES), F32), pltpu.VMEM((tm, LANES), F32)],
        compiler_params=pltpu.CompilerParams(dimension_semantics=("arbitrary",), vmem_limit_bytes=VMEM_LIMIT),
        name="merge",
    )(x2, h2, w_mq, w_g, b_g, kvm, y_mla, w_bm, o_dil[0], o_dil[1], o_dil[2], lse_dil[0], lse_dil[1], lse_dil[2],
      w_bd, w_bmem, w_o, g_pm, g_pf)


def _ffn_kernel(x1_ref, h2_ref, halo_ref, wup_ref, cw_ref, cb_ref, wd_ref, gpost_ref,
                out_ref, hcat_ref, ua_ref, ub_ref, acc_ref, *, tiles_per_seq):
    i = pl.program_id(0)
    tm = x1_ref.shape[0]
    halo, tf = FFN_HALO, FFN_TF
    dff = wd_ref.shape[0]
    nchunk = dff // tf
    lanes_per_chunk = tf // LANES

    first = (i % tiles_per_seq) == 0
    hcat_ref[:halo, :] = jnp.where(first, jnp.zeros_like(halo_ref[...]), halo_ref[...])
    hcat_ref[halo:, :] = h2_ref[...]

    def up(c, u_ref):
        hc = hcat_ref[...]
        for part, off in enumerate((c * tf, dff + c * tf)):
            u = _dot(hc, wup_ref[:, off:off + tf])
            for j in range(lanes_per_chunk):
                u_ref[part * lanes_per_chunk + j] = u[:, j * LANES:(j + 1) * LANES]

    def conv(u_ref, slab, col):
        cols = slice(col, col + LANES)
        z = cb_ref[:, cols] + cw_ref[0:1, cols] * u_ref[slab, halo - 2:halo - 2 + tm, :]
        z = z + cw_ref[1:2, cols] * u_ref[slab, halo - 1:halo - 1 + tm, :]
        return z + cw_ref[2:3, cols] * u_ref[slab, halo:halo + tm, :]

    bufs = (ua_ref, ub_ref)
    up(0, bufs[0])
    for c in range(nchunk):
        cur = bufs[c % 2]
        if c + 1 < nchunk:
            up(c + 1, bufs[(c + 1) % 2])
        acts = []
        for j in range(lanes_per_chunk):
            gate = conv(cur, j, c * tf + j * LANES)
            val = conv(cur, lanes_per_chunk + j, dff + c * tf + j * LANES)
            acts.append((gate * _sigmoid(gate) * val).astype(BF16))
        down = _dot(jnp.concatenate(acts, axis=-1), wd_ref[c * tf:(c + 1) * tf, :])
        if c == 0:
            acc_ref[...] = down
        else:
            acc_ref[...] += down

    out_ref[...] = x1_ref[...] + _rms(acc_ref[...], gpost_ref[...])


def _ffn(x1, h2, w_up, conv_w, conv_b, w_down, g_post, seq):
    t, dm = x1.shape
    tm, tf, halo = FFN_TM, FFN_TF, FFN_HALO
    tps = seq // tm
    row = lambda i: (i, 0)
    const = lambda a: pl.BlockSpec(a.shape, lambda i: (0,) * a.ndim, pipeline_mode=pl.Buffered(1))
    u_scratch = pltpu.VMEM((2 * tf // LANES, tm + halo, LANES), F32)
    return pl.pallas_call(
        functools.partial(_ffn_kernel, tiles_per_seq=tps),
        grid=(t // tm,),
        in_specs=[
            pl.BlockSpec((tm, dm), row),
            pl.BlockSpec((tm, dm), row),
            pl.BlockSpec((halo, dm), lambda i: (jnp.maximum(i * (tm // halo) - 1, 0), 0)),
            const(w_up), const(conv_w), const(conv_b), const(w_down), const(g_post),
        ],
        out_specs=pl.BlockSpec((tm, dm), row),
        out_shape=jax.ShapeDtypeStruct((t, dm), F32),
        scratch_shapes=[pltpu.VMEM((tm + halo, dm), BF16), u_scratch, u_scratch, pltpu.VMEM((tm, dm), F32)],
        compiler_params=pltpu.CompilerParams(dimension_semantics=("arbitrary",), vmem_limit_bytes=VMEM_LIMIT),
        name="ffn",
    )(x1, h2, h2, w_up, conv_w, conv_b, w_down, g_post)


def _rot_half_cols(w):
    return jnp.concatenate([-w[..., ROPE_HALF:], w[..., :ROPE_HALF]], axis=-1)


def _prep_weights(w_in, w_uq, w_ukv):
    dm = w_in.shape[0]
    pad_hi = LANES - MLA_QK_DIM
    kr = w_in[:, OFF_KV:OFF_KR]
    z_lo = jnp.zeros((dm, MLA_NOPE), F32)
    z_hi = jnp.zeros((dm, pad_hi), F32)
    w_a = jnp.concatenate([w_in[:, :OFF_KV], z_lo, kr, z_hi, z_lo, _rot_half_cols(kr), z_hi], axis=1)

    uq = w_uq.reshape(MLA_Q_RANK, MLA_HEADS, MLA_QK_DIM)
    zq_lo = jnp.zeros((MLA_Q_RANK, MLA_HEADS, MLA_NOPE), F32)
    zq_hi = jnp.zeros((MLA_Q_RANK, MLA_HEADS, pad_hi), F32)
    w_qm = jnp.concatenate([uq, zq_hi], axis=-1).reshape(MLA_Q_RANK, MLA_HEADS * LANES)
    w_qs = jnp.concatenate([zq_lo, _rot_half_cols(uq[..., MLA_NOPE:]), zq_hi], axis=-1)
    w_qs = w_qs.reshape(MLA_Q_RANK, MLA_HEADS * LANES)

    ukv = w_ukv.reshape(MLA_KV_RANK, MLA_HEADS, MLA_NOPE + MLA_V)
    zk = jnp.zeros((MLA_KV_RANK, MLA_HEADS, LANES - MLA_NOPE), F32)
    w_k = jnp.concatenate([ukv[..., :MLA_NOPE], zk], axis=-1).reshape(MLA_KV_RANK, MLA_HEADS * LANES)
    w_v = ukv[..., MLA_NOPE:].reshape(MLA_KV_RANK, MLA_HEADS * MLA_V)
    return tuple(a.astype(BF16) for a in (w_a, w_qm.T, w_qs.T, w_k, w_v.T))


def _layer(x, mem, positions, g_pre_mix, w_in, b_gate, mla_q_norm, w_uq, mla_kv_norm, w_ukv, g_mem, w_mem_kv,
           w_br_mla, w_br_dil, w_br_mem, w_o, g_post_mix, g_pre_ffn, w_ffn_up, conv_w, conv_b, w_ffn_down,
           g_post_ffn):
    batch, seq, dm = x.shape
    t = batch * seq
    x2 = x.reshape(t, dm)
    r2 = lambda v: v.reshape(1, -1)

    w_a, w_qmt, w_qst, w_k, w_vt = _prep_weights(w_in, w_uq, w_ukv)
    invf = (ROPE_THETA ** (-jnp.arange(ROPE_HALF, dtype=F32) / ROPE_HALF)).reshape(ROPE_HALF, 1)
    pos_rows = positions.reshape(t // PREP_TM, 1, PREP_TM)

    h2d, qt, k, vt = _prep(x2, pos_rows, invf, r2(g_pre_mix), w_a, r2(mla_q_norm), w_qmt, w_qst,
                           r2(mla_kv_norm), w_k, w_vt, batch, seq)
    y_mla = _mla(qt, k.reshape(batch, seq, MLA_HEADS * LANES), vt).reshape(t, MLA_HEADS * MLA_V)

    w_dil = w_in[:, OFF_KR:OFF_DIL].reshape(dm, 3, DIL_GROUPS, DIL_WIDTH)
    o_dil, lse_dil = [], []
    for g, (_, dil) in enumerate(DIL_PAIRS):
        w_g = w_dil[:, :, g, :].reshape(dm, 3 * DIL_WIDTH).astype(BF16)
        qkv = _dilproj(h2d, w_g, batch, seq, dil)
        o, lse = _dilattn(qkv, dil, g)
        o_dil.append(o)
        lse_dil.append(lse)

    kvm = _memkv(mem.reshape(-1, dm), r2(g_mem), w_mem_kv.astype(BF16))

    x1, h2 = _merge(x2, h2d, w_in[:, OFF_DIL:OFF_MEMQ].astype(BF16), w_in[:, OFF_MEMQ:].astype(BF16),
                    r2(b_gate), kvm, y_mla, w_br_mla.astype(BF16), o_dil, lse_dil, w_br_dil.astype(BF16),
                    w_br_mem.astype(BF16), w_o.astype(BF16), r2(g_post_mix), r2(g_pre_ffn), batch, seq)

    out = _ffn(x1, h2, w_ffn_up.astype(BF16), conv_w, r2(conv_b), w_ffn_down.astype(BF16), r2(g_post_ffn), seq)
    return out.reshape(batch, seq, dm)


def kernel(x, mem, positions, g_pre_mix, w_in, b_gate, mla_q_norm, w_uq, mla_kv_norm, w_ukv, g_mem, w_mem_kv,
           w_br_mla, w_br_dil, w_br_mem, w_o, g_post_mix, g_pre_ffn, w_ffn_up, conv_w, conv_b, w_ffn_down,
           g_post_ffn):
    for l in range(w_in.shape[0]):
        x = _layer(x, mem, positions, g_pre_mix[l], w_in[l], b_gate[l], mla_q_norm[l], w_uq[l], mla_kv_norm[l],
                   w_ukv[l], g_mem[l], w_mem_kv[l], w_br_mla[l], w_br_dil[l], w_br_mem[l], w_o[l], g_post_mix[l],
                   g_pre_ffn[l], w_ffn_up[l], conv_w[l], conv_b[l], w_ffn_down[l], g_post_ffn[l])
    return x
```

```python
import functools

import jax
import jax.numpy as jnp
from jax import lax
from jax.experimental import pallas as pl
from jax.experimental.pallas import tpu as pltpu

F32 = jnp.float32
BF16 = jnp.bfloat16

RMS_EPS = 1e-6
LOG2E = 1.4426950408889634
NEG_INF = -1e30
LANES = 128

BLOCK = 128
MLA_HEADS = 8
MLA_NOPE = 64
MLA_ROPE = 32
MLA_V = 64
MLA_QK_DIM = MLA_NOPE + MLA_ROPE
MLA_Q_RANK = 384
MLA_KV_RANK = 256
ROPE_THETA = 10000.0
ROPE_HALF = MLA_ROPE // 2

DIL_PAIRS = ((128, 1), (512, 4), (2048, 16))
DIL_GROUPS = 3
DIL_HPG = 4
DIL_HEADS = DIL_GROUPS * DIL_HPG
DIL_HEAD_DIM = 128
DIL_WIDTH = DIL_HPG * DIL_HEAD_DIM

MEM_HEADS = 4
MEM_HEAD_DIM = 128
MEM_WIDTH = MEM_HEADS * MEM_HEAD_DIM

N_BRANCH = 3
CONV_WIDTH = 3

OFF_Q = MLA_Q_RANK
OFF_KV = OFF_Q + MLA_KV_RANK
OFF_KR = OFF_KV + MLA_ROPE
OFF_DIL = OFF_KR + 3 * DIL_HEADS * DIL_HEAD_DIM
OFF_MEMQ = OFF_DIL + MEM_WIDTH
ALIGNED_DIL = 1024
ALIGNED_MEMQ = ALIGNED_DIL + (OFF_DIL - OFF_KR)
ALIGNED_GATE = ALIGNED_MEMQ + MEM_WIDTH

PREP_TM = 512
DILPROJ_TM = 2048
DILPROJ_TN = 512
MLA_TQ = 1024
MLA_TK = 512
MLA_HPS = 2
MLA_QSTRIP = 256
MLA_VROWS = MLA_V + 16
assert PREP_TM == MLA_TK
DIL_TB = 512
DIL_AHEAD = 4
MERGE_TM = 512
FFN_TM = 512
FFN_TF = 256
FFN_HALO = 16

VMEM_LIMIT = 56 * 1024 * 1024


def _rms(xf, g):
    return xf * lax.rsqrt(jnp.mean(xf * xf, axis=-1, keepdims=True) + RMS_EPS) * g


def _dot(a, b):
    return jnp.dot(a, b, preferred_element_type=F32)


def _dot_nt(a, b):
    return lax.dot_general(a, b, (((1,), (1,)), ((), ())), preferred_element_type=F32)


def _const_spec(shape):
    nd = len(shape)
    return pl.BlockSpec(shape, lambda *_: (0,) * nd)


def _prep_kernel(x_ref, pos_ref, invf_ref, g_ref, wa_ref, qn_ref, wqmt_ref, wqst_ref, kvn_ref, wk_ref, wvt_ref,
                 h_ref, qt_ref, k_ref, vt_ref):
    tm = x_ref.shape[0]
    h = _rms(x_ref[...], g_ref[...]).astype(BF16)
    h_ref[...] = h
    p = _dot(h, wa_ref[...])
    cq = _rms(p[:, :OFF_Q], qn_ref[...]).astype(BF16)
    ckv = _rms(p[:, OFF_Q:OFF_KV], kvn_ref[...]).astype(BF16)

    ang = invf_ref[...] * pos_ref[0].astype(F32)
    c16 = jnp.cos(ang)
    s16 = jnp.sin(ang)
    ones = jnp.ones((MLA_NOPE, tm), F32)
    zeros_lo = jnp.zeros((MLA_NOPE, tm), F32)
    zeros_hi = jnp.zeros((LANES - MLA_QK_DIM, tm), F32)
    cos_c = jnp.concatenate([ones, c16, c16, zeros_hi], axis=0)
    sin_c = jnp.concatenate([zeros_lo, s16, s16, zeros_hi], axis=0)

    qmt = _dot_nt(wqmt_ref[...], cq)
    qst = _dot_nt(wqst_ref[...], cq)
    qscale = MLA_QK_DIM ** -0.5 * LOG2E
    for hd in range(MLA_HEADS):
        sl = slice(hd * LANES, (hd + 1) * LANES)
        qt_ref[sl, :] = ((qmt[sl, :] * cos_c + qst[sl, :] * sin_c) * qscale).astype(BF16)

    cos_r = cos_c.T
    sin_r = sin_c.T
    kn = _dot(ckv, wk_ref[...])
    kpe = p[:, OFF_KV:OFF_KV + LANES] * cos_r + p[:, OFF_KV + LANES:OFF_KV + 2 * LANES] * sin_r
    for hd in range(MLA_HEADS):
        sl = slice(hd * LANES, (hd + 1) * LANES)
        k_ref[:, sl] = (kn[:, sl] + kpe).astype(BF16)
    vt = _dot_nt(wvt_ref[...], ckv).astype(BF16)
    ones_rows = jnp.ones((MLA_VROWS - MLA_V, tm), BF16)
    for hd in range(MLA_HEADS):
        vt_ref[0, hd * MLA_VROWS:hd * MLA_VROWS + MLA_V, :] = vt[hd * MLA_V:(hd + 1) * MLA_V, :]
        vt_ref[0, hd * MLA_VROWS + MLA_V:(hd + 1) * MLA_VROWS, :] = ones_rows


def _prep(x2, pos_rows, invf, g_pre, w_a, q_norm, w_qmt, w_qst, kv_norm, w_k, w_vt, batch, seq):
    t, d = x2.shape
    tm = PREP_TM
    tps = seq // tm
    hw = MLA_HEADS * LANES
    vw = MLA_HEADS * MLA_VROWS
    row = lambda i: (i, 0)
    return pl.pallas_call(
        _prep_kernel,
        grid=(t // tm,),
        in_specs=[
            pl.BlockSpec((tm, d), row),
            pl.BlockSpec((1, 1, tm), lambda i: (i, 0, 0)),
            _const_spec(invf.shape),
            _const_spec(g_pre.shape),
            _const_spec(w_a.shape),
            _const_spec(q_norm.shape),
            _const_spec(w_qmt.shape),
            _const_spec(w_qst.shape),
            _const_spec(kv_norm.shape),
            _const_spec(w_k.shape),
            _const_spec(w_vt.shape),
        ],
        out_specs=[pl.BlockSpec((tm, d), row),
                   pl.BlockSpec((None, hw, tm), lambda i: (i // tps, 0, i % tps)),
                   pl.BlockSpec((tm, hw), row),
                   pl.BlockSpec((None, 1, vw, tm), lambda i: (i // tps, i % tps, 0, 0))],
        out_shape=[jax.ShapeDtypeStruct((t, d), BF16),
                   jax.ShapeDtypeStruct((batch, hw, seq), BF16),
                   jax.ShapeDtypeStruct((t, hw), BF16),
                   jax.ShapeDtypeStruct((batch, tps, vw, tm), BF16)],
        compiler_params=pltpu.CompilerParams(dimension_semantics=("arbitrary",), vmem_limit_bytes=VMEM_LIMIT),
        name="prep",
    )(x2, pos_rows, invf, g_pre, w_a, q_norm, w_qmt, w_qst, kv_norm, w_k, w_vt)


def _dilproj_kernel(h_ref, w_ref, o_ref, acc_ref, *, dil):
    acc = _dot(h_ref[...], w_ref[...])
    if dil == 1:
        o_ref[0] = acc.astype(BF16)
    else:
        chunks, tm, _ = acc_ref.shape
        rows = tm // dil
        for c in range(chunks):
            acc_ref[c] = acc[:, c * LANES:(c + 1) * LANES]
        for r in range(dil):
            for c in range(chunks):
                o_ref[r, :, c * LANES:(c + 1) * LANES] = acc_ref[c, pl.ds(r, rows, stride=dil), :].astype(BF16)


def _dilproj(h2, w_al, batch, seq, dil, group):
    t, d = h2.shape
    n = 3 * DIL_WIDTH
    tm, tn = DILPROJ_TM, DILPROJ_TN
    assert tn == DIL_WIDTH and ALIGNED_DIL % tn == 0
    tiles_per_seq = seq // tm
    col0 = ALIGNED_DIL // tn + group
    return pl.pallas_call(
        functools.partial(_dilproj_kernel, dil=dil),
        grid=(t // tm, n // tn),
        in_specs=[pl.BlockSpec((tm, d), lambda i, j: (i, 0)),
                  pl.BlockSpec((d, tn), lambda i, j: (0, col0 + j * DIL_GROUPS))],
        out_specs=pl.BlockSpec((None, dil, tm // dil, tn),
                               lambda i, j: (i // tiles_per_seq, 0, i % tiles_per_seq, j)),
        out_shape=jax.ShapeDtypeStruct((batch, dil, seq // dil, n), BF16),
        scratch_shapes=[pltpu.VMEM((tn // LANES, tm, LANES), F32)],
        compiler_params=pltpu.CompilerParams(dimension_semantics=("arbitrary", "arbitrary"),
                                             vmem_limit_bytes=VMEM_LIMIT),
        name=f"dilproj{dil}",
    )(h2, w_al)


def _mla_kernel(qt_ref, k_ref, vt_ref, o_ref, m_ref, acc_ref, sta_ref, stb_ref):
    tq = qt_ref.shape[1]
    tk = MLA_TK
    qi = pl.program_id(2)
    nstrip = tq // MLA_QSTRIP
    m_ref[...] = jnp.full(m_ref.shape, NEG_INF, F32)
    acc_ref[...] = jnp.zeros(acc_ref.shape, F32)
    chains = [(a, hq) for a in range(MLA_HPS) for hq in range(nstrip)]

    def scores(c, i):
        a, hq = chains[i]
        k = k_ref[pl.ds(pl.multiple_of(c * tk, tk), tk), a * LANES:(a + 1) * LANES]
        return _dot(k, qt_ref[a * LANES:(a + 1) * LANES, hq * MLA_QSTRIP:(hq + 1) * MLA_QSTRIP])

    def step(c, src_ref, dst_ref, key_off=None):
        def live(hq, off):
            return off is None or (hq + 1) * MLA_QSTRIP > off

        next_off = None if key_off is None else key_off + tk
        m_all = m_ref[...]
        acc_all = acc_ref[...]
        m_new, acc_new = {}, {}
        for i, (a, hq) in enumerate(chains):
            qs = slice(hq * MLA_QSTRIP, (hq + 1) * MLA_QSTRIP)
            if dst_ref is not None and live(hq, next_off):
                dst_ref[i] = scores(c + 1, i)
            if not live(hq, key_off):
                m_new[a, hq], acc_new[a, hq] = m_all[a, :, qs], acc_all[a, :, qs]
                continue
            st = src_ref[i]
            if key_off is not None and key_off + tk - 1 > hq * MLA_QSTRIP:
                key = lax.broadcasted_iota(jnp.int32, st.shape, 0) + key_off
                qry = lax.broadcasted_iota(jnp.int32, st.shape, 1) + hq * MLA_QSTRIP
                st = jnp.where(key <= qry, st, NEG_INF)
            m_prev = m_all[a, :, qs]
            m_cur = jnp.maximum(m_prev, jnp.max(st, axis=0, keepdims=True))
            alpha = jnp.exp2(m_prev - m_cur)
            p = jnp.exp2(st - m_cur).astype(BF16)
            vt = vt_ref[c, a * MLA_VROWS:(a + 1) * MLA_VROWS, :]
            acc_new[a, hq] = alpha * acc_all[a, :, qs] + _dot(vt, p)
            m_new[a, hq] = m_cur
        for a in range(MLA_HPS):
            m_ref[a] = jnp.concatenate([m_new[a, hq] for hq in range(nstrip)], axis=-1)
            acc_ref[a] = jnp.concatenate([acc_new[a, hq] for hq in range(nstrip)], axis=-1)

    for i in range(len(chains)):
        sta_ref[i] = scores(0, i)

    cpt = tq // tk
    assert cpt == 2

    def pair(j, carry):
        step(2 * j, sta_ref, stb_ref)
        step(2 * j + 1, stb_ref, sta_ref)
        return carry

    lax.fori_loop(0, qi, pair, 0)
    step(2 * qi, sta_ref, stb_ref, key_off=0)
    step(2 * qi + 1, stb_ref, None, key_off=tk)

    out_t = jnp.concatenate([acc_ref[a, :MLA_V, :] / acc_ref[a, MLA_V:MLA_V + 1, :] for a in range(MLA_HPS)],
                            axis=0)
    o_ref[...] = out_t.T.astype(BF16)


def _mla(qt, k3, vt):
    b, s, _ = k3.shape
    tq, tk = MLA_TQ, MLA_TK
    hps = MLA_HPS
    st_scratch = pltpu.VMEM((hps * (tq // MLA_QSTRIP), tk, MLA_QSTRIP), F32)
    return pl.pallas_call(
        _mla_kernel,
        grid=(b, MLA_HEADS // hps, s // tq),
        in_specs=[
            pl.BlockSpec((None, hps * LANES, tq), lambda bi, h, i: (bi, h, i)),
            pl.BlockSpec((None, s, hps * LANES), lambda bi, h, i: (bi, 0, h)),
            pl.BlockSpec((None, s // tk, hps * MLA_VROWS, tk), lambda bi, h, i: (bi, 0, h, 0)),
        ],
        out_specs=pl.BlockSpec((None, tq, hps * MLA_V), lambda bi, h, i: (bi, i, h)),
        out_shape=jax.ShapeDtypeStruct((b, s, MLA_HEADS * MLA_V), BF16),
        scratch_shapes=[pltpu.VMEM((hps, 1, tq), F32), pltpu.VMEM((hps, MLA_VROWS, tq), F32),
                        st_scratch, st_scratch],
        compiler_params=pltpu.CompilerParams(dimension_semantics=("arbitrary", "arbitrary", "arbitrary"),
                                             vmem_limit_bytes=VMEM_LIMIT),
        name="mla",
    )(qt, k3, vt)


def _dilattn_kernel(q_ref, kc_ref, vc_ref, kp_ref, vp_ref, o_ref, lse_ref, kx_ref, vx_ref, *, dil, group):
    n = pl.program_id(2)
    tb = q_ref.shape[0]
    nblk = tb // BLOCK
    kx_ref[:BLOCK, :] = kp_ref[...]
    kx_ref[BLOCK:, :] = kc_ref[...]
    vx_ref[:BLOCK, :] = vp_ref[...]
    vx_ref[BLOCK:, :] = vc_ref[...]

    qry = lax.broadcasted_iota(jnp.int32, (BLOCK, 2 * BLOCK), 0)
    key = lax.broadcasted_iota(jnp.int32, (BLOCK, 2 * BLOCK), 1)
    dist = qry + BLOCK - key
    in_window = jnp.logical_and(dist >= 0, dist <= BLOCK)
    first_ok = jnp.logical_and(in_window, jnp.logical_or(key >= BLOCK, n > 0))
    distf = (dist * dil).astype(F32)
    lane = lax.broadcasted_iota(jnp.int32, (BLOCK, LANES), 1)
    scale = DIL_HEAD_DIM ** -0.5
    bias, bias_first = [], []
    for hh in range(DIL_HPG):
        slope = float(2.0 ** (-8.0 * (hh * DIL_GROUPS + group + 1) / DIL_HEADS))
        bias.append(jnp.where(in_window, -slope * distf, NEG_INF))
        bias_first.append(jnp.where(first_ok, -slope * distf, NEG_INF))

    units = [(t, hh) for t in range(nblk) for hh in range(DIL_HPG)]

    def scores(u):
        t, hh = units[u]
        cs = slice(hh * DIL_HEAD_DIM, (hh + 1) * DIL_HEAD_DIM)
        return _dot_nt(q_ref[t * BLOCK:(t + 1) * BLOCK, cs], kx_ref[t * BLOCK:(t + 2) * BLOCK, cs])

    pending = [scores(u) for u in range(min(DIL_AHEAD, len(units)))]
    lse_tile = None
    for u, (t, hh) in enumerate(units):
        if u + DIL_AHEAD < len(units):
            pending.append(scores(u + DIL_AHEAD))
        rs = slice(t * BLOCK, (t + 1) * BLOCK)
        cs = slice(hh * DIL_HEAD_DIM, (hh + 1) * DIL_HEAD_DIM)
        s = pending[u] * scale + (bias_first[hh] if t == 0 else bias[hh])
        m = jnp.max(s, axis=-1, keepdims=True)
        e = jnp.exp(s - m)
        den = jnp.sum(e, axis=-1, keepdims=True)
        o = _dot(e.astype(BF16), vx_ref[t * BLOCK:(t + 2) * BLOCK, cs]) / den
        o_ref[rs, cs] = o.astype(BF16)
        lse = m + jnp.log(den)
        lse_tile = jnp.where(lane == hh, lse, jnp.zeros((BLOCK, LANES), F32) if hh == 0 else lse_tile)
        if hh == DIL_HPG - 1:
            lse_ref[rs, :] = lse_tile


def _dilattn(qkv, dil, group):
    b, d, l, _ = qkv.shape
    tb = DIL_TB
    bpt = tb // BLOCK
    w = DIL_WIDTH
    cur = lambda c: pl.BlockSpec((None, None, tb, w), lambda bi, r, n: (bi, r, n, c))
    prev = lambda c: pl.BlockSpec((None, None, BLOCK, w), lambda bi, r, n: (bi, r, jnp.maximum(n * bpt - 1, 0), c))
    return pl.pallas_call(
        functools.partial(_dilattn_kernel, dil=dil, group=group),
        grid=(b, d, l // tb),
        in_specs=[cur(0), cur(1), cur(2), prev(1), prev(2)],
        out_specs=[pl.BlockSpec((None, None, tb, w), lambda bi, r, n: (bi, r, n, 0)),
                   pl.BlockSpec((None, None, tb, LANES), lambda bi, r, n: (bi, r, n, 0))],
        out_shape=[jax.ShapeDtypeStruct((b, d, l, w), BF16), jax.ShapeDtypeStruct((b, d, l, LANES), F32)],
        scratch_shapes=[pltpu.VMEM((tb + BLOCK, w), BF16), pltpu.VMEM((tb + BLOCK, w), BF16)],
        compiler_params=pltpu.CompilerParams(dimension_semantics=("arbitrary", "arbitrary", "arbitrary"),
                                             vmem_limit_bytes=VMEM_LIMIT),
        name=f"dilattn{dil}",
    )(qkv, qkv, qkv, qkv, qkv)


def _memkv_kernel(mem_ref, g_ref, w_ref, o_ref):
    o_ref[...] = _dot(_rms(mem_ref[...], g_ref[...]).astype(BF16), w_ref[...]).astype(BF16)


def _memkv(mem2, g_mem, w):
    return pl.pallas_call(
        _memkv_kernel,
        out_shape=jax.ShapeDtypeStruct((mem2.shape[0], w.shape[1]), BF16),
        compiler_params=pltpu.CompilerParams(vmem_limit_bytes=VMEM_LIMIT),
        name="memkv",
    )(mem2, g_mem, w)


def _sigmoid(z):
    return 1.0 / (1.0 + jnp.exp(-z))


def _merge_kernel(x_ref, h_ref, wmq_ref, wg_ref, bg_ref, kvm_ref, ymla_ref, wbm_ref,
                  o0_ref, o1_ref, o2_ref, l0_ref, l1_ref, l2_ref, wbd_ref, wbmem_ref, wo_ref, gpm_ref, gpf_ref,
                  x1_ref, h2_ref, nat1_ref, nat2_ref, lse1_ref, lse2_ref):
    tm, dm = x_ref.shape
    h = h_ref[...]

    for src, lsrc, dst, ldst in ((o1_ref, l1_ref, nat1_ref, lse1_ref), (o2_ref, l2_ref, nat2_ref, lse2_ref)):
        d = src.shape[0]
        rows = src.shape[1]
        for r in range(d):
            for hh in range(DIL_HPG):
                cs = slice(hh * DIL_HEAD_DIM, (hh + 1) * DIL_HEAD_DIM)
                dst[hh, pl.ds(r, rows, stride=d), :] = src[r, :, cs].astype(F32)
            ldst[pl.ds(r, rows, stride=d), :] = lsrc[r]
    lg = (l0_ref[0], lse1_ref[...], lse2_ref[...])
    heads = []
    for hh in range(DIL_HPG):
        cs = slice(hh * DIL_HEAD_DIM, (hh + 1) * DIL_HEAD_DIM)
        og = (o0_ref[0, :, cs].astype(F32), nat1_ref[hh], nat2_ref[hh])
        ls = [l[:, hh:hh + 1] for l in lg]
        mx = jnp.maximum(jnp.maximum(ls[0], ls[1]), ls[2])
        ws = [jnp.exp(l - mx) for l in ls]
        num = ws[0] * og[0] + ws[1] * og[1] + ws[2] * og[2]
        heads.append((num / (ws[0] + ws[1] + ws[2])).astype(BF16))
    y_dil = jnp.concatenate(heads, axis=-1)

    memq = _dot(h, wmq_ref[...])
    mheads = []
    for hh in range(MEM_HEADS):
        cs = slice(hh * MEM_HEAD_DIM, (hh + 1) * MEM_HEAD_DIM)
        q = (memq[:, cs] * MEM_HEAD_DIM ** -0.5).astype(BF16)
        s = _dot_nt(q, kvm_ref[:, cs])
        e = jnp.exp(s - jnp.max(s, axis=-1, keepdims=True))
        o = _dot(e.astype(BF16), kvm_ref[:, MEM_WIDTH + hh * MEM_HEAD_DIM:MEM_WIDTH + (hh + 1) * MEM_HEAD_DIM])
        mheads.append((o / jnp.sum(e, axis=-1, keepdims=True)).astype(BF16))
    y_mem = jnp.concatenate(mheads, axis=-1)

    merged = jnp.zeros((tm, dm), F32)
    for br, (y, w_ref) in enumerate(((ymla_ref[...], wbm_ref), (y_dil, wbd_ref), (y_mem, wbmem_ref))):
        cs = slice(br * dm, (br + 1) * dm)
        gate = _sigmoid(_dot(h, wg_ref[:, cs]) + bg_ref[:, cs])
        merged = merged + gate * _dot(y, w_ref[...])
    mixed = _dot(merged.astype(BF16), wo_ref[...])
    x1 = x_ref[...] + _rms(mixed, gpm_ref[...])
    x1_ref[...] = x1
    h2_ref[...] = _rms(x1, gpf_ref[...]).astype(BF16)


def _merge(x2, h2, w_al, b_g, kvm, y_mla, w_bm, o_dil, lse_dil, w_bd, w_bmem, w_o, g_pm, g_pf, batch, seq):
    t, dm = x2.shape
    tm = MERGE_TM
    tps = seq // tm
    n_mem = kvm.shape[0] // batch
    row = lambda i: (i, 0)
    single = pl.Buffered(1)
    const = lambda a: pl.BlockSpec(a.shape, lambda i: (0,) * a.ndim, pipeline_mode=single)
    gate_w = N_BRANCH * dm
    assert ALIGNED_MEMQ % MEM_WIDTH == 0 and ALIGNED_GATE % gate_w == 0
    w_mq = pl.BlockSpec((dm, MEM_WIDTH), lambda i: (0, ALIGNED_MEMQ // MEM_WIDTH), pipeline_mode=single)
    w_g = pl.BlockSpec((dm, gate_w), lambda i: (0, ALIGNED_GATE // gate_w), pipeline_mode=single)

    def dil_spec(a):
        d, width = a.shape[1], a.shape[3]
        return pl.BlockSpec((None, d, tm // d, width), lambda i: (i // tps, 0, i % tps, 0))

    return pl.pallas_call(
        _merge_kernel,
        grid=(t // tm,),
        in_specs=[
            pl.BlockSpec((tm, dm), row), pl.BlockSpec((tm, dm), row), w_mq, w_g, const(b_g),
            pl.BlockSpec((n_mem, kvm.shape[1]), lambda i: (i // tps, 0)),
            pl.BlockSpec((tm, y_mla.shape[1]), row), const(w_bm),
            dil_spec(o_dil[0]), dil_spec(o_dil[1]), dil_spec(o_dil[2]),
            dil_spec(lse_dil[0]), dil_spec(lse_dil[1]), dil_spec(lse_dil[2]),
            const(w_bd), const(w_bmem), const(w_o), const(g_pm), const(g_pf),
        ],
        out_specs=[pl.BlockSpec((tm, dm), row), pl.BlockSpec((tm, dm), row)],
        out_shape=[jax.ShapeDtypeStruct((t, dm), F32), jax.ShapeDtypeStruct((t, dm), BF16)],
        scratch_shapes=[pltpu.VMEM((DIL_HPG, tm, DIL_HEAD_DIM), F32), pltpu.VMEM((DIL_HPG, tm, DIL_HEAD_DIM), F32),
                        pltpu.VMEM((tm, LANES), F32), pltpu.VMEM((tm, LANES), F32)],
        compiler_params=pltpu.CompilerParams(dimension_semantics=("arbitrary",), vmem_limit_bytes=VMEM_LIMIT),
        name="merge",
    )(x2, h2, w_al, w_al, b_g, kvm, y_mla, w_bm, o_dil[0], o_dil[1], o_dil[2], lse_dil[0], lse_dil[1], lse_dil[2],
      w_bd, w_bmem, w_o, g_pm, g_pf)


def _ffn_kernel(x1_ref, h2_ref, halo_ref, wup_ref, cw_ref, cb_ref, wd_ref, gpost_ref,
                out_ref, hcat_ref, ua_ref, ub_ref, acc_ref, *, tiles_per_seq):
    i = pl.program_id(0)
    tm = x1_ref.shape[0]
    halo, tf = FFN_HALO, FFN_TF
    dff = wd_ref.shape[0]
    nchunk = dff // tf
    lanes_per_chunk = tf // LANES

    first = (i % tiles_per_seq) == 0
    hcat_ref[:halo, :] = jnp.where(first, jnp.zeros_like(halo_ref[...]), halo_ref[...])
    hcat_ref[halo:, :] = h2_ref[...]

    def up(c, u_ref):
        hc = hcat_ref[...]
        for part, off in enumerate((c * tf, dff + c * tf)):
            u = _dot(hc, wup_ref[:, off:off + tf])
            for j in range(lanes_per_chunk):
                u_ref[part * lanes_per_chunk + j] = u[:, j * LANES:(j + 1) * LANES]

    def conv(u_ref, slab, col):
        cols = slice(col, col + LANES)
        z = cb_ref[:, cols] + cw_ref[0:1, cols] * u_ref[slab, halo - 2:halo - 2 + tm, :]
        z = z + cw_ref[1:2, cols] * u_ref[slab, halo - 1:halo - 1 + tm, :]
        return z + cw_ref[2:3, cols] * u_ref[slab, halo:halo + tm, :]

    bufs = (ua_ref, ub_ref)
    up(0, bufs[0])
    for c in range(nchunk):
        cur = bufs[c % 2]
        if c + 1 < nchunk:
            up(c + 1, bufs[(c + 1) % 2])
        acts = []
        for j in range(lanes_per_chunk):
            gate = conv(cur, j, c * tf + j * LANES)
            val = conv(cur, lanes_per_chunk + j, dff + c * tf + j * LANES)
            acts.append((gate * _sigmoid(gate) * val).astype(BF16))
        down = _dot(jnp.concatenate(acts, axis=-1), wd_ref[c * tf:(c + 1) * tf, :])
        if c == 0:
            acc_ref[...] = down
        else:
            acc_ref[...] += down

    out_ref[...] = x1_ref[...] + _rms(acc_ref[...], gpost_ref[...])


def _ffn(x1, h2, w_up, conv_w, conv_b, w_down, g_post, seq):
    t, dm = x1.shape
    tm, tf, halo = FFN_TM, FFN_TF, FFN_HALO
    tps = seq // tm
    row = lambda i: (i, 0)
    const = lambda a: pl.BlockSpec(a.shape, lambda i: (0,) * a.ndim, pipeline_mode=pl.Buffered(1))
    u_scratch = pltpu.VMEM((2 * tf // LANES, tm + halo, LANES), F32)
    return pl.pallas_call(
        functools.partial(_ffn_kernel, tiles_per_seq=tps),
        grid=(t // tm,),
        in_specs=[
            pl.BlockSpec((tm, dm), row),
            pl.BlockSpec((tm, dm), row),
            pl.BlockSpec((halo, dm), lambda i: (jnp.maximum(i * (tm // halo) - 1, 0), 0)),
            const(w_up), const(conv_w), const(conv_b), const(w_down), const(g_post),
        ],
        out_specs=pl.BlockSpec((tm, dm), row),
        out_shape=jax.ShapeDtypeStruct((t, dm), F32),
        scratch_shapes=[pltpu.VMEM((tm + halo, dm), BF16), u_scratch, u_scratch, pltpu.VMEM((tm, dm), F32)],
        compiler_params=pltpu.CompilerParams(dimension_semantics=("arbitrary",), vmem_limit_bytes=VMEM_LIMIT),
        name="ffn",
    )(x1, h2, h2, w_up, conv_w, conv_b, w_down, g_post)


def _rot_half_cols(w):
    return jnp.concatenate([-w[..., ROPE_HALF:], w[..., :ROPE_HALF]], axis=-1)


def _prep_weights(w_in, w_uq, w_ukv):
    dm = w_in.shape[0]
    pad_hi = LANES - MLA_QK_DIM
    kr = w_in[:, OFF_KV:OFF_KR]
    z_lo = jnp.zeros((dm, MLA_NOPE), F32)
    z_hi = jnp.zeros((dm, pad_hi), F32)
    w_a = jnp.concatenate([w_in[:, :OFF_KV], z_lo, kr, z_hi, z_lo, _rot_half_cols(kr), z_hi], axis=1)

    uq = w_uq.reshape(MLA_Q_RANK, MLA_HEADS, MLA_QK_DIM)
    zq_lo = jnp.zeros((MLA_Q_RANK, MLA_HEADS, MLA_NOPE), F32)
    zq_hi = jnp.zeros((MLA_Q_RANK, MLA_HEADS, pad_hi), F32)
    w_qm = jnp.concatenate([uq, zq_hi], axis=-1).reshape(MLA_Q_RANK, MLA_HEADS * LANES)
    w_qs = jnp.concatenate([zq_lo, _rot_half_cols(uq[..., MLA_NOPE:]), zq_hi], axis=-1)
    w_qs = w_qs.reshape(MLA_Q_RANK, MLA_HEADS * LANES)

    ukv = w_ukv.reshape(MLA_KV_RANK, MLA_HEADS, MLA_NOPE + MLA_V)
    zk = jnp.zeros((MLA_KV_RANK, MLA_HEADS, LANES - MLA_NOPE), F32)
    w_k = jnp.concatenate([ukv[..., :MLA_NOPE], zk], axis=-1).reshape(MLA_KV_RANK, MLA_HEADS * LANES)
    w_v = ukv[..., MLA_NOPE:].reshape(MLA_KV_RANK, MLA_HEADS * MLA_V)
    return tuple(a.astype(BF16) for a in (w_a, w_qm.T, w_qs.T, w_k, w_v.T))


def _layer(x, mem, positions, g_pre_mix, w_in, b_gate, mla_q_norm, w_uq, mla_kv_norm, w_ukv, g_mem, w_mem_kv,
           w_br_mla, w_br_dil, w_br_mem, w_o, g_post_mix, g_pre_ffn, w_ffn_up, conv_w, conv_b, w_ffn_down,
           g_post_ffn):
    batch, seq, dm = x.shape
    t = batch * seq
    x2 = x.reshape(t, dm)
    r2 = lambda v: v.reshape(1, -1)

    w_a, w_qmt, w_qst, w_k, w_vt = _prep_weights(w_in, w_uq, w_ukv)
    invf = (ROPE_THETA ** (-jnp.arange(ROPE_HALF, dtype=F32) / ROPE_HALF)).reshape(ROPE_HALF, 1)
    pos_rows = positions.reshape(t // PREP_TM, 1, PREP_TM)

    h2d, qt, k, vt = _prep(x2, pos_rows, invf, r2(g_pre_mix), w_a, r2(mla_q_norm), w_qmt, w_qst,
                           r2(mla_kv_norm), w_k, w_vt, batch, seq)
    y_mla = _mla(qt, k.reshape(batch, seq, MLA_HEADS * LANES), vt).reshape(t, MLA_HEADS * MLA_V)

    w_al = jnp.concatenate([w_in[:, :OFF_KR], jnp.zeros((dm, ALIGNED_DIL - OFF_KR), F32), w_in[:, OFF_KR:]],
                           axis=1).astype(BF16)
    o_dil, lse_dil = [], []
    for g, (_, dil) in enumerate(DIL_PAIRS):
        qkv = _dilproj(h2d, w_al, batch, seq, dil, g)
        o, lse = _dilattn(qkv, dil, g)
        o_dil.append(o)
        lse_dil.append(lse)

    kvm = _memkv(mem.reshape(-1, dm), r2(g_mem), w_mem_kv.astype(BF16))

    x1, h2 = _merge(x2, h2d, w_al, r2(b_gate), kvm, y_mla, w_br_mla.astype(BF16), o_dil, lse_dil,
                    w_br_dil.astype(BF16),
                    w_br_mem.astype(BF16), w_o.astype(BF16), r2(g_post_mix), r2(g_pre_ffn), batch, seq)

    out = _ffn(x1, h2, w_ffn_up.astype(BF16), conv_w, r2(conv_b), w_ffn_down.astype(BF16), r2(g_post_ffn), seq)
    return out.reshape(batch, seq, dm)


def kernel(x, mem, positions, g_pre_mix, w_in, b_gate, mla_q_norm, w_uq, mla_kv_norm, w_ukv, g_mem, w_mem_kv,
           w_br_mla, w_br_dil, w_br_mem, w_o, g_post_mix, g_pre_ffn, w_ffn_up, conv_w, conv_b, w_ffn_down,
           g_post_ffn):
    for l in range(w_in.shape[0]):
        x = _layer(x, mem, positions, g_pre_mix[l], w_in[l], b_gate[l], mla_q_norm[l], w_uq[l], mla_kv_norm[l],
                   w_ukv[l], g_mem[l], w_mem_kv[l], w_br_mla[l], w_br_dil[l], w_br_mem[l], w_o[l], g_post_mix[l],
                   g_pre_ffn[l], w_ffn_up[l], conv_w[l], conv_b[l], w_ffn_down[l], g_post_ffn[l])
    return x
```

```python
import functools

import jax
import jax.numpy as jnp
from jax import lax
from jax.experimental import pallas as pl
from jax.experimental.pallas import tpu as pltpu

F32 = jnp.float32
BF16 = jnp.bfloat16

RMS_EPS = 1e-6
LOG2E = 1.4426950408889634
NEG_INF = -1e30
LANES = 128

BLOCK = 128
MLA_HEADS = 8
MLA_NOPE = 64
MLA_ROPE = 32
MLA_V = 64
MLA_QK_DIM = MLA_NOPE + MLA_ROPE
MLA_Q_RANK = 384
MLA_KV_RANK = 256
ROPE_THETA = 10000.0
ROPE_HALF = MLA_ROPE // 2

DIL_PAIRS = ((128, 1), (512, 4), (2048, 16))
DIL_GROUPS = 3
DIL_HPG = 4
DIL_HEADS = DIL_GROUPS * DIL_HPG
DIL_HEAD_DIM = 128
DIL_WIDTH = DIL_HPG * DIL_HEAD_DIM

MEM_HEADS = 4
MEM_HEAD_DIM = 128
MEM_WIDTH = MEM_HEADS * MEM_HEAD_DIM

N_BRANCH = 3
CONV_WIDTH = 3

OFF_Q = MLA_Q_RANK
OFF_KV = OFF_Q + MLA_KV_RANK
OFF_KR = OFF_KV + MLA_ROPE
OFF_DIL = OFF_KR + 3 * DIL_HEADS * DIL_HEAD_DIM
OFF_MEMQ = OFF_DIL + MEM_WIDTH
ALIGNED_DIL = 1024
ALIGNED_MEMQ = ALIGNED_DIL + (OFF_DIL - OFF_KR)
ALIGNED_GATE = ALIGNED_MEMQ + MEM_WIDTH

PREP_TM = 512
DILPROJ_TM = 2048
DILPROJ_TN = 512
ALIGN_TR = 128
MLA_TQ = 1024
MLA_TK = 512
MLA_HPS = 2
MLA_QSTRIP = 256
MLA_VROWS = MLA_V + 16
assert PREP_TM == MLA_TK
DIL_TB = 512
DIL_AHEAD = 4
MERGE_TM = 512
FFN_TM = 512
FFN_TF = 256
FFN_HALO = 16

VMEM_LIMIT = 56 * 1024 * 1024


def _rms(xf, g):
    return xf * lax.rsqrt(jnp.mean(xf * xf, axis=-1, keepdims=True) + RMS_EPS) * g


def _dot(a, b):
    return jnp.dot(a, b, preferred_element_type=F32)


def _dot_nt(a, b):
    return lax.dot_general(a, b, (((1,), (1,)), ((), ())), preferred_element_type=F32)


def _const_spec(shape):
    nd = len(shape)
    return pl.BlockSpec(shape, lambda *_: (0,) * nd)


def _prep_kernel(x_ref, pos_ref, invf_ref, g_ref, wa_ref, qn_ref, wqmt_ref, wqst_ref, kvn_ref, wk_ref, wvt_ref,
                 h_ref, qt_ref, k_ref, vt_ref):
    tm = x_ref.shape[0]
    h = _rms(x_ref[...], g_ref[...]).astype(BF16)
    h_ref[...] = h
    p = _dot(h, wa_ref[...])
    cq = _rms(p[:, :OFF_Q], qn_ref[...]).astype(BF16)
    ckv = _rms(p[:, OFF_Q:OFF_KV], kvn_ref[...]).astype(BF16)

    ang = invf_ref[...] * pos_ref[0].astype(F32)
    c16 = jnp.cos(ang)
    s16 = jnp.sin(ang)
    ones = jnp.ones((MLA_NOPE, tm), F32)
    zeros_lo = jnp.zeros((MLA_NOPE, tm), F32)
    zeros_hi = jnp.zeros((LANES - MLA_QK_DIM, tm), F32)
    cos_c = jnp.concatenate([ones, c16, c16, zeros_hi], axis=0)
    sin_c = jnp.concatenate([zeros_lo, s16, s16, zeros_hi], axis=0)

    qmt = _dot_nt(wqmt_ref[...], cq)
    qst = _dot_nt(wqst_ref[...], cq)
    qscale = MLA_QK_DIM ** -0.5 * LOG2E
    for hd in range(MLA_HEADS):
        sl = slice(hd * LANES, (hd + 1) * LANES)
        qt_ref[sl, :] = ((qmt[sl, :] * cos_c + qst[sl, :] * sin_c) * qscale).astype(BF16)

    cos_r = cos_c.T
    sin_r = sin_c.T
    kn = _dot(ckv, wk_ref[...])
    kpe = p[:, OFF_KV:OFF_KV + LANES] * cos_r + p[:, OFF_KV + LANES:OFF_KV + 2 * LANES] * sin_r
    for hd in range(MLA_HEADS):
        sl = slice(hd * LANES, (hd + 1) * LANES)
        k_ref[:, sl] = (kn[:, sl] + kpe).astype(BF16)
    vt = _dot_nt(wvt_ref[...], ckv).astype(BF16)
    ones_rows = jnp.ones((MLA_VROWS - MLA_V, tm), BF16)
    for hd in range(MLA_HEADS):
        vt_ref[0, hd * MLA_VROWS:hd * MLA_VROWS + MLA_V, :] = vt[hd * MLA_V:(hd + 1) * MLA_V, :]
        vt_ref[0, hd * MLA_VROWS + MLA_V:(hd + 1) * MLA_VROWS, :] = ones_rows


def _prep(x2, pos_rows, invf, g_pre, w_al, q_norm, w_qmt, w_qst, kv_norm, w_k, w_vt, batch, seq):
    t, d = x2.shape
    tm = PREP_TM
    tps = seq // tm
    hw = MLA_HEADS * LANES
    vw = MLA_HEADS * MLA_VROWS
    row = lambda i: (i, 0)
    return pl.pallas_call(
        _prep_kernel,
        grid=(t // tm,),
        in_specs=[
            pl.BlockSpec((tm, d), row),
            pl.BlockSpec((1, 1, tm), lambda i: (i, 0, 0)),
            _const_spec(invf.shape),
            _const_spec(g_pre.shape),
            pl.BlockSpec((d, OFF_KV + 2 * LANES), lambda i: (0, 0)),
            _const_spec(q_norm.shape),
            _const_spec(w_qmt.shape),
            _const_spec(w_qst.shape),
            _const_spec(kv_norm.shape),
            _const_spec(w_k.shape),
            _const_spec(w_vt.shape),
        ],
        out_specs=[pl.BlockSpec((tm, d), row),
                   pl.BlockSpec((None, hw, tm), lambda i: (i // tps, 0, i % tps)),
                   pl.BlockSpec((tm, hw), row),
                   pl.BlockSpec((None, 1, vw, tm), lambda i: (i // tps, i % tps, 0, 0))],
        out_shape=[jax.ShapeDtypeStruct((t, d), BF16),
                   jax.ShapeDtypeStruct((batch, hw, seq), BF16),
                   jax.ShapeDtypeStruct((t, hw), BF16),
                   jax.ShapeDtypeStruct((batch, tps, vw, tm), BF16)],
        compiler_params=pltpu.CompilerParams(dimension_semantics=("arbitrary",), vmem_limit_bytes=VMEM_LIMIT),
        name="prep",
    )(x2, pos_rows, invf, g_pre, w_al, q_norm, w_qmt, w_qst, kv_norm, w_k, w_vt)


def _dilproj_kernel(h_ref, w_ref, o_ref, acc_ref, *, dil):
    acc = _dot(h_ref[...], w_ref[...])
    if dil == 1:
        o_ref[0] = acc.astype(BF16)
    else:
        chunks, tm, _ = acc_ref.shape
        rows = tm // dil
        for c in range(chunks):
            acc_ref[c] = acc[:, c * LANES:(c + 1) * LANES]
        for r in range(dil):
            for c in range(chunks):
                o_ref[r, :, c * LANES:(c + 1) * LANES] = acc_ref[c, pl.ds(r, rows, stride=dil), :].astype(BF16)


def _dilproj(h2, w_al, batch, seq, dil, group):
    t, d = h2.shape
    n = 3 * DIL_WIDTH
    tm, tn = DILPROJ_TM, DILPROJ_TN
    assert tn == DIL_WIDTH and ALIGNED_DIL % tn == 0
    tiles_per_seq = seq // tm
    col0 = ALIGNED_DIL // tn + group
    return pl.pallas_call(
        functools.partial(_dilproj_kernel, dil=dil),
        grid=(t // tm, n // tn),
        in_specs=[pl.BlockSpec((tm, d), lambda i, j: (i, 0)),
                  pl.BlockSpec((d, tn), lambda i, j: (0, col0 + j * DIL_GROUPS))],
        out_specs=pl.BlockSpec((None, dil, tm // dil, tn),
                               lambda i, j: (i // tiles_per_seq, 0, i % tiles_per_seq, j)),
        out_shape=jax.ShapeDtypeStruct((batch, dil, seq // dil, n), BF16),
        scratch_shapes=[pltpu.VMEM((tn // LANES, tm, LANES), F32)],
        compiler_params=pltpu.CompilerParams(dimension_semantics=("arbitrary", "arbitrary"),
                                             vmem_limit_bytes=VMEM_LIMIT),
        name=f"dilproj{dil}",
    )(h2, w_al)


def _mla_kernel(qt_ref, k_ref, vt_ref, o_ref, m_ref, acc_ref, sta_ref, stb_ref):
    tq = qt_ref.shape[1]
    tk = MLA_TK
    qi = pl.program_id(2)
    nstrip = tq // MLA_QSTRIP
    m_ref[...] = jnp.full(m_ref.shape, NEG_INF, F32)
    acc_ref[...] = jnp.zeros(acc_ref.shape, F32)
    chains = [(a, hq) for a in range(MLA_HPS) for hq in range(nstrip)]

    def scores(c, i):
        a, hq = chains[i]
        k = k_ref[pl.ds(pl.multiple_of(c * tk, tk), tk), a * LANES:(a + 1) * LANES]
        return _dot(k, qt_ref[a * LANES:(a + 1) * LANES, hq * MLA_QSTRIP:(hq + 1) * MLA_QSTRIP])

    def step(c, src_ref, dst_ref, key_off=None):
        def live(hq, off):
            return off is None or (hq + 1) * MLA_QSTRIP > off

        next_off = None if key_off is None else key_off + tk
        m_all = m_ref[...]
        acc_all = acc_ref[...]
        m_new, acc_new = {}, {}
        for i, (a, hq) in enumerate(chains):
            qs = slice(hq * MLA_QSTRIP, (hq + 1) * MLA_QSTRIP)
            if dst_ref is not None and live(hq, next_off):
                dst_ref[i] = scores(c + 1, i)
            if not live(hq, key_off):
                m_new[a, hq], acc_new[a, hq] = m_all[a, :, qs], acc_all[a, :, qs]
                continue
            st = src_ref[i]
            if key_off is not None and key_off + tk - 1 > hq * MLA_QSTRIP:
                key = lax.broadcasted_iota(jnp.int32, st.shape, 0) + key_off
                qry = lax.broadcasted_iota(jnp.int32, st.shape, 1) + hq * MLA_QSTRIP
                st = jnp.where(key <= qry, st, NEG_INF)
            m_prev = m_all[a, :, qs]
            m_cur = jnp.maximum(m_prev, jnp.max(st, axis=0, keepdims=True))
            alpha = jnp.exp2(m_prev - m_cur)
            p = jnp.exp2(st - m_cur).astype(BF16)
            vt = vt_ref[c, a * MLA_VROWS:(a + 1) * MLA_VROWS, :]
            acc_new[a, hq] = alpha * acc_all[a, :, qs] + _dot(vt, p)
            m_new[a, hq] = m_cur
        for a in range(MLA_HPS):
            m_ref[a] = jnp.concatenate([m_new[a, hq] for hq in range(nstrip)], axis=-1)
            acc_ref[a] = jnp.concatenate([acc_new[a, hq] for hq in range(nstrip)], axis=-1)

    for i in range(len(chains)):
        sta_ref[i] = scores(0, i)

    cpt = tq // tk
    assert cpt == 2

    def pair(j, carry):
        step(2 * j, sta_ref, stb_ref)
        step(2 * j + 1, stb_ref, sta_ref)
        return carry

    lax.fori_loop(0, qi, pair, 0)
    step(2 * qi, sta_ref, stb_ref, key_off=0)
    step(2 * qi + 1, stb_ref, None, key_off=tk)

    out_t = jnp.concatenate([acc_ref[a, :MLA_V, :] / acc_ref[a, MLA_V:MLA_V + 1, :] for a in range(MLA_HPS)],
                            axis=0)
    o_ref[...] = out_t.T.astype(BF16)


def _mla(qt, k3, vt):
    b, s, _ = k3.shape
    tq, tk = MLA_TQ, MLA_TK
    hps = MLA_HPS
    st_scratch = pltpu.VMEM((hps * (tq // MLA_QSTRIP), tk, MLA_QSTRIP), F32)
    return pl.pallas_call(
        _mla_kernel,
        grid=(b, MLA_HEADS // hps, s // tq),
        in_specs=[
            pl.BlockSpec((None, hps * LANES, tq), lambda bi, h, i: (bi, h, i)),
            pl.BlockSpec((None, s, hps * LANES), lambda bi, h, i: (bi, 0, h)),
            pl.BlockSpec((None, s // tk, hps * MLA_VROWS, tk), lambda bi, h, i: (bi, 0, h, 0)),
        ],
        out_specs=pl.BlockSpec((None, tq, hps * MLA_V), lambda bi, h, i: (bi, i, h)),
        out_shape=jax.ShapeDtypeStruct((b, s, MLA_HEADS * MLA_V), BF16),
        scratch_shapes=[pltpu.VMEM((hps, 1, tq), F32), pltpu.VMEM((hps, MLA_VROWS, tq), F32),
                        st_scratch, st_scratch],
        compiler_params=pltpu.CompilerParams(dimension_semantics=("arbitrary", "arbitrary", "arbitrary"),
                                             vmem_limit_bytes=VMEM_LIMIT),
        name="mla",
    )(qt, k3, vt)


def _dilattn_kernel(q_ref, kc_ref, vc_ref, kp_ref, vp_ref, o_ref, lse_ref, kx_ref, vx_ref, *, dil, group):
    n = pl.program_id(2)
    tb = q_ref.shape[0]
    nblk = tb // BLOCK
    kx_ref[:BLOCK, :] = kp_ref[...]
    kx_ref[BLOCK:, :] = kc_ref[...]
    vx_ref[:BLOCK, :] = vp_ref[...]
    vx_ref[BLOCK:, :] = vc_ref[...]

    qry = lax.broadcasted_iota(jnp.int32, (BLOCK, 2 * BLOCK), 0)
    key = lax.broadcasted_iota(jnp.int32, (BLOCK, 2 * BLOCK), 1)
    dist = qry + BLOCK - key
    in_window = jnp.logical_and(dist >= 0, dist <= BLOCK)
    first_ok = jnp.logical_and(in_window, jnp.logical_or(key >= BLOCK, n > 0))
    distf = (dist * dil).astype(F32)
    lane = lax.broadcasted_iota(jnp.int32, (BLOCK, LANES), 1)
    scale = DIL_HEAD_DIM ** -0.5
    bias, bias_first = [], []
    for hh in range(DIL_HPG):
        slope = float(2.0 ** (-8.0 * (hh * DIL_GROUPS + group + 1) / DIL_HEADS))
        bias.append(jnp.where(in_window, -slope * distf, NEG_INF))
        bias_first.append(jnp.where(first_ok, -slope * distf, NEG_INF))

    units = [(t, hh) for t in range(nblk) for hh in range(DIL_HPG)]

    def scores(u):
        t, hh = units[u]
        cs = slice(hh * DIL_HEAD_DIM, (hh + 1) * DIL_HEAD_DIM)
        return _dot_nt(q_ref[t * BLOCK:(t + 1) * BLOCK, cs], kx_ref[t * BLOCK:(t + 2) * BLOCK, cs])

    pending = [scores(u) for u in range(min(DIL_AHEAD, len(units)))]
    lse_tile = None
    for u, (t, hh) in enumerate(units):
        if u + DIL_AHEAD < len(units):
            pending.append(scores(u + DIL_AHEAD))
        rs = slice(t * BLOCK, (t + 1) * BLOCK)
        cs = slice(hh * DIL_HEAD_DIM, (hh + 1) * DIL_HEAD_DIM)
        s = pending[u] * scale + (bias_first[hh] if t == 0 else bias[hh])
        m = jnp.max(s, axis=-1, keepdims=True)
        e = jnp.exp(s - m)
        den = jnp.sum(e, axis=-1, keepdims=True)
        o = _dot(e.astype(BF16), vx_ref[t * BLOCK:(t + 2) * BLOCK, cs]) / den
        o_ref[rs, cs] = o.astype(BF16)
        lse = m + jnp.log(den)
        lse_tile = jnp.where(lane == hh, lse, jnp.zeros((BLOCK, LANES), F32) if hh == 0 else lse_tile)
        if hh == DIL_HPG - 1:
            lse_ref[rs, :] = lse_tile


def _dilattn(qkv, dil, group):
    b, d, l, _ = qkv.shape
    tb = DIL_TB
    bpt = tb // BLOCK
    w = DIL_WIDTH
    cur = lambda c: pl.BlockSpec((None, None, tb, w), lambda bi, r, n: (bi, r, n, c))
    prev = lambda c: pl.BlockSpec((None, None, BLOCK, w), lambda bi, r, n: (bi, r, jnp.maximum(n * bpt - 1, 0), c))
    return pl.pallas_call(
        functools.partial(_dilattn_kernel, dil=dil, group=group),
        grid=(b, d, l // tb),
        in_specs=[cur(0), cur(1), cur(2), prev(1), prev(2)],
        out_specs=[pl.BlockSpec((None, None, tb, w), lambda bi, r, n: (bi, r, n, 0)),
                   pl.BlockSpec((None, None, tb, LANES), lambda bi, r, n: (bi, r, n, 0))],
        out_shape=[jax.ShapeDtypeStruct((b, d, l, w), BF16), jax.ShapeDtypeStruct((b, d, l, LANES), F32)],
        scratch_shapes=[pltpu.VMEM((tb + BLOCK, w), BF16), pltpu.VMEM((tb + BLOCK, w), BF16)],
        compiler_params=pltpu.CompilerParams(dimension_semantics=("arbitrary", "arbitrary", "arbitrary"),
                                             vmem_limit_bytes=VMEM_LIMIT),
        name=f"dilattn{dil}",
    )(qkv, qkv, qkv, qkv, qkv)


def _memkv_kernel(mem_ref, g_ref, w_ref, o_ref):
    o_ref[...] = _dot(_rms(mem_ref[...], g_ref[...]).astype(BF16), w_ref[...]).astype(BF16)


def _memkv(mem2, g_mem, w):
    return pl.pallas_call(
        _memkv_kernel,
        out_shape=jax.ShapeDtypeStruct((mem2.shape[0], w.shape[1]), BF16),
        compiler_params=pltpu.CompilerParams(vmem_limit_bytes=VMEM_LIMIT),
        name="memkv",
    )(mem2, g_mem, w)


def _sigmoid(z):
    return 1.0 / (1.0 + jnp.exp(-z))


def _merge_kernel(x_ref, h_ref, wmq_ref, wg_ref, bg_ref, kvm_ref, ymla_ref, wbm_ref,
                  o0_ref, o1_ref, o2_ref, l0_ref, l1_ref, l2_ref, wbd_ref, wbmem_ref, wo_ref, gpm_ref, gpf_ref,
                  x1_ref, h2_ref, nat1_ref, nat2_ref, lse1_ref, lse2_ref):
    tm, dm = x_ref.shape
    h = h_ref[...]

    for src, lsrc, dst, ldst in ((o1_ref, l1_ref, nat1_ref, lse1_ref), (o2_ref, l2_ref, nat2_ref, lse2_ref)):
        d = src.shape[0]
        rows = src.shape[1]
        for r in range(d):
            for hh in range(DIL_HPG):
                cs = slice(hh * DIL_HEAD_DIM, (hh + 1) * DIL_HEAD_DIM)
                dst[hh, pl.ds(r, rows, stride=d), :] = src[r, :, cs].astype(F32)
            ldst[pl.ds(r, rows, stride=d), :] = lsrc[r]
    lg = (l0_ref[0], lse1_ref[...], lse2_ref[...])
    heads = []
    for hh in range(DIL_HPG):
        cs = slice(hh * DIL_HEAD_DIM, (hh + 1) * DIL_HEAD_DIM)
        og = (o0_ref[0, :, cs].astype(F32), nat1_ref[hh], nat2_ref[hh])
        ls = [l[:, hh:hh + 1] for l in lg]
        mx = jnp.maximum(jnp.maximum(ls[0], ls[1]), ls[2])
        ws = [jnp.exp(l - mx) for l in ls]
        num = ws[0] * og[0] + ws[1] * og[1] + ws[2] * og[2]
        heads.append((num / (ws[0] + ws[1] + ws[2])).astype(BF16))
    y_dil = jnp.concatenate(heads, axis=-1)

    memq = _dot(h, wmq_ref[...])
    mheads = []
    for hh in range(MEM_HEADS):
        cs = slice(hh * MEM_HEAD_DIM, (hh + 1) * MEM_HEAD_DIM)
        q = (memq[:, cs] * MEM_HEAD_DIM ** -0.5).astype(BF16)
        s = _dot_nt(q, kvm_ref[:, cs])
        e = jnp.exp(s - jnp.max(s, axis=-1, keepdims=True))
        o = _dot(e.astype(BF16), kvm_ref[:, MEM_WIDTH + hh * MEM_HEAD_DIM:MEM_WIDTH + (hh + 1) * MEM_HEAD_DIM])
        mheads.append((o / jnp.sum(e, axis=-1, keepdims=True)).astype(BF16))
    y_mem = jnp.concatenate(mheads, axis=-1)

    merged = jnp.zeros((tm, dm), F32)
    for br, (y, w_ref) in enumerate(((ymla_ref[...], wbm_ref), (y_dil, wbd_ref), (y_mem, wbmem_ref))):
        cs = slice(br * dm, (br + 1) * dm)
        gate = _sigmoid(_dot(h, wg_ref[:, cs]) + bg_ref[:, cs])
        merged = merged + gate * _dot(y, w_ref[...])
    mixed = _dot(merged.astype(BF16), wo_ref[...])
    x1 = x_ref[...] + _rms(mixed, gpm_ref[...])
    x1_ref[...] = x1
    h2_ref[...] = _rms(x1, gpf_ref[...]).astype(BF16)


def _merge(x2, h2, w_al, b_g, kvm, y_mla, w_bm, o_dil, lse_dil, w_bd, w_bmem, w_o, g_pm, g_pf, batch, seq):
    t, dm = x2.shape
    tm = MERGE_TM
    tps = seq // tm
    n_mem = kvm.shape[0] // batch
    row = lambda i: (i, 0)
    single = pl.Buffered(1)
    const = lambda a: pl.BlockSpec(a.shape, lambda i: (0,) * a.ndim, pipeline_mode=single)
    gate_w = N_BRANCH * dm
    assert ALIGNED_MEMQ % MEM_WIDTH == 0 and ALIGNED_GATE % gate_w == 0
    w_mq = pl.BlockSpec((dm, MEM_WIDTH), lambda i: (0, ALIGNED_MEMQ // MEM_WIDTH), pipeline_mode=single)
    w_g = pl.BlockSpec((dm, gate_w), lambda i: (0, ALIGNED_GATE // gate_w), pipeline_mode=single)

    def dil_spec(a):
        d, width = a.shape[1], a.shape[3]
        return pl.BlockSpec((None, d, tm // d, width), lambda i: (i // tps, 0, i % tps, 0))

    return pl.pallas_call(
        _merge_kernel,
        grid=(t // tm,),
        in_specs=[
            pl.BlockSpec((tm, dm), row), pl.BlockSpec((tm, dm), row), w_mq, w_g, const(b_g),
            pl.BlockSpec((n_mem, kvm.shape[1]), lambda i: (i // tps, 0)),
            pl.BlockSpec((tm, y_mla.shape[1]), row), const(w_bm),
            dil_spec(o_dil[0]), dil_spec(o_dil[1]), dil_spec(o_dil[2]),
            dil_spec(lse_dil[0]), dil_spec(lse_dil[1]), dil_spec(lse_dil[2]),
            const(w_bd), const(w_bmem), const(w_o), const(g_pm), const(g_pf),
        ],
        out_specs=[pl.BlockSpec((tm, dm), row), pl.BlockSpec((tm, dm), row)],
        out_shape=[jax.ShapeDtypeStruct((t, dm), F32), jax.ShapeDtypeStruct((t, dm), BF16)],
        scratch_shapes=[pltpu.VMEM((DIL_HPG, tm, DIL_HEAD_DIM), F32), pltpu.VMEM((DIL_HPG, tm, DIL_HEAD_DIM), F32),
                        pltpu.VMEM((tm, LANES), F32), pltpu.VMEM((tm, LANES), F32)],
        compiler_params=pltpu.CompilerParams(dimension_semantics=("arbitrary",), vmem_limit_bytes=VMEM_LIMIT),
        name="merge",
    )(x2, h2, w_al, w_al, b_g, kvm, y_mla, w_bm, o_dil[0], o_dil[1], o_dil[2], lse_dil[0], lse_dil[1], lse_dil[2],
      w_bd, w_bmem, w_o, g_pm, g_pf)


def _ffn_kernel(x1_ref, h2_ref, halo_ref, wup_ref, cw_ref, cb_ref, wd_ref, gpost_ref,
                out_ref, hcat_ref, ua_ref, ub_ref, acc_ref, *, tiles_per_seq):
    i = pl.program_id(0)
    tm = x1_ref.shape[0]
    halo, tf = FFN_HALO, FFN_TF
    dff = wd_ref.shape[0]
    nchunk = dff // tf
    lanes_per_chunk = tf // LANES

    first = (i % tiles_per_seq) == 0
    hcat_ref[:halo, :] = jnp.where(first, jnp.zeros_like(halo_ref[...]), halo_ref[...])
    hcat_ref[halo:, :] = h2_ref[...]

    def up(c, u_ref):
        hc = hcat_ref[...]
        for part, off in enumerate((c * tf, dff + c * tf)):
            u = _dot(hc, wup_ref[:, off:off + tf])
            for j in range(lanes_per_chunk):
                u_ref[part * lanes_per_chunk + j] = u[:, j * LANES:(j + 1) * LANES]

    def conv(u_ref, slab, col):
        cols = slice(col, col + LANES)
        z = cb_ref[:, cols] + cw_ref[0:1, cols] * u_ref[slab, halo - 2:halo - 2 + tm, :]
        z = z + cw_ref[1:2, cols] * u_ref[slab, halo - 1:halo - 1 + tm, :]
        return z + cw_ref[2:3, cols] * u_ref[slab, halo:halo + tm, :]

    bufs = (ua_ref, ub_ref)
    up(0, bufs[0])
    for c in range(nchunk):
        cur = bufs[c % 2]
        if c + 1 < nchunk:
            up(c + 1, bufs[(c + 1) % 2])
        acts = []
        for j in range(lanes_per_chunk):
            gate = conv(cur, j, c * tf + j * LANES)
            val = conv(cur, lanes_per_chunk + j, dff + c * tf + j * LANES)
            acts.append((gate * _sigmoid(gate) * val).astype(BF16))
        down = _dot(jnp.concatenate(acts, axis=-1), wd_ref[c * tf:(c + 1) * tf, :])
        if c == 0:
            acc_ref[...] = down
        else:
            acc_ref[...] += down

    out_ref[...] = x1_ref[...] + _rms(acc_ref[...], gpost_ref[...])


def _ffn(x1, h2, w_up, conv_w, conv_b, w_down, g_post, seq):
    t, dm = x1.shape
    tm, tf, halo = FFN_TM, FFN_TF, FFN_HALO
    tps = seq // tm
    row = lambda i: (i, 0)
    const = lambda a: pl.BlockSpec(a.shape, lambda i: (0,) * a.ndim, pipeline_mode=pl.Buffered(1))
    u_scratch = pltpu.VMEM((2 * tf // LANES, tm + halo, LANES), F32)
    return pl.pallas_call(
        functools.partial(_ffn_kernel, tiles_per_seq=tps),
        grid=(t // tm,),
        in_specs=[
            pl.BlockSpec((tm, dm), row),
            pl.BlockSpec((tm, dm), row),
            pl.BlockSpec((halo, dm), lambda i: (jnp.maximum(i * (tm // halo) - 1, 0), 0)),
            const(w_up), const(conv_w), const(conv_b), const(w_down), const(g_post),
        ],
        out_specs=pl.BlockSpec((tm, dm), row),
        out_shape=jax.ShapeDtypeStruct((t, dm), F32),
        scratch_shapes=[pltpu.VMEM((tm + halo, dm), BF16), u_scratch, u_scratch, pltpu.VMEM((tm, dm), F32)],
        compiler_params=pltpu.CompilerParams(dimension_semantics=("arbitrary",), vmem_limit_bytes=VMEM_LIMIT),
        name="ffn",
    )(x1, h2, h2, w_up, conv_w, conv_b, w_down, g_post)


def _rot_half_cols(w):
    return jnp.concatenate([-w[..., ROPE_HALF:], w[..., :ROPE_HALF]], axis=-1)


def _align_kernel(w_ref, o_ref):
    tr = w_ref.shape[0]
    o_ref[:, :OFF_KV] = w_ref[:, :OFF_KV].astype(BF16)
    blk = w_ref[:, OFF_KV:OFF_KV + LANES]
    lane = lax.broadcasted_iota(jnp.int32, (tr, LANES), 1)
    in_lo = jnp.logical_and(lane >= MLA_NOPE, lane < MLA_NOPE + ROPE_HALF)
    in_hi = jnp.logical_and(lane >= MLA_NOPE + ROPE_HALF, lane < MLA_QK_DIM)
    placed = jnp.where(jnp.logical_or(in_lo, in_hi), pltpu.roll(blk, MLA_NOPE, 1), 0.0)
    second_half = pltpu.roll(blk, MLA_NOPE - ROPE_HALF, 1)
    first_half = pltpu.roll(blk, MLA_NOPE + ROPE_HALF, 1)
    rotated = jnp.where(in_lo, -second_half, jnp.where(in_hi, first_half, 0.0))
    o_ref[:, OFF_KV:OFF_KV + LANES] = placed.astype(BF16)
    o_ref[:, OFF_KV + LANES:OFF_KV + 2 * LANES] = rotated.astype(BF16)
    o_ref[:, OFF_KV + 2 * LANES:ALIGNED_DIL] = jnp.zeros((tr, ALIGNED_DIL - OFF_KV - 2 * LANES), BF16)
    o_ref[:, ALIGNED_DIL:] = w_ref[:, OFF_KR:].astype(BF16)


def _align_w_in(w_in):
    dm, d_in = w_in.shape
    tr = ALIGN_TR
    width = ALIGNED_DIL + d_in - OFF_KR
    return pl.pallas_call(
        _align_kernel,
        grid=(dm // tr,),
        in_specs=[pl.BlockSpec((tr, d_in), lambda i: (i, 0))],
        out_specs=pl.BlockSpec((tr, width), lambda i: (i, 0)),
        out_shape=jax.ShapeDtypeStruct((dm, width), BF16),
        compiler_params=pltpu.CompilerParams(dimension_semantics=("arbitrary",), vmem_limit_bytes=VMEM_LIMIT),
        name="align",
    )(w_in)


def _prep_weights(w_uq, w_ukv):
    pad_hi = LANES - MLA_QK_DIM

    uq = w_uq.reshape(MLA_Q_RANK, MLA_HEADS, MLA_QK_DIM)
    zq_lo = jnp.zeros((MLA_Q_RANK, MLA_HEADS, MLA_NOPE), F32)
    zq_hi = jnp.zeros((MLA_Q_RANK, MLA_HEADS, pad_hi), F32)
    w_qm = jnp.concatenate([uq, zq_hi], axis=-1).reshape(MLA_Q_RANK, MLA_HEADS * LANES)
    w_qs = jnp.concatenate([zq_lo, _rot_half_cols(uq[..., MLA_NOPE:]), zq_hi], axis=-1)
    w_qs = w_qs.reshape(MLA_Q_RANK, MLA_HEADS * LANES)

    ukv = w_ukv.reshape(MLA_KV_RANK, MLA_HEADS, MLA_NOPE + MLA_V)
    zk = jnp.zeros((MLA_KV_RANK, MLA_HEADS, LANES - MLA_NOPE), F32)
    w_k = jnp.concatenate([ukv[..., :MLA_NOPE], zk], axis=-1).reshape(MLA_KV_RANK, MLA_HEADS * LANES)
    w_v = ukv[..., MLA_NOPE:].reshape(MLA_KV_RANK, MLA_HEADS * MLA_V)
    return tuple(a.astype(BF16) for a in (w_qm.T, w_qs.T, w_k, w_v.T))


def _layer(x, mem, positions, g_pre_mix, w_in, b_gate, mla_q_norm, w_uq, mla_kv_norm, w_ukv, g_mem, w_mem_kv,
           w_br_mla, w_br_dil, w_br_mem, w_o, g_post_mix, g_pre_ffn, w_ffn_up, conv_w, conv_b, w_ffn_down,
           g_post_ffn):
    batch, seq, dm = x.shape
    t = batch * seq
    x2 = x.reshape(t, dm)
    r2 = lambda v: v.reshape(1, -1)

    w_al = _align_w_in(w_in)
    w_qmt, w_qst, w_k, w_vt = _prep_weights(w_uq, w_ukv)
    invf = (ROPE_THETA ** (-jnp.arange(ROPE_HALF, dtype=F32) / ROPE_HALF)).reshape(ROPE_HALF, 1)
    pos_rows = positions.reshape(t // PREP_TM, 1, PREP_TM)

    h2d, qt, k, vt = _prep(x2, pos_rows, invf, r2(g_pre_mix), w_al, r2(mla_q_norm), w_qmt, w_qst,
                           r2(mla_kv_norm), w_k, w_vt, batch, seq)
    y_mla = _mla(qt, k.reshape(batch, seq, MLA_HEADS * LANES), vt).reshape(t, MLA_HEADS * MLA_V)

    o_dil, lse_dil = [], []
    for g, (_, dil) in enumerate(DIL_PAIRS):
        qkv = _dilproj(h2d, w_al, batch, seq, dil, g)
        o, lse = _dilattn(qkv, dil, g)
        o_dil.append(o)
        lse_dil.append(lse)

    kvm = _memkv(mem.reshape(-1, dm), r2(g_mem), w_mem_kv.astype(BF16))

    x1, h2 = _merge(x2, h2d, w_al, r2(b_gate), kvm, y_mla, w_br_mla.astype(BF16), o_dil, lse_dil,
                    w_br_dil.astype(BF16),
                    w_br_mem.astype(BF16), w_o.astype(BF16), r2(g_post_mix), r2(g_pre_ffn), batch, seq)

    out = _ffn(x1, h2, w_ffn_up.astype(BF16), conv_w, r2(conv_b), w_ffn_down.astype(BF16), r2(g_post_ffn), seq)
    return out.reshape(batch, seq, dm)


def kernel(x, mem, positions, g_pre_mix, w_in, b_gate, mla_q_norm, w_uq, mla_kv_norm, w_ukv, g_mem, w_mem_kv,
           w_br_mla, w_br_dil, w_br_mem, w_o, g_post_mix, g_pre_ffn, w_ffn_up, conv_w, conv_b, w_ffn_down,
           g_post_ffn):
    for l in range(w_in.shape[0]):
        x = _layer(x, mem, positions, g_pre_mix[l], w_in[l], b_gate[l], mla_q_norm[l], w_uq[l], mla_kv_norm[l],
                   w_ukv[l], g_mem[l], w_mem_kv[l], w_br_mla[l], w_br_dil[l], w_br_mem[l], w_o[l], g_post_mix[l],
                   g_pre_ffn[l], w_ffn_up[l], conv_w[l], conv_b[l], w_ffn_down[l], g_post_ffn[l])
    return x
```

```python
import functools

import jax
import jax.numpy as jnp
from jax import lax
from jax.experimental import pallas as pl
from jax.experimental.pallas import tpu as pltpu

F32 = jnp.float32
BF16 = jnp.bfloat16

RMS_EPS = 1e-6
LOG2E = 1.4426950408889634
NEG_INF = -1e30
LANES = 128

BLOCK = 128
MLA_HEADS = 8
MLA_NOPE = 64
MLA_ROPE = 32
MLA_V = 64
MLA_QK_DIM = MLA_NOPE + MLA_ROPE
MLA_Q_RANK = 384
MLA_KV_RANK = 256
ROPE_THETA = 10000.0
ROPE_HALF = MLA_ROPE // 2

DIL_PAIRS = ((128, 1), (512, 4), (2048, 16))
DIL_GROUPS = 3
DIL_HPG = 4
DIL_HEADS = DIL_GROUPS * DIL_HPG
DIL_HEAD_DIM = 128
DIL_WIDTH = DIL_HPG * DIL_HEAD_DIM

MEM_HEADS = 4
MEM_HEAD_DIM = 128
MEM_WIDTH = MEM_HEADS * MEM_HEAD_DIM

N_BRANCH = 3
CONV_WIDTH = 3

OFF_Q = MLA_Q_RANK
OFF_KV = OFF_Q + MLA_KV_RANK
OFF_KR = OFF_KV + MLA_ROPE
OFF_DIL = OFF_KR + 3 * DIL_HEADS * DIL_HEAD_DIM
OFF_MEMQ = OFF_DIL + MEM_WIDTH
ALIGNED_DIL = 1024
ALIGNED_MEMQ = ALIGNED_DIL + (OFF_DIL - OFF_KR)
ALIGNED_GATE = ALIGNED_MEMQ + MEM_WIDTH

PREP_TM = 512
DILPROJ_TM = 1024
ALIGN_TR = 128
MLA_TQ = 1024
MLA_TK = 512
MLA_HPS = 2
MLA_QSTRIP = 256
MLA_VROWS = MLA_V + 16
assert PREP_TM == MLA_TK
DIL_TB = 512
DIL_AHEAD = 4
MERGE_TM = 512
FFN_TM = 512
FFN_TF = 256
FFN_HALO = 16

VMEM_LIMIT = 56 * 1024 * 1024


def _rms(xf, g):
    return xf * lax.rsqrt(jnp.mean(xf * xf, axis=-1, keepdims=True) + RMS_EPS) * g


def _dot(a, b):
    return jnp.dot(a, b, preferred_element_type=F32)


def _dot_nt(a, b):
    return lax.dot_general(a, b, (((1,), (1,)), ((), ())), preferred_element_type=F32)


def _const_spec(shape):
    nd = len(shape)
    return pl.BlockSpec(shape, lambda *_: (0,) * nd)


def _prep_kernel(x_ref, pos_ref, invf_ref, g_ref, wa_ref, qn_ref, wqmt_ref, wqst_ref, kvn_ref, wk_ref, wvt_ref,
                 h_ref, qt_ref, k_ref, vt_ref):
    tm = x_ref.shape[0]
    h = _rms(x_ref[...], g_ref[...]).astype(BF16)
    h_ref[...] = h
    p = _dot(h, wa_ref[...])
    cq = _rms(p[:, :OFF_Q], qn_ref[...]).astype(BF16)
    ckv = _rms(p[:, OFF_Q:OFF_KV], kvn_ref[...]).astype(BF16)

    ang = invf_ref[...] * pos_ref[0].astype(F32)
    c16 = jnp.cos(ang)
    s16 = jnp.sin(ang)
    ones = jnp.ones((MLA_NOPE, tm), F32)
    zeros_lo = jnp.zeros((MLA_NOPE, tm), F32)
    zeros_hi = jnp.zeros((LANES - MLA_QK_DIM, tm), F32)
    cos_c = jnp.concatenate([ones, c16, c16, zeros_hi], axis=0)
    sin_c = jnp.concatenate([zeros_lo, s16, s16, zeros_hi], axis=0)

    qmt = _dot_nt(wqmt_ref[...], cq)
    qst = _dot_nt(wqst_ref[...], cq)
    qscale = MLA_QK_DIM ** -0.5 * LOG2E
    for hd in range(MLA_HEADS):
        sl = slice(hd * LANES, (hd + 1) * LANES)
        qt_ref[sl, :] = ((qmt[sl, :] * cos_c + qst[sl, :] * sin_c) * qscale).astype(BF16)

    cos_r = cos_c.T
    sin_r = sin_c.T
    kn = _dot(ckv, wk_ref[...])
    kpe = p[:, OFF_KV:OFF_KV + LANES] * cos_r + p[:, OFF_KV + LANES:OFF_KV + 2 * LANES] * sin_r
    for hd in range(MLA_HEADS):
        sl = slice(hd * LANES, (hd + 1) * LANES)
        k_ref[:, sl] = (kn[:, sl] + kpe).astype(BF16)
    vt = _dot_nt(wvt_ref[...], ckv).astype(BF16)
    ones_rows = jnp.ones((MLA_VROWS - MLA_V, tm), BF16)
    for hd in range(MLA_HEADS):
        vt_ref[0, hd * MLA_VROWS:hd * MLA_VROWS + MLA_V, :] = vt[hd * MLA_V:(hd + 1) * MLA_V, :]
        vt_ref[0, hd * MLA_VROWS + MLA_V:(hd + 1) * MLA_VROWS, :] = ones_rows


def _prep(x2, pos_rows, invf, g_pre, w_al, q_norm, w_qmt, w_qst, kv_norm, w_k, w_vt, batch, seq):
    t, d = x2.shape
    tm = PREP_TM
    tps = seq // tm
    hw = MLA_HEADS * LANES
    vw = MLA_HEADS * MLA_VROWS
    row = lambda i: (i, 0)
    return pl.pallas_call(
        _prep_kernel,
        grid=(t // tm,),
        in_specs=[
            pl.BlockSpec((tm, d), row),
            pl.BlockSpec((1, 1, tm), lambda i: (i, 0, 0)),
            _const_spec(invf.shape),
            _const_spec(g_pre.shape),
            pl.BlockSpec((d, OFF_KV + 2 * LANES), lambda i: (0, 0)),
            _const_spec(q_norm.shape),
            _const_spec(w_qmt.shape),
            _const_spec(w_qst.shape),
            _const_spec(kv_norm.shape),
            _const_spec(w_k.shape),
            _const_spec(w_vt.shape),
        ],
        out_specs=[pl.BlockSpec((tm, d), row),
                   pl.BlockSpec((None, hw, tm), lambda i: (i // tps, 0, i % tps)),
                   pl.BlockSpec((tm, hw), row),
                   pl.BlockSpec((None, 1, vw, tm), lambda i: (i // tps, i % tps, 0, 0))],
        out_shape=[jax.ShapeDtypeStruct((t, d), BF16),
                   jax.ShapeDtypeStruct((batch, hw, seq), BF16),
                   jax.ShapeDtypeStruct((t, hw), BF16),
                   jax.ShapeDtypeStruct((batch, tps, vw, tm), BF16)],
        compiler_params=pltpu.CompilerParams(dimension_semantics=("arbitrary",), vmem_limit_bytes=VMEM_LIMIT),
        name="prep",
    )(x2, pos_rows, invf, g_pre, w_al, q_norm, w_qmt, w_qst, kv_norm, w_k, w_vt)


def _dilproj_kernel(h_ref, wq_ref, wk_ref, wv_ref, o_ref, *acc_refs, dil):
    h = h_ref[...]
    w_refs = (wq_ref, wk_ref, wv_ref)
    tm, tn = h_ref.shape[0], wq_ref.shape[1]
    if dil == 1:
        for j, w_ref in enumerate(w_refs):
            o_ref[0, :, j * tn:(j + 1) * tn] = _dot(h, w_ref[...]).astype(BF16)
        return
    chunks = tn // LANES
    rows = tm // dil

    def project(j):
        acc = _dot(h, w_refs[j][...])
        for c in range(chunks):
            acc_refs[j][c] = acc[:, c * LANES:(c + 1) * LANES]

    def scatter(j):
        for r in range(dil):
            for c in range(chunks):
                cols = slice(j * tn + c * LANES, j * tn + (c + 1) * LANES)
                o_ref[r, :, cols] = acc_refs[j][c, pl.ds(r, rows, stride=dil), :].astype(BF16)

    project(0)
    project(1)
    scatter(0)
    project(2)
    scatter(1)
    scatter(2)


def _dilproj(h2, w_al, batch, seq, dil, group):
    t, d = h2.shape
    n = 3 * DIL_WIDTH
    tm, tn = DILPROJ_TM, DIL_WIDTH
    assert ALIGNED_DIL % tn == 0
    tiles_per_seq = seq // tm
    col0 = ALIGNED_DIL // tn + group
    w_spec = lambda j: pl.BlockSpec((d, tn), lambda i: (0, col0 + j * DIL_GROUPS))
    scratch = [] if dil == 1 else [pltpu.VMEM((tn // LANES, tm, LANES), F32)] * 3
    return pl.pallas_call(
        functools.partial(_dilproj_kernel, dil=dil),
        grid=(t // tm,),
        in_specs=[pl.BlockSpec((tm, d), lambda i: (i, 0)), w_spec(0), w_spec(1), w_spec(2)],
        out_specs=pl.BlockSpec((None, dil, tm // dil, n),
                               lambda i: (i // tiles_per_seq, 0, i % tiles_per_seq, 0)),
        out_shape=jax.ShapeDtypeStruct((batch, dil, seq // dil, n), BF16),
        scratch_shapes=scratch,
        compiler_params=pltpu.CompilerParams(dimension_semantics=("arbitrary",), vmem_limit_bytes=VMEM_LIMIT),
        name=f"dilproj{dil}",
    )(h2, w_al, w_al, w_al)


def _mla_kernel(qt_ref, k_ref, vt_ref, o_ref, m_ref, acc_ref, sta_ref, stb_ref):
    tq = qt_ref.shape[1]
    tk = MLA_TK
    qi = pl.program_id(2)
    nstrip = tq // MLA_QSTRIP
    m_ref[...] = jnp.full(m_ref.shape, NEG_INF, F32)
    acc_ref[...] = jnp.zeros(acc_ref.shape, F32)
    chains = [(a, hq) for a in range(MLA_HPS) for hq in range(nstrip)]

    def scores(c, i):
        a, hq = chains[i]
        k = k_ref[pl.ds(pl.multiple_of(c * tk, tk), tk), a * LANES:(a + 1) * LANES]
        return _dot(k, qt_ref[a * LANES:(a + 1) * LANES, hq * MLA_QSTRIP:(hq + 1) * MLA_QSTRIP])

    def step(c, src_ref, dst_ref, key_off=None):
        def live(hq, off):
            return off is None or (hq + 1) * MLA_QSTRIP > off

        next_off = None if key_off is None else key_off + tk
        m_all = m_ref[...]
        acc_all = acc_ref[...]
        m_new, acc_new = {}, {}
        for i, (a, hq) in enumerate(chains):
            qs = slice(hq * MLA_QSTRIP, (hq + 1) * MLA_QSTRIP)
            if dst_ref is not None and live(hq, next_off):
                dst_ref[i] = scores(c + 1, i)
            if not live(hq, key_off):
                m_new[a, hq], acc_new[a, hq] = m_all[a, :, qs], acc_all[a, :, qs]
                continue
            st = src_ref[i]
            if key_off is not None and key_off + tk - 1 > hq * MLA_QSTRIP:
                key = lax.broadcasted_iota(jnp.int32, st.shape, 0) + key_off
                qry = lax.broadcasted_iota(jnp.int32, st.shape, 1) + hq * MLA_QSTRIP
                st = jnp.where(key <= qry, st, NEG_INF)
            m_prev = m_all[a, :, qs]
            m_cur = jnp.maximum(m_prev, jnp.max(st, axis=0, keepdims=True))
            alpha = jnp.exp2(m_prev - m_cur)
            p = jnp.exp2(st - m_cur).astype(BF16)
            vt = vt_ref[c, a * MLA_VROWS:(a + 1) * MLA_VROWS, :]
            acc_new[a, hq] = alpha * acc_all[a, :, qs] + _dot(vt, p)
            m_new[a, hq] = m_cur
        for a in range(MLA_HPS):
            m_ref[a] = jnp.concatenate([m_new[a, hq] for hq in range(nstrip)], axis=-1)
            acc_ref[a] = jnp.concatenate([acc_new[a, hq] for hq in range(nstrip)], axis=-1)

    for i in range(len(chains)):
        sta_ref[i] = scores(0, i)

    cpt = tq // tk
    assert cpt == 2

    def pair(j, carry):
        step(2 * j, sta_ref, stb_ref)
        step(2 * j + 1, stb_ref, sta_ref)
        return carry

    lax.fori_loop(0, qi, pair, 0)
    step(2 * qi, sta_ref, stb_ref, key_off=0)
    step(2 * qi + 1, stb_ref, None, key_off=tk)

    out_t = jnp.concatenate([acc_ref[a, :MLA_V, :] / acc_ref[a, MLA_V:MLA_V + 1, :] for a in range(MLA_HPS)],
                            axis=0)
    o_ref[...] = out_t.T.astype(BF16)


def _mla(qt, k3, vt):
    b, s, _ = k3.shape
    tq, tk = MLA_TQ, MLA_TK
    hps = MLA_HPS
    st_scratch = pltpu.VMEM((hps * (tq // MLA_QSTRIP), tk, MLA_QSTRIP), F32)
    return pl.pallas_call(
        _mla_kernel,
        grid=(b, MLA_HEADS // hps, s // tq),
        in_specs=[
            pl.BlockSpec((None, hps * LANES, tq), lambda bi, h, i: (bi, h, i)),
            pl.BlockSpec((None, s, hps * LANES), lambda bi, h, i: (bi, 0, h)),
            pl.BlockSpec((None, s // tk, hps * MLA_VROWS, tk), lambda bi, h, i: (bi, 0, h, 0)),
        ],
        out_specs=pl.BlockSpec((None, tq, hps * MLA_V), lambda bi, h, i: (bi, i, h)),
        out_shape=jax.ShapeDtypeStruct((b, s, MLA_HEADS * MLA_V), BF16),
        scratch_shapes=[pltpu.VMEM((hps, 1, tq), F32), pltpu.VMEM((hps, MLA_VROWS, tq), F32),
                        st_scratch, st_scratch],
        compiler_params=pltpu.CompilerParams(dimension_semantics=("arbitrary", "arbitrary", "arbitrary"),
                                             vmem_limit_bytes=VMEM_LIMIT),
        name="mla",
    )(qt, k3, vt)


def _dilattn_kernel(q_ref, kc_ref, vc_ref, kp_ref, vp_ref, o_ref, lse_ref, kx_ref, vx_ref, *, dil, group):
    n = pl.program_id(2)
    tb = q_ref.shape[0]
    nblk = tb // BLOCK
    kx_ref[:BLOCK, :] = kp_ref[...]
    kx_ref[BLOCK:, :] = kc_ref[...]
    vx_ref[:BLOCK, :] = vp_ref[...]
    vx_ref[BLOCK:, :] = vc_ref[...]

    qry = lax.broadcasted_iota(jnp.int32, (BLOCK, 2 * BLOCK), 0)
    key = lax.broadcasted_iota(jnp.int32, (BLOCK, 2 * BLOCK), 1)
    dist = qry + BLOCK - key
    in_window = jnp.logical_and(dist >= 0, dist <= BLOCK)
    first_ok = jnp.logical_and(in_window, jnp.logical_or(key >= BLOCK, n > 0))
    distf = (dist * dil).astype(F32)
    lane = lax.broadcasted_iota(jnp.int32, (BLOCK, LANES), 1)
    scale = DIL_HEAD_DIM ** -0.5
    bias, bias_first = [], []
    for hh in range(DIL_HPG):
        slope = float(2.0 ** (-8.0 * (hh * DIL_GROUPS + group + 1) / DIL_HEADS))
        bias.append(jnp.where(in_window, -slope * distf, NEG_INF))
        bias_first.append(jnp.where(first_ok, -slope * distf, NEG_INF))

    units = [(t, hh) for t in range(nblk) for hh in range(DIL_HPG)]

    def scores(u):
        t, hh = units[u]
        cs = slice(hh * DIL_HEAD_DIM, (hh + 1) * DIL_HEAD_DIM)
        return _dot_nt(q_ref[t * BLOCK:(t + 1) * BLOCK, cs], kx_ref[t * BLOCK:(t + 2) * BLOCK, cs])

    pending = [scores(u) for u in range(min(DIL_AHEAD, len(units)))]
    lse_tile = None
    for u, (t, hh) in enumerate(units):
        if u + DIL_AHEAD < len(units):
            pending.append(scores(u + DIL_AHEAD))
        rs = slice(t * BLOCK, (t + 1) * BLOCK)
        cs = slice(hh * DIL_HEAD_DIM, (hh + 1) * DIL_HEAD_DIM)
        s = pending[u] * scale + (bias_first[hh] if t == 0 else bias[hh])
        m = jnp.max(s, axis=-1, keepdims=True)
        e = jnp.exp(s - m)
        den = jnp.sum(e, axis=-1, keepdims=True)
        o = _dot(e.astype(BF16), vx_ref[t * BLOCK:(t + 2) * BLOCK, cs]) / den
        o_ref[rs, cs] = o.astype(BF16)
        lse = m + jnp.log(den)
        lse_tile = jnp.where(lane == hh, lse, jnp.zeros((BLOCK, LANES), F32) if hh == 0 else lse_tile)
        if hh == DIL_HPG - 1:
            lse_ref[rs, :] = lse_tile


def _dilattn(qkv, dil, group):
    b, d, l, _ = qkv.shape
    tb = DIL_TB
    bpt = tb // BLOCK
    w = DIL_WIDTH
    cur = lambda c: pl.BlockSpec((None, None, tb, w), lambda bi, r, n: (bi, r, n, c))
    prev = lambda c: pl.BlockSpec((None, None, BLOCK, w), lambda bi, r, n: (bi, r, jnp.maximum(n * bpt - 1, 0), c))
    return pl.pallas_call(
        functools.partial(_dilattn_kernel, dil=dil, group=group),
        grid=(b, d, l // tb),
        in_specs=[cur(0), cur(1), cur(2), prev(1), prev(2)],
        out_specs=[pl.BlockSpec((None, None, tb, w), lambda bi, r, n: (bi, r, n, 0)),
                   pl.BlockSpec((None, None, tb, LANES), lambda bi, r, n: (bi, r, n, 0))],
        out_shape=[jax.ShapeDtypeStruct((b, d, l, w), BF16), jax.ShapeDtypeStruct((b, d, l, LANES), F32)],
        scratch_shapes=[pltpu.VMEM((tb + BLOCK, w), BF16), pltpu.VMEM((tb + BLOCK, w), BF16)],
        compiler_params=pltpu.CompilerParams(dimension_semantics=("arbitrary", "arbitrary", "arbitrary"),
                                             vmem_limit_bytes=VMEM_LIMIT),
        name=f"dilattn{dil}",
    )(qkv, qkv, qkv, qkv, qkv)


def _memkv_kernel(mem_ref, g_ref, w_ref, o_ref):
    o_ref[...] = _dot(_rms(mem_ref[...], g_ref[...]).astype(BF16), w_ref[...]).astype(BF16)


def _memkv(mem2, g_mem, w):
    return pl.pallas_call(
        _memkv_kernel,
        out_shape=jax.ShapeDtypeStruct((mem2.shape[0], w.shape[1]), BF16),
        compiler_params=pltpu.CompilerParams(vmem_limit_bytes=VMEM_LIMIT),
        name="memkv",
    )(mem2, g_mem, w)


def _sigmoid(z):
    return 1.0 / (1.0 + jnp.exp(-z))


def _merge_kernel(x_ref, h_ref, wmq_ref, wg_ref, bg_ref, kvm_ref, ymla_ref, wbm_ref,
                  o0_ref, o1_ref, o2_ref, l0_ref, l1_ref, l2_ref, wbd_ref, wbmem_ref, wo_ref, gpm_ref, gpf_ref,
                  x1_ref, h2_ref, nat1_ref, nat2_ref, lse1_ref, lse2_ref):
    tm, dm = x_ref.shape
    h = h_ref[...]

    def gate(br):
        cs = slice(br * dm, (br + 1) * dm)
        return _dot(h, wg_ref[:, cs]) + bg_ref[:, cs]

    memq = _dot(h, wmq_ref[...])
    scores = []
    for hh in range(MEM_HEADS):
        cs = slice(hh * MEM_HEAD_DIM, (hh + 1) * MEM_HEAD_DIM)
        q = (memq[:, cs] * MEM_HEAD_DIM ** -0.5).astype(BF16)
        scores.append(_dot_nt(q, kvm_ref[:, cs]))
    merged = _sigmoid(gate(0)) * _dot(ymla_ref[...], wbm_ref[...])
    mheads = []
    for hh in range(MEM_HEADS):
        s = scores[hh]
        e = jnp.exp(s - jnp.max(s, axis=-1, keepdims=True))
        o = _dot(e.astype(BF16), kvm_ref[:, MEM_WIDTH + hh * MEM_HEAD_DIM:MEM_WIDTH + (hh + 1) * MEM_HEAD_DIM])
        mheads.append((o / jnp.sum(e, axis=-1, keepdims=True)).astype(BF16))
    y_mem = jnp.concatenate(mheads, axis=-1)
    gate_dil = gate(1)

    for src, lsrc, dst, ldst in ((o1_ref, l1_ref, nat1_ref, lse1_ref), (o2_ref, l2_ref, nat2_ref, lse2_ref)):
        d = src.shape[0]
        rows = src.shape[1]
        for r in range(d):
            for hh in range(DIL_HPG):
                cs = slice(hh * DIL_HEAD_DIM, (hh + 1) * DIL_HEAD_DIM)
                dst[hh, pl.ds(r, rows, stride=d), :] = src[r, :, cs].astype(F32)
            ldst[pl.ds(r, rows, stride=d), :] = lsrc[r]
    lg = (l0_ref[0], lse1_ref[...], lse2_ref[...])
    heads = []
    for hh in range(DIL_HPG):
        cs = slice(hh * DIL_HEAD_DIM, (hh + 1) * DIL_HEAD_DIM)
        og = (o0_ref[0, :, cs].astype(F32), nat1_ref[hh], nat2_ref[hh])
        ls = [l[:, hh:hh + 1] for l in lg]
        mx = jnp.maximum(jnp.maximum(ls[0], ls[1]), ls[2])
        ws = [jnp.exp(l - mx) for l in ls]
        num = ws[0] * og[0] + ws[1] * og[1] + ws[2] * og[2]
        heads.append((num / (ws[0] + ws[1] + ws[2])).astype(BF16))
    y_dil = jnp.concatenate(heads, axis=-1)

    merged = merged + _sigmoid(gate(2)) * _dot(y_mem, wbmem_ref[...])
    merged = merged + _sigmoid(gate_dil) * _dot(y_dil, wbd_ref[...])
    mixed = _dot(merged.astype(BF16), wo_ref[...])
    x1 = x_ref[...] + _rms(mixed, gpm_ref[...])
    x1_ref[...] = x1
    h2_ref[...] = _rms(x1, gpf_ref[...]).astype(BF16)


def _merge(x2, h2, w_al, b_g, kvm, y_mla, w_bm, o_dil, lse_dil, w_bd, w_bmem, w_o, g_pm, g_pf, batch, seq):
    t, dm = x2.shape
    tm = MERGE_TM
    tps = seq // tm
    n_mem = kvm.shape[0] // batch
    row = lambda i: (i, 0)
    single = pl.Buffered(1)
    const = lambda a: pl.BlockSpec(a.shape, lambda i: (0,) * a.ndim, pipeline_mode=single)
    gate_w = N_BRANCH * dm
    assert ALIGNED_MEMQ % MEM_WIDTH == 0 and ALIGNED_GATE % gate_w == 0
    w_mq = pl.BlockSpec((dm, MEM_WIDTH), lambda i: (0, ALIGNED_MEMQ // MEM_WIDTH), pipeline_mode=single)
    w_g = pl.BlockSpec((dm, gate_w), lambda i: (0, ALIGNED_GATE // gate_w), pipeline_mode=single)

    def dil_spec(a):
        d, width = a.shape[1], a.shape[3]
        return pl.BlockSpec((None, d, tm // d, width), lambda i: (i // tps, 0, i % tps, 0))

    return pl.pallas_call(
        _merge_kernel,
        grid=(t // tm,),
        in_specs=[
            pl.BlockSpec((tm, dm), row), pl.BlockSpec((tm, dm), row), w_mq, w_g, const(b_g),
            pl.BlockSpec((n_mem, kvm.shape[1]), lambda i: (i // tps, 0)),
            pl.BlockSpec((tm, y_mla.shape[1]), row), const(w_bm),
            dil_spec(o_dil[0]), dil_spec(o_dil[1]), dil_spec(o_dil[2]),
            dil_spec(lse_dil[0]), dil_spec(lse_dil[1]), dil_spec(lse_dil[2]),
            const(w_bd), const(w_bmem), const(w_o), const(g_pm), const(g_pf),
        ],
        out_specs=[pl.BlockSpec((tm, dm), row), pl.BlockSpec((tm, dm), row)],
        out_shape=[jax.ShapeDtypeStruct((t, dm), F32), jax.ShapeDtypeStruct((t, dm), BF16)],
        scratch_shapes=[pltpu.VMEM((DIL_HPG, tm, DIL_HEAD_DIM), F32), pltpu.VMEM((DIL_HPG, tm, DIL_HEAD_DIM), F32),
                        pltpu.VMEM((tm, LANES), F32), pltpu.VMEM((tm, LANES), F32)],
        compiler_params=pltpu.CompilerParams(dimension_semantics=("arbitrary",), vmem_limit_bytes=VMEM_LIMIT),
        name="merge",
    )(x2, h2, w_al, w_al, b_g, kvm, y_mla, w_bm, o_dil[0], o_dil[1], o_dil[2], lse_dil[0], lse_dil[1], lse_dil[2],
      w_bd, w_bmem, w_o, g_pm, g_pf)


def _ffn_kernel(x1_ref, h2_ref, halo_ref, wup_ref, cw_ref, cb_ref, wd_ref, gpost_ref,
                out_ref, hcat_ref, ua_ref, ub_ref, acc_ref, *, tiles_per_seq):
    i = pl.program_id(0)
    tm = x1_ref.shape[0]
    halo, tf = FFN_HALO, FFN_TF
    dff = wd_ref.shape[0]
    nchunk = dff // tf
    lanes_per_chunk = tf // LANES

    first = (i % tiles_per_seq) == 0
    hcat_ref[:halo, :] = jnp.where(first, jnp.zeros_like(halo_ref[...]), halo_ref[...])
    hcat_ref[halo:, :] = h2_ref[...]

    def up(c, u_ref):
        hc = hcat_ref[...]
        for part, off in enumerate((c * tf, dff + c * tf)):
            u = _dot(hc, wup_ref[:, off:off + tf])
            for j in range(lanes_per_chunk):
                u_ref[part * lanes_per_chunk + j] = u[:, j * LANES:(j + 1) * LANES]

    def conv(u_ref, slab, col):
        cols = slice(col, col + LANES)
        z = cb_ref[:, cols] + cw_ref[0:1, cols] * u_ref[slab, halo - 2:halo - 2 + tm, :]
        z = z + cw_ref[1:2, cols] * u_ref[slab, halo - 1:halo - 1 + tm, :]
        return z + cw_ref[2:3, cols] * u_ref[slab, halo:halo + tm, :]

    bufs = (ua_ref, ub_ref)
    up(0, bufs[0])
    for c in range(nchunk):
        cur = bufs[c % 2]
        if c + 1 < nchunk:
            up(c + 1, bufs[(c + 1) % 2])
        acts = []
        for j in range(lanes_per_chunk):
            gate = conv(cur, j, c * tf + j * LANES)
            val = conv(cur, lanes_per_chunk + j, dff + c * tf + j * LANES)
            acts.append((gate * _sigmoid(gate) * val).astype(BF16))
        down = _dot(jnp.concatenate(acts, axis=-1), wd_ref[c * tf:(c + 1) * tf, :])
        if c == 0:
            acc_ref[...] = down
        else:
            acc_ref[...] += down

    out_ref[...] = x1_ref[...] + _rms(acc_ref[...], gpost_ref[...])


def _ffn(x1, h2, w_up, conv_w, conv_b, w_down, g_post, seq):
    t, dm = x1.shape
    tm, tf, halo = FFN_TM, FFN_TF, FFN_HALO
    tps = seq // tm
    row = lambda i: (i, 0)
    const = lambda a: pl.BlockSpec(a.shape, lambda i: (0,) * a.ndim, pipeline_mode=pl.Buffered(1))
    u_scratch = pltpu.VMEM((2 * tf // LANES, tm + halo, LANES), F32)
    return pl.pallas_call(
        functools.partial(_ffn_kernel, tiles_per_seq=tps),
        grid=(t // tm,),
        in_specs=[
            pl.BlockSpec((tm, dm), row),
            pl.BlockSpec((tm, dm), row),
            pl.BlockSpec((halo, dm), lambda i: (jnp.maximum(i * (tm // halo) - 1, 0), 0)),
            const(w_up), const(conv_w), const(conv_b), const(w_down), const(g_post),
        ],
        out_specs=pl.BlockSpec((tm, dm), row),
        out_shape=jax.ShapeDtypeStruct((t, dm), F32),
        scratch_shapes=[pltpu.VMEM((tm + halo, dm), BF16), u_scratch, u_scratch, pltpu.VMEM((tm, dm), F32)],
        compiler_params=pltpu.CompilerParams(dimension_semantics=("arbitrary",), vmem_limit_bytes=VMEM_LIMIT),
        name="ffn",
    )(x1, h2, h2, w_up, conv_w, conv_b, w_down, g_post)


def _rot_half_cols(w):
    return jnp.concatenate([-w[..., ROPE_HALF:], w[..., :ROPE_HALF]], axis=-1)


def _align_kernel(w_ref, o_ref):
    tr = w_ref.shape[0]
    o_ref[:, :OFF_KV] = w_ref[:, :OFF_KV].astype(BF16)
    blk = w_ref[:, OFF_KV:OFF_KV + LANES]
    lane = lax.broadcasted_iota(jnp.int32, (tr, LANES), 1)
    in_lo = jnp.logical_and(lane >= MLA_NOPE, lane < MLA_NOPE + ROPE_HALF)
    in_hi = jnp.logical_and(lane >= MLA_NOPE + ROPE_HALF, lane < MLA_QK_DIM)
    placed = jnp.where(jnp.logical_or(in_lo, in_hi), pltpu.roll(blk, MLA_NOPE, 1), 0.0)
    second_half = pltpu.roll(blk, MLA_NOPE - ROPE_HALF, 1)
    first_half = pltpu.roll(blk, MLA_NOPE + ROPE_HALF, 1)
    rotated = jnp.where(in_lo, -second_half, jnp.where(in_hi, first_half, 0.0))
    o_ref[:, OFF_KV:OFF_KV + LANES] = placed.astype(BF16)
    o_ref[:, OFF_KV + LANES:OFF_KV + 2 * LANES] = rotated.astype(BF16)
    o_ref[:, OFF_KV + 2 * LANES:ALIGNED_DIL] = jnp.zeros((tr, ALIGNED_DIL - OFF_KV - 2 * LANES), BF16)
    o_ref[:, ALIGNED_DIL:] = w_ref[:, OFF_KR:].astype(BF16)


def _align_w_in(w_in_all, layer):
    _, dm, d_in = w_in_all.shape
    tr = ALIGN_TR
    width = ALIGNED_DIL + d_in - OFF_KR
    return pl.pallas_call(
        _align_kernel,
        grid=(dm // tr,),
        in_specs=[pl.BlockSpec((None, tr, d_in), lambda i: (layer, i, 0))],
        out_specs=pl.BlockSpec((tr, width), lambda i: (i, 0)),
        out_shape=jax.ShapeDtypeStruct((dm, width), BF16),
        compiler_params=pltpu.CompilerParams(dimension_semantics=("arbitrary",), vmem_limit_bytes=VMEM_LIMIT),
        name="align",
    )(w_in_all)


def _prep_weights(w_uq, w_ukv):
    pad_hi = LANES - MLA_QK_DIM

    uq = w_uq.reshape(MLA_Q_RANK, MLA_HEADS, MLA_QK_DIM)
    zq_lo = jnp.zeros((MLA_Q_RANK, MLA_HEADS, MLA_NOPE), F32)
    zq_hi = jnp.zeros((MLA_Q_RANK, MLA_HEADS, pad_hi), F32)
    w_qm = jnp.concatenate([uq, zq_hi], axis=-1).reshape(MLA_Q_RANK, MLA_HEADS * LANES)
    w_qs = jnp.concatenate([zq_lo, _rot_half_cols(uq[..., MLA_NOPE:]), zq_hi], axis=-1)
    w_qs = w_qs.reshape(MLA_Q_RANK, MLA_HEADS * LANES)

    ukv = w_ukv.reshape(MLA_KV_RANK, MLA_HEADS, MLA_NOPE + MLA_V)
    zk = jnp.zeros((MLA_KV_RANK, MLA_HEADS, LANES - MLA_NOPE), F32)
    w_k = jnp.concatenate([ukv[..., :MLA_NOPE], zk], axis=-1).reshape(MLA_KV_RANK, MLA_HEADS * LANES)
    w_v = ukv[..., MLA_NOPE:].reshape(MLA_KV_RANK, MLA_HEADS * MLA_V)
    return tuple(a.astype(BF16) for a in (w_qm.T, w_qs.T, w_k, w_v.T))


def _layer(layer, w_in_all, x, mem, positions, g_pre_mix, b_gate, mla_q_norm, w_uq, mla_kv_norm, w_ukv, g_mem, w_mem_kv,
           w_br_mla, w_br_dil, w_br_mem, w_o, g_post_mix, g_pre_ffn, w_ffn_up, conv_w, conv_b, w_ffn_down,
           g_post_ffn):
    batch, seq, dm = x.shape
    t = batch * seq
    x2 = x.reshape(t, dm)
    r2 = lambda v: v.reshape(1, -1)

    w_al = _align_w_in(w_in_all, layer)
    w_qmt, w_qst, w_k, w_vt = _prep_weights(w_uq, w_ukv)
    invf = (ROPE_THETA ** (-jnp.arange(ROPE_HALF, dtype=F32) / ROPE_HALF)).reshape(ROPE_HALF, 1)
    pos_rows = positions.reshape(t // PREP_TM, 1, PREP_TM)

    h2d, qt, k, vt = _prep(x2, pos_rows, invf, r2(g_pre_mix), w_al, r2(mla_q_norm), w_qmt, w_qst,
                           r2(mla_kv_norm), w_k, w_vt, batch, seq)
    y_mla = _mla(qt, k.reshape(batch, seq, MLA_HEADS * LANES), vt).reshape(t, MLA_HEADS * MLA_V)

    o_dil, lse_dil = [], []
    for g, (_, dil) in enumerate(DIL_PAIRS):
        qkv = _dilproj(h2d, w_al, batch, seq, dil, g)
        o, lse = _dilattn(qkv, dil, g)
        o_dil.append(o)
        lse_dil.append(lse)

    kvm = _memkv(mem.reshape(-1, dm), r2(g_mem), w_mem_kv.astype(BF16))

    x1, h2 = _merge(x2, h2d, w_al, r2(b_gate), kvm, y_mla, w_br_mla.astype(BF16), o_dil, lse_dil,
                    w_br_dil.astype(BF16),
                    w_br_mem.astype(BF16), w_o.astype(BF16), r2(g_post_mix), r2(g_pre_ffn), batch, seq)

    out = _ffn(x1, h2, w_ffn_up.astype(BF16), conv_w, r2(conv_b), w_ffn_down.astype(BF16), r2(g_post_ffn), seq)
    return out.reshape(batch, seq, dm)


def kernel(x, mem, positions, g_pre_mix, w_in, b_gate, mla_q_norm, w_uq, mla_kv_norm, w_ukv, g_mem, w_mem_kv,
           w_br_mla, w_br_dil, w_br_mem, w_o, g_post_mix, g_pre_ffn, w_ffn_up, conv_w, conv_b, w_ffn_down,
           g_post_ffn):
    for l in range(w_in.shape[0]):
        x = _layer(l, w_in, x, mem, positions, g_pre_mix[l], b_gate[l], mla_q_norm[l], w_uq[l], mla_kv_norm[l],
                   w_ukv[l], g_mem[l], w_mem_kv[l], w_br_mla[l], w_br_dil[l], w_br_mem[l], w_o[l], g_post_mix[l],
                   g_pre_ffn[l], w_ffn_up[l], conv_w[l], conv_b[l], w_ffn_down[l], g_post_ffn[l])
    return x
```

```python
import functools

import jax
import jax.numpy as jnp
from jax import lax
from jax.experimental import pallas as pl
from jax.experimental.pallas import tpu as pltpu

F32 = jnp.float32
BF16 = jnp.bfloat16

RMS_EPS = 1e-6
LOG2E = 1.4426950408889634
NEG_INF = -1e30
LANES = 128

BLOCK = 128
MLA_HEADS = 8
MLA_NOPE = 64
MLA_ROPE = 32
MLA_V = 64
MLA_QK_DIM = MLA_NOPE + MLA_ROPE
MLA_Q_RANK = 384
MLA_KV_RANK = 256
ROPE_THETA = 10000.0
ROPE_HALF = MLA_ROPE // 2

DIL_PAIRS = ((128, 1), (512, 4), (2048, 16))
DIL_GROUPS = 3
DIL_HPG = 4
DIL_HEADS = DIL_GROUPS * DIL_HPG
DIL_HEAD_DIM = 128
DIL_WIDTH = DIL_HPG * DIL_HEAD_DIM

MEM_HEADS = 4
MEM_HEAD_DIM = 128
MEM_WIDTH = MEM_HEADS * MEM_HEAD_DIM

N_BRANCH = 3
CONV_WIDTH = 3

OFF_Q = MLA_Q_RANK
OFF_KV = OFF_Q + MLA_KV_RANK
OFF_KR = OFF_KV + MLA_ROPE
OFF_DIL = OFF_KR + 3 * DIL_HEADS * DIL_HEAD_DIM
OFF_MEMQ = OFF_DIL + MEM_WIDTH
ALIGNED_DIL = 1024
ALIGNED_MEMQ = ALIGNED_DIL + (OFF_DIL - OFF_KR)
ALIGNED_GATE = ALIGNED_MEMQ + MEM_WIDTH

PREP_TM = 512
DILPROJ_TM = 1024
ALIGN_CB = 256
MLA_TQ = 1024
MLA_TK = 512
MLA_HPS = 2
MLA_QSTRIP = 256
MLA_VROWS = MLA_V + 16
assert PREP_TM == MLA_TK
DIL_TB = 512
DIL_AHEAD = 4
MERGE_TM = 512
FFN_TM = 512
FFN_TF = 256
FFN_HALO = 16

VMEM_LIMIT = 56 * 1024 * 1024


def _rms(xf, g):
    return xf * lax.rsqrt(jnp.mean(xf * xf, axis=-1, keepdims=True) + RMS_EPS) * g


def _dot(a, b):
    return jnp.dot(a, b, preferred_element_type=F32)


def _dot_nt(a, b):
    return lax.dot_general(a, b, (((1,), (1,)), ((), ())), preferred_element_type=F32)


def _const_spec(shape):
    nd = len(shape)
    return pl.BlockSpec(shape, lambda *_: (0,) * nd)


def _prep_kernel(x_ref, pos_ref, invf_ref, g_ref, wa_ref, qn_ref, wqmt_ref, wqst_ref, kvn_ref, wk_ref, wvt_ref,
                 h_ref, qt_ref, k_ref, vt_ref):
    tm = x_ref.shape[0]
    h = _rms(x_ref[...], g_ref[...]).astype(BF16)
    h_ref[...] = h
    p = _dot_nt(h, wa_ref[...])
    cq = _rms(p[:, :OFF_Q], qn_ref[...]).astype(BF16)
    ckv = _rms(p[:, OFF_Q:OFF_KV], kvn_ref[...]).astype(BF16)

    ang = invf_ref[...] * pos_ref[0].astype(F32)
    c16 = jnp.cos(ang)
    s16 = jnp.sin(ang)
    ones = jnp.ones((MLA_NOPE, tm), F32)
    zeros_lo = jnp.zeros((MLA_NOPE, tm), F32)
    zeros_hi = jnp.zeros((LANES - MLA_QK_DIM, tm), F32)
    cos_c = jnp.concatenate([ones, c16, c16, zeros_hi], axis=0)
    sin_c = jnp.concatenate([zeros_lo, s16, s16, zeros_hi], axis=0)

    qmt = _dot_nt(wqmt_ref[...], cq)
    qst = _dot_nt(wqst_ref[...], cq)
    qscale = MLA_QK_DIM ** -0.5 * LOG2E
    for hd in range(MLA_HEADS):
        sl = slice(hd * LANES, (hd + 1) * LANES)
        qt_ref[sl, :] = ((qmt[sl, :] * cos_c + qst[sl, :] * sin_c) * qscale).astype(BF16)

    cos_r = cos_c.T
    sin_r = sin_c.T
    kn = _dot(ckv, wk_ref[...])
    kpe = p[:, OFF_KV:OFF_KV + LANES] * cos_r + p[:, OFF_KV + LANES:OFF_KV + 2 * LANES] * sin_r
    for hd in range(MLA_HEADS):
        sl = slice(hd * LANES, (hd + 1) * LANES)
        k_ref[:, sl] = (kn[:, sl] + kpe).astype(BF16)
    vt = _dot_nt(wvt_ref[...], ckv).astype(BF16)
    ones_rows = jnp.ones((MLA_VROWS - MLA_V, tm), BF16)
    for hd in range(MLA_HEADS):
        vt_ref[0, hd * MLA_VROWS:hd * MLA_VROWS + MLA_V, :] = vt[hd * MLA_V:(hd + 1) * MLA_V, :]
        vt_ref[0, hd * MLA_VROWS + MLA_V:(hd + 1) * MLA_VROWS, :] = ones_rows


def _prep(x2, pos_rows, invf, g_pre, w_al, q_norm, w_qmt, w_qst, kv_norm, w_k, w_vt, batch, seq):
    t, d = x2.shape
    tm = PREP_TM
    tps = seq // tm
    hw = MLA_HEADS * LANES
    vw = MLA_HEADS * MLA_VROWS
    row = lambda i: (i, 0)
    return pl.pallas_call(
        _prep_kernel,
        grid=(t // tm,),
        in_specs=[
            pl.BlockSpec((tm, d), row),
            pl.BlockSpec((1, 1, tm), lambda i: (i, 0, 0)),
            _const_spec(invf.shape),
            _const_spec(g_pre.shape),
            pl.BlockSpec((OFF_KV + 2 * LANES, d), lambda i: (0, 0)),
            _const_spec(q_norm.shape),
            _const_spec(w_qmt.shape),
            _const_spec(w_qst.shape),
            _const_spec(kv_norm.shape),
            _const_spec(w_k.shape),
            _const_spec(w_vt.shape),
        ],
        out_specs=[pl.BlockSpec((tm, d), row),
                   pl.BlockSpec((None, hw, tm), lambda i: (i // tps, 0, i % tps)),
                   pl.BlockSpec((tm, hw), row),
                   pl.BlockSpec((None, 1, vw, tm), lambda i: (i // tps, i % tps, 0, 0))],
        out_shape=[jax.ShapeDtypeStruct((t, d), BF16),
                   jax.ShapeDtypeStruct((batch, hw, seq), BF16),
                   jax.ShapeDtypeStruct((t, hw), BF16),
                   jax.ShapeDtypeStruct((batch, tps, vw, tm), BF16)],
        compiler_params=pltpu.CompilerParams(dimension_semantics=("arbitrary",), vmem_limit_bytes=VMEM_LIMIT),
        name="prep",
    )(x2, pos_rows, invf, g_pre, w_al, q_norm, w_qmt, w_qst, kv_norm, w_k, w_vt)


def _dilproj_kernel(h_ref, wq_ref, wk_ref, wv_ref, o_ref, *acc_refs, dil):
    h = h_ref[...]
    w_refs = (wq_ref, wk_ref, wv_ref)
    tm, tn = h_ref.shape[0], wq_ref.shape[0]
    if dil == 1:
        for j, w_ref in enumerate(w_refs):
            o_ref[0, :, j * tn:(j + 1) * tn] = _dot_nt(h, w_ref[...]).astype(BF16)
        return
    chunks = tn // LANES
    rows = tm // dil

    def project(j):
        acc = _dot_nt(h, w_refs[j][...])
        for c in range(chunks):
            acc_refs[j][c] = acc[:, c * LANES:(c + 1) * LANES]

    def scatter(j):
        for r in range(dil):
            for c in range(chunks):
                cols = slice(j * tn + c * LANES, j * tn + (c + 1) * LANES)
                o_ref[r, :, cols] = acc_refs[j][c, pl.ds(r, rows, stride=dil), :].astype(BF16)

    project(0)
    project(1)
    scatter(0)
    project(2)
    scatter(1)
    scatter(2)


def _dilproj(h2, w_al, batch, seq, dil, group):
    t, d = h2.shape
    n = 3 * DIL_WIDTH
    tm, tn = DILPROJ_TM, DIL_WIDTH
    assert ALIGNED_DIL % tn == 0
    tiles_per_seq = seq // tm
    col0 = ALIGNED_DIL // tn + group
    w_spec = lambda j: pl.BlockSpec((tn, d), lambda i: (col0 + j * DIL_GROUPS, 0))
    scratch = [] if dil == 1 else [pltpu.VMEM((tn // LANES, tm, LANES), F32)] * 3
    return pl.pallas_call(
        functools.partial(_dilproj_kernel, dil=dil),
        grid=(t // tm,),
        in_specs=[pl.BlockSpec((tm, d), lambda i: (i, 0)), w_spec(0), w_spec(1), w_spec(2)],
        out_specs=pl.BlockSpec((None, dil, tm // dil, n),
                               lambda i: (i // tiles_per_seq, 0, i % tiles_per_seq, 0)),
        out_shape=jax.ShapeDtypeStruct((batch, dil, seq // dil, n), BF16),
        scratch_shapes=scratch,
        compiler_params=pltpu.CompilerParams(dimension_semantics=("arbitrary",), vmem_limit_bytes=VMEM_LIMIT),
        name=f"dilproj{dil}",
    )(h2, w_al, w_al, w_al)


def _mla_kernel(qt_ref, k_ref, vt_ref, o_ref, m_ref, acc_ref, sta_ref, stb_ref):
    tq = qt_ref.shape[1]
    tk = MLA_TK
    qi = pl.program_id(2)
    nstrip = tq // MLA_QSTRIP
    m_ref[...] = jnp.full(m_ref.shape, NEG_INF, F32)
    acc_ref[...] = jnp.zeros(acc_ref.shape, F32)
    chains = [(a, hq) for a in range(MLA_HPS) for hq in range(nstrip)]

    def scores(c, i):
        a, hq = chains[i]
        k = k_ref[pl.ds(pl.multiple_of(c * tk, tk), tk), a * LANES:(a + 1) * LANES]
        return _dot(k, qt_ref[a * LANES:(a + 1) * LANES, hq * MLA_QSTRIP:(hq + 1) * MLA_QSTRIP])

    def step(c, src_ref, dst_ref, key_off=None):
        def live(hq, off):
            return off is None or (hq + 1) * MLA_QSTRIP > off

        next_off = None if key_off is None else key_off + tk
        m_all = m_ref[...]
        acc_all = acc_ref[...]
        m_new, acc_new = {}, {}
        for i, (a, hq) in enumerate(chains):
            qs = slice(hq * MLA_QSTRIP, (hq + 1) * MLA_QSTRIP)
            if dst_ref is not None and live(hq, next_off):
                dst_ref[i] = scores(c + 1, i)
            if not live(hq, key_off):
                m_new[a, hq], acc_new[a, hq] = m_all[a, :, qs], acc_all[a, :, qs]
                continue
            st = src_ref[i]
            if key_off is not None and key_off + tk - 1 > hq * MLA_QSTRIP:
                key = lax.broadcasted_iota(jnp.int32, st.shape, 0) + key_off
                qry = lax.broadcasted_iota(jnp.int32, st.shape, 1) + hq * MLA_QSTRIP
                st = jnp.where(key <= qry, st, NEG_INF)
            m_prev = m_all[a, :, qs]
            m_cur = jnp.maximum(m_prev, jnp.max(st, axis=0, keepdims=True))
            alpha = jnp.exp2(m_prev - m_cur)
            p = jnp.exp2(st - m_cur).astype(BF16)
            vt = vt_ref[c, a * MLA_VROWS:(a + 1) * MLA_VROWS, :]
            acc_new[a, hq] = alpha * acc_all[a, :, qs] + _dot(vt, p)
            m_new[a, hq] = m_cur
        for a in range(MLA_HPS):
            m_ref[a] = jnp.concatenate([m_new[a, hq] for hq in range(nstrip)], axis=-1)
            acc_ref[a] = jnp.concatenate([acc_new[a, hq] for hq in range(nstrip)], axis=-1)

    for i in range(len(chains)):
        sta_ref[i] = scores(0, i)

    cpt = tq // tk
    assert cpt == 2

    def pair(j, carry):
        step(2 * j, sta_ref, stb_ref)
        step(2 * j + 1, stb_ref, sta_ref)
        return carry

    lax.fori_loop(0, qi, pair, 0)
    step(2 * qi, sta_ref, stb_ref, key_off=0)
    step(2 * qi + 1, stb_ref, None, key_off=tk)

    out_t = jnp.concatenate([acc_ref[a, :MLA_V, :] / acc_ref[a, MLA_V:MLA_V + 1, :] for a in range(MLA_HPS)],
                            axis=0)
    o_ref[...] = out_t.T.astype(BF16)


def _mla(qt, k3, vt):
    b, s, _ = k3.shape
    tq, tk = MLA_TQ, MLA_TK
    hps = MLA_HPS
    st_scratch = pltpu.VMEM((hps * (tq // MLA_QSTRIP), tk, MLA_QSTRIP), F32)
    return pl.pallas_call(
        _mla_kernel,
        grid=(b, MLA_HEADS // hps, s // tq),
        in_specs=[
            pl.BlockSpec((None, hps * LANES, tq), lambda bi, h, i: (bi, h, i)),
            pl.BlockSpec((None, s, hps * LANES), lambda bi, h, i: (bi, 0, h)),
            pl.BlockSpec((None, s // tk, hps * MLA_VROWS, tk), lambda bi, h, i: (bi, 0, h, 0)),
        ],
        out_specs=pl.BlockSpec((None, tq, hps * MLA_V), lambda bi, h, i: (bi, i, h)),
        out_shape=jax.ShapeDtypeStruct((b, s, MLA_HEADS * MLA_V), BF16),
        scratch_shapes=[pltpu.VMEM((hps, 1, tq), F32), pltpu.VMEM((hps, MLA_VROWS, tq), F32),
                        st_scratch, st_scratch],
        compiler_params=pltpu.CompilerParams(dimension_semantics=("arbitrary", "arbitrary", "arbitrary"),
                                             vmem_limit_bytes=VMEM_LIMIT),
        name="mla",
    )(qt, k3, vt)


def _dilattn_kernel(q_ref, kc_ref, vc_ref, kp_ref, vp_ref, o_ref, lse_ref, kx_ref, vx_ref, *, dil, group):
    n = pl.program_id(2)
    tb = q_ref.shape[0]
    nblk = tb // BLOCK
    kx_ref[:BLOCK, :] = kp_ref[...]
    kx_ref[BLOCK:, :] = kc_ref[...]
    vx_ref[:BLOCK, :] = vp_ref[...]
    vx_ref[BLOCK:, :] = vc_ref[...]

    qry = lax.broadcasted_iota(jnp.int32, (BLOCK, 2 * BLOCK), 0)
    key = lax.broadcasted_iota(jnp.int32, (BLOCK, 2 * BLOCK), 1)
    dist = qry + BLOCK - key
    in_window = jnp.logical_and(dist >= 0, dist <= BLOCK)
    first_ok = jnp.logical_and(in_window, jnp.logical_or(key >= BLOCK, n > 0))
    distf = (dist * dil).astype(F32)
    lane = lax.broadcasted_iota(jnp.int32, (BLOCK, LANES), 1)
    scale = DIL_HEAD_DIM ** -0.5
    bias, bias_first = [], []
    for hh in range(DIL_HPG):
        slope = float(2.0 ** (-8.0 * (hh * DIL_GROUPS + group + 1) / DIL_HEADS))
        bias.append(jnp.where(in_window, -slope * distf, NEG_INF))
        bias_first.append(jnp.where(first_ok, -slope * distf, NEG_INF))

    units = [(t, hh) for t in range(nblk) for hh in range(DIL_HPG)]

    def scores(u):
        t, hh = units[u]
        cs = slice(hh * DIL_HEAD_DIM, (hh + 1) * DIL_HEAD_DIM)
        return _dot_nt(q_ref[t * BLOCK:(t + 1) * BLOCK, cs], kx_ref[t * BLOCK:(t + 2) * BLOCK, cs])

    pending = [scores(u) for u in range(min(DIL_AHEAD, len(units)))]
    lse_tile = None
    for u, (t, hh) in enumerate(units):
        if u + DIL_AHEAD < len(units):
            pending.append(scores(u + DIL_AHEAD))
        rs = slice(t * BLOCK, (t + 1) * BLOCK)
        cs = slice(hh * DIL_HEAD_DIM, (hh + 1) * DIL_HEAD_DIM)
        s = pending[u] * scale + (bias_first[hh] if t == 0 else bias[hh])
        m = jnp.max(s, axis=-1, keepdims=True)
        e = jnp.exp(s - m)
        den = jnp.sum(e, axis=-1, keepdims=True)
        o = _dot(e.astype(BF16), vx_ref[t * BLOCK:(t + 2) * BLOCK, cs]) / den
        o_ref[rs, cs] = o.astype(BF16)
        lse = m + jnp.log(den)
        lse_tile = jnp.where(lane == hh, lse, jnp.zeros((BLOCK, LANES), F32) if hh == 0 else lse_tile)
        if hh == DIL_HPG - 1:
            lse_ref[rs, :] = lse_tile


def _dilattn(qkv, dil, group):
    b, d, l, _ = qkv.shape
    tb = DIL_TB
    bpt = tb // BLOCK
    w = DIL_WIDTH
    cur = lambda c: pl.BlockSpec((None, None, tb, w), lambda bi, r, n: (bi, r, n, c))
    prev = lambda c: pl.BlockSpec((None, None, BLOCK, w), lambda bi, r, n: (bi, r, jnp.maximum(n * bpt - 1, 0), c))
    return pl.pallas_call(
        functools.partial(_dilattn_kernel, dil=dil, group=group),
        grid=(b, d, l // tb),
        in_specs=[cur(0), cur(1), cur(2), prev(1), prev(2)],
        out_specs=[pl.BlockSpec((None, None, tb, w), lambda bi, r, n: (bi, r, n, 0)),
                   pl.BlockSpec((None, None, tb, LANES), lambda bi, r, n: (bi, r, n, 0))],
        out_shape=[jax.ShapeDtypeStruct((b, d, l, w), BF16), jax.ShapeDtypeStruct((b, d, l, LANES), F32)],
        scratch_shapes=[pltpu.VMEM((tb + BLOCK, w), BF16), pltpu.VMEM((tb + BLOCK, w), BF16)],
        compiler_params=pltpu.CompilerParams(dimension_semantics=("arbitrary", "arbitrary", "arbitrary"),
                                             vmem_limit_bytes=VMEM_LIMIT),
        name=f"dilattn{dil}",
    )(qkv, qkv, qkv, qkv, qkv)


def _memkv_kernel(mem_ref, g_ref, w_ref, o_ref):
    o_ref[...] = _dot(_rms(mem_ref[...], g_ref[...]).astype(BF16), w_ref[...]).astype(BF16)


def _memkv(mem2, g_mem, w):
    return pl.pallas_call(
        _memkv_kernel,
        out_shape=jax.ShapeDtypeStruct((mem2.shape[0], w.shape[1]), BF16),
        compiler_params=pltpu.CompilerParams(vmem_limit_bytes=VMEM_LIMIT),
        name="memkv",
    )(mem2, g_mem, w)


def _sigmoid(z):
    return 1.0 / (1.0 + jnp.exp(-z))


def _merge_kernel(x_ref, h_ref, wmq_ref, wg_ref, bg_ref, kvm_ref, ymla_ref, wbm_ref,
                  o0_ref, o1_ref, o2_ref, l0_ref, l1_ref, l2_ref, wbd_ref, wbmem_ref, wo_ref, gpm_ref, gpf_ref,
                  x1_ref, h2_ref, nat1_ref, nat2_ref, lse1_ref, lse2_ref):
    tm, dm = x_ref.shape
    h = h_ref[...]

    def gate(br):
        cs = slice(br * dm, (br + 1) * dm)
        return _dot_nt(h, wg_ref[cs, :]) + bg_ref[:, cs]

    for src, lsrc, dst, ldst in ((o1_ref, l1_ref, nat1_ref, lse1_ref), (o2_ref, l2_ref, nat2_ref, lse2_ref)):
        d = src.shape[0]
        rows = src.shape[1]
        for r in range(d):
            for hh in range(DIL_HPG):
                cs = slice(hh * DIL_HEAD_DIM, (hh + 1) * DIL_HEAD_DIM)
                dst[hh, pl.ds(r, rows, stride=d), :] = src[r, :, cs].astype(F32)
            ldst[pl.ds(r, rows, stride=d), :] = lsrc[r]
    lg = (l0_ref[0], lse1_ref[...], lse2_ref[...])
    heads = []
    for hh in range(DIL_HPG):
        cs = slice(hh * DIL_HEAD_DIM, (hh + 1) * DIL_HEAD_DIM)
        og = (o0_ref[0, :, cs].astype(F32), nat1_ref[hh], nat2_ref[hh])
        ls = [l[:, hh:hh + 1] for l in lg]
        mx = jnp.maximum(jnp.maximum(ls[0], ls[1]), ls[2])
        ws = [jnp.exp(l - mx) for l in ls]
        num = ws[0] * og[0] + ws[1] * og[1] + ws[2] * og[2]
        heads.append((num / (ws[0] + ws[1] + ws[2])).astype(BF16))
    y_dil = jnp.concatenate(heads, axis=-1)

    memq = _dot_nt(h, wmq_ref[...])
    mheads = []
    for hh in range(MEM_HEADS):
        cs = slice(hh * MEM_HEAD_DIM, (hh + 1) * MEM_HEAD_DIM)
        q = (memq[:, cs] * MEM_HEAD_DIM ** -0.5).astype(BF16)
        s = _dot_nt(q, kvm_ref[:, cs])
        e = jnp.exp(s - jnp.max(s, axis=-1, keepdims=True))
        o = _dot(e.astype(BF16), kvm_ref[:, MEM_WIDTH + hh * MEM_HEAD_DIM:MEM_WIDTH + (hh + 1) * MEM_HEAD_DIM])
        mheads.append((o / jnp.sum(e, axis=-1, keepdims=True)).astype(BF16))
    y_mem = jnp.concatenate(mheads, axis=-1)

    merged = jnp.zeros((tm, dm), F32)
    for br, (y, w_ref) in enumerate(((ymla_ref[...], wbm_ref), (y_dil, wbd_ref), (y_mem, wbmem_ref))):
        merged = merged + _sigmoid(gate(br)) * _dot(y, w_ref[...])
    mixed = _dot(merged.astype(BF16), wo_ref[...])
    x1 = x_ref[...] + _rms(mixed, gpm_ref[...])
    x1_ref[...] = x1
    h2_ref[...] = _rms(x1, gpf_ref[...]).astype(BF16)


def _merge(x2, h2, w_al, b_g, kvm, y_mla, w_bm, o_dil, lse_dil, w_bd, w_bmem, w_o, g_pm, g_pf, batch, seq):
    t, dm = x2.shape
    tm = MERGE_TM
    tps = seq // tm
    n_mem = kvm.shape[0] // batch
    row = lambda i: (i, 0)
    single = pl.Buffered(1)
    const = lambda a: pl.BlockSpec(a.shape, lambda i: (0,) * a.ndim, pipeline_mode=single)
    gate_w = N_BRANCH * dm
    assert ALIGNED_MEMQ % MEM_WIDTH == 0 and ALIGNED_GATE % gate_w == 0
    w_mq = pl.BlockSpec((MEM_WIDTH, dm), lambda i: (ALIGNED_MEMQ // MEM_WIDTH, 0), pipeline_mode=single)
    w_g = pl.BlockSpec((gate_w, dm), lambda i: (ALIGNED_GATE // gate_w, 0), pipeline_mode=single)

    def dil_spec(a):
        d, width = a.shape[1], a.shape[3]
        return pl.BlockSpec((None, d, tm // d, width), lambda i: (i // tps, 0, i % tps, 0))

    return pl.pallas_call(
        _merge_kernel,
        grid=(t // tm,),
        in_specs=[
            pl.BlockSpec((tm, dm), row), pl.BlockSpec((tm, dm), row), w_mq, w_g, const(b_g),
            pl.BlockSpec((n_mem, kvm.shape[1]), lambda i: (i // tps, 0)),
            pl.BlockSpec((tm, y_mla.shape[1]), row), const(w_bm),
            dil_spec(o_dil[0]), dil_spec(o_dil[1]), dil_spec(o_dil[2]),
            dil_spec(lse_dil[0]), dil_spec(lse_dil[1]), dil_spec(lse_dil[2]),
            const(w_bd), const(w_bmem), const(w_o), const(g_pm), const(g_pf),
        ],
        out_specs=[pl.BlockSpec((tm, dm), row), pl.BlockSpec((tm, dm), row)],
        out_shape=[jax.ShapeDtypeStruct((t, dm), F32), jax.ShapeDtypeStruct((t, dm), BF16)],
        scratch_shapes=[pltpu.VMEM((DIL_HPG, tm, DIL_HEAD_DIM), F32), pltpu.VMEM((DIL_HPG, tm, DIL_HEAD_DIM), F32),
                        pltpu.VMEM((tm, LANES), F32), pltpu.VMEM((tm, LANES), F32)],
        compiler_params=pltpu.CompilerParams(dimension_semantics=("arbitrary",), vmem_limit_bytes=VMEM_LIMIT),
        name="merge",
    )(x2, h2, w_al, w_al, b_g, kvm, y_mla, w_bm, o_dil[0], o_dil[1], o_dil[2], lse_dil[0], lse_dil[1], lse_dil[2],
      w_bd, w_bmem, w_o, g_pm, g_pf)


def _ffn_kernel(x1_ref, h2_ref, halo_ref, wup_ref, cw_ref, cb_ref, wd_ref, gpost_ref,
                out_ref, hcat_ref, ua_ref, ub_ref, acc_ref, *, tiles_per_seq):
    i = pl.program_id(0)
    tm = x1_ref.shape[0]
    halo, tf = FFN_HALO, FFN_TF
    dff = wd_ref.shape[0]
    nchunk = dff // tf
    lanes_per_chunk = tf // LANES

    first = (i % tiles_per_seq) == 0
    hcat_ref[:halo, :] = jnp.where(first, jnp.zeros_like(halo_ref[...]), halo_ref[...])
    hcat_ref[halo:, :] = h2_ref[...]

    def up(c, u_ref):
        hc = hcat_ref[...]
        for part, off in enumerate((c * tf, dff + c * tf)):
            u = _dot(hc, wup_ref[:, off:off + tf])
            for j in range(lanes_per_chunk):
                u_ref[part * lanes_per_chunk + j] = u[:, j * LANES:(j + 1) * LANES]

    def conv(u_ref, slab, col):
        cols = slice(col, col + LANES)
        z = cb_ref[:, cols] + cw_ref[0:1, cols] * u_ref[slab, halo - 2:halo - 2 + tm, :]
        z = z + cw_ref[1:2, cols] * u_ref[slab, halo - 1:halo - 1 + tm, :]
        return z + cw_ref[2:3, cols] * u_ref[slab, halo:halo + tm, :]

    bufs = (ua_ref, ub_ref)
    up(0, bufs[0])
    for c in range(nchunk):
        cur = bufs[c % 2]
        if c + 1 < nchunk:
            up(c + 1, bufs[(c + 1) % 2])
        acts = []
        for j in range(lanes_per_chunk):
            gate = conv(cur, j, c * tf + j * LANES)
            val = conv(cur, lanes_per_chunk + j, dff + c * tf + j * LANES)
            acts.append((gate * _sigmoid(gate) * val).astype(BF16))
        down = _dot(jnp.concatenate(acts, axis=-1), wd_ref[c * tf:(c + 1) * tf, :])
        if c == 0:
            acc_ref[...] = down
        else:
            acc_ref[...] += down

    out_ref[...] = x1_ref[...] + _rms(acc_ref[...], gpost_ref[...])


def _ffn(x1, h2, w_up, conv_w, conv_b, w_down, g_post, seq):
    t, dm = x1.shape
    tm, tf, halo = FFN_TM, FFN_TF, FFN_HALO
    tps = seq // tm
    row = lambda i: (i, 0)
    const = lambda a: pl.BlockSpec(a.shape, lambda i: (0,) * a.ndim, pipeline_mode=pl.Buffered(1))
    u_scratch = pltpu.VMEM((2 * tf // LANES, tm + halo, LANES), F32)
    return pl.pallas_call(
        functools.partial(_ffn_kernel, tiles_per_seq=tps),
        grid=(t // tm,),
        in_specs=[
            pl.BlockSpec((tm, dm), row),
            pl.BlockSpec((tm, dm), row),
            pl.BlockSpec((halo, dm), lambda i: (jnp.maximum(i * (tm // halo) - 1, 0), 0)),
            const(w_up), const(conv_w), const(conv_b), const(w_down), const(g_post),
        ],
        out_specs=pl.BlockSpec((tm, dm), row),
        out_shape=jax.ShapeDtypeStruct((t, dm), F32),
        scratch_shapes=[pltpu.VMEM((tm + halo, dm), BF16), u_scratch, u_scratch, pltpu.VMEM((tm, dm), F32)],
        compiler_params=pltpu.CompilerParams(dimension_semantics=("arbitrary",), vmem_limit_bytes=VMEM_LIMIT),
        name="ffn",
    )(x1, h2, h2, w_up, conv_w, conv_b, w_down, g_post)


def _rot_half_cols(w):
    return jnp.concatenate([-w[..., ROPE_HALF:], w[..., :ROPE_HALF]], axis=-1)


def _align_kernel(wt_ref, o_ref):
    cb = wt_ref.shape[1]
    zeros = lambda n: jnp.zeros((n, cb), BF16)
    kr = wt_ref[OFF_KV:OFF_KR, :]
    o_ref[:OFF_KV, :] = wt_ref[:OFF_KV, :].astype(BF16)
    o_ref[OFF_KV:OFF_KV + MLA_NOPE, :] = zeros(MLA_NOPE)
    o_ref[OFF_KV + MLA_NOPE:OFF_KV + MLA_QK_DIM, :] = kr.astype(BF16)
    o_ref[OFF_KV + MLA_QK_DIM:OFF_KV + LANES + MLA_NOPE, :] = zeros(LANES - MLA_QK_DIM + MLA_NOPE)
    o_ref[OFF_KV + LANES + MLA_NOPE:OFF_KV + LANES + MLA_NOPE + ROPE_HALF, :] = (-kr[ROPE_HALF:, :]).astype(BF16)
    o_ref[OFF_KV + LANES + MLA_NOPE + ROPE_HALF:OFF_KV + LANES + MLA_QK_DIM, :] = kr[:ROPE_HALF, :].astype(BF16)
    o_ref[OFF_KV + LANES + MLA_QK_DIM:ALIGNED_DIL, :] = zeros(ALIGNED_DIL - OFF_KV - LANES - MLA_QK_DIM)
    o_ref[ALIGNED_DIL:, :] = wt_ref[OFF_KR:, :].astype(BF16)


def _align_w_in(w_in_all, layer):
    wt_all = jnp.swapaxes(w_in_all, 1, 2)
    _, d_in, dm = wt_all.shape
    cb = ALIGN_CB
    width = ALIGNED_DIL + d_in - OFF_KR
    return pl.pallas_call(
        _align_kernel,
        grid=(dm // cb,),
        in_specs=[pl.BlockSpec((None, d_in, cb), lambda i: (layer, 0, i))],
        out_specs=pl.BlockSpec((width, cb), lambda i: (0, i)),
        out_shape=jax.ShapeDtypeStruct((width, dm), BF16),
        compiler_params=pltpu.CompilerParams(dimension_semantics=("arbitrary",), vmem_limit_bytes=VMEM_LIMIT),
        name="align",
    )(wt_all)


def _prep_weights(w_uq, w_ukv):
    pad_hi = LANES - MLA_QK_DIM

    uq = w_uq.reshape(MLA_Q_RANK, MLA_HEADS, MLA_QK_DIM)
    zq_lo = jnp.zeros((MLA_Q_RANK, MLA_HEADS, MLA_NOPE), F32)
    zq_hi = jnp.zeros((MLA_Q_RANK, MLA_HEADS, pad_hi), F32)
    w_qm = jnp.concatenate([uq, zq_hi], axis=-1).reshape(MLA_Q_RANK, MLA_HEADS * LANES)
    w_qs = jnp.concatenate([zq_lo, _rot_half_cols(uq[..., MLA_NOPE:]), zq_hi], axis=-1)
    w_qs = w_qs.reshape(MLA_Q_RANK, MLA_HEADS * LANES)

    ukv = w_ukv.reshape(MLA_KV_RANK, MLA_HEADS, MLA_NOPE + MLA_V)
    zk = jnp.zeros((MLA_KV_RANK, MLA_HEADS, LANES - MLA_NOPE), F32)
    w_k = jnp.concatenate([ukv[..., :MLA_NOPE], zk], axis=-1).reshape(MLA_KV_RANK, MLA_HEADS * LANES)
    w_v = ukv[..., MLA_NOPE:].reshape(MLA_KV_RANK, MLA_HEADS * MLA_V)
    return tuple(a.astype(BF16) for a in (w_qm.T, w_qs.T, w_k, w_v.T))


def _layer(layer, w_in_all, x, mem, positions, g_pre_mix, b_gate, mla_q_norm, w_uq, mla_kv_norm, w_ukv, g_mem, w_mem_kv,
           w_br_mla, w_br_dil, w_br_mem, w_o, g_post_mix, g_pre_ffn, w_ffn_up, conv_w, conv_b, w_ffn_down,
           g_post_ffn):
    batch, seq, dm = x.shape
    t = batch * seq
    x2 = x.reshape(t, dm)
    r2 = lambda v: v.reshape(1, -1)

    w_al = _align_w_in(w_in_all, layer)
    w_qmt, w_qst, w_k, w_vt = _prep_weights(w_uq, w_ukv)
    invf = (ROPE_THETA ** (-jnp.arange(ROPE_HALF, dtype=F32) / ROPE_HALF)).reshape(ROPE_HALF, 1)
    pos_rows = positions.reshape(t // PREP_TM, 1, PREP_TM)

    h2d, qt, k, vt = _prep(x2, pos_rows, invf, r2(g_pre_mix), w_al, r2(mla_q_norm), w_qmt, w_qst,
                           r2(mla_kv_norm), w_k, w_vt, batch, seq)
    y_mla = _mla(qt, k.reshape(batch, seq, MLA_HEADS * LANES), vt).reshape(t, MLA_HEADS * MLA_V)

    o_dil, lse_dil = [], []
    for g, (_, dil) in enumerate(DIL_PAIRS):
        qkv = _dilproj(h2d, w_al, batch, seq, dil, g)
        o, lse = _dilattn(qkv, dil, g)
        o_dil.append(o)
        lse_dil.append(lse)

    kvm = _memkv(mem.reshape(-1, dm), r2(g_mem), w_mem_kv.astype(BF16))

    x1, h2 = _merge(x2, h2d, w_al, r2(b_gate), kvm, y_mla, w_br_mla.astype(BF16), o_dil, lse_dil,
                    w_br_dil.astype(BF16),
                    w_br_mem.astype(BF16), w_o.astype(BF16), r2(g_post_mix), r2(g_pre_ffn), batch, seq)

    out = _ffn(x1, h2, w_ffn_up.astype(BF16), conv_w, r2(conv_b), w_ffn_down.astype(BF16), r2(g_post_ffn), seq)
    return out.reshape(batch, seq, dm)


def kernel(x, mem, positions, g_pre_mix, w_in, b_gate, mla_q_norm, w_uq, mla_kv_norm, w_ukv, g_mem, w_mem_kv,
           w_br_mla, w_br_dil, w_br_mem, w_o, g_post_mix, g_pre_ffn, w_ffn_up, conv_w, conv_b, w_ffn_down,
           g_post_ffn):
    for l in range(w_in.shape[0]):
        x = _layer(l, w_in, x, mem, positions, g_pre_mix[l], b_gate[l], mla_q_norm[l], w_uq[l], mla_kv_norm[l],
                   w_ukv[l], g_mem[l], w_mem_kv[l], w_br_mla[l], w_br_dil[l], w_br_mem[l], w_o[l], g_post_mix[l],
                   g_pre_ffn[l], w_ffn_up[l], conv_w[l], conv_b[l], w_ffn_down[l], g_post_ffn[l])
    return x
```

```python
import functools

import jax
import jax.numpy as jnp
from jax import lax
from jax.experimental import pallas as pl
from jax.experimental.pallas import tpu as pltpu

F32 = jnp.float32
BF16 = jnp.bfloat16

RMS_EPS = 1e-6
LOG2E = 1.4426950408889634
NEG_INF = -1e30
LANES = 128

BLOCK = 128
MLA_HEADS = 8
MLA_NOPE = 64
MLA_ROPE = 32
MLA_V = 64
MLA_QK_DIM = MLA_NOPE + MLA_ROPE
MLA_Q_RANK = 384
MLA_KV_RANK = 256
ROPE_THETA = 10000.0
ROPE_HALF = MLA_ROPE // 2

DIL_PAIRS = ((128, 1), (512, 4), (2048, 16))
DIL_GROUPS = 3
DIL_HPG = 4
DIL_HEADS = DIL_GROUPS * DIL_HPG
DIL_HEAD_DIM = 128
DIL_WIDTH = DIL_HPG * DIL_HEAD_DIM

MEM_HEADS = 4
MEM_HEAD_DIM = 128
MEM_WIDTH = MEM_HEADS * MEM_HEAD_DIM

N_BRANCH = 3
CONV_WIDTH = 3

OFF_Q = MLA_Q_RANK
OFF_KV = OFF_Q + MLA_KV_RANK
OFF_KR = OFF_KV + MLA_ROPE
OFF_DIL = OFF_KR + 3 * DIL_HEADS * DIL_HEAD_DIM
OFF_MEMQ = OFF_DIL + MEM_WIDTH
ALIGNED_DIL = 1024
ALIGNED_MEMQ = ALIGNED_DIL + (OFF_DIL - OFF_KR)
ALIGNED_GATE = ALIGNED_MEMQ + MEM_WIDTH

PREP_TM = 512
DILPROJ_TM = 1024
ALIGN_CB = 256
MLA_TQ = 1024
MLA_TK = 512
MLA_HPS = 2
MLA_QSTRIP = 256
MLA_VROWS = MLA_V + 16
assert PREP_TM == MLA_TK
DIL_TB = 512
DIL_AHEAD = 4
MERGE_TM = 512
FFN_TM = 512
FFN_TF = 256
FFN_HALO = 16

VMEM_LIMIT = 56 * 1024 * 1024


def _rms(xf, g):
    return xf * lax.rsqrt(jnp.mean(xf * xf, axis=-1, keepdims=True) + RMS_EPS) * g


def _dot(a, b):
    return jnp.dot(a, b, preferred_element_type=F32)


def _dot_nt(a, b):
    return lax.dot_general(a, b, (((1,), (1,)), ((), ())), preferred_element_type=F32)


def _const_spec(shape):
    nd = len(shape)
    return pl.BlockSpec(shape, lambda *_: (0,) * nd)


def _prep_kernel(x_ref, pos_ref, invf_ref, g_ref, wa_ref, qn_ref, wqmt_ref, wqst_ref, kvn_ref, wk_ref, wvt_ref,
                 h_ref, qt_ref, k_ref, vt_ref):
    tm = x_ref.shape[0]
    h = _rms(x_ref[...], g_ref[...]).astype(BF16)
    h_ref[...] = h
    p = _dot_nt(h, wa_ref[...])
    cq = _rms(p[:, :OFF_Q], qn_ref[...]).astype(BF16)
    ckv = _rms(p[:, OFF_Q:OFF_KV], kvn_ref[...]).astype(BF16)

    ang = invf_ref[...] * pos_ref[0].astype(F32)
    c16 = jnp.cos(ang)
    s16 = jnp.sin(ang)
    ones = jnp.ones((MLA_NOPE, tm), F32)
    zeros_lo = jnp.zeros((MLA_NOPE, tm), F32)
    zeros_hi = jnp.zeros((LANES - MLA_QK_DIM, tm), F32)
    cos_c = jnp.concatenate([ones, c16, c16, zeros_hi], axis=0)
    sin_c = jnp.concatenate([zeros_lo, s16, s16, zeros_hi], axis=0)

    qmt = _dot_nt(wqmt_ref[...], cq)
    qst = _dot_nt(wqst_ref[...], cq)
    qscale = MLA_QK_DIM ** -0.5 * LOG2E
    for hd in range(MLA_HEADS):
        sl = slice(hd * LANES, (hd + 1) * LANES)
        qt_ref[sl, :] = ((qmt[sl, :] * cos_c + qst[sl, :] * sin_c) * qscale).astype(BF16)

    cos_r = cos_c.T
    sin_r = sin_c.T
    kn = _dot(ckv, wk_ref[...])
    kpe = p[:, OFF_KV:OFF_KV + LANES] * cos_r + p[:, OFF_KV + LANES:OFF_KV + 2 * LANES] * sin_r
    for hd in range(MLA_HEADS):
        sl = slice(hd * LANES, (hd + 1) * LANES)
        k_ref[:, sl] = (kn[:, sl] + kpe).astype(BF16)
    vt = _dot_nt(wvt_ref[...], ckv).astype(BF16)
    ones_rows = jnp.ones((MLA_VROWS - MLA_V, tm), BF16)
    for hd in range(MLA_HEADS):
        vt_ref[0, hd * MLA_VROWS:hd * MLA_VROWS + MLA_V, :] = vt[hd * MLA_V:(hd + 1) * MLA_V, :]
        vt_ref[0, hd * MLA_VROWS + MLA_V:(hd + 1) * MLA_VROWS, :] = ones_rows


def _prep(x2, pos_rows, invf, g_pre, w_al, q_norm, w_qmt, w_qst, kv_norm, w_k, w_vt, batch, seq):
    t, d = x2.shape
    tm = PREP_TM
    tps = seq // tm
    hw = MLA_HEADS * LANES
    vw = MLA_HEADS * MLA_VROWS
    row = lambda i: (i, 0)
    return pl.pallas_call(
        _prep_kernel,
        grid=(t // tm,),
        in_specs=[
            pl.BlockSpec((tm, d), row),
            pl.BlockSpec((1, 1, tm), lambda i: (i, 0, 0)),
            _const_spec(invf.shape),
            _const_spec(g_pre.shape),
            pl.BlockSpec((OFF_KV + 2 * LANES, d), lambda i: (0, 0)),
            _const_spec(q_norm.shape),
            _const_spec(w_qmt.shape),
            _const_spec(w_qst.shape),
            _const_spec(kv_norm.shape),
            _const_spec(w_k.shape),
            _const_spec(w_vt.shape),
        ],
        out_specs=[pl.BlockSpec((tm, d), row),
                   pl.BlockSpec((None, hw, tm), lambda i: (i // tps, 0, i % tps)),
                   pl.BlockSpec((tm, hw), row),
                   pl.BlockSpec((None, 1, vw, tm), lambda i: (i // tps, i % tps, 0, 0))],
        out_shape=[jax.ShapeDtypeStruct((t, d), BF16),
                   jax.ShapeDtypeStruct((batch, hw, seq), BF16),
                   jax.ShapeDtypeStruct((t, hw), BF16),
                   jax.ShapeDtypeStruct((batch, tps, vw, tm), BF16)],
        compiler_params=pltpu.CompilerParams(dimension_semantics=("arbitrary",), vmem_limit_bytes=VMEM_LIMIT),
        name="prep",
    )(x2, pos_rows, invf, g_pre, w_al, q_norm, w_qmt, w_qst, kv_norm, w_k, w_vt)


def _dilproj_kernel(h_ref, wq_ref, wk_ref, wv_ref, o_ref, *acc_refs, dil):
    h = h_ref[...]
    w_refs = (wq_ref, wk_ref, wv_ref)
    tm, tn = h_ref.shape[0], wq_ref.shape[0]
    if dil == 1:
        for j, w_ref in enumerate(w_refs):
            o_ref[0, :, j * tn:(j + 1) * tn] = _dot_nt(h, w_ref[...]).astype(BF16)
        return
    chunks = tn // LANES
    rows = tm // dil

    def project(j):
        acc = _dot_nt(h, w_refs[j][...])
        for c in range(chunks):
            acc_refs[j][c] = acc[:, c * LANES:(c + 1) * LANES]

    def scatter(j):
        for r in range(dil):
            for c in range(chunks):
                cols = slice(j * tn + c * LANES, j * tn + (c + 1) * LANES)
                o_ref[r, :, cols] = acc_refs[j][c, pl.ds(r, rows, stride=dil), :].astype(BF16)

    project(0)
    project(1)
    scatter(0)
    project(2)
    scatter(1)
    scatter(2)


def _dilproj(h2, w_al, batch, seq, dil, group):
    t, d = h2.shape
    n = 3 * DIL_WIDTH
    tm, tn = DILPROJ_TM, DIL_WIDTH
    assert ALIGNED_DIL % tn == 0
    tiles_per_seq = seq // tm
    col0 = ALIGNED_DIL // tn + group
    w_spec = lambda j: pl.BlockSpec((tn, d), lambda i: (col0 + j * DIL_GROUPS, 0))
    scratch = [] if dil == 1 else [pltpu.VMEM((tn // LANES, tm, LANES), F32)] * 3
    return pl.pallas_call(
        functools.partial(_dilproj_kernel, dil=dil),
        grid=(t // tm,),
        in_specs=[pl.BlockSpec((tm, d), lambda i: (i, 0)), w_spec(0), w_spec(1), w_spec(2)],
        out_specs=pl.BlockSpec((None, dil, tm // dil, n),
                               lambda i: (i // tiles_per_seq, 0, i % tiles_per_seq, 0)),
        out_shape=jax.ShapeDtypeStruct((batch, dil, seq // dil, n), BF16),
        scratch_shapes=scratch,
        compiler_params=pltpu.CompilerParams(dimension_semantics=("arbitrary",), vmem_limit_bytes=VMEM_LIMIT),
        name=f"dilproj{dil}",
    )(h2, w_al, w_al, w_al)


def _mla_kernel(qt_ref, k_ref, vt_ref, qtn_ref, k0n_ref, o_ref, m_ref, acc_ref, sta_ref, stb_ref):
    tq = qt_ref.shape[1]
    tk = MLA_TK
    qi = pl.program_id(2)
    nstrip = tq // MLA_QSTRIP
    m_ref[...] = jnp.full(m_ref.shape, NEG_INF, F32)
    acc_ref[...] = jnp.zeros(acc_ref.shape, F32)
    chains = [(a, hq) for a in range(MLA_HPS) for hq in range(nstrip)]

    def scores(c, i):
        a, hq = chains[i]
        k = k_ref[pl.ds(pl.multiple_of(c * tk, tk), tk), a * LANES:(a + 1) * LANES]
        return _dot(k, qt_ref[a * LANES:(a + 1) * LANES, hq * MLA_QSTRIP:(hq + 1) * MLA_QSTRIP])

    def successor_scores(i):
        a, hq = chains[i]
        return _dot(k0n_ref[:, a * LANES:(a + 1) * LANES],
                    qtn_ref[a * LANES:(a + 1) * LANES, hq * MLA_QSTRIP:(hq + 1) * MLA_QSTRIP])

    def step(c, src_ref, dst_ref, key_off=None, last=False):
        def live(hq, off):
            return off is None or (hq + 1) * MLA_QSTRIP > off

        next_off = None if key_off is None else key_off + tk
        m_all = m_ref[...]
        acc_all = acc_ref[...]
        m_new, acc_new = {}, {}
        for i, (a, hq) in enumerate(chains):
            qs = slice(hq * MLA_QSTRIP, (hq + 1) * MLA_QSTRIP)
            if last:
                dst_ref[i] = successor_scores(i)
            elif live(hq, next_off):
                dst_ref[i] = scores(c + 1, i)
            if not live(hq, key_off):
                m_new[a, hq], acc_new[a, hq] = m_all[a, :, qs], acc_all[a, :, qs]
                continue
            st = src_ref[i]
            if key_off is not None and key_off + tk - 1 > hq * MLA_QSTRIP:
                key = lax.broadcasted_iota(jnp.int32, st.shape, 0) + key_off
                qry = lax.broadcasted_iota(jnp.int32, st.shape, 1) + hq * MLA_QSTRIP
                st = jnp.where(key <= qry, st, NEG_INF)
            m_prev = m_all[a, :, qs]
            m_cur = jnp.maximum(m_prev, jnp.max(st, axis=0, keepdims=True))
            alpha = jnp.exp2(m_prev - m_cur)
            p = jnp.exp2(st - m_cur).astype(BF16)
            vt = vt_ref[c, a * MLA_VROWS:(a + 1) * MLA_VROWS, :]
            acc_new[a, hq] = alpha * acc_all[a, :, qs] + _dot(vt, p)
            m_new[a, hq] = m_cur
        for a in range(MLA_HPS):
            m_ref[a] = jnp.concatenate([m_new[a, hq] for hq in range(nstrip)], axis=-1)
            acc_ref[a] = jnp.concatenate([acc_new[a, hq] for hq in range(nstrip)], axis=-1)

    @pl.when(jnp.logical_and(jnp.logical_and(pl.program_id(0) == 0, pl.program_id(1) == 0), qi == 0))
    def _():
        for i in range(len(chains)):
            sta_ref[i] = scores(0, i)

    assert tq == 2 * tk

    def quad(j, carry):
        for c in range(4):
            step(4 * j + c, (sta_ref, stb_ref)[c % 2], (stb_ref, sta_ref)[c % 2])
        return carry

    lax.fori_loop(0, qi // 2, quad, 0)

    @pl.when(qi % 2 == 1)
    def _():
        step(2 * qi - 2, sta_ref, stb_ref)
        step(2 * qi - 1, stb_ref, sta_ref)

    step(2 * qi, sta_ref, stb_ref, key_off=0)
    step(2 * qi + 1, stb_ref, sta_ref, key_off=tk, last=True)

    out_t = jnp.concatenate([acc_ref[a, :MLA_V, :] / acc_ref[a, MLA_V:MLA_V + 1, :] for a in range(MLA_HPS)],
                            axis=0)
    o_ref[...] = out_t.T.astype(BF16)


def _mla(qt, k3, vt):
    b, s, _ = k3.shape
    tq, tk = MLA_TQ, MLA_TK
    hps = MLA_HPS
    st_scratch = pltpu.VMEM((hps * (tq // MLA_QSTRIP), tk, MLA_QSTRIP), F32)
    nh, nq = MLA_HEADS // hps, s // tq

    def successor(bi, h, i):
        wrap_i = i == nq - 1
        wrap_h = jnp.logical_and(wrap_i, h == nh - 1)
        i2 = jnp.where(wrap_i, 0, i + 1)
        h2 = jnp.where(wrap_h, 0, jnp.where(wrap_i, h + 1, h))
        b2 = jnp.minimum(bi + wrap_h.astype(jnp.int32), b - 1)
        return b2, h2, i2
    return pl.pallas_call(
        _mla_kernel,
        grid=(b, MLA_HEADS // hps, s // tq),
        in_specs=[
            pl.BlockSpec((None, hps * LANES, tq), lambda bi, h, i: (bi, h, i)),
            pl.BlockSpec((None, s, hps * LANES), lambda bi, h, i: (bi, 0, h)),
            pl.BlockSpec((None, s // tk, hps * MLA_VROWS, tk), lambda bi, h, i: (bi, 0, h, 0)),
            pl.BlockSpec((None, hps * LANES, tq), lambda bi, h, i: successor(bi, h, i)[:3]),
            pl.BlockSpec((None, tk, hps * LANES), lambda bi, h, i: (successor(bi, h, i)[0], 0, successor(bi, h, i)[1])),
        ],
        out_specs=pl.BlockSpec((None, tq, hps * MLA_V), lambda bi, h, i: (bi, i, h)),
        out_shape=jax.ShapeDtypeStruct((b, s, MLA_HEADS * MLA_V), BF16),
        scratch_shapes=[pltpu.VMEM((hps, 1, tq), F32), pltpu.VMEM((hps, MLA_VROWS, tq), F32),
                        st_scratch, st_scratch],
        compiler_params=pltpu.CompilerParams(dimension_semantics=("arbitrary", "arbitrary", "arbitrary"),
                                             vmem_limit_bytes=VMEM_LIMIT),
        name="mla",
    )(qt, k3, vt, qt, k3)


def _dilattn_kernel(q_ref, kc_ref, vc_ref, kp_ref, vp_ref, o_ref, lse_ref, kx_ref, vx_ref, *, dil, group):
    n = pl.program_id(2)
    tb = q_ref.shape[0]
    nblk = tb // BLOCK
    kx_ref[:BLOCK, :] = kp_ref[...]
    kx_ref[BLOCK:, :] = kc_ref[...]
    vx_ref[:BLOCK, :] = vp_ref[...]
    vx_ref[BLOCK:, :] = vc_ref[...]

    qry = lax.broadcasted_iota(jnp.int32, (BLOCK, 2 * BLOCK), 0)
    key = lax.broadcasted_iota(jnp.int32, (BLOCK, 2 * BLOCK), 1)
    dist = qry + BLOCK - key
    in_window = jnp.logical_and(dist >= 0, dist <= BLOCK)
    first_ok = jnp.logical_and(in_window, jnp.logical_or(key >= BLOCK, n > 0))
    distf = (dist * dil).astype(F32)
    lane = lax.broadcasted_iota(jnp.int32, (BLOCK, LANES), 1)
    scale = DIL_HEAD_DIM ** -0.5
    bias, bias_first = [], []
    for hh in range(DIL_HPG):
        slope = float(2.0 ** (-8.0 * (hh * DIL_GROUPS + group + 1) / DIL_HEADS))
        bias.append(jnp.where(in_window, -slope * distf, NEG_INF))
        bias_first.append(jnp.where(first_ok, -slope * distf, NEG_INF))

    units = [(t, hh) for t in range(nblk) for hh in range(DIL_HPG)]

    def scores(u):
        t, hh = units[u]
        cs = slice(hh * DIL_HEAD_DIM, (hh + 1) * DIL_HEAD_DIM)
        return _dot_nt(q_ref[t * BLOCK:(t + 1) * BLOCK, cs], kx_ref[t * BLOCK:(t + 2) * BLOCK, cs])

    pending = [scores(u) for u in range(min(DIL_AHEAD, len(units)))]
    lse_tile = None
    for u, (t, hh) in enumerate(units):
        if u + DIL_AHEAD < len(units):
            pending.append(scores(u + DIL_AHEAD))
        rs = slice(t * BLOCK, (t + 1) * BLOCK)
        cs = slice(hh * DIL_HEAD_DIM, (hh + 1) * DIL_HEAD_DIM)
        s = pending[u] * scale + (bias_first[hh] if t == 0 else bias[hh])
        m = jnp.max(s, axis=-1, keepdims=True)
        e = jnp.exp(s - m)
        den = jnp.sum(e, axis=-1, keepdims=True)
        o = _dot(e.astype(BF16), vx_ref[t * BLOCK:(t + 2) * BLOCK, cs]) / den
        o_ref[rs, cs] = o.astype(BF16)
        lse = m + jnp.log(den)
        lse_tile = jnp.where(lane == hh, lse, jnp.zeros((BLOCK, LANES), F32) if hh == 0 else lse_tile)
        if hh == DIL_HPG - 1:
            lse_ref[rs, :] = lse_tile


def _dilattn(qkv, dil, group):
    b, d, l, _ = qkv.shape
    tb = DIL_TB
    bpt = tb // BLOCK
    w = DIL_WIDTH
    cur = lambda c: pl.BlockSpec((None, None, tb, w), lambda bi, r, n: (bi, r, n, c))
    prev = lambda c: pl.BlockSpec((None, None, BLOCK, w), lambda bi, r, n: (bi, r, jnp.maximum(n * bpt - 1, 0), c))
    return pl.pallas_call(
        functools.partial(_dilattn_kernel, dil=dil, group=group),
        grid=(b, d, l // tb),
        in_specs=[cur(0), cur(1), cur(2), prev(1), prev(2)],
        out_specs=[pl.BlockSpec((None, None, tb, w), lambda bi, r, n: (bi, r, n, 0)),
                   pl.BlockSpec((None, None, tb, LANES), lambda bi, r, n: (bi, r, n, 0))],
        out_shape=[jax.ShapeDtypeStruct((b, d, l, w), BF16), jax.ShapeDtypeStruct((b, d, l, LANES), F32)],
        scratch_shapes=[pltpu.VMEM((tb + BLOCK, w), BF16), pltpu.VMEM((tb + BLOCK, w), BF16)],
        compiler_params=pltpu.CompilerParams(dimension_semantics=("arbitrary", "arbitrary", "arbitrary"),
                                             vmem_limit_bytes=VMEM_LIMIT),
        name=f"dilattn{dil}",
    )(qkv, qkv, qkv, qkv, qkv)


def _memkv_kernel(mem_ref, g_ref, w_ref, o_ref):
    o_ref[...] = _dot(_rms(mem_ref[...], g_ref[...]).astype(BF16), w_ref[...]).astype(BF16)


def _memkv(mem2, g_mem, w):
    return pl.pallas_call(
        _memkv_kernel,
        out_shape=jax.ShapeDtypeStruct((mem2.shape[0], w.shape[1]), BF16),
        compiler_params=pltpu.CompilerParams(vmem_limit_bytes=VMEM_LIMIT),
        name="memkv",
    )(mem2, g_mem, w)


def _sigmoid(z):
    return 1.0 / (1.0 + jnp.exp(-z))


def _merge_kernel(x_ref, h_ref, wmq_ref, wg_ref, bg_ref, kvm_ref, ymla_ref, wbm_ref,
                  o0_ref, o1_ref, o2_ref, l0_ref, l1_ref, l2_ref, wbd_ref, wbmem_ref, wo_ref, gpm_ref, gpf_ref,
                  x1_ref, h2_ref, nat1_ref, nat2_ref, lse1_ref, lse2_ref):
    tm, dm = x_ref.shape
    h = h_ref[...]

    def gate(br):
        cs = slice(br * dm, (br + 1) * dm)
        return _dot_nt(h, wg_ref[cs, :]) + bg_ref[:, cs]

    for src, lsrc, dst, ldst in ((o1_ref, l1_ref, nat1_ref, lse1_ref), (o2_ref, l2_ref, nat2_ref, lse2_ref)):
        d = src.shape[0]
        rows = src.shape[1]
        for r in range(d):
            for hh in range(DIL_HPG):
                cs = slice(hh * DIL_HEAD_DIM, (hh + 1) * DIL_HEAD_DIM)
                dst[hh, pl.ds(r, rows, stride=d), :] = src[r, :, cs].astype(F32)
            ldst[pl.ds(r, rows, stride=d), :] = lsrc[r]
    lg = (l0_ref[0], lse1_ref[...], lse2_ref[...])
    heads = []
    for hh in range(DIL_HPG):
        cs = slice(hh * DIL_HEAD_DIM, (hh + 1) * DIL_HEAD_DIM)
        og = (o0_ref[0, :, cs].astype(F32), nat1_ref[hh], nat2_ref[hh])
        ls = [l[:, hh:hh + 1] for l in lg]
        mx = jnp.maximum(jnp.maximum(ls[0], ls[1]), ls[2])
        ws = [jnp.exp(l - mx) for l in ls]
        num = ws[0] * og[0] + ws[1] * og[1] + ws[2] * og[2]
        heads.append((num / (ws[0] + ws[1] + ws[2])).astype(BF16))
    y_dil = jnp.concatenate(heads, axis=-1)

    memq = _dot_nt(h, wmq_ref[...])
    mheads = []
    for hh in range(MEM_HEADS):
        cs = slice(hh * MEM_HEAD_DIM, (hh + 1) * MEM_HEAD_DIM)
        q = (memq[:, cs] * MEM_HEAD_DIM ** -0.5).astype(BF16)
        s = _dot_nt(q, kvm_ref[:, cs])
        e = jnp.exp(s - jnp.max(s, axis=-1, keepdims=True))
        o = _dot(e.astype(BF16), kvm_ref[:, MEM_WIDTH + hh * MEM_HEAD_DIM:MEM_WIDTH + (hh + 1) * MEM_HEAD_DIM])
        mheads.append((o / jnp.sum(e, axis=-1, keepdims=True)).astype(BF16))
    y_mem = jnp.concatenate(mheads, axis=-1)

    merged = jnp.zeros((tm, dm), F32)
    for br, (y, w_ref) in enumerate(((ymla_ref[...], wbm_ref), (y_dil, wbd_ref), (y_mem, wbmem_ref))):
        merged = merged + _sigmoid(gate(br)) * _dot(y, w_ref[...])
    mixed = _dot(merged.astype(BF16), wo_ref[...])
    x1 = x_ref[...] + _rms(mixed, gpm_ref[...])
    x1_ref[...] = x1
    h2_ref[...] = _rms(x1, gpf_ref[...]).astype(BF16)


def _merge(x2, h2, w_al, b_g, kvm, y_mla, w_bm, o_dil, lse_dil, w_bd, w_bmem, w_o, g_pm, g_pf, batch, seq):
    t, dm = x2.shape
    tm = MERGE_TM
    tps = seq // tm
    n_mem = kvm.shape[0] // batch
    row = lambda i: (i, 0)
    single = pl.Buffered(1)
    const = lambda a: pl.BlockSpec(a.shape, lambda i: (0,) * a.ndim, pipeline_mode=single)
    gate_w = N_BRANCH * dm
    assert ALIGNED_MEMQ % MEM_WIDTH == 0 and ALIGNED_GATE % gate_w == 0
    w_mq = pl.BlockSpec((MEM_WIDTH, dm), lambda i: (ALIGNED_MEMQ // MEM_WIDTH, 0), pipeline_mode=single)
    w_g = pl.BlockSpec((gate_w, dm), lambda i: (ALIGNED_GATE // gate_w, 0), pipeline_mode=single)

    def dil_spec(a):
        d, width = a.shape[1], a.shape[3]
        return pl.BlockSpec((None, d, tm // d, width), lambda i: (i // tps, 0, i % tps, 0))

    return pl.pallas_call(
        _merge_kernel,
        grid=(t // tm,),
        in_specs=[
            pl.BlockSpec((tm, dm), row), pl.BlockSpec((tm, dm), row), w_mq, w_g, const(b_g),
            pl.BlockSpec((n_mem, kvm.shape[1]), lambda i: (i // tps, 0)),
            pl.BlockSpec((tm, y_mla.shape[1]), row), const(w_bm),
            dil_spec(o_dil[0]), dil_spec(o_dil[1]), dil_spec(o_dil[2]),
            dil_spec(lse_dil[0]), dil_spec(lse_dil[1]), dil_spec(lse_dil[2]),
            const(w_bd), const(w_bmem), const(w_o), const(g_pm), const(g_pf),
        ],
        out_specs=[pl.BlockSpec((tm, dm), row), pl.BlockSpec((tm, dm), row)],
        out_shape=[jax.ShapeDtypeStruct((t, dm), F32), jax.ShapeDtypeStruct((t, dm), BF16)],
        scratch_shapes=[pltpu.VMEM((DIL_HPG, tm, DIL_HEAD_DIM), F32), pltpu.VMEM((DIL_HPG, tm, DIL_HEAD_DIM), F32),
                        pltpu.VMEM((tm, LANES), F32), pltpu.VMEM((tm, LANES), F32)],
        compiler_params=pltpu.CompilerParams(dimension_semantics=("arbitrary",), vmem_limit_bytes=VMEM_LIMIT),
        name="merge",
    )(x2, h2, w_al, w_al, b_g, kvm, y_mla, w_bm, o_dil[0], o_dil[1], o_dil[2], lse_dil[0], lse_dil[1], lse_dil[2],
      w_bd, w_bmem, w_o, g_pm, g_pf)


def _ffn_kernel(x1_ref, h2_ref, halo_ref, wup_ref, cw_ref, cb_ref, wd_ref, gpost_ref,
                out_ref, hcat_ref, ua_ref, ub_ref, acc_ref, *, tiles_per_seq):
    i = pl.program_id(0)
    tm = x1_ref.shape[0]
    halo, tf = FFN_HALO, FFN_TF
    dff = wd_ref.shape[0]
    nchunk = dff // tf
    lanes_per_chunk = tf // LANES

    first = (i % tiles_per_seq) == 0
    hcat_ref[:halo, :] = jnp.where(first, jnp.zeros_like(halo_ref[...]), halo_ref[...])
    hcat_ref[halo:, :] = h2_ref[...]

    def up(c, u_ref):
        hc = hcat_ref[...]
        for part, off in enumerate((c * tf, dff + c * tf)):
            u = _dot(hc, wup_ref[:, off:off + tf])
            for j in range(lanes_per_chunk):
                u_ref[part * lanes_per_chunk + j] = u[:, j * LANES:(j + 1) * LANES]

    def conv(u_ref, slab, col):
        cols = slice(col, col + LANES)
        z = cb_ref[:, cols] + cw_ref[0:1, cols] * u_ref[slab, halo - 2:halo - 2 + tm, :]
        z = z + cw_ref[1:2, cols] * u_ref[slab, halo - 1:halo - 1 + tm, :]
        return z + cw_ref[2:3, cols] * u_ref[slab, halo:halo + tm, :]

    bufs = (ua_ref, ub_ref)
    up(0, bufs[0])
    for c in range(nchunk):
        cur = bufs[c % 2]
        if c + 1 < nchunk:
            up(c + 1, bufs[(c + 1) % 2])
        acts = []
        for j in range(lanes_per_chunk):
            gate = conv(cur, j, c * tf + j * LANES)
            val = conv(cur, lanes_per_chunk + j, dff + c * tf + j * LANES)
            acts.append((gate * _sigmoid(gate) * val).astype(BF16))
        down = _dot(jnp.concatenate(acts, axis=-1), wd_ref[c * tf:(c + 1) * tf, :])
        if c == 0:
            acc_ref[...] = down
        else:
            acc_ref[...] += down

    out_ref[...] = x1_ref[...] + _rms(acc_ref[...], gpost_ref[...])


def _ffn(x1, h2, w_up, conv_w, conv_b, w_down, g_post, seq):
    t, dm = x1.shape
    tm, tf, halo = FFN_TM, FFN_TF, FFN_HALO
    tps = seq // tm
    row = lambda i: (i, 0)
    const = lambda a: pl.BlockSpec(a.shape, lambda i: (0,) * a.ndim, pipeline_mode=pl.Buffered(1))
    u_scratch = pltpu.VMEM((2 * tf // LANES, tm + halo, LANES), F32)
    return pl.pallas_call(
        functools.partial(_ffn_kernel, tiles_per_seq=tps),
        grid=(t // tm,),
        in_specs=[
            pl.BlockSpec((tm, dm), row),
            pl.BlockSpec((tm, dm), row),
            pl.BlockSpec((halo, dm), lambda i: (jnp.maximum(i * (tm // halo) - 1, 0), 0)),
            const(w_up), const(conv_w), const(conv_b), const(w_down), const(g_post),
        ],
        out_specs=pl.BlockSpec((tm, dm), row),
        out_shape=jax.ShapeDtypeStruct((t, dm), F32),
        scratch_shapes=[pltpu.VMEM((tm + halo, dm), BF16), u_scratch, u_scratch, pltpu.VMEM((tm, dm), F32)],
        compiler_params=pltpu.CompilerParams(dimension_semantics=("arbitrary",), vmem_limit_bytes=VMEM_LIMIT),
        name="ffn",
    )(x1, h2, h2, w_up, conv_w, conv_b, w_down, g_post)


def _rot_half_cols(w):
    return jnp.concatenate([-w[..., ROPE_HALF:], w[..., :ROPE_HALF]], axis=-1)


def _align_kernel(wt_ref, o_ref):
    cb = wt_ref.shape[1]
    zeros = lambda n: jnp.zeros((n, cb), BF16)
    kr = wt_ref[OFF_KV:OFF_KR, :]
    o_ref[:OFF_KV, :] = wt_ref[:OFF_KV, :].astype(BF16)
    o_ref[OFF_KV:OFF_KV + MLA_NOPE, :] = zeros(MLA_NOPE)
    o_ref[OFF_KV + MLA_NOPE:OFF_KV + MLA_QK_DIM, :] = kr.astype(BF16)
    o_ref[OFF_KV + MLA_QK_DIM:OFF_KV + LANES + MLA_NOPE, :] = zeros(LANES - MLA_QK_DIM + MLA_NOPE)
    o_ref[OFF_KV + LANES + MLA_NOPE:OFF_KV + LANES + MLA_NOPE + ROPE_HALF, :] = (-kr[ROPE_HALF:, :]).astype(BF16)
    o_ref[OFF_KV + LANES + MLA_NOPE + ROPE_HALF:OFF_KV + LANES + MLA_QK_DIM, :] = kr[:ROPE_HALF, :].astype(BF16)
    o_ref[OFF_KV + LANES + MLA_QK_DIM:ALIGNED_DIL, :] = zeros(ALIGNED_DIL - OFF_KV - LANES - MLA_QK_DIM)
    o_ref[ALIGNED_DIL:, :] = wt_ref[OFF_KR:, :].astype(BF16)


def _align_w_in(w_in_all, layer):
    wt_all = jnp.swapaxes(w_in_all, 1, 2)
    _, d_in, dm = wt_all.shape
    cb = ALIGN_CB
    width = ALIGNED_DIL + d_in - OFF_KR
    return pl.pallas_call(
        _align_kernel,
        grid=(dm // cb,),
        in_specs=[pl.BlockSpec((None, d_in, cb), lambda i: (layer, 0, i))],
        out_specs=pl.BlockSpec((width, cb), lambda i: (0, i)),
        out_shape=jax.ShapeDtypeStruct((width, dm), BF16),
        compiler_params=pltpu.CompilerParams(dimension_semantics=("arbitrary",), vmem_limit_bytes=VMEM_LIMIT),
        name="align",
    )(wt_all)


def _prep_weights(w_uq, w_ukv):
    pad_hi = LANES - MLA_QK_DIM

    uq = w_uq.reshape(MLA_Q_RANK, MLA_HEADS, MLA_QK_DIM)
    zq_lo = jnp.zeros((MLA_Q_RANK, MLA_HEADS, MLA_NOPE), F32)
    zq_hi = jnp.zeros((MLA_Q_RANK, MLA_HEADS, pad_hi), F32)
    w_qm = jnp.concatenate([uq, zq_hi], axis=-1).reshape(MLA_Q_RANK, MLA_HEADS * LANES)
    w_qs = jnp.concatenate([zq_lo, _rot_half_cols(uq[..., MLA_NOPE:]), zq_hi], axis=-1)
    w_qs = w_qs.reshape(MLA_Q_RANK, MLA_HEADS * LANES)

    ukv = w_ukv.reshape(MLA_KV_RANK, MLA_HEADS, MLA_NOPE + MLA_V)
    zk = jnp.zeros((MLA_KV_RANK, MLA_HEADS, LANES - MLA_NOPE), F32)
    w_k = jnp.concatenate([ukv[..., :MLA_NOPE], zk], axis=-1).reshape(MLA_KV_RANK, MLA_HEADS * LANES)
    w_v = ukv[..., MLA_NOPE:].reshape(MLA_KV_RANK, MLA_HEADS * MLA_V)
    return tuple(a.astype(BF16) for a in (w_qm.T, w_qs.T, w_k, w_v.T))


def _layer(layer, w_in_all, x, mem, positions, g_pre_mix, b_gate, mla_q_norm, w_uq, mla_kv_norm, w_ukv, g_mem, w_mem_kv,
           w_br_mla, w_br_dil, w_br_mem, w_o, g_post_mix, g_pre_ffn, w_ffn_up, conv_w, conv_b, w_ffn_down,
           g_post_ffn):
    batch, seq, dm = x.shape
    t = batch * seq
    x2 = x.reshape(t, dm)
    r2 = lambda v: v.reshape(1, -1)

    w_al = _align_w_in(w_in_all, layer)
    w_qmt, w_qst, w_k, w_vt = _prep_weights(w_uq, w_ukv)
    invf = (ROPE_THETA ** (-jnp.arange(ROPE_HALF, dtype=F32) / ROPE_HALF)).reshape(ROPE_HALF, 1)
    pos_rows = positions.reshape(t // PREP_TM, 1, PREP_TM)

    h2d, qt, k, vt = _prep(x2, pos_rows, invf, r2(g_pre_mix), w_al, r2(mla_q_norm), w_qmt, w_qst,
                           r2(mla_kv_norm), w_k, w_vt, batch, seq)
    y_mla = _mla(qt, k.reshape(batch, seq, MLA_HEADS * LANES), vt).reshape(t, MLA_HEADS * MLA_V)

    o_dil, lse_dil = [], []
    for g, (_, dil) in enumerate(DIL_PAIRS):
        qkv = _dilproj(h2d, w_al, batch, seq, dil, g)
        o, lse = _dilattn(qkv, dil, g)
        o_dil.append(o)
        lse_dil.append(lse)

    kvm = _memkv(mem.reshape(-1, dm), r2(g_mem), w_mem_kv.astype(BF16))

    x1, h2 = _merge(x2, h2d, w_al, r2(b_gate), kvm, y_mla, w_br_mla.astype(BF16), o_dil, lse_dil,
                    w_br_dil.astype(BF16),
                    w_br_mem.astype(BF16), w_o.astype(BF16), r2(g_post_mix), r2(g_pre_ffn), batch, seq)

    out = _ffn(x1, h2, w_ffn_up.astype(BF16), conv_w, r2(conv_b), w_ffn_down.astype(BF16), r2(g_post_ffn), seq)
    return out.reshape(batch, seq, dm)


def kernel(x, mem, positions, g_pre_mix, w_in, b_gate, mla_q_norm, w_uq, mla_kv_norm, w_ukv, g_mem, w_mem_kv,
           w_br_mla, w_br_dil, w_br_mem, w_o, g_post_mix, g_pre_ffn, w_ffn_up, conv_w, conv_b, w_ffn_down,
           g_post_ffn):
    for l in range(w_in.shape[0]):
        x = _layer(l, w_in, x, mem, positions, g_pre_mix[l], b_gate[l], mla_q_norm[l], w_uq[l], mla_kv_norm[l],
                   w_ukv[l], g_mem[l], w_mem_kv[l], w_br_mla[l], w_br_dil[l], w_br_mem[l], w_o[l], g_post_mix[l],
                   g_pre_ffn[l], w_ffn_up[l], conv_w[l], conv_b[l], w_ffn_down[l], g_post_ffn[l])
    return x
```

```python
import functools

import jax
import jax.numpy as jnp
from jax import lax
from jax.experimental import pallas as pl
from jax.experimental.pallas import tpu as pltpu

F32 = jnp.float32
BF16 = jnp.bfloat16

RMS_EPS = 1e-6
LOG2E = 1.4426950408889634
NEG_INF = -1e30
LANES = 128

BLOCK = 128
MLA_HEADS = 8
MLA_NOPE = 64
MLA_ROPE = 32
MLA_V = 64
MLA_QK_DIM = MLA_NOPE + MLA_ROPE
MLA_Q_RANK = 384
MLA_KV_RANK = 256
ROPE_THETA = 10000.0
ROPE_HALF = MLA_ROPE // 2

DIL_PAIRS = ((128, 1), (512, 4), (2048, 16))
DIL_GROUPS = 3
DIL_HPG = 4
DIL_HEADS = DIL_GROUPS * DIL_HPG
DIL_HEAD_DIM = 128
DIL_WIDTH = DIL_HPG * DIL_HEAD_DIM

MEM_HEADS = 4
MEM_HEAD_DIM = 128
MEM_WIDTH = MEM_HEADS * MEM_HEAD_DIM

N_BRANCH = 3
CONV_WIDTH = 3

OFF_Q = MLA_Q_RANK
OFF_KV = OFF_Q + MLA_KV_RANK
OFF_KR = OFF_KV + MLA_ROPE
OFF_DIL = OFF_KR + 3 * DIL_HEADS * DIL_HEAD_DIM
OFF_MEMQ = OFF_DIL + MEM_WIDTH
ALIGNED_DIL = 1024
ALIGNED_MEMQ = ALIGNED_DIL + (OFF_DIL - OFF_KR)
ALIGNED_GATE = ALIGNED_MEMQ + MEM_WIDTH

PREP_TM = 512
DILPROJ_TM = 1024
DILPROJ_MAX_STRIDE = 4
ALIGN_CB = 256
MLA_TQ = 1024
MLA_TK = 512
MLA_HPS = 2
MLA_QSTRIP = 256
MLA_VROWS = MLA_V + 16
assert PREP_TM == MLA_TK
DIL_TB = 1024
DIL_AHEAD = 4
MERGE_TM = 512
FFN_TM = 512
FFN_TF = 256
FFN_HALO = 16

VMEM_LIMIT = 56 * 1024 * 1024


def _rms(xf, g):
    return xf * lax.rsqrt(jnp.mean(xf * xf, axis=-1, keepdims=True) + RMS_EPS) * g


def _dot(a, b):
    return jnp.dot(a, b, preferred_element_type=F32)


def _dot_nt(a, b):
    return lax.dot_general(a, b, (((1,), (1,)), ((), ())), preferred_element_type=F32)


def _const_spec(shape):
    nd = len(shape)
    return pl.BlockSpec(shape, lambda *_: (0,) * nd)


def _prep_kernel(x_ref, pos_ref, invf_ref, g_ref, wa_ref, qn_ref, wqmt_ref, wqst_ref, kvn_ref, wk_ref, wvt_ref,
                 h_ref, qt_ref, k_ref, vt_ref):
    tm = x_ref.shape[0]
    h = _rms(x_ref[...], g_ref[...]).astype(BF16)
    h_ref[...] = h
    p = _dot_nt(h, wa_ref[...])
    cq = _rms(p[:, :OFF_Q], qn_ref[...]).astype(BF16)
    ckv = _rms(p[:, OFF_Q:OFF_KV], kvn_ref[...]).astype(BF16)

    ang = invf_ref[...] * pos_ref[0].astype(F32)
    c16 = jnp.cos(ang)
    s16 = jnp.sin(ang)
    ones = jnp.ones((MLA_NOPE, tm), F32)
    zeros_lo = jnp.zeros((MLA_NOPE, tm), F32)
    zeros_hi = jnp.zeros((LANES - MLA_QK_DIM, tm), F32)
    cos_c = jnp.concatenate([ones, c16, c16, zeros_hi], axis=0)
    sin_c = jnp.concatenate([zeros_lo, s16, s16, zeros_hi], axis=0)

    qmt = _dot_nt(wqmt_ref[...], cq)
    qst = _dot_nt(wqst_ref[...], cq)
    qscale = MLA_QK_DIM ** -0.5 * LOG2E
    cos_rope = jnp.concatenate([c16, c16], axis=0)
    sin_rope = jnp.concatenate([s16, s16], axis=0)
    for hd in range(MLA_HEADS):
        base = hd * LANES
        nope = qmt[hd * MLA_QK_DIM:hd * MLA_QK_DIM + MLA_NOPE, :]
        rope = qmt[hd * MLA_QK_DIM + MLA_NOPE:(hd + 1) * MLA_QK_DIM, :]
        rot = qst[hd * MLA_ROPE:(hd + 1) * MLA_ROPE, :]
        qt_ref[base:base + MLA_NOPE, :] = (nope * qscale).astype(BF16)
        qt_ref[base + MLA_NOPE:base + MLA_QK_DIM, :] = ((rope * cos_rope + rot * sin_rope) * qscale).astype(BF16)
        qt_ref[base + MLA_QK_DIM:base + LANES, :] = jnp.zeros((LANES - MLA_QK_DIM, tm), BF16)

    cos_r = cos_c.T
    sin_r = sin_c.T
    kn = _dot(ckv, wk_ref[...])
    kpe = p[:, OFF_KV:OFF_KV + LANES] * cos_r + p[:, OFF_KV + LANES:OFF_KV + 2 * LANES] * sin_r
    for hd in range(MLA_HEADS):
        sl = slice(hd * LANES, (hd + 1) * LANES)
        k_ref[:, sl] = (kn[:, sl] + kpe).astype(BF16)
    vt = _dot_nt(wvt_ref[...], ckv).astype(BF16)
    ones_rows = jnp.ones((MLA_VROWS - MLA_V, tm), BF16)
    for hd in range(MLA_HEADS):
        vt_ref[0, hd * MLA_VROWS:hd * MLA_VROWS + MLA_V, :] = vt[hd * MLA_V:(hd + 1) * MLA_V, :]
        vt_ref[0, hd * MLA_VROWS + MLA_V:(hd + 1) * MLA_VROWS, :] = ones_rows


def _prep(x2, pos_rows, invf, g_pre, w_al, q_norm, w_qmt, w_qst, kv_norm, w_k, w_vt, batch, seq):
    t, d = x2.shape
    tm = PREP_TM
    tps = seq // tm
    hw = MLA_HEADS * LANES
    vw = MLA_HEADS * MLA_VROWS
    row = lambda i: (i, 0)
    return pl.pallas_call(
        _prep_kernel,
        grid=(t // tm,),
        in_specs=[
            pl.BlockSpec((tm, d), row),
            pl.BlockSpec((1, 1, tm), lambda i: (i, 0, 0)),
            _const_spec(invf.shape),
            _const_spec(g_pre.shape),
            pl.BlockSpec((OFF_KV + 2 * LANES, d), lambda i: (0, 0)),
            _const_spec(q_norm.shape),
            _const_spec(w_qmt.shape),
            _const_spec(w_qst.shape),
            _const_spec(kv_norm.shape),
            _const_spec(w_k.shape),
            _const_spec(w_vt.shape),
        ],
        out_specs=[pl.BlockSpec((tm, d), row),
                   pl.BlockSpec((None, hw, tm), lambda i: (i // tps, 0, i % tps)),
                   pl.BlockSpec((tm, hw), row),
                   pl.BlockSpec((None, 1, vw, tm), lambda i: (i // tps, i % tps, 0, 0))],
        out_shape=[jax.ShapeDtypeStruct((t, d), BF16),
                   jax.ShapeDtypeStruct((batch, hw, seq), BF16),
                   jax.ShapeDtypeStruct((t, hw), BF16),
                   jax.ShapeDtypeStruct((batch, tps, vw, tm), BF16)],
        compiler_params=pltpu.CompilerParams(dimension_semantics=("arbitrary",), vmem_limit_bytes=VMEM_LIMIT),
        name="prep",
    )(x2, pos_rows, invf, g_pre, w_al, q_norm, w_qmt, w_qst, kv_norm, w_k, w_vt)


def _dilproj_kernel(h_ref, wq_ref, wk_ref, wv_ref, o_ref, *acc_refs, dil):
    h = h_ref[...]
    w_refs = (wq_ref, wk_ref, wv_ref)
    tm, tn = h_ref.shape[0], wq_ref.shape[0]
    if dil == 1:
        for j, w_ref in enumerate(w_refs):
            o_ref[0, :, j * tn:(j + 1) * tn] = _dot_nt(h, w_ref[...]).astype(BF16)
        return
    chunks = tn // LANES
    rows = tm // dil

    def project(j):
        acc = _dot_nt(h, w_refs[j][...])
        for c in range(chunks):
            acc_refs[j][c] = acc[:, c * LANES:(c + 1) * LANES]

    def scatter(j):
        src = acc_refs[j]
        if dil > DILPROJ_MAX_STRIDE:
            s1 = DILPROJ_MAX_STRIDE
            s2 = dil // s1
            tmp = acc_refs[-1]
            for r1 in range(s1):
                for c in range(chunks):
                    tmp[c, r1 * (tm // s1):(r1 + 1) * (tm // s1), :] = src[c, pl.ds(r1, tm // s1, stride=s1), :]
            for r in range(dil):
                r1, r2 = r % s1, r // s1
                for c in range(chunks):
                    cols = slice(j * tn + c * LANES, j * tn + (c + 1) * LANES)
                    o_ref[r, :, cols] = tmp[c, pl.ds(r1 * (tm // s1) + r2, rows, stride=s2), :].astype(BF16)
            return
        for r in range(dil):
            for c in range(chunks):
                cols = slice(j * tn + c * LANES, j * tn + (c + 1) * LANES)
                o_ref[r, :, cols] = src[c, pl.ds(r, rows, stride=dil), :].astype(BF16)

    project(0)
    project(1)
    scatter(0)
    project(2)
    scatter(1)
    scatter(2)


def _dilproj(h2, w_al, batch, seq, dil, group):
    t, d = h2.shape
    n = 3 * DIL_WIDTH
    tm, tn = DILPROJ_TM, DIL_WIDTH
    assert ALIGNED_DIL % tn == 0
    tiles_per_seq = seq // tm
    col0 = ALIGNED_DIL // tn + group
    w_spec = lambda j: pl.BlockSpec((tn, d), lambda i: (col0 + j * DIL_GROUPS, 0))
    n_scratch = 0 if dil == 1 else (3 if dil <= DILPROJ_MAX_STRIDE else 4)
    scratch = [pltpu.VMEM((tn // LANES, tm, LANES), F32)] * n_scratch
    return pl.pallas_call(
        functools.partial(_dilproj_kernel, dil=dil),
        grid=(t // tm,),
        in_specs=[pl.BlockSpec((tm, d), lambda i: (i, 0)), w_spec(0), w_spec(1), w_spec(2)],
        out_specs=pl.BlockSpec((None, dil, tm // dil, n),
                               lambda i: (i // tiles_per_seq, 0, i % tiles_per_seq, 0)),
        out_shape=jax.ShapeDtypeStruct((batch, dil, seq // dil, n), BF16),
        scratch_shapes=scratch,
        compiler_params=pltpu.CompilerParams(dimension_semantics=("arbitrary",), vmem_limit_bytes=VMEM_LIMIT),
        name=f"dilproj{dil}",
    )(h2, w_al, w_al, w_al)


def _mla_kernel(qt_ref, k_ref, vt_ref, qtn_ref, k0n_ref, o_ref, m_ref, acc_ref, sta_ref, stb_ref):
    tq = qt_ref.shape[1]
    tk = MLA_TK
    qi = pl.program_id(2)
    nstrip = tq // MLA_QSTRIP
    m_ref[...] = jnp.full(m_ref.shape, NEG_INF, F32)
    acc_ref[...] = jnp.zeros(acc_ref.shape, F32)
    chains = [(a, hq) for a in range(MLA_HPS) for hq in range(nstrip)]

    def scores(c, i):
        a, hq = chains[i]
        k = k_ref[pl.ds(pl.multiple_of(c * tk, tk), tk), a * LANES:(a + 1) * LANES]
        return _dot(k, qt_ref[a * LANES:(a + 1) * LANES, hq * MLA_QSTRIP:(hq + 1) * MLA_QSTRIP])

    def successor_scores(i):
        a, hq = chains[i]
        return _dot(k0n_ref[:, a * LANES:(a + 1) * LANES],
                    qtn_ref[a * LANES:(a + 1) * LANES, hq * MLA_QSTRIP:(hq + 1) * MLA_QSTRIP])

    def step(c, src_ref, dst_ref, key_off=None, last=False):
        def live(hq, off):
            return off is None or (hq + 1) * MLA_QSTRIP > off

        next_off = None if key_off is None else key_off + tk
        m_all = m_ref[...]
        acc_all = acc_ref[...]
        m_new, acc_new = {}, {}
        for i, (a, hq) in enumerate(chains):
            qs = slice(hq * MLA_QSTRIP, (hq + 1) * MLA_QSTRIP)
            if last:
                dst_ref[i] = successor_scores(i)
            elif live(hq, next_off):
                dst_ref[i] = scores(c + 1, i)
            if not live(hq, key_off):
                m_new[a, hq], acc_new[a, hq] = m_all[a, :, qs], acc_all[a, :, qs]
                continue
            st = src_ref[i]
            if key_off is not None and key_off + tk - 1 > hq * MLA_QSTRIP:
                key = lax.broadcasted_iota(jnp.int32, st.shape, 0) + key_off
                qry = lax.broadcasted_iota(jnp.int32, st.shape, 1) + hq * MLA_QSTRIP
                st = jnp.where(key <= qry, st, NEG_INF)
            m_prev = m_all[a, :, qs]
            m_cur = jnp.maximum(m_prev, jnp.max(st, axis=0, keepdims=True))
            alpha = jnp.exp2(m_prev - m_cur)
            p = jnp.exp2(st - m_cur).astype(BF16)
            vt = vt_ref[c, a * MLA_VROWS:(a + 1) * MLA_VROWS, :]
            acc_new[a, hq] = alpha * acc_all[a, :, qs] + _dot(vt, p)
            m_new[a, hq] = m_cur
        for a in range(MLA_HPS):
            m_ref[a] = jnp.concatenate([m_new[a, hq] for hq in range(nstrip)], axis=-1)
            acc_ref[a] = jnp.concatenate([acc_new[a, hq] for hq in range(nstrip)], axis=-1)

    @pl.when(jnp.logical_and(jnp.logical_and(pl.program_id(0) == 0, pl.program_id(1) == 0), qi == 0))
    def _():
        for i in range(len(chains)):
            sta_ref[i] = scores(0, i)

    assert tq == 2 * tk

    def quad(j, carry):
        for c in range(4):
            step(4 * j + c, (sta_ref, stb_ref)[c % 2], (stb_ref, sta_ref)[c % 2])
        return carry

    lax.fori_loop(0, qi // 2, quad, 0)

    @pl.when(qi % 2 == 1)
    def _():
        step(2 * qi - 2, sta_ref, stb_ref)
        step(2 * qi - 1, stb_ref, sta_ref)

    step(2 * qi, sta_ref, stb_ref, key_off=0)
    step(2 * qi + 1, stb_ref, sta_ref, key_off=tk, last=True)

    out_t = jnp.concatenate([acc_ref[a, :MLA_V, :] / acc_ref[a, MLA_V:MLA_V + 1, :] for a in range(MLA_HPS)],
                            axis=0)
    o_ref[...] = out_t.T.astype(BF16)


def _mla(qt, k3, vt):
    b, s, _ = k3.shape
    tq, tk = MLA_TQ, MLA_TK
    hps = MLA_HPS
    st_scratch = pltpu.VMEM((hps * (tq // MLA_QSTRIP), tk, MLA_QSTRIP), F32)
    nh, nq = MLA_HEADS // hps, s // tq

    def successor(bi, h, i):
        wrap_i = i == nq - 1
        wrap_h = jnp.logical_and(wrap_i, h == nh - 1)
        i2 = jnp.where(wrap_i, 0, i + 1)
        h2 = jnp.where(wrap_h, 0, jnp.where(wrap_i, h + 1, h))
        b2 = jnp.minimum(bi + wrap_h.astype(jnp.int32), b - 1)
        return b2, h2, i2
    return pl.pallas_call(
        _mla_kernel,
        grid=(b, MLA_HEADS // hps, s // tq),
        in_specs=[
            pl.BlockSpec((None, hps * LANES, tq), lambda bi, h, i: (bi, h, i)),
            pl.BlockSpec((None, s, hps * LANES), lambda bi, h, i: (bi, 0, h)),
            pl.BlockSpec((None, s // tk, hps * MLA_VROWS, tk), lambda bi, h, i: (bi, 0, h, 0)),
            pl.BlockSpec((None, hps * LANES, tq), lambda bi, h, i: successor(bi, h, i)[:3]),
            pl.BlockSpec((None, tk, hps * LANES), lambda bi, h, i: (successor(bi, h, i)[0], 0, successor(bi, h, i)[1])),
        ],
        out_specs=pl.BlockSpec((None, tq, hps * MLA_V), lambda bi, h, i: (bi, i, h)),
        out_shape=jax.ShapeDtypeStruct((b, s, MLA_HEADS * MLA_V), BF16),
        scratch_shapes=[pltpu.VMEM((hps, 1, tq), F32), pltpu.VMEM((hps, MLA_VROWS, tq), F32),
                        st_scratch, st_scratch],
        compiler_params=pltpu.CompilerParams(dimension_semantics=("arbitrary", "arbitrary", "arbitrary"),
                                             vmem_limit_bytes=VMEM_LIMIT),
        name="mla",
    )(qt, k3, vt, qt, k3)


def _dilattn_kernel(q_ref, kc_ref, vc_ref, kp_ref, vp_ref, o_ref, lse_ref, kx_ref, vx_ref, *, dil, group):
    n = pl.program_id(2)
    tb = q_ref.shape[0]
    nblk = tb // BLOCK
    kx_ref[:BLOCK, :] = kp_ref[...]
    kx_ref[BLOCK:, :] = kc_ref[...]
    vx_ref[:BLOCK, :] = vp_ref[...]
    vx_ref[BLOCK:, :] = vc_ref[...]

    qry = lax.broadcasted_iota(jnp.int32, (BLOCK, 2 * BLOCK), 0)
    key = lax.broadcasted_iota(jnp.int32, (BLOCK, 2 * BLOCK), 1)
    dist = qry + BLOCK - key
    in_window = jnp.logical_and(dist >= 0, dist <= BLOCK)
    first_ok = jnp.logical_and(in_window, jnp.logical_or(key >= BLOCK, n > 0))
    distf = (dist * dil).astype(F32)
    lane = lax.broadcasted_iota(jnp.int32, (BLOCK, LANES), 1)
    scale = DIL_HEAD_DIM ** -0.5
    bias, bias_first = [], []
    for hh in range(DIL_HPG):
        slope = float(2.0 ** (-8.0 * (hh * DIL_GROUPS + group + 1) / DIL_HEADS))
        bias.append(jnp.where(in_window, -slope * distf, NEG_INF))
        bias_first.append(jnp.where(first_ok, -slope * distf, NEG_INF))

    units = [(t, hh) for t in range(nblk) for hh in range(DIL_HPG)]

    def scores(u):
        t, hh = units[u]
        cs = slice(hh * DIL_HEAD_DIM, (hh + 1) * DIL_HEAD_DIM)
        return _dot_nt(q_ref[t * BLOCK:(t + 1) * BLOCK, cs], kx_ref[t * BLOCK:(t + 2) * BLOCK, cs])

    pending = [scores(u) for u in range(min(DIL_AHEAD, len(units)))]
    lse_tile = None
    for u, (t, hh) in enumerate(units):
        if u + DIL_AHEAD < len(units):
            pending.append(scores(u + DIL_AHEAD))
        rs = slice(t * BLOCK, (t + 1) * BLOCK)
        cs = slice(hh * DIL_HEAD_DIM, (hh + 1) * DIL_HEAD_DIM)
        s = pending[u] * scale + (bias_first[hh] if t == 0 else bias[hh])
        m = jnp.max(s, axis=-1, keepdims=True)
        e = jnp.exp(s - m)
        den = jnp.sum(e, axis=-1, keepdims=True)
        o = _dot(e.astype(BF16), vx_ref[t * BLOCK:(t + 2) * BLOCK, cs]) / den
        o_ref[rs, cs] = o.astype(BF16)
        lse = m + jnp.log(den)
        lse_tile = jnp.where(lane == hh, lse, jnp.zeros((BLOCK, LANES), F32) if hh == 0 else lse_tile)
        if hh == DIL_HPG - 1:
            lse_ref[rs, :] = lse_tile


def _dilattn(qkv, dil, group):
    b, d, l, _ = qkv.shape
    tb = min(DIL_TB, l)
    bpt = tb // BLOCK
    w = DIL_WIDTH
    cur = lambda c: pl.BlockSpec((None, None, tb, w), lambda bi, r, n: (bi, r, n, c))
    prev = lambda c: pl.BlockSpec((None, None, BLOCK, w), lambda bi, r, n: (bi, r, jnp.maximum(n * bpt - 1, 0), c))
    return pl.pallas_call(
        functools.partial(_dilattn_kernel, dil=dil, group=group),
        grid=(b, d, l // tb),
        in_specs=[cur(0), cur(1), cur(2), prev(1), prev(2)],
        out_specs=[pl.BlockSpec((None, None, tb, w), lambda bi, r, n: (bi, r, n, 0)),
                   pl.BlockSpec((None, None, tb, LANES), lambda bi, r, n: (bi, r, n, 0))],
        out_shape=[jax.ShapeDtypeStruct((b, d, l, w), BF16), jax.ShapeDtypeStruct((b, d, l, LANES), F32)],
        scratch_shapes=[pltpu.VMEM((tb + BLOCK, w), BF16), pltpu.VMEM((tb + BLOCK, w), BF16)],
        compiler_params=pltpu.CompilerParams(dimension_semantics=("arbitrary", "arbitrary", "arbitrary"),
                                             vmem_limit_bytes=VMEM_LIMIT),
        name=f"dilattn{dil}",
    )(qkv, qkv, qkv, qkv, qkv)


def _memkv_kernel(mem_ref, g_ref, w_ref, o_ref):
    o_ref[...] = _dot(_rms(mem_ref[...], g_ref[...]).astype(BF16), w_ref[...]).astype(BF16)


def _memkv(mem2, g_mem, w):
    return pl.pallas_call(
        _memkv_kernel,
        out_shape=jax.ShapeDtypeStruct((mem2.shape[0], w.shape[1]), BF16),
        compiler_params=pltpu.CompilerParams(vmem_limit_bytes=VMEM_LIMIT),
        name="memkv",
    )(mem2, g_mem, w)


def _sigmoid(z):
    return 1.0 / (1.0 + jnp.exp(-z))


def _merge_kernel(x_ref, h_ref, wmq_ref, wg_ref, bg_ref, kvm_ref, ymla_ref, wbm_ref,
                  o0_ref, o1_ref, o2_ref, l0_ref, l1_ref, l2_ref, wbd_ref, wbmem_ref, wo_ref, gpm_ref, gpf_ref,
                  x1_ref, h2_ref, nat1_ref, nat2_ref, lse1_ref, lse2_ref):
    tm, dm = x_ref.shape
    h = h_ref[...]

    def gate(br):
        cs = slice(br * dm, (br + 1) * dm)
        return _dot_nt(h, wg_ref[cs, :]) + bg_ref[:, cs]

    for src, lsrc, dst, ldst in ((o1_ref, l1_ref, nat1_ref, lse1_ref), (o2_ref, l2_ref, nat2_ref, lse2_ref)):
        d = src.shape[0]
        rows = src.shape[1]
        for r in range(d):
            for hh in range(DIL_HPG):
                cs = slice(hh * DIL_HEAD_DIM, (hh + 1) * DIL_HEAD_DIM)
                dst[hh, pl.ds(r, rows, stride=d), :] = src[r, :, cs].astype(F32)
            ldst[pl.ds(r, rows, stride=d), :] = lsrc[r]
    lg = (l0_ref[0], lse1_ref[...], lse2_ref[...])
    heads = []
    for hh in range(DIL_HPG):
        cs = slice(hh * DIL_HEAD_DIM, (hh + 1) * DIL_HEAD_DIM)
        og = (o0_ref[0, :, cs].astype(F32), nat1_ref[hh], nat2_ref[hh])
        ls = [l[:, hh:hh + 1] for l in lg]
        mx = jnp.maximum(jnp.maximum(ls[0], ls[1]), ls[2])
        ws = [jnp.exp(l - mx) for l in ls]
        num = ws[0] * og[0] + ws[1] * og[1] + ws[2] * og[2]
        heads.append((num / (ws[0] + ws[1] + ws[2])).astype(BF16))
    y_dil = jnp.concatenate(heads, axis=-1)

    memq = _dot_nt(h, wmq_ref[...])
    mheads = []
    for hh in range(MEM_HEADS):
        cs = slice(hh * MEM_HEAD_DIM, (hh + 1) * MEM_HEAD_DIM)
        q = (memq[:, cs] * MEM_HEAD_DIM ** -0.5).astype(BF16)
        s = _dot_nt(q, kvm_ref[:, cs])
        e = jnp.exp(s - jnp.max(s, axis=-1, keepdims=True))
        o = _dot(e.astype(BF16), kvm_ref[:, MEM_WIDTH + hh * MEM_HEAD_DIM:MEM_WIDTH + (hh + 1) * MEM_HEAD_DIM])
        mheads.append((o / jnp.sum(e, axis=-1, keepdims=True)).astype(BF16))
    y_mem = jnp.concatenate(mheads, axis=-1)

    merged = jnp.zeros((tm, dm), F32)
    for br, (y, w_ref) in enumerate(((ymla_ref[...], wbm_ref), (y_dil, wbd_ref), (y_mem, wbmem_ref))):
        merged = merged + _sigmoid(gate(br)) * _dot(y, w_ref[...])
    mixed = _dot(merged.astype(BF16), wo_ref[...])
    x1 = x_ref[...] + _rms(mixed, gpm_ref[...])
    x1_ref[...] = x1
    h2_ref[...] = _rms(x1, gpf_ref[...]).astype(BF16)


def _merge(x2, h2, w_al, b_g, kvm, y_mla, w_bm, o_dil, lse_dil, w_bd, w_bmem, w_o, g_pm, g_pf, batch, seq):
    t, dm = x2.shape
    tm = MERGE_TM
    tps = seq // tm
    n_mem = kvm.shape[0] // batch
    row = lambda i: (i, 0)
    single = pl.Buffered(1)
    const = lambda a: pl.BlockSpec(a.shape, lambda i: (0,) * a.ndim, pipeline_mode=single)
    gate_w = N_BRANCH * dm
    assert ALIGNED_MEMQ % MEM_WIDTH == 0 and ALIGNED_GATE % gate_w == 0
    w_mq = pl.BlockSpec((MEM_WIDTH, dm), lambda i: (ALIGNED_MEMQ // MEM_WIDTH, 0), pipeline_mode=single)
    w_g = pl.BlockSpec((gate_w, dm), lambda i: (ALIGNED_GATE // gate_w, 0), pipeline_mode=single)

    def dil_spec(a):
        d, width = a.shape[1], a.shape[3]
        return pl.BlockSpec((None, d, tm // d, width), lambda i: (i // tps, 0, i % tps, 0))

    return pl.pallas_call(
        _merge_kernel,
        grid=(t // tm,),
        in_specs=[
            pl.BlockSpec((tm, dm), row), pl.BlockSpec((tm, dm), row), w_mq, w_g, const(b_g),
            pl.BlockSpec((n_mem, kvm.shape[1]), lambda i: (i // tps, 0)),
            pl.BlockSpec((tm, y_mla.shape[1]), row), const(w_bm),
            dil_spec(o_dil[0]), dil_spec(o_dil[1]), dil_spec(o_dil[2]),
            dil_spec(lse_dil[0]), dil_spec(lse_dil[1]), dil_spec(lse_dil[2]),
            const(w_bd), const(w_bmem), const(w_o), const(g_pm), const(g_pf),
        ],
        out_specs=[pl.BlockSpec((tm, dm), row), pl.BlockSpec((tm, dm), row)],
        out_shape=[jax.ShapeDtypeStruct((t, dm), F32), jax.ShapeDtypeStruct((t, dm), BF16)],
        scratch_shapes=[pltpu.VMEM((DIL_HPG, tm, DIL_HEAD_DIM), F32), pltpu.VMEM((DIL_HPG, tm, DIL_HEAD_DIM), F32),
                        pltpu.VMEM((tm, LANES), F32), pltpu.VMEM((tm, LANES), F32)],
        compiler_params=pltpu.CompilerParams(dimension_semantics=("arbitrary",), vmem_limit_bytes=VMEM_LIMIT),
        name="merge",
    )(x2, h2, w_al, w_al, b_g, kvm, y_mla, w_bm, o_dil[0], o_dil[1], o_dil[2], lse_dil[0], lse_dil[1], lse_dil[2],
      w_bd, w_bmem, w_o, g_pm, g_pf)


def _ffn_kernel(x1_ref, h2_ref, halo_ref, wup_ref, cw_ref, cb_ref, wd_ref, gpost_ref,
                out_ref, hcat_ref, ua_ref, ub_ref, acc_ref, *, tiles_per_seq):
    i = pl.program_id(0)
    tm = x1_ref.shape[0]
    halo, tf = FFN_HALO, FFN_TF
    dff = wd_ref.shape[0]
    nchunk = dff // tf
    lanes_per_chunk = tf // LANES

    first = (i % tiles_per_seq) == 0
    hcat_ref[:halo, :] = jnp.where(first, jnp.zeros_like(halo_ref[...]), halo_ref[...])
    hcat_ref[halo:, :] = h2_ref[...]

    def up(c, u_ref):
        hc = hcat_ref[...]
        for part, off in enumerate((c * tf, dff + c * tf)):
            u = _dot(hc, wup_ref[:, off:off + tf])
            for j in range(lanes_per_chunk):
                u_ref[part * lanes_per_chunk + j] = u[:, j * LANES:(j + 1) * LANES]

    def conv(u_ref, slab, col):
        cols = slice(col, col + LANES)
        z = cb_ref[:, cols] + cw_ref[0:1, cols] * u_ref[slab, halo - 2:halo - 2 + tm, :]
        z = z + cw_ref[1:2, cols] * u_ref[slab, halo - 1:halo - 1 + tm, :]
        return z + cw_ref[2:3, cols] * u_ref[slab, halo:halo + tm, :]

    bufs = (ua_ref, ub_ref)
    up(0, bufs[0])
    for c in range(nchunk):
        cur = bufs[c % 2]
        if c + 1 < nchunk:
            up(c + 1, bufs[(c + 1) % 2])
        acts = []
        for j in range(lanes_per_chunk):
            gate = conv(cur, j, c * tf + j * LANES)
            val = conv(cur, lanes_per_chunk + j, dff + c * tf + j * LANES)
            acts.append((gate * _sigmoid(gate) * val).astype(BF16))
        down = _dot(jnp.concatenate(acts, axis=-1), wd_ref[c * tf:(c + 1) * tf, :])
        if c == 0:
            acc_ref[...] = down
        else:
            acc_ref[...] += down

    out_ref[...] = x1_ref[...] + _rms(acc_ref[...], gpost_ref[...])


def _ffn(x1, h2, w_up, conv_w, conv_b, w_down, g_post, seq):
    t, dm = x1.shape
    tm, tf, halo = FFN_TM, FFN_TF, FFN_HALO
    tps = seq // tm
    row = lambda i: (i, 0)
    const = lambda a: pl.BlockSpec(a.shape, lambda i: (0,) * a.ndim, pipeline_mode=pl.Buffered(1))
    u_scratch = pltpu.VMEM((2 * tf // LANES, tm + halo, LANES), F32)
    return pl.pallas_call(
        functools.partial(_ffn_kernel, tiles_per_seq=tps),
        grid=(t // tm,),
        in_specs=[
            pl.BlockSpec((tm, dm), row),
            pl.BlockSpec((tm, dm), row),
            pl.BlockSpec((halo, dm), lambda i: (jnp.maximum(i * (tm // halo) - 1, 0), 0)),
            const(w_up), const(conv_w), const(conv_b), const(w_down), const(g_post),
        ],
        out_specs=pl.BlockSpec((tm, dm), row),
        out_shape=jax.ShapeDtypeStruct((t, dm), F32),
        scratch_shapes=[pltpu.VMEM((tm + halo, dm), BF16), u_scratch, u_scratch, pltpu.VMEM((tm, dm), F32)],
        compiler_params=pltpu.CompilerParams(dimension_semantics=("arbitrary",), vmem_limit_bytes=VMEM_LIMIT),
        name="ffn",
    )(x1, h2, h2, w_up, conv_w, conv_b, w_down, g_post)


def _rot_half_cols(w):
    return jnp.concatenate([-w[..., ROPE_HALF:], w[..., :ROPE_HALF]], axis=-1)


def _align_kernel(wt_ref, o_ref):
    cb = wt_ref.shape[1]
    zeros = lambda n: jnp.zeros((n, cb), BF16)
    kr = wt_ref[OFF_KV:OFF_KR, :]
    o_ref[:OFF_KV, :] = wt_ref[:OFF_KV, :].astype(BF16)
    o_ref[OFF_KV:OFF_KV + MLA_NOPE, :] = zeros(MLA_NOPE)
    o_ref[OFF_KV + MLA_NOPE:OFF_KV + MLA_QK_DIM, :] = kr.astype(BF16)
    o_ref[OFF_KV + MLA_QK_DIM:OFF_KV + LANES + MLA_NOPE, :] = zeros(LANES - MLA_QK_DIM + MLA_NOPE)
    o_ref[OFF_KV + LANES + MLA_NOPE:OFF_KV + LANES + MLA_NOPE + ROPE_HALF, :] = (-kr[ROPE_HALF:, :]).astype(BF16)
    o_ref[OFF_KV + LANES + MLA_NOPE + ROPE_HALF:OFF_KV + LANES + MLA_QK_DIM, :] = kr[:ROPE_HALF, :].astype(BF16)
    o_ref[OFF_KV + LANES + MLA_QK_DIM:ALIGNED_DIL, :] = zeros(ALIGNED_DIL - OFF_KV - LANES - MLA_QK_DIM)
    o_ref[ALIGNED_DIL:, :] = wt_ref[OFF_KR:, :].astype(BF16)


def _align_w_in(w_in_all, layer):
    wt_all = jnp.swapaxes(w_in_all, 1, 2)
    _, d_in, dm = wt_all.shape
    cb = ALIGN_CB
    width = ALIGNED_DIL + d_in - OFF_KR
    return pl.pallas_call(
        _align_kernel,
        grid=(dm // cb,),
        in_specs=[pl.BlockSpec((None, d_in, cb), lambda i: (layer, 0, i))],
        out_specs=pl.BlockSpec((width, cb), lambda i: (0, i)),
        out_shape=jax.ShapeDtypeStruct((width, dm), BF16),
        compiler_params=pltpu.CompilerParams(dimension_semantics=("arbitrary",), vmem_limit_bytes=VMEM_LIMIT),
        name="align",
    )(wt_all)


def _prep_weights(w_uq, w_ukv):
    uq = w_uq.reshape(MLA_Q_RANK, MLA_HEADS, MLA_QK_DIM)
    w_qm = w_uq
    w_qs = _rot_half_cols(uq[..., MLA_NOPE:]).reshape(MLA_Q_RANK, MLA_HEADS * MLA_ROPE)

    ukv = w_ukv.reshape(MLA_KV_RANK, MLA_HEADS, MLA_NOPE + MLA_V)
    zk = jnp.zeros((MLA_KV_RANK, MLA_HEADS, LANES - MLA_NOPE), F32)
    w_k = jnp.concatenate([ukv[..., :MLA_NOPE], zk], axis=-1).reshape(MLA_KV_RANK, MLA_HEADS * LANES)
    w_v = ukv[..., MLA_NOPE:].reshape(MLA_KV_RANK, MLA_HEADS * MLA_V)
    return tuple(a.astype(BF16) for a in (w_qm.T, w_qs.T, w_k, w_v.T))


def _layer(layer, w_in_all, x, mem, positions, g_pre_mix, b_gate, mla_q_norm, w_uq, mla_kv_norm, w_ukv, g_mem, w_mem_kv,
           w_br_mla, w_br_dil, w_br_mem, w_o, g_post_mix, g_pre_ffn, w_ffn_up, conv_w, conv_b, w_ffn_down,
           g_post_ffn):
    batch, seq, dm = x.shape
    t = batch * seq
    x2 = x.reshape(t, dm)
    r2 = lambda v: v.reshape(1, -1)

    w_al = _align_w_in(w_in_all, layer)
    w_qmt, w_qst, w_k, w_vt = _prep_weights(w_uq, w_ukv)
    invf = (ROPE_THETA ** (-jnp.arange(ROPE_HALF, dtype=F32) / ROPE_HALF)).reshape(ROPE_HALF, 1)
    pos_rows = positions.reshape(t // PREP_TM, 1, PREP_TM)

    h2d, qt, k, vt = _prep(x2, pos_rows, invf, r2(g_pre_mix), w_al, r2(mla_q_norm), w_qmt, w_qst,
                           r2(mla_kv_norm), w_k, w_vt, batch, seq)
    y_mla = _mla(qt, k.reshape(batch, seq, MLA_HEADS * LANES), vt).reshape(t, MLA_HEADS * MLA_V)

    o_dil, lse_dil = [], []
    for g, (_, dil) in enumerate(DIL_PAIRS):
        qkv = _dilproj(h2d, w_al, batch, seq, dil, g)
        o, lse = _dilattn(qkv, dil, g)
        o_dil.append(o)
        lse_dil.append(lse)

    kvm = _memkv(mem.reshape(-1, dm), r2(g_mem), w_mem_kv.astype(BF16))

    x1, h2 = _merge(x2, h2d, w_al, r2(b_gate), kvm, y_mla, w_br_mla.astype(BF16), o_dil, lse_dil,
                    w_br_dil.astype(BF16),
                    w_br_mem.astype(BF16), w_o.astype(BF16), r2(g_post_mix), r2(g_pre_ffn), batch, seq)

    out = _ffn(x1, h2, w_ffn_up.astype(BF16), conv_w, r2(conv_b), w_ffn_down.astype(BF16), r2(g_post_ffn), seq)
    return out.reshape(batch, seq, dm)


def kernel(x, mem, positions, g_pre_mix, w_in, b_gate, mla_q_norm, w_uq, mla_kv_norm, w_ukv, g_mem, w_mem_kv,
           w_br_mla, w_br_dil, w_br_mem, w_o, g_post_mix, g_pre_ffn, w_ffn_up, conv_w, conv_b, w_ffn_down,
           g_post_ffn):
    for l in range(w_in.shape[0]):
        x = _layer(l, w_in, x, mem, positions, g_pre_mix[l], b_gate[l], mla_q_norm[l], w_uq[l], mla_kv_norm[l],
                   w_ukv[l], g_mem[l], w_mem_kv[l], w_br_mla[l], w_br_dil[l], w_br_mem[l], w_o[l], g_post_mix[l],
                   g_pre_ffn[l], w_ffn_up[l], conv_w[l], conv_b[l], w_ffn_down[l], g_post_ffn[l])
    return x
```

```python
import functools

import jax
import jax.numpy as jnp
from jax import lax
from jax.experimental import pallas as pl
from jax.experimental.pallas import tpu as pltpu

F32 = jnp.float32
BF16 = jnp.bfloat16

RMS_EPS = 1e-6
LOG2E = 1.4426950408889634
NEG_INF = -1e30
LANES = 128

BLOCK = 128
MLA_HEADS = 8
MLA_NOPE = 64
MLA_ROPE = 32
MLA_V = 64
MLA_QK_DIM = MLA_NOPE + MLA_ROPE
MLA_Q_RANK = 384
MLA_KV_RANK = 256
ROPE_THETA = 10000.0
ROPE_HALF = MLA_ROPE // 2

DIL_PAIRS = ((128, 1), (512, 4), (2048, 16))
DIL_GROUPS = 3
DIL_HPG = 4
DIL_HEADS = DIL_GROUPS * DIL_HPG
DIL_HEAD_DIM = 128
DIL_WIDTH = DIL_HPG * DIL_HEAD_DIM

MEM_HEADS = 4
MEM_HEAD_DIM = 128
MEM_WIDTH = MEM_HEADS * MEM_HEAD_DIM

N_BRANCH = 3
CONV_WIDTH = 3

OFF_Q = MLA_Q_RANK
OFF_KV = OFF_Q + MLA_KV_RANK
OFF_KR = OFF_KV + MLA_ROPE
OFF_DIL = OFF_KR + 3 * DIL_HEADS * DIL_HEAD_DIM
OFF_MEMQ = OFF_DIL + MEM_WIDTH
ALIGNED_DIL = 1024
ALIGNED_MEMQ = ALIGNED_DIL + (OFF_DIL - OFF_KR)
ALIGNED_GATE = ALIGNED_MEMQ + MEM_WIDTH

PREP_TM = 512
DILPROJ_TM = 1024
DILPROJ_MAX_STRIDE = 4
ALIGN_CB = 256
MLA_TQ = 2048
MLA_TK = 512
MLA_HPS = 2
MLA_QSTRIP = 256
MLA_VROWS = MLA_V + 16
assert PREP_TM == MLA_TK
DIL_TB = 1024
DIL_AHEAD = 4
MERGE_TM = 512
FFN_TM = 512
FFN_TF = 256
FFN_HALO = 16

VMEM_LIMIT = 56 * 1024 * 1024


def _rms(xf, g):
    return xf * lax.rsqrt(jnp.mean(xf * xf, axis=-1, keepdims=True) + RMS_EPS) * g


def _dot(a, b):
    return jnp.dot(a, b, preferred_element_type=F32)


def _dot_nt(a, b):
    return lax.dot_general(a, b, (((1,), (1,)), ((), ())), preferred_element_type=F32)


def _const_spec(shape):
    nd = len(shape)
    return pl.BlockSpec(shape, lambda *_: (0,) * nd)


def _prep_kernel(x_ref, pos_ref, invf_ref, g_ref, wa_ref, qn_ref, wqmt_ref, wqst_ref, kvn_ref, wk_ref, wvt_ref,
                 h_ref, qt_ref, k_ref, vt_ref):
    tm = x_ref.shape[0]
    h = _rms(x_ref[...], g_ref[...]).astype(BF16)
    h_ref[...] = h
    p = _dot_nt(h, wa_ref[...])
    cq = _rms(p[:, :OFF_Q], qn_ref[...]).astype(BF16)
    ckv = _rms(p[:, OFF_Q:OFF_KV], kvn_ref[...]).astype(BF16)

    ang = invf_ref[...] * pos_ref[0].astype(F32)
    c16 = jnp.cos(ang)
    s16 = jnp.sin(ang)
    ones = jnp.ones((MLA_NOPE, tm), F32)
    zeros_lo = jnp.zeros((MLA_NOPE, tm), F32)
    zeros_hi = jnp.zeros((LANES - MLA_QK_DIM, tm), F32)
    cos_c = jnp.concatenate([ones, c16, c16, zeros_hi], axis=0)
    sin_c = jnp.concatenate([zeros_lo, s16, s16, zeros_hi], axis=0)

    qmt = _dot_nt(wqmt_ref[...], cq)
    qst = _dot_nt(wqst_ref[...], cq)
    qscale = MLA_QK_DIM ** -0.5 * LOG2E
    cos_rope = jnp.concatenate([c16, c16], axis=0)
    sin_rope = jnp.concatenate([s16, s16], axis=0)
    for hd in range(MLA_HEADS):
        base = hd * LANES
        nope = qmt[hd * MLA_QK_DIM:hd * MLA_QK_DIM + MLA_NOPE, :]
        rope = qmt[hd * MLA_QK_DIM + MLA_NOPE:(hd + 1) * MLA_QK_DIM, :]
        rot = qst[hd * MLA_ROPE:(hd + 1) * MLA_ROPE, :]
        qt_ref[base:base + MLA_NOPE, :] = (nope * qscale).astype(BF16)
        qt_ref[base + MLA_NOPE:base + MLA_QK_DIM, :] = ((rope * cos_rope + rot * sin_rope) * qscale).astype(BF16)
        qt_ref[base + MLA_QK_DIM:base + LANES, :] = jnp.zeros((LANES - MLA_QK_DIM, tm), BF16)

    cos_r = cos_c.T
    sin_r = sin_c.T
    kn = _dot(ckv, wk_ref[...])
    kpe = p[:, OFF_KV:OFF_KV + LANES] * cos_r + p[:, OFF_KV + LANES:OFF_KV + 2 * LANES] * sin_r
    for hd in range(MLA_HEADS):
        sl = slice(hd * LANES, (hd + 1) * LANES)
        k_ref[:, sl] = (kn[:, sl] + kpe).astype(BF16)
    vt = _dot_nt(wvt_ref[...], ckv).astype(BF16)
    ones_rows = jnp.ones((MLA_VROWS - MLA_V, tm), BF16)
    for hd in range(MLA_HEADS):
        vt_ref[0, hd * MLA_VROWS:hd * MLA_VROWS + MLA_V, :] = vt[hd * MLA_V:(hd + 1) * MLA_V, :]
        vt_ref[0, hd * MLA_VROWS + MLA_V:(hd + 1) * MLA_VROWS, :] = ones_rows


def _prep(x2, pos_rows, invf, g_pre, w_al, q_norm, w_qmt, w_qst, kv_norm, w_k, w_vt, batch, seq):
    t, d = x2.shape
    tm = PREP_TM
    tps = seq // tm
    hw = MLA_HEADS * LANES
    vw = MLA_HEADS * MLA_VROWS
    row = lambda i: (i, 0)
    return pl.pallas_call(
        _prep_kernel,
        grid=(t // tm,),
        in_specs=[
            pl.BlockSpec((tm, d), row),
            pl.BlockSpec((1, 1, tm), lambda i: (i, 0, 0)),
            _const_spec(invf.shape),
            _const_spec(g_pre.shape),
            pl.BlockSpec((OFF_KV + 2 * LANES, d), lambda i: (0, 0)),
            _const_spec(q_norm.shape),
            _const_spec(w_qmt.shape),
            _const_spec(w_qst.shape),
            _const_spec(kv_norm.shape),
            _const_spec(w_k.shape),
            _const_spec(w_vt.shape),
        ],
        out_specs=[pl.BlockSpec((tm, d), row),
                   pl.BlockSpec((None, hw, tm), lambda i: (i // tps, 0, i % tps)),
                   pl.BlockSpec((tm, hw), row),
                   pl.BlockSpec((None, 1, vw, tm), lambda i: (i // tps, i % tps, 0, 0))],
        out_shape=[jax.ShapeDtypeStruct((t, d), BF16),
                   jax.ShapeDtypeStruct((batch, hw, seq), BF16),
                   jax.ShapeDtypeStruct((t, hw), BF16),
                   jax.ShapeDtypeStruct((batch, tps, vw, tm), BF16)],
        compiler_params=pltpu.CompilerParams(dimension_semantics=("arbitrary",), vmem_limit_bytes=VMEM_LIMIT),
        name="prep",
    )(x2, pos_rows, invf, g_pre, w_al, q_norm, w_qmt, w_qst, kv_norm, w_k, w_vt)


def _dilproj_kernel(h_ref, wq_ref, wk_ref, wv_ref, o_ref, *acc_refs, dil):
    h = h_ref[...]
    w_refs = (wq_ref, wk_ref, wv_ref)
    tm, tn = h_ref.shape[0], wq_ref.shape[0]
    if dil == 1:
        for j, w_ref in enumerate(w_refs):
            o_ref[0, :, j * tn:(j + 1) * tn] = _dot_nt(h, w_ref[...]).astype(BF16)
        return
    chunks = tn // LANES
    rows = tm // dil

    def project(j):
        acc = _dot_nt(h, w_refs[j][...])
        for c in range(chunks):
            acc_refs[j][c] = acc[:, c * LANES:(c + 1) * LANES]

    def scatter(j):
        src = acc_refs[j]
        if dil > DILPROJ_MAX_STRIDE:
            s1 = DILPROJ_MAX_STRIDE
            s2 = dil // s1
            tmp = acc_refs[-1]
            for r1 in range(s1):
                for c in range(chunks):
                    tmp[c, r1 * (tm // s1):(r1 + 1) * (tm // s1), :] = src[c, pl.ds(r1, tm // s1, stride=s1), :]
            for r in range(dil):
                r1, r2 = r % s1, r // s1
                for c in range(chunks):
                    cols = slice(j * tn + c * LANES, j * tn + (c + 1) * LANES)
                    o_ref[r, :, cols] = tmp[c, pl.ds(r1 * (tm // s1) + r2, rows, stride=s2), :].astype(BF16)
            return
        for r in range(dil):
            for c in range(chunks):
                cols = slice(j * tn + c * LANES, j * tn + (c + 1) * LANES)
                o_ref[r, :, cols] = src[c, pl.ds(r, rows, stride=dil), :].astype(BF16)

    project(0)
    project(1)
    scatter(0)
    project(2)
    scatter(1)
    scatter(2)


def _dilproj(h2, w_al, batch, seq, dil, group):
    t, d = h2.shape
    n = 3 * DIL_WIDTH
    tm, tn = DILPROJ_TM, DIL_WIDTH
    assert ALIGNED_DIL % tn == 0
    tiles_per_seq = seq // tm
    col0 = ALIGNED_DIL // tn + group
    w_spec = lambda j: pl.BlockSpec((tn, d), lambda i: (col0 + j * DIL_GROUPS, 0))
    n_scratch = 0 if dil == 1 else (3 if dil <= DILPROJ_MAX_STRIDE else 4)
    scratch = [pltpu.VMEM((tn // LANES, tm, LANES), F32)] * n_scratch
    return pl.pallas_call(
        functools.partial(_dilproj_kernel, dil=dil),
        grid=(t // tm,),
        in_specs=[pl.BlockSpec((tm, d), lambda i: (i, 0)), w_spec(0), w_spec(1), w_spec(2)],
        out_specs=pl.BlockSpec((None, dil, tm // dil, n),
                               lambda i: (i // tiles_per_seq, 0, i % tiles_per_seq, 0)),
        out_shape=jax.ShapeDtypeStruct((batch, dil, seq // dil, n), BF16),
        scratch_shapes=scratch,
        compiler_params=pltpu.CompilerParams(dimension_semantics=("arbitrary",), vmem_limit_bytes=VMEM_LIMIT),
        name=f"dilproj{dil}",
    )(h2, w_al, w_al, w_al)


def _mla_kernel(qt_ref, k_ref, vt_ref, qtn_ref, k0n_ref, o_ref, m_ref, acc_ref, sta_ref, stb_ref):
    tq = qt_ref.shape[1]
    tk = MLA_TK
    qi = pl.program_id(2)
    nstrip = tq // MLA_QSTRIP
    m_ref[...] = jnp.full(m_ref.shape, NEG_INF, F32)
    acc_ref[...] = jnp.zeros(acc_ref.shape, F32)
    chains = [(a, hq) for a in range(MLA_HPS) for hq in range(nstrip)]

    def scores(c, i):
        a, hq = chains[i]
        k = k_ref[pl.ds(pl.multiple_of(c * tk, tk), tk), a * LANES:(a + 1) * LANES]
        return _dot(k, qt_ref[a * LANES:(a + 1) * LANES, hq * MLA_QSTRIP:(hq + 1) * MLA_QSTRIP])

    def successor_scores(i):
        a, hq = chains[i]
        return _dot(k0n_ref[:, a * LANES:(a + 1) * LANES],
                    qtn_ref[a * LANES:(a + 1) * LANES, hq * MLA_QSTRIP:(hq + 1) * MLA_QSTRIP])

    def step(c, src_ref, dst_ref, key_off=None, last=False):
        def live(hq, off):
            return off is None or (hq + 1) * MLA_QSTRIP > off

        next_off = None if key_off is None else key_off + tk
        m_all = m_ref[...]
        acc_all = acc_ref[...]
        m_new, acc_new = {}, {}
        for i, (a, hq) in enumerate(chains):
            qs = slice(hq * MLA_QSTRIP, (hq + 1) * MLA_QSTRIP)
            if last:
                dst_ref[i] = successor_scores(i)
            elif live(hq, next_off):
                dst_ref[i] = scores(c + 1, i)
            if not live(hq, key_off):
                m_new[a, hq], acc_new[a, hq] = m_all[a, :, qs], acc_all[a, :, qs]
                continue
            st = src_ref[i]
            if key_off is not None and key_off + tk - 1 > hq * MLA_QSTRIP:
                key = lax.broadcasted_iota(jnp.int32, st.shape, 0) + key_off
                qry = lax.broadcasted_iota(jnp.int32, st.shape, 1) + hq * MLA_QSTRIP
                st = jnp.where(key <= qry, st, NEG_INF)
            m_prev = m_all[a, :, qs]
            m_cur = jnp.maximum(m_prev, jnp.max(st, axis=0, keepdims=True))
            alpha = jnp.exp2(m_prev - m_cur)
            p = jnp.exp2(st - m_cur).astype(BF16)
            vt = vt_ref[c, a * MLA_VROWS:(a + 1) * MLA_VROWS, :]
            acc_new[a, hq] = alpha * acc_all[a, :, qs] + _dot(vt, p)
            m_new[a, hq] = m_cur
        for a in range(MLA_HPS):
            m_ref[a] = jnp.concatenate([m_new[a, hq] for hq in range(nstrip)], axis=-1)
            acc_ref[a] = jnp.concatenate([acc_new[a, hq] for hq in range(nstrip)], axis=-1)

    @pl.when(jnp.logical_and(jnp.logical_and(pl.program_id(0) == 0, pl.program_id(1) == 0), qi == 0))
    def _():
        for i in range(len(chains)):
            sta_ref[i] = scores(0, i)

    cpt = tq // tk
    assert cpt % 2 == 0
    bufs = (sta_ref, stb_ref)

    def trip(j, carry):
        for cc in range(cpt):
            step(cpt * j + cc, bufs[cc % 2], bufs[(cc + 1) % 2])
        return carry

    lax.fori_loop(0, qi, trip, 0)
    for cc in range(cpt):
        step(cpt * qi + cc, bufs[cc % 2], bufs[(cc + 1) % 2], key_off=cc * tk, last=cc == cpt - 1)

    out_t = jnp.concatenate([acc_ref[a, :MLA_V, :] / acc_ref[a, MLA_V:MLA_V + 1, :] for a in range(MLA_HPS)],
                            axis=0)
    o_ref[...] = out_t.T.astype(BF16)


def _mla(qt, k3, vt):
    b, s, _ = k3.shape
    tq, tk = MLA_TQ, MLA_TK
    hps = MLA_HPS
    st_scratch = pltpu.VMEM((hps * (tq // MLA_QSTRIP), tk, MLA_QSTRIP), F32)
    nh, nq = MLA_HEADS // hps, s // tq

    def successor(bi, h, i):
        wrap_i = i == nq - 1
        wrap_h = jnp.logical_and(wrap_i, h == nh - 1)
        i2 = jnp.where(wrap_i, 0, i + 1)
        h2 = jnp.where(wrap_h, 0, jnp.where(wrap_i, h + 1, h))
        b2 = jnp.minimum(bi + wrap_h.astype(jnp.int32), b - 1)
        return b2, h2, i2
    return pl.pallas_call(
        _mla_kernel,
        grid=(b, MLA_HEADS // hps, s // tq),
        in_specs=[
            pl.BlockSpec((None, hps * LANES, tq), lambda bi, h, i: (bi, h, i)),
            pl.BlockSpec((None, s, hps * LANES), lambda bi, h, i: (bi, 0, h)),
            pl.BlockSpec((None, s // tk, hps * MLA_VROWS, tk), lambda bi, h, i: (bi, 0, h, 0)),
            pl.BlockSpec((None, hps * LANES, tq), lambda bi, h, i: successor(bi, h, i)[:3]),
            pl.BlockSpec((None, tk, hps * LANES), lambda bi, h, i: (successor(bi, h, i)[0], 0, successor(bi, h, i)[1])),
        ],
        out_specs=pl.BlockSpec((None, tq, hps * MLA_V), lambda bi, h, i: (bi, i, h)),
        out_shape=jax.ShapeDtypeStruct((b, s, MLA_HEADS * MLA_V), BF16),
        scratch_shapes=[pltpu.VMEM((hps, 1, tq), F32), pltpu.VMEM((hps, MLA_VROWS, tq), F32),
                        st_scratch, st_scratch],
        compiler_params=pltpu.CompilerParams(dimension_semantics=("arbitrary", "arbitrary", "arbitrary"),
                                             vmem_limit_bytes=VMEM_LIMIT),
        name="mla",
    )(qt, k3, vt, qt, k3)


def _dilattn_kernel(q_ref, kc_ref, vc_ref, kp_ref, vp_ref, o_ref, lse_ref, kx_ref, vx_ref, *, dil, group):
    n = pl.program_id(2)
    tb = q_ref.shape[0]
    nblk = tb // BLOCK
    kx_ref[:BLOCK, :] = kp_ref[...]
    kx_ref[BLOCK:, :] = kc_ref[...]
    vx_ref[:BLOCK, :] = vp_ref[...]
    vx_ref[BLOCK:, :] = vc_ref[...]

    qry = lax.broadcasted_iota(jnp.int32, (BLOCK, 2 * BLOCK), 0)
    key = lax.broadcasted_iota(jnp.int32, (BLOCK, 2 * BLOCK), 1)
    dist = qry + BLOCK - key
    in_window = jnp.logical_and(dist >= 0, dist <= BLOCK)
    first_ok = jnp.logical_and(in_window, jnp.logical_or(key >= BLOCK, n > 0))
    distf = (dist * dil).astype(F32)
    lane = lax.broadcasted_iota(jnp.int32, (BLOCK, LANES), 1)
    scale = DIL_HEAD_DIM ** -0.5
    bias, bias_first = [], []
    for hh in range(DIL_HPG):
        slope = float(2.0 ** (-8.0 * (hh * DIL_GROUPS + group + 1) / DIL_HEADS))
        bias.append(jnp.where(in_window, -slope * distf, NEG_INF))
        bias_first.append(jnp.where(first_ok, -slope * distf, NEG_INF))

    units = [(t, hh) for t in range(nblk) for hh in range(DIL_HPG)]

    def scores(u):
        t, hh = units[u]
        cs = slice(hh * DIL_HEAD_DIM, (hh + 1) * DIL_HEAD_DIM)
        return _dot_nt(q_ref[t * BLOCK:(t + 1) * BLOCK, cs], kx_ref[t * BLOCK:(t + 2) * BLOCK, cs])

    pending = [scores(u) for u in range(min(DIL_AHEAD, len(units)))]
    lse_tile = None
    for u, (t, hh) in enumerate(units):
        if u + DIL_AHEAD < len(units):
            pending.append(scores(u + DIL_AHEAD))
        rs = slice(t * BLOCK, (t + 1) * BLOCK)
        cs = slice(hh * DIL_HEAD_DIM, (hh + 1) * DIL_HEAD_DIM)
        s = pending[u] * scale + (bias_first[hh] if t == 0 else bias[hh])
        m = jnp.max(s, axis=-1, keepdims=True)
        e = jnp.exp(s - m)
        den = jnp.sum(e, axis=-1, keepdims=True)
        o = _dot(e.astype(BF16), vx_ref[t * BLOCK:(t + 2) * BLOCK, cs]) / den
        o_ref[rs, cs] = o.astype(BF16)
        lse = m + jnp.log(den)
        lse_tile = jnp.where(lane == hh, lse, jnp.zeros((BLOCK, LANES), F32) if hh == 0 else lse_tile)
        if hh == DIL_HPG - 1:
            lse_ref[rs, :] = lse_tile


def _dilattn(qkv, dil, group):
    b, d, l, _ = qkv.shape
    tb = min(DIL_TB, l)
    bpt = tb // BLOCK
    w = DIL_WIDTH
    cur = lambda c: pl.BlockSpec((None, None, tb, w), lambda bi, r, n: (bi, r, n, c))
    prev = lambda c: pl.BlockSpec((None, None, BLOCK, w), lambda bi, r, n: (bi, r, jnp.maximum(n * bpt - 1, 0), c))
    return pl.pallas_call(
        functools.partial(_dilattn_kernel, dil=dil, group=group),
        grid=(b, d, l // tb),
        in_specs=[cur(0), cur(1), cur(2), prev(1), prev(2)],
        out_specs=[pl.BlockSpec((None, None, tb, w), lambda bi, r, n: (bi, r, n, 0)),
                   pl.BlockSpec((None, None, tb, LANES), lambda bi, r, n: (bi, r, n, 0))],
        out_shape=[jax.ShapeDtypeStruct((b, d, l, w), BF16), jax.ShapeDtypeStruct((b, d, l, LANES), F32)],
        scratch_shapes=[pltpu.VMEM((tb + BLOCK, w), BF16), pltpu.VMEM((tb + BLOCK, w), BF16)],
        compiler_params=pltpu.CompilerParams(dimension_semantics=("arbitrary", "arbitrary", "arbitrary"),
                                             vmem_limit_bytes=VMEM_LIMIT),
        name=f"dilattn{dil}",
    )(qkv, qkv, qkv, qkv, qkv)


def _memkv_kernel(mem_ref, g_ref, w_ref, o_ref):
    o_ref[...] = _dot(_rms(mem_ref[...], g_ref[...]).astype(BF16), w_ref[...]).astype(BF16)


def _memkv(mem2, g_mem, w):
    return pl.pallas_call(
        _memkv_kernel,
        out_shape=jax.ShapeDtypeStruct((mem2.shape[0], w.shape[1]), BF16),
        compiler_params=pltpu.CompilerParams(vmem_limit_bytes=VMEM_LIMIT),
        name="memkv",
    )(mem2, g_mem, w)


def _sigmoid(z):
    return 1.0 / (1.0 + jnp.exp(-z))


def _merge_kernel(x_ref, h_ref, wmq_ref, wg_ref, bg_ref, kvm_ref, ymla_ref, wbm_ref,
                  o0_ref, o1_ref, o2_ref, l0_ref, l1_ref, l2_ref, wbd_ref, wbmem_ref, wo_ref, gpm_ref, gpf_ref,
                  x1_ref, h2_ref, nat1_ref, nat2_ref, lse1_ref, lse2_ref):
    tm, dm = x_ref.shape
    h = h_ref[...]

    def gate(br):
        cs = slice(br * dm, (br + 1) * dm)
        return _dot_nt(h, wg_ref[cs, :]) + bg_ref[:, cs]

    for src, lsrc, dst, ldst in ((o1_ref, l1_ref, nat1_ref, lse1_ref), (o2_ref, l2_ref, nat2_ref, lse2_ref)):
        d = src.shape[0]
        rows = src.shape[1]
        for r in range(d):
            for hh in range(DIL_HPG):
                cs = slice(hh * DIL_HEAD_DIM, (hh + 1) * DIL_HEAD_DIM)
                dst[hh, pl.ds(r, rows, stride=d), :] = src[r, :, cs].astype(F32)
            ldst[pl.ds(r, rows, stride=d), :] = lsrc[r]
    lg = (l0_ref[0], lse1_ref[...], lse2_ref[...])
    heads = []
    for hh in range(DIL_HPG):
        cs = slice(hh * DIL_HEAD_DIM, (hh + 1) * DIL_HEAD_DIM)
        og = (o0_ref[0, :, cs].astype(F32), nat1_ref[hh], nat2_ref[hh])
        ls = [l[:, hh:hh + 1] for l in lg]
        mx = jnp.maximum(jnp.maximum(ls[0], ls[1]), ls[2])
        ws = [jnp.exp(l - mx) for l in ls]
        num = ws[0] * og[0] + ws[1] * og[1] + ws[2] * og[2]
        heads.append((num / (ws[0] + ws[1] + ws[2])).astype(BF16))
    y_dil = jnp.concatenate(heads, axis=-1)

    memq = _dot_nt(h, wmq_ref[...])
    mheads = []
    for hh in range(MEM_HEADS):
        cs = slice(hh * MEM_HEAD_DIM, (hh + 1) * MEM_HEAD_DIM)
        q = (memq[:, cs] * MEM_HEAD_DIM ** -0.5).astype(BF16)
        s = _dot_nt(q, kvm_ref[:, cs])
        e = jnp.exp(s - jnp.max(s, axis=-1, keepdims=True))
        o = _dot(e.astype(BF16), kvm_ref[:, MEM_WIDTH + hh * MEM_HEAD_DIM:MEM_WIDTH + (hh + 1) * MEM_HEAD_DIM])
        mheads.append((o / jnp.sum(e, axis=-1, keepdims=True)).astype(BF16))
    y_mem = jnp.concatenate(mheads, axis=-1)

    merged = jnp.zeros((tm, dm), F32)
    for br, (y, w_ref) in enumerate(((ymla_ref[...], wbm_ref), (y_dil, wbd_ref), (y_mem, wbmem_ref))):
        merged = merged + _sigmoid(gate(br)) * _dot(y, w_ref[...])
    mixed = _dot(merged.astype(BF16), wo_ref[...])
    x1 = x_ref[...] + _rms(mixed, gpm_ref[...])
    x1_ref[...] = x1
    h2_ref[...] = _rms(x1, gpf_ref[...]).astype(BF16)


def _merge(x2, h2, w_al, b_g, kvm, y_mla, w_bm, o_dil, lse_dil, w_bd, w_bmem, w_o, g_pm, g_pf, batch, seq):
    t, dm = x2.shape
    tm = MERGE_TM
    tps = seq // tm
    n_mem = kvm.shape[0] // batch
    row = lambda i: (i, 0)
    single = pl.Buffered(1)
    const = lambda a: pl.BlockSpec(a.shape, lambda i: (0,) * a.ndim, pipeline_mode=single)
    gate_w = N_BRANCH * dm
    assert ALIGNED_MEMQ % MEM_WIDTH == 0 and ALIGNED_GATE % gate_w == 0
    w_mq = pl.BlockSpec((MEM_WIDTH, dm), lambda i: (ALIGNED_MEMQ // MEM_WIDTH, 0), pipeline_mode=single)
    w_g = pl.BlockSpec((gate_w, dm), lambda i: (ALIGNED_GATE // gate_w, 0), pipeline_mode=single)

    def dil_spec(a):
        d, width = a.shape[1], a.shape[3]
        return pl.BlockSpec((None, d, tm // d, width), lambda i: (i // tps, 0, i % tps, 0))

    return pl.pallas_call(
        _merge_kernel,
        grid=(t // tm,),
        in_specs=[
            pl.BlockSpec((tm, dm), row), pl.BlockSpec((tm, dm), row), w_mq, w_g, const(b_g),
            pl.BlockSpec((n_mem, kvm.shape[1]), lambda i: (i // tps, 0)),
            pl.BlockSpec((tm, y_mla.shape[1]), row), const(w_bm),
            dil_spec(o_dil[0]), dil_spec(o_dil[1]), dil_spec(o_dil[2]),
            dil_spec(lse_dil[0]), dil_spec(lse_dil[1]), dil_spec(lse_dil[2]),
            const(w_bd), const(w_bmem), const(w_o), const(g_pm), const(g_pf),
        ],
        out_specs=[pl.BlockSpec((tm, dm), row), pl.BlockSpec((tm, dm), row)],
        out_shape=[jax.ShapeDtypeStruct((t, dm), F32), jax.ShapeDtypeStruct((t, dm), BF16)],
        scratch_shapes=[pltpu.VMEM((DIL_HPG, tm, DIL_HEAD_DIM), F32), pltpu.VMEM((DIL_HPG, tm, DIL_HEAD_DIM), F32),
                        pltpu.VMEM((tm, LANES), F32), pltpu.VMEM((tm, LANES), F32)],
        compiler_params=pltpu.CompilerParams(dimension_semantics=("arbitrary",), vmem_limit_bytes=VMEM_LIMIT),
        name="merge",
    )(x2, h2, w_al, w_al, b_g, kvm, y_mla, w_bm, o_dil[0], o_dil[1], o_dil[2], lse_dil[0], lse_dil[1], lse_dil[2],
      w_bd, w_bmem, w_o, g_pm, g_pf)


def _ffn_kernel(x1_ref, h2_ref, halo_ref, wup_ref, cw_ref, cb_ref, wd_ref, gpost_ref,
                out_ref, hcat_ref, ua_ref, ub_ref, acc_ref, *, tiles_per_seq):
    i = pl.program_id(0)
    tm = x1_ref.shape[0]
    halo, tf = FFN_HALO, FFN_TF
    dff = wd_ref.shape[0]
    nchunk = dff // tf
    lanes_per_chunk = tf // LANES

    first = (i % tiles_per_seq) == 0
    hcat_ref[:halo, :] = jnp.where(first, jnp.zeros_like(halo_ref[...]), halo_ref[...])
    hcat_ref[halo:, :] = h2_ref[...]

    def up(c, u_ref):
        hc = hcat_ref[...]
        for part, off in enumerate((c * tf, dff + c * tf)):
            u = _dot(hc, wup_ref[:, off:off + tf])
            for j in range(lanes_per_chunk):
                u_ref[part * lanes_per_chunk + j] = u[:, j * LANES:(j + 1) * LANES]

    def conv(u_ref, slab, col):
        cols = slice(col, col + LANES)
        z = cb_ref[:, cols] + cw_ref[0:1, cols] * u_ref[slab, halo - 2:halo - 2 + tm, :]
        z = z + cw_ref[1:2, cols] * u_ref[slab, halo - 1:halo - 1 + tm, :]
        return z + cw_ref[2:3, cols] * u_ref[slab, halo:halo + tm, :]

    bufs = (ua_ref, ub_ref)
    up(0, bufs[0])
    for c in range(nchunk):
        cur = bufs[c % 2]
        if c + 1 < nchunk:
            up(c + 1, bufs[(c + 1) % 2])
        acts = []
        for j in range(lanes_per_chunk):
            gate = conv(cur, j, c * tf + j * LANES)
            val = conv(cur, lanes_per_chunk + j, dff + c * tf + j * LANES)
            acts.append((gate * _sigmoid(gate) * val).astype(BF16))
        down = _dot(jnp.concatenate(acts, axis=-1), wd_ref[c * tf:(c + 1) * tf, :])
        if c == 0:
            acc_ref[...] = down
        else:
            acc_ref[...] += down

    out_ref[...] = x1_ref[...] + _rms(acc_ref[...], gpost_ref[...])


def _ffn(x1, h2, w_up, conv_w, conv_b, w_down, g_post, seq):
    t, dm = x1.shape
    tm, tf, halo = FFN_TM, FFN_TF, FFN_HALO
    tps = seq // tm
    row = lambda i: (i, 0)
    const = lambda a: pl.BlockSpec(a.shape, lambda i: (0,) * a.ndim, pipeline_mode=pl.Buffered(1))
    u_scratch = pltpu.VMEM((2 * tf // LANES, tm + halo, LANES), F32)
    return pl.pallas_call(
        functools.partial(_ffn_kernel, tiles_per_seq=tps),
        grid=(t // tm,),
        in_specs=[
            pl.BlockSpec((tm, dm), row),
            pl.BlockSpec((tm, dm), row),
            pl.BlockSpec((halo, dm), lambda i: (jnp.maximum(i * (tm // halo) - 1, 0), 0)),
            const(w_up), const(conv_w), const(conv_b), const(w_down), const(g_post),
        ],
        out_specs=pl.BlockSpec((tm, dm), row),
        out_shape=jax.ShapeDtypeStruct((t, dm), F32),
        scratch_shapes=[pltpu.VMEM((tm + halo, dm), BF16), u_scratch, u_scratch, pltpu.VMEM((tm, dm), F32)],
        compiler_params=pltpu.CompilerParams(dimension_semantics=("arbitrary",), vmem_limit_bytes=VMEM_LIMIT),
        name="ffn",
    )(x1, h2, h2, w_up, conv_w, conv_b, w_down, g_post)


def _rot_half_cols(w):
    return jnp.concatenate([-w[..., ROPE_HALF:], w[..., :ROPE_HALF]], axis=-1)


def _align_kernel(wt_ref, o_ref):
    cb = wt_ref.shape[1]
    zeros = lambda n: jnp.zeros((n, cb), BF16)
    kr = wt_ref[OFF_KV:OFF_KR, :]
    o_ref[:OFF_KV, :] = wt_ref[:OFF_KV, :].astype(BF16)
    o_ref[OFF_KV:OFF_KV + MLA_NOPE, :] = zeros(MLA_NOPE)
    o_ref[OFF_KV + MLA_NOPE:OFF_KV + MLA_QK_DIM, :] = kr.astype(BF16)
    o_ref[OFF_KV + MLA_QK_DIM:OFF_KV + LANES + MLA_NOPE, :] = zeros(LANES - MLA_QK_DIM + MLA_NOPE)
    o_ref[OFF_KV + LANES + MLA_NOPE:OFF_KV + LANES + MLA_NOPE + ROPE_HALF, :] = (-kr[ROPE_HALF:, :]).astype(BF16)
    o_ref[OFF_KV + LANES + MLA_NOPE + ROPE_HALF:OFF_KV + LANES + MLA_QK_DIM, :] = kr[:ROPE_HALF, :].astype(BF16)
    o_ref[OFF_KV + LANES + MLA_QK_DIM:ALIGNED_DIL, :] = zeros(ALIGNED_DIL - OFF_KV - LANES - MLA_QK_DIM)
    o_ref[ALIGNED_DIL:, :] = wt_ref[OFF_KR:, :].astype(BF16)


def _align_w_in(w_in_all, layer):
    wt_all = jnp.swapaxes(w_in_all, 1, 2)
    _, d_in, dm = wt_all.shape
    cb = ALIGN_CB
    width = ALIGNED_DIL + d_in - OFF_KR
    return pl.pallas_call(
        _align_kernel,
        grid=(dm // cb,),
        in_specs=[pl.BlockSpec((None, d_in, cb), lambda i: (layer, 0, i))],
        out_specs=pl.BlockSpec((width, cb), lambda i: (0, i)),
        out_shape=jax.ShapeDtypeStruct((width, dm), BF16),
        compiler_params=pltpu.CompilerParams(dimension_semantics=("arbitrary",), vmem_limit_bytes=VMEM_LIMIT),
        name="align",
    )(wt_all)


def _prep_weights(w_uq, w_ukv):
    uq = w_uq.reshape(MLA_Q_RANK, MLA_HEADS, MLA_QK_DIM)
    w_qm = w_uq
    w_qs = _rot_half_cols(uq[..., MLA_NOPE:]).reshape(MLA_Q_RANK, MLA_HEADS * MLA_ROPE)

    ukv = w_ukv.reshape(MLA_KV_RANK, MLA_HEADS, MLA_NOPE + MLA_V)
    zk = jnp.zeros((MLA_KV_RANK, MLA_HEADS, LANES - MLA_NOPE), F32)
    w_k = jnp.concatenate([ukv[..., :MLA_NOPE], zk], axis=-1).reshape(MLA_KV_RANK, MLA_HEADS * LANES)
    w_v = ukv[..., MLA_NOPE:].reshape(MLA_KV_RANK, MLA_HEADS * MLA_V)
    return tuple(a.astype(BF16) for a in (w_qm.T, w_qs.T, w_k, w_v.T))


def _layer(layer, w_in_all, x, mem, positions, g_pre_mix, b_gate, mla_q_norm, w_uq, mla_kv_norm, w_ukv, g_mem, w_mem_kv,
           w_br_mla, w_br_dil, w_br_mem, w_o, g_post_mix, g_pre_ffn, w_ffn_up, conv_w, conv_b, w_ffn_down,
           g_post_ffn):
    batch, seq, dm = x.shape
    t = batch * seq
    x2 = x.reshape(t, dm)
    r2 = lambda v: v.reshape(1, -1)

    w_al = _align_w_in(w_in_all, layer)
    w_qmt, w_qst, w_k, w_vt = _prep_weights(w_uq, w_ukv)
    invf = (ROPE_THETA ** (-jnp.arange(ROPE_HALF, dtype=F32) / ROPE_HALF)).reshape(ROPE_HALF, 1)
    pos_rows = positions.reshape(t // PREP_TM, 1, PREP_TM)

    h2d, qt, k, vt = _prep(x2, pos_rows, invf, r2(g_pre_mix), w_al, r2(mla_q_norm), w_qmt, w_qst,
                           r2(mla_kv_norm), w_k, w_vt, batch, seq)
    y_mla = _mla(qt, k.reshape(batch, seq, MLA_HEADS * LANES), vt).reshape(t, MLA_HEADS * MLA_V)

    o_dil, lse_dil = [], []
    for g, (_, dil) in enumerate(DIL_PAIRS):
        qkv = _dilproj(h2d, w_al, batch, seq, dil, g)
        o, lse = _dilattn(qkv, dil, g)
        o_dil.append(o)
        lse_dil.append(lse)

    kvm = _memkv(mem.reshape(-1, dm), r2(g_mem), w_mem_kv.astype(BF16))

    x1, h2 = _merge(x2, h2d, w_al, r2(b_gate), kvm, y_mla, w_br_mla.astype(BF16), o_dil, lse_dil,
                    w_br_dil.astype(BF16),
                    w_br_mem.astype(BF16), w_o.astype(BF16), r2(g_post_mix), r2(g_pre_ffn), batch, seq)

    out = _ffn(x1, h2, w_ffn_up.astype(BF16), conv_w, r2(conv_b), w_ffn_down.astype(BF16), r2(g_post_ffn), seq)
    return out.reshape(batch, seq, dm)


def kernel(x, mem, positions, g_pre_mix, w_in, b_gate, mla_q_norm, w_uq, mla_kv_norm, w_ukv, g_mem, w_mem_kv,
           w_br_mla, w_br_dil, w_br_mem, w_o, g_post_mix, g_pre_ffn, w_ffn_up, conv_w, conv_b, w_ffn_down,
           g_post_ffn):
    for l in range(w_in.shape[0]):
        x = _layer(l, w_in, x, mem, positions, g_pre_mix[l], b_gate[l], mla_q_norm[l], w_uq[l], mla_kv_norm[l],
                   w_ukv[l], g_mem[l], w_mem_kv[l], w_br_mla[l], w_br_dil[l], w_br_mem[l], w_o[l], g_post_mix[l],
                   g_pre_ffn[l], w_ffn_up[l], conv_w[l], conv_b[l], w_ffn_down[l], g_post_ffn[l])
    return x
```

```python
import functools

import jax
import jax.numpy as jnp
from jax import lax
from jax.experimental import pallas as pl
from jax.experimental.pallas import tpu as pltpu

F32 = jnp.float32
BF16 = jnp.bfloat16

RMS_EPS = 1e-6
LOG2E = 1.4426950408889634
LN2 = 0.6931471805599453
NEG_INF = -1e30
LANES = 128

BLOCK = 128
MLA_HEADS = 8
MLA_NOPE = 64
MLA_ROPE = 32
MLA_V = 64
MLA_QK_DIM = MLA_NOPE + MLA_ROPE
MLA_Q_RANK = 384
MLA_KV_RANK = 256
ROPE_THETA = 10000.0
ROPE_HALF = MLA_ROPE // 2

DIL_PAIRS = ((128, 1), (512, 4), (2048, 16))
DIL_GROUPS = 3
DIL_HPG = 4
DIL_HEADS = DIL_GROUPS * DIL_HPG
DIL_HEAD_DIM = 128
DIL_WIDTH = DIL_HPG * DIL_HEAD_DIM
DIL_QSCALE = DIL_HEAD_DIM ** -0.5 * LOG2E

MEM_HEADS = 4
MEM_HEAD_DIM = 128
MEM_WIDTH = MEM_HEADS * MEM_HEAD_DIM

N_BRANCH = 3
CONV_WIDTH = 3

OFF_Q = MLA_Q_RANK
OFF_KV = OFF_Q + MLA_KV_RANK
OFF_KR = OFF_KV + MLA_ROPE
OFF_DIL = OFF_KR + 3 * DIL_HEADS * DIL_HEAD_DIM
OFF_MEMQ = OFF_DIL + MEM_WIDTH
ALIGNED_DIL = 1024
ALIGNED_MEMQ = ALIGNED_DIL + (OFF_DIL - OFF_KR)
ALIGNED_GATE = ALIGNED_MEMQ + MEM_WIDTH

PREP_TM = 512
DILPROJ_TM = 1024
DILPROJ_MAX_STRIDE = 4
ALIGN_CB = 256
MLA_TQ = 2048
MLA_TK = 512
MLA_HPS = 2
MLA_QSTRIP = 256
MLA_VROWS = MLA_V + 16
assert PREP_TM == MLA_TK
DIL_TB = 1024
DIL_AHEAD = 4
MERGE_TM = 512
FFN_TM = 512
FFN_TF = 256
FFN_HALO = 16

TAIL_SPLIT = 4

VMEM_LIMIT = 56 * 1024 * 1024


def _rms(xf, g):
    return xf * lax.rsqrt(jnp.mean(xf * xf, axis=-1, keepdims=True) + RMS_EPS) * g


def _dot(a, b):
    return jnp.dot(a, b, preferred_element_type=F32)


def _dot_nt(a, b):
    return lax.dot_general(a, b, (((1,), (1,)), ((), ())), preferred_element_type=F32)


def _const_spec(shape):
    nd = len(shape)
    return pl.BlockSpec(shape, lambda *_: (0,) * nd)


def _prep_kernel(x_ref, pos_ref, invf_ref, g_ref, wa_ref, qn_ref, wqmt_ref, wqst_ref, kvn_ref, wk_ref, wvt_ref,
                 h_ref, qt_ref, k_ref, vt_ref):
    tm = x_ref.shape[0]
    h = _rms(x_ref[...], g_ref[...]).astype(BF16)
    h_ref[...] = h
    p = _dot_nt(h, wa_ref[...])
    cq = _rms(p[:, :OFF_Q], qn_ref[...]).astype(BF16)
    ckv = _rms(p[:, OFF_Q:OFF_KV], kvn_ref[...]).astype(BF16)

    ang = invf_ref[...] * pos_ref[0].astype(F32)
    c16 = jnp.cos(ang)
    s16 = jnp.sin(ang)
    ones = jnp.ones((MLA_NOPE, tm), F32)
    zeros_lo = jnp.zeros((MLA_NOPE, tm), F32)
    zeros_hi = jnp.zeros((LANES - MLA_QK_DIM, tm), F32)
    cos_c = jnp.concatenate([ones, c16, c16, zeros_hi], axis=0)
    sin_c = jnp.concatenate([zeros_lo, s16, s16, zeros_hi], axis=0)

    qmt = _dot_nt(wqmt_ref[...], cq)
    qst = _dot_nt(wqst_ref[...], cq)
    qscale = MLA_QK_DIM ** -0.5 * LOG2E
    cos_rope = jnp.concatenate([c16, c16], axis=0)
    sin_rope = jnp.concatenate([s16, s16], axis=0)
    for hd in range(MLA_HEADS):
        base = hd * LANES
        nope = qmt[hd * MLA_QK_DIM:hd * MLA_QK_DIM + MLA_NOPE, :]
        rope = qmt[hd * MLA_QK_DIM + MLA_NOPE:(hd + 1) * MLA_QK_DIM, :]
        rot = qst[hd * MLA_ROPE:(hd + 1) * MLA_ROPE, :]
        qt_ref[base:base + MLA_NOPE, :] = (nope * qscale).astype(BF16)
        qt_ref[base + MLA_NOPE:base + MLA_QK_DIM, :] = ((rope * cos_rope + rot * sin_rope) * qscale).astype(BF16)
        qt_ref[base + MLA_QK_DIM:base + LANES, :] = jnp.zeros((LANES - MLA_QK_DIM, tm), BF16)

    cos_r = cos_c.T
    sin_r = sin_c.T
    kn = _dot(ckv, wk_ref[...])
    kpe = p[:, OFF_KV:OFF_KV + LANES] * cos_r + p[:, OFF_KV + LANES:OFF_KV + 2 * LANES] * sin_r
    for hd in range(MLA_HEADS):
        sl = slice(hd * LANES, (hd + 1) * LANES)
        k_ref[:, sl] = (kn[:, sl] + kpe).astype(BF16)
    vt = _dot_nt(wvt_ref[...], ckv).astype(BF16)
    ones_rows = jnp.ones((MLA_VROWS - MLA_V, tm), BF16)
    for hd in range(MLA_HEADS):
        vt_ref[0, hd * MLA_VROWS:hd * MLA_VROWS + MLA_V, :] = vt[hd * MLA_V:(hd + 1) * MLA_V, :]
        vt_ref[0, hd * MLA_VROWS + MLA_V:(hd + 1) * MLA_VROWS, :] = ones_rows


def _prep(x2, pos_rows, invf, g_pre, w_al, q_norm, w_qmt, w_qst, kv_norm, w_k, w_vt, batch, seq):
    t, d = x2.shape
    tm = PREP_TM
    tps = seq // tm
    hw = MLA_HEADS * LANES
    vw = MLA_HEADS * MLA_VROWS
    row = lambda i: (i, 0)
    return pl.pallas_call(
        _prep_kernel,
        grid=(t // tm,),
        in_specs=[
            pl.BlockSpec((tm, d), row),
            pl.BlockSpec((1, 1, tm), lambda i: (i, 0, 0)),
            _const_spec(invf.shape),
            _const_spec(g_pre.shape),
            pl.BlockSpec((OFF_KV + 2 * LANES, d), lambda i: (0, 0)),
            _const_spec(q_norm.shape),
            _const_spec(w_qmt.shape),
            _const_spec(w_qst.shape),
            _const_spec(kv_norm.shape),
            _const_spec(w_k.shape),
            _const_spec(w_vt.shape),
        ],
        out_specs=[pl.BlockSpec((tm, d), row),
                   pl.BlockSpec((None, hw, tm), lambda i: (i // tps, 0, i % tps)),
                   pl.BlockSpec((tm, hw), row),
                   pl.BlockSpec((None, 1, vw, tm), lambda i: (i // tps, i % tps, 0, 0))],
        out_shape=[jax.ShapeDtypeStruct((t, d), BF16),
                   jax.ShapeDtypeStruct((batch, hw, seq), BF16),
                   jax.ShapeDtypeStruct((t, hw), BF16),
                   jax.ShapeDtypeStruct((batch, tps, vw, tm), BF16)],
        compiler_params=pltpu.CompilerParams(dimension_semantics=("arbitrary",), vmem_limit_bytes=VMEM_LIMIT),
        name="prep",
    )(x2, pos_rows, invf, g_pre, w_al, q_norm, w_qmt, w_qst, kv_norm, w_k, w_vt)


def _dilproj_kernel(h_ref, wq_ref, wk_ref, wv_ref, o_ref, *acc_refs, dil):
    h = h_ref[...]
    w_refs = (wq_ref, wk_ref, wv_ref)
    tm, tn = h_ref.shape[0], wq_ref.shape[0]

    def projection(j):
        acc = _dot_nt(h, w_refs[j][...])
        return acc * DIL_QSCALE if j == 0 else acc

    if dil == 1:
        for j in range(len(w_refs)):
            o_ref[0, :, j * tn:(j + 1) * tn] = projection(j).astype(BF16)
        return
    chunks = tn // LANES
    rows = tm // dil

    def project(j):
        acc = projection(j)
        for c in range(chunks):
            acc_refs[j][c] = acc[:, c * LANES:(c + 1) * LANES]

    def scatter(j):
        src = acc_refs[j]
        if dil > DILPROJ_MAX_STRIDE:
            s1 = DILPROJ_MAX_STRIDE
            s2 = dil // s1
            tmp = acc_refs[-1]
            for r1 in range(s1):
                for c in range(chunks):
                    tmp[c, r1 * (tm // s1):(r1 + 1) * (tm // s1), :] = src[c, pl.ds(r1, tm // s1, stride=s1), :]
            for r in range(dil):
                r1, r2 = r % s1, r // s1
                for c in range(chunks):
                    cols = slice(j * tn + c * LANES, j * tn + (c + 1) * LANES)
                    o_ref[r, :, cols] = tmp[c, pl.ds(r1 * (tm // s1) + r2, rows, stride=s2), :].astype(BF16)
            return
        for r in range(dil):
            for c in range(chunks):
                cols = slice(j * tn + c * LANES, j * tn + (c + 1) * LANES)
                o_ref[r, :, cols] = src[c, pl.ds(r, rows, stride=dil), :].astype(BF16)

    project(0)
    project(1)
    scatter(0)
    project(2)
    scatter(1)
    scatter(2)


def _dilproj(h2, w_al, batch, seq, dil, group):
    t, d = h2.shape
    n = 3 * DIL_WIDTH
    tm, tn = DILPROJ_TM, DIL_WIDTH
    assert ALIGNED_DIL % tn == 0
    tiles_per_seq = seq // tm
    col0 = ALIGNED_DIL // tn + group
    w_spec = lambda j: pl.BlockSpec((tn, d), lambda i: (col0 + j * DIL_GROUPS, 0))
    n_scratch = 0 if dil == 1 else (3 if dil <= DILPROJ_MAX_STRIDE else 4)
    scratch = [pltpu.VMEM((tn // LANES, tm, LANES), F32)] * n_scratch
    return pl.pallas_call(
        functools.partial(_dilproj_kernel, dil=dil),
        grid=(t // tm,),
        in_specs=[pl.BlockSpec((tm, d), lambda i: (i, 0)), w_spec(0), w_spec(1), w_spec(2)],
        out_specs=pl.BlockSpec((None, dil, tm // dil, n),
                               lambda i: (i // tiles_per_seq, 0, i % tiles_per_seq, 0)),
        out_shape=jax.ShapeDtypeStruct((batch, dil, seq // dil, n), BF16),
        scratch_shapes=scratch,
        compiler_params=pltpu.CompilerParams(dimension_semantics=("arbitrary",), vmem_limit_bytes=VMEM_LIMIT),
        name=f"dilproj{dil}",
    )(h2, w_al, w_al, w_al)


def _mla_kernel(qt_ref, k_ref, vt_ref, qtn_ref, k0n_ref, o_ref, m_ref, acc_ref, sta_ref, stb_ref):
    tq = qt_ref.shape[1]
    tk = MLA_TK
    qi = pl.program_id(2)
    nstrip = tq // MLA_QSTRIP
    m_ref[...] = jnp.full(m_ref.shape, NEG_INF, F32)
    acc_ref[...] = jnp.zeros(acc_ref.shape, F32)
    chains = [(a, hq) for a in range(MLA_HPS) for hq in range(nstrip)]

    def scores(c, i):
        a, hq = chains[i]
        k = k_ref[pl.ds(pl.multiple_of(c * tk, tk), tk), a * LANES:(a + 1) * LANES]
        return _dot(k, qt_ref[a * LANES:(a + 1) * LANES, hq * MLA_QSTRIP:(hq + 1) * MLA_QSTRIP])

    def successor_scores(i):
        a, hq = chains[i]
        return _dot(k0n_ref[:, a * LANES:(a + 1) * LANES],
                    qtn_ref[a * LANES:(a + 1) * LANES, hq * MLA_QSTRIP:(hq + 1) * MLA_QSTRIP])

    def step(c, src_ref, dst_ref, key_off=None, last=False):
        def live(hq, off):
            return off is None or (hq + 1) * MLA_QSTRIP > off

        next_off = None if key_off is None else key_off + tk
        m_all = m_ref[...]
        acc_all = acc_ref[...]
        m_new, acc_new = {}, {}
        for i, (a, hq) in enumerate(chains):
            qs = slice(hq * MLA_QSTRIP, (hq + 1) * MLA_QSTRIP)
            if last:
                dst_ref[i] = successor_scores(i)
            elif live(hq, next_off):
                dst_ref[i] = scores(c + 1, i)
            if not live(hq, key_off):
                m_new[a, hq], acc_new[a, hq] = m_all[a, :, qs], acc_all[a, :, qs]
                continue
            st = src_ref[i]
            if key_off is not None and key_off + tk - 1 > hq * MLA_QSTRIP:
                key = lax.broadcasted_iota(jnp.int32, st.shape, 0) + key_off
                qry = lax.broadcasted_iota(jnp.int32, st.shape, 1) + hq * MLA_QSTRIP
                st = jnp.where(key <= qry, st, NEG_INF)
            m_prev = m_all[a, :, qs]
            m_cur = jnp.maximum(m_prev, jnp.max(st, axis=0, keepdims=True))
            alpha = jnp.exp2(m_prev - m_cur)
            p = jnp.exp2(st - m_cur).astype(BF16)
            vt = vt_ref[c, a * MLA_VROWS:(a + 1) * MLA_VROWS, :]
            acc_new[a, hq] = alpha * acc_all[a, :, qs] + _dot(vt, p)
            m_new[a, hq] = m_cur
        for a in range(MLA_HPS):
            m_ref[a] = jnp.concatenate([m_new[a, hq] for hq in range(nstrip)], axis=-1)
            acc_ref[a] = jnp.concatenate([acc_new[a, hq] for hq in range(nstrip)], axis=-1)

    @pl.when(jnp.logical_and(jnp.logical_and(pl.program_id(0) == 0, pl.program_id(1) == 0), qi == 0))
    def _():
        for i in range(len(chains)):
            sta_ref[i] = scores(0, i)

    cpt = tq // tk
    assert cpt % 2 == 0
    bufs = (sta_ref, stb_ref)

    def trip(j, carry):
        for cc in range(cpt):
            step(cpt * j + cc, bufs[cc % 2], bufs[(cc + 1) % 2])
        return carry

    lax.fori_loop(0, qi, trip, 0)
    for cc in range(cpt):
        step(cpt * qi + cc, bufs[cc % 2], bufs[(cc + 1) % 2], key_off=cc * tk, last=cc == cpt - 1)

    out_t = jnp.concatenate([acc_ref[a, :MLA_V, :] / acc_ref[a, MLA_V:MLA_V + 1, :] for a in range(MLA_HPS)],
                            axis=0)
    o_ref[...] = out_t.T.astype(BF16)


def _mla(qt, k3, vt):
    b, s, _ = k3.shape
    tq, tk = MLA_TQ, MLA_TK
    hps = MLA_HPS
    st_scratch = pltpu.VMEM((hps * (tq // MLA_QSTRIP), tk, MLA_QSTRIP), F32)
    nh, nq = MLA_HEADS // hps, s // tq

    def successor(bi, h, i):
        wrap_i = i == nq - 1
        wrap_h = jnp.logical_and(wrap_i, h == nh - 1)
        i2 = jnp.where(wrap_i, 0, i + 1)
        h2 = jnp.where(wrap_h, 0, jnp.where(wrap_i, h + 1, h))
        b2 = jnp.minimum(bi + wrap_h.astype(jnp.int32), b - 1)
        return b2, h2, i2
    return pl.pallas_call(
        _mla_kernel,
        grid=(b, MLA_HEADS // hps, s // tq),
        in_specs=[
            pl.BlockSpec((None, hps * LANES, tq), lambda bi, h, i: (bi, h, i)),
            pl.BlockSpec((None, s, hps * LANES), lambda bi, h, i: (bi, 0, h)),
            pl.BlockSpec((None, s // tk, hps * MLA_VROWS, tk), lambda bi, h, i: (bi, 0, h, 0)),
            pl.BlockSpec((None, hps * LANES, tq), lambda bi, h, i: successor(bi, h, i)[:3]),
            pl.BlockSpec((None, tk, hps * LANES), lambda bi, h, i: (successor(bi, h, i)[0], 0, successor(bi, h, i)[1])),
        ],
        out_specs=pl.BlockSpec((None, tq, hps * MLA_V), lambda bi, h, i: (bi, i, h)),
        out_shape=jax.ShapeDtypeStruct((b, s, MLA_HEADS * MLA_V), BF16),
        scratch_shapes=[pltpu.VMEM((hps, 1, tq), F32), pltpu.VMEM((hps, MLA_VROWS, tq), F32),
                        st_scratch, st_scratch],
        compiler_params=pltpu.CompilerParams(dimension_semantics=("arbitrary", "arbitrary", "arbitrary"),
                                             vmem_limit_bytes=VMEM_LIMIT),
        name="mla",
    )(qt, k3, vt, qt, k3)


def _dilattn_kernel(q_ref, kc_ref, vc_ref, kp_ref, vp_ref, o_ref, lse_ref, kx_ref, vx_ref, *, dil, group):
    n = pl.program_id(2)
    tb = q_ref.shape[0]
    nblk = tb // BLOCK
    kx_ref[:BLOCK, :] = kp_ref[...]
    kx_ref[BLOCK:, :] = kc_ref[...]
    vx_ref[:BLOCK, :] = vp_ref[...]
    vx_ref[BLOCK:, :] = vc_ref[...]

    qry = lax.broadcasted_iota(jnp.int32, (BLOCK, 2 * BLOCK), 0)
    key = lax.broadcasted_iota(jnp.int32, (BLOCK, 2 * BLOCK), 1)
    dist = qry + BLOCK - key
    in_window = jnp.logical_and(dist >= 0, dist <= BLOCK)
    first_ok = jnp.logical_and(in_window, jnp.logical_or(key >= BLOCK, n > 0))
    distf = (dist * dil).astype(F32)
    lane = lax.broadcasted_iota(jnp.int32, (BLOCK, LANES), 1)
    bias, bias_first = [], []
    for hh in range(DIL_HPG):
        slope = float(2.0 ** (-8.0 * (hh * DIL_GROUPS + group + 1) / DIL_HEADS))
        bias.append(jnp.where(in_window, -(slope * LOG2E) * distf, NEG_INF))
        bias_first.append(jnp.where(first_ok, -(slope * LOG2E) * distf, NEG_INF))

    units = [(t, hh) for t in range(nblk) for hh in range(DIL_HPG)]

    def scores(u):
        t, hh = units[u]
        cs = slice(hh * DIL_HEAD_DIM, (hh + 1) * DIL_HEAD_DIM)
        return _dot_nt(q_ref[t * BLOCK:(t + 1) * BLOCK, cs], kx_ref[t * BLOCK:(t + 2) * BLOCK, cs])

    pending = [scores(u) for u in range(min(DIL_AHEAD, len(units)))]
    lse_tile = None
    for u, (t, hh) in enumerate(units):
        if u + DIL_AHEAD < len(units):
            pending.append(scores(u + DIL_AHEAD))
        rs = slice(t * BLOCK, (t + 1) * BLOCK)
        cs = slice(hh * DIL_HEAD_DIM, (hh + 1) * DIL_HEAD_DIM)
        s = pending[u] + (bias_first[hh] if t == 0 else bias[hh])
        m = jnp.max(s, axis=-1, keepdims=True)
        e = jnp.exp2(s - m)
        den = jnp.sum(e, axis=-1, keepdims=True)
        o = _dot(e.astype(BF16), vx_ref[t * BLOCK:(t + 2) * BLOCK, cs]) / den
        o_ref[rs, cs] = o.astype(BF16)
        lse = (m + jnp.log2(den)) * LN2
        lse_tile = jnp.where(lane == hh, lse, jnp.zeros((BLOCK, LANES), F32) if hh == 0 else lse_tile)
        if hh == DIL_HPG - 1:
            lse_ref[rs, :] = lse_tile


def _dilattn(qkv, dil, group):
    b, d, l, _ = qkv.shape
    tb = min(DIL_TB, l)
    bpt = tb // BLOCK
    w = DIL_WIDTH
    cur = lambda c: pl.BlockSpec((None, None, tb, w), lambda bi, r, n: (bi, r, n, c))
    prev = lambda c: pl.BlockSpec((None, None, BLOCK, w), lambda bi, r, n: (bi, r, jnp.maximum(n * bpt - 1, 0), c))
    return pl.pallas_call(
        functools.partial(_dilattn_kernel, dil=dil, group=group),
        grid=(b, d, l // tb),
        in_specs=[cur(0), cur(1), cur(2), prev(1), prev(2)],
        out_specs=[pl.BlockSpec((None, None, tb, w), lambda bi, r, n: (bi, r, n, 0)),
                   pl.BlockSpec((None, None, tb, LANES), lambda bi, r, n: (bi, r, n, 0))],
        out_shape=[jax.ShapeDtypeStruct((b, d, l, w), BF16), jax.ShapeDtypeStruct((b, d, l, LANES), F32)],
        scratch_shapes=[pltpu.VMEM((tb + BLOCK, w), BF16), pltpu.VMEM((tb + BLOCK, w), BF16)],
        compiler_params=pltpu.CompilerParams(dimension_semantics=("arbitrary", "arbitrary", "arbitrary"),
                                             vmem_limit_bytes=VMEM_LIMIT),
        name=f"dilattn{dil}",
    )(qkv, qkv, qkv, qkv, qkv)


def _memkv_kernel(mem_ref, g_ref, w_ref, o_ref):
    o_ref[...] = _dot(_rms(mem_ref[...], g_ref[...]).astype(BF16), w_ref[...]).astype(BF16)


def _memkv(mem2, g_mem, w):
    return pl.pallas_call(
        _memkv_kernel,
        out_shape=jax.ShapeDtypeStruct((mem2.shape[0], w.shape[1]), BF16),
        compiler_params=pltpu.CompilerParams(vmem_limit_bytes=VMEM_LIMIT),
        name="memkv",
    )(mem2, g_mem, w)


def _sigmoid(z):
    return 1.0 / (1.0 + jnp.exp(-z))


def _merge_kernel(x_ref, h_ref, wmq_ref, wg_ref, bg_ref, kvm_ref, ymla_ref, wbm_ref,
                  o0_ref, o1_ref, o2_ref, l0_ref, l1_ref, l2_ref, wbd_ref, wbmem_ref, wo_ref, gpm_ref, gpf_ref,
                  x1_ref, h2_ref, nat1_ref, nat2_ref, lse1_ref, lse2_ref):
    tm, dm = x_ref.shape
    h = h_ref[...]

    def gate(br):
        cs = slice(br * dm, (br + 1) * dm)
        return _dot_nt(h, wg_ref[cs, :]) + bg_ref[:, cs]

    for src, lsrc, dst, ldst in ((o1_ref, l1_ref, nat1_ref, lse1_ref), (o2_ref, l2_ref, nat2_ref, lse2_ref)):
        d = src.shape[0]
        rows = src.shape[1]
        for r in range(d):
            for hh in range(DIL_HPG):
                cs = slice(hh * DIL_HEAD_DIM, (hh + 1) * DIL_HEAD_DIM)
                dst[hh, pl.ds(r, rows, stride=d), :] = src[r, :, cs].astype(F32)
            ldst[pl.ds(r, rows, stride=d), :] = lsrc[r]
    lg = (l0_ref[0], lse1_ref[...], lse2_ref[...])
    heads = []
    for hh in range(DIL_HPG):
        cs = slice(hh * DIL_HEAD_DIM, (hh + 1) * DIL_HEAD_DIM)
        og = (o0_ref[0, :, cs].astype(F32), nat1_ref[hh], nat2_ref[hh])
        ls = [l[:, hh:hh + 1] for l in lg]
        mx = jnp.maximum(jnp.maximum(ls[0], ls[1]), ls[2])
        ws = [jnp.exp(l - mx) for l in ls]
        num = ws[0] * og[0] + ws[1] * og[1] + ws[2] * og[2]
        heads.append((num / (ws[0] + ws[1] + ws[2])).astype(BF16))
    y_dil = jnp.concatenate(heads, axis=-1)

    memq = _dot_nt(h, wmq_ref[...])
    mheads = []
    for hh in range(MEM_HEADS):
        cs = slice(hh * MEM_HEAD_DIM, (hh + 1) * MEM_HEAD_DIM)
        q = (memq[:, cs] * MEM_HEAD_DIM ** -0.5).astype(BF16)
        s = _dot_nt(q, kvm_ref[:, cs])
        e = jnp.exp(s - jnp.max(s, axis=-1, keepdims=True))
        o = _dot(e.astype(BF16), kvm_ref[:, MEM_WIDTH + hh * MEM_HEAD_DIM:MEM_WIDTH + (hh + 1) * MEM_HEAD_DIM])
        mheads.append((o / jnp.sum(e, axis=-1, keepdims=True)).astype(BF16))
    y_mem = jnp.concatenate(mheads, axis=-1)

    merged = jnp.zeros((tm, dm), F32)
    for br, (y, w_ref) in enumerate(((ymla_ref[...], wbm_ref), (y_dil, wbd_ref), (y_mem, wbmem_ref))):
        merged = merged + _sigmoid(gate(br)) * _dot(y, w_ref[...])
    merged_b = merged.astype(BF16)
    rows = tm // TAIL_SPLIT
    mixed = [_dot(merged_b[g * rows:(g + 1) * rows, :], wo_ref[...]) for g in range(TAIL_SPLIT)]
    for g in range(TAIL_SPLIT):
        rs = slice(g * rows, (g + 1) * rows)
        x1 = x_ref[rs, :] + _rms(mixed[g], gpm_ref[...])
        x1_ref[rs, :] = x1
        h2_ref[rs, :] = _rms(x1, gpf_ref[...]).astype(BF16)


def _merge(x2, h2, w_al, b_g, kvm, y_mla, w_bm, o_dil, lse_dil, w_bd, w_bmem, w_o, g_pm, g_pf, batch, seq):
    t, dm = x2.shape
    tm = MERGE_TM
    tps = seq // tm
    n_mem = kvm.shape[0] // batch
    row = lambda i: (i, 0)
    single = pl.Buffered(1)
    const = lambda a: pl.BlockSpec(a.shape, lambda i: (0,) * a.ndim, pipeline_mode=single)
    gate_w = N_BRANCH * dm
    assert ALIGNED_MEMQ % MEM_WIDTH == 0 and ALIGNED_GATE % gate_w == 0
    w_mq = pl.BlockSpec((MEM_WIDTH, dm), lambda i: (ALIGNED_MEMQ // MEM_WIDTH, 0), pipeline_mode=single)
    w_g = pl.BlockSpec((gate_w, dm), lambda i: (ALIGNED_GATE // gate_w, 0), pipeline_mode=single)

    def dil_spec(a):
        d, width = a.shape[1], a.shape[3]
        return pl.BlockSpec((None, d, tm // d, width), lambda i: (i // tps, 0, i % tps, 0))

    return pl.pallas_call(
        _merge_kernel,
        grid=(t // tm,),
        in_specs=[
            pl.BlockSpec((tm, dm), row), pl.BlockSpec((tm, dm), row), w_mq, w_g, const(b_g),
            pl.BlockSpec((n_mem, kvm.shape[1]), lambda i: (i // tps, 0)),
            pl.BlockSpec((tm, y_mla.shape[1]), row), const(w_bm),
            dil_spec(o_dil[0]), dil_spec(o_dil[1]), dil_spec(o_dil[2]),
            dil_spec(lse_dil[0]), dil_spec(lse_dil[1]), dil_spec(lse_dil[2]),
            const(w_bd), const(w_bmem), const(w_o), const(g_pm), const(g_pf),
        ],
        out_specs=[pl.BlockSpec((tm, dm), row), pl.BlockSpec((tm, dm), row)],
        out_shape=[jax.ShapeDtypeStruct((t, dm), F32), jax.ShapeDtypeStruct((t, dm), BF16)],
        scratch_shapes=[pltpu.VMEM((DIL_HPG, tm, DIL_HEAD_DIM), F32), pltpu.VMEM((DIL_HPG, tm, DIL_HEAD_DIM), F32),
                        pltpu.VMEM((tm, LANES), F32), pltpu.VMEM((tm, LANES), F32)],
        compiler_params=pltpu.CompilerParams(dimension_semantics=("arbitrary",), vmem_limit_bytes=VMEM_LIMIT),
        name="merge",
    )(x2, h2, w_al, w_al, b_g, kvm, y_mla, w_bm, o_dil[0], o_dil[1], o_dil[2], lse_dil[0], lse_dil[1], lse_dil[2],
      w_bd, w_bmem, w_o, g_pm, g_pf)


def _ffn_kernel(x1_ref, h2_ref, halo_ref, wup_ref, cw_ref, cb_ref, wd_ref, gpost_ref,
                out_ref, hcat_ref, ua_ref, ub_ref, acc_ref, *, tiles_per_seq):
    i = pl.program_id(0)
    tm = x1_ref.shape[0]
    halo, tf = FFN_HALO, FFN_TF
    dff = wd_ref.shape[0]
    nchunk = dff // tf
    lanes_per_chunk = tf // LANES

    first = (i % tiles_per_seq) == 0
    hcat_ref[:halo, :] = jnp.where(first, jnp.zeros_like(halo_ref[...]), halo_ref[...])
    hcat_ref[halo:, :] = h2_ref[...]

    def up(c, u_ref):
        hc = hcat_ref[...]
        for part, off in enumerate((c * tf, dff + c * tf)):
            u = _dot(hc, wup_ref[:, off:off + tf])
            for j in range(lanes_per_chunk):
                u_ref[part * lanes_per_chunk + j] = u[:, j * LANES:(j + 1) * LANES]

    def conv(u_ref, slab, col):
        cols = slice(col, col + LANES)
        z = cb_ref[:, cols] + cw_ref[0:1, cols] * u_ref[slab, halo - 2:halo - 2 + tm, :]
        z = z + cw_ref[1:2, cols] * u_ref[slab, halo - 1:halo - 1 + tm, :]
        return z + cw_ref[2:3, cols] * u_ref[slab, halo:halo + tm, :]

    bufs = (ua_ref, ub_ref)
    up(0, bufs[0])
    for c in range(nchunk):
        cur = bufs[c % 2]
        if c + 1 < nchunk:
            up(c + 1, bufs[(c + 1) % 2])
        acts = []
        for j in range(lanes_per_chunk):
            gate = conv(cur, j, c * tf + j * LANES)
            val = conv(cur, lanes_per_chunk + j, dff + c * tf + j * LANES)
            acts.append((gate * _sigmoid(gate) * val).astype(BF16))
        act = jnp.concatenate(acts, axis=-1)
        wd = wd_ref[c * tf:(c + 1) * tf, :]
        if c == 0:
            acc_ref[...] = _dot(act, wd)
        elif c + 1 < nchunk:
            acc_ref[...] += _dot(act, wd)
        else:
            rows = tm // TAIL_SPLIT
            down = [_dot(act[g * rows:(g + 1) * rows, :], wd) for g in range(TAIL_SPLIT)]
            for g in range(TAIL_SPLIT):
                rs = slice(g * rows, (g + 1) * rows)
                out_ref[rs, :] = x1_ref[rs, :] + _rms(acc_ref[rs, :] + down[g], gpost_ref[...])


def _ffn(x1, h2, w_up, conv_w, conv_b, w_down, g_post, seq):
    t, dm = x1.shape
    tm, tf, halo = FFN_TM, FFN_TF, FFN_HALO
    tps = seq // tm
    row = lambda i: (i, 0)
    const = lambda a: pl.BlockSpec(a.shape, lambda i: (0,) * a.ndim, pipeline_mode=pl.Buffered(1))
    u_scratch = pltpu.VMEM((2 * tf // LANES, tm + halo, LANES), F32)
    return pl.pallas_call(
        functools.partial(_ffn_kernel, tiles_per_seq=tps),
        grid=(t // tm,),
        in_specs=[
            pl.BlockSpec((tm, dm), row),
            pl.BlockSpec((tm, dm), row),
            pl.BlockSpec((halo, dm), lambda i: (jnp.maximum(i * (tm // halo) - 1, 0), 0)),
            const(w_up), const(conv_w), const(conv_b), const(w_down), const(g_post),
        ],
        out_specs=pl.BlockSpec((tm, dm), row),
        out_shape=jax.ShapeDtypeStruct((t, dm), F32),
        scratch_shapes=[pltpu.VMEM((tm + halo, dm), BF16), u_scratch, u_scratch, pltpu.VMEM((tm, dm), F32)],
        compiler_params=pltpu.CompilerParams(dimension_semantics=("arbitrary",), vmem_limit_bytes=VMEM_LIMIT),
        name="ffn",
    )(x1, h2, h2, w_up, conv_w, conv_b, w_down, g_post)


def _rot_half_cols(w):
    return jnp.concatenate([-w[..., ROPE_HALF:], w[..., :ROPE_HALF]], axis=-1)


def _align_kernel(wt_ref, o_ref):
    cb = wt_ref.shape[1]
    zeros = lambda n: jnp.zeros((n, cb), BF16)
    kr = wt_ref[OFF_KV:OFF_KR, :]
    o_ref[:OFF_KV, :] = wt_ref[:OFF_KV, :].astype(BF16)
    o_ref[OFF_KV:OFF_KV + MLA_NOPE, :] = zeros(MLA_NOPE)
    o_ref[OFF_KV + MLA_NOPE:OFF_KV + MLA_QK_DIM, :] = kr.astype(BF16)
    o_ref[OFF_KV + MLA_QK_DIM:OFF_KV + LANES + MLA_NOPE, :] = zeros(LANES - MLA_QK_DIM + MLA_NOPE)
    o_ref[OFF_KV + LANES + MLA_NOPE:OFF_KV + LANES + MLA_NOPE + ROPE_HALF, :] = (-kr[ROPE_HALF:, :]).astype(BF16)
    o_ref[OFF_KV + LANES + MLA_NOPE + ROPE_HALF:OFF_KV + LANES + MLA_QK_DIM, :] = kr[:ROPE_HALF, :].astype(BF16)
    o_ref[OFF_KV + LANES + MLA_QK_DIM:ALIGNED_DIL, :] = zeros(ALIGNED_DIL - OFF_KV - LANES - MLA_QK_DIM)
    o_ref[ALIGNED_DIL:, :] = wt_ref[OFF_KR:, :].astype(BF16)


def _align_w_in(w_in_all, layer):
    wt_all = jnp.swapaxes(w_in_all, 1, 2)
    _, d_in, dm = wt_all.shape
    cb = ALIGN_CB
    width = ALIGNED_DIL + d_in - OFF_KR
    return pl.pallas_call(
        _align_kernel,
        grid=(dm // cb,),
        in_specs=[pl.BlockSpec((None, d_in, cb), lambda i: (layer, 0, i))],
        out_specs=pl.BlockSpec((width, cb), lambda i: (0, i)),
        out_shape=jax.ShapeDtypeStruct((width, dm), BF16),
        compiler_params=pltpu.CompilerParams(dimension_semantics=("arbitrary",), vmem_limit_bytes=VMEM_LIMIT),
        name="align",
    )(wt_all)


def _prep_weights(w_uq, w_ukv):
    uq = w_uq.reshape(MLA_Q_RANK, MLA_HEADS, MLA_QK_DIM)
    w_qm = w_uq
    w_qs = _rot_half_cols(uq[..., MLA_NOPE:]).reshape(MLA_Q_RANK, MLA_HEADS * MLA_ROPE)

    ukv = w_ukv.reshape(MLA_KV_RANK, MLA_HEADS, MLA_NOPE + MLA_V)
    zk = jnp.zeros((MLA_KV_RANK, MLA_HEADS, LANES - MLA_NOPE), F32)
    w_k = jnp.concatenate([ukv[..., :MLA_NOPE], zk], axis=-1).reshape(MLA_KV_RANK, MLA_HEADS * LANES)
    w_v = ukv[..., MLA_NOPE:].reshape(MLA_KV_RANK, MLA_HEADS * MLA_V)
    return tuple(a.astype(BF16) for a in (w_qm.T, w_qs.T, w_k, w_v.T))


def _layer(layer, w_in_all, x, mem, positions, g_pre_mix, b_gate, mla_q_norm, w_uq, mla_kv_norm, w_ukv, g_mem, w_mem_kv,
           w_br_mla, w_br_dil, w_br_mem, w_o, g_post_mix, g_pre_ffn, w_ffn_up, conv_w, conv_b, w_ffn_down,
           g_post_ffn):
    batch, seq, dm = x.shape
    t = batch * seq
    x2 = x.reshape(t, dm)
    r2 = lambda v: v.reshape(1, -1)

    w_al = _align_w_in(w_in_all, layer)
    w_qmt, w_qst, w_k, w_vt = _prep_weights(w_uq, w_ukv)
    invf = (ROPE_THETA ** (-jnp.arange(ROPE_HALF, dtype=F32) / ROPE_HALF)).reshape(ROPE_HALF, 1)
    pos_rows = positions.reshape(t // PREP_TM, 1, PREP_TM)

    h2d, qt, k, vt = _prep(x2, pos_rows, invf, r2(g_pre_mix), w_al, r2(mla_q_norm), w_qmt, w_qst,
                           r2(mla_kv_norm), w_k, w_vt, batch, seq)
    y_mla = _mla(qt, k.reshape(batch, seq, MLA_HEADS * LANES), vt).reshape(t, MLA_HEADS * MLA_V)

    o_dil, lse_dil = [], []
    for g, (_, dil) in enumerate(DIL_PAIRS):
        qkv = _dilproj(h2d, w_al, batch, seq, dil, g)
        o, lse = _dilattn(qkv, dil, g)
        o_dil.append(o)
        lse_dil.append(lse)

    kvm = _memkv(mem.reshape(-1, dm), r2(g_mem), w_mem_kv.astype(BF16))

    x1, h2 = _merge(x2, h2d, w_al, r2(b_gate), kvm, y_mla, w_br_mla.astype(BF16), o_dil, lse_dil,
                    w_br_dil.astype(BF16),
                    w_br_mem.astype(BF16), w_o.astype(BF16), r2(g_post_mix), r2(g_pre_ffn), batch, seq)

    out = _ffn(x1, h2, w_ffn_up.astype(BF16), conv_w, r2(conv_b), w_ffn_down.astype(BF16), r2(g_post_ffn), seq)
    return out.reshape(batch, seq, dm)


def kernel(x, mem, positions, g_pre_mix, w_in, b_gate, mla_q_norm, w_uq, mla_kv_norm, w_ukv, g_mem, w_mem_kv,
           w_br_mla, w_br_dil, w_br_mem, w_o, g_post_mix, g_pre_ffn, w_ffn_up, conv_w, conv_b, w_ffn_down,
           g_post_ffn):
    for l in range(w_in.shape[0]):
        x = _layer(l, w_in, x, mem, positions, g_pre_mix[l], b_gate[l], mla_q_norm[l], w_uq[l], mla_kv_norm[l],
                   w_ukv[l], g_mem[l], w_mem_kv[l], w_br_mla[l], w_br_dil[l], w_br_mem[l], w_o[l], g_post_mix[l],
                   g_pre_ffn[l], w_ffn_up[l], conv_w[l], conv_b[l], w_ffn_down[l], g_post_ffn[l])
    return x
```

```python
import functools

import jax
import jax.numpy as jnp
from jax import lax
from jax.experimental import pallas as pl
from jax.experimental.pallas import tpu as pltpu

F32 = jnp.float32
BF16 = jnp.bfloat16

RMS_EPS = 1e-6
LOG2E = 1.4426950408889634
LN2 = 0.6931471805599453
NEG_INF = -1e30
LANES = 128

BLOCK = 128
MLA_HEADS = 8
MLA_NOPE = 64
MLA_ROPE = 32
MLA_V = 64
MLA_QK_DIM = MLA_NOPE + MLA_ROPE
MLA_Q_RANK = 384
MLA_KV_RANK = 256
ROPE_THETA = 10000.0
ROPE_HALF = MLA_ROPE // 2

DIL_PAIRS = ((128, 1), (512, 4), (2048, 16))
DIL_GROUPS = 3
DIL_HPG = 4
DIL_HEADS = DIL_GROUPS * DIL_HPG
DIL_HEAD_DIM = 128
DIL_WIDTH = DIL_HPG * DIL_HEAD_DIM
DIL_QSCALE = DIL_HEAD_DIM ** -0.5 * LOG2E

MEM_HEADS = 4
MEM_HEAD_DIM = 128
MEM_WIDTH = MEM_HEADS * MEM_HEAD_DIM

N_BRANCH = 3
CONV_WIDTH = 3

OFF_Q = MLA_Q_RANK
OFF_KV = OFF_Q + MLA_KV_RANK
OFF_KR = OFF_KV + MLA_ROPE
OFF_DIL = OFF_KR + 3 * DIL_HEADS * DIL_HEAD_DIM
OFF_MEMQ = OFF_DIL + MEM_WIDTH
ALIGNED_DIL = 1024
ALIGNED_MEMQ = ALIGNED_DIL + (OFF_DIL - OFF_KR)
ALIGNED_GATE = ALIGNED_MEMQ + MEM_WIDTH

PREP_TM = 512
DILPROJ_TM = 1024
DILPROJ_MAX_STRIDE = 4
ALIGN_CB = 256
MLA_TQ = 2048
MLA_TK = 512
MLA_HPS = 2
MLA_QSTRIP = 256
MLA_VROWS = MLA_V + 16
assert PREP_TM == MLA_TK
DIL_TB = 1024
DIL_AHEAD = 4
MERGE_TM = 512
FFN_TM = 512
FFN_TF = 256
FFN_HALO = 16

TAIL_SPLIT = 4

VMEM_LIMIT = 56 * 1024 * 1024


def _rms(xf, g):
    return xf * lax.rsqrt(jnp.mean(xf * xf, axis=-1, keepdims=True) + RMS_EPS) * g


def _dot(a, b):
    return jnp.dot(a, b, preferred_element_type=F32)


def _dot_nt(a, b):
    return lax.dot_general(a, b, (((1,), (1,)), ((), ())), preferred_element_type=F32)


def _const_spec(shape):
    nd = len(shape)
    return pl.BlockSpec(shape, lambda *_: (0,) * nd)


def _prep_kernel(x_ref, pos_ref, invf_ref, g_ref, wa_ref, qn_ref, wqmt_ref, wqst_ref, kvn_ref, wk_ref, wvt_ref,
                 h_ref, qt_ref, k_ref, vt_ref):
    tm = x_ref.shape[0]
    h = _rms(x_ref[...], g_ref[...]).astype(BF16)
    h_ref[...] = h
    p = _dot_nt(h, wa_ref[...])
    cq = _rms(p[:, :OFF_Q], qn_ref[...]).astype(BF16)
    ckv = _rms(p[:, OFF_Q:OFF_KV], kvn_ref[...]).astype(BF16)

    ang = invf_ref[...] * pos_ref[0].astype(F32)
    c16 = jnp.cos(ang)
    s16 = jnp.sin(ang)
    ones = jnp.ones((MLA_NOPE, tm), F32)
    zeros_lo = jnp.zeros((MLA_NOPE, tm), F32)
    zeros_hi = jnp.zeros((LANES - MLA_QK_DIM, tm), F32)
    cos_c = jnp.concatenate([ones, c16, c16, zeros_hi], axis=0)
    sin_c = jnp.concatenate([zeros_lo, s16, s16, zeros_hi], axis=0)

    qmt = _dot_nt(wqmt_ref[...], cq)
    qst = _dot_nt(wqst_ref[...], cq)
    qscale = MLA_QK_DIM ** -0.5 * LOG2E
    cos_rope = jnp.concatenate([c16, c16], axis=0)
    sin_rope = jnp.concatenate([s16, s16], axis=0)
    for hd in range(MLA_HEADS):
        base = hd * LANES
        nope = qmt[hd * MLA_QK_DIM:hd * MLA_QK_DIM + MLA_NOPE, :]
        rope = qmt[hd * MLA_QK_DIM + MLA_NOPE:(hd + 1) * MLA_QK_DIM, :]
        rot = qst[hd * MLA_ROPE:(hd + 1) * MLA_ROPE, :]
        qt_ref[base:base + MLA_NOPE, :] = (nope * qscale).astype(BF16)
        qt_ref[base + MLA_NOPE:base + MLA_QK_DIM, :] = ((rope * cos_rope + rot * sin_rope) * qscale).astype(BF16)
        qt_ref[base + MLA_QK_DIM:base + LANES, :] = jnp.zeros((LANES - MLA_QK_DIM, tm), BF16)

    cos_r = cos_c.T
    sin_r = sin_c.T
    kn = _dot(ckv, wk_ref[...])
    kpe = p[:, OFF_KV:OFF_KV + LANES] * cos_r + p[:, OFF_KV + LANES:OFF_KV + 2 * LANES] * sin_r
    for hd in range(MLA_HEADS):
        sl = slice(hd * LANES, (hd + 1) * LANES)
        k_ref[:, sl] = (kn[:, sl] + kpe).astype(BF16)
    vt = _dot_nt(wvt_ref[...], ckv).astype(BF16)
    ones_rows = jnp.ones((MLA_VROWS - MLA_V, tm), BF16)
    for hd in range(MLA_HEADS):
        vt_ref[0, hd * MLA_VROWS:hd * MLA_VROWS + MLA_V, :] = vt[hd * MLA_V:(hd + 1) * MLA_V, :]
        vt_ref[0, hd * MLA_VROWS + MLA_V:(hd + 1) * MLA_VROWS, :] = ones_rows


def _prep(x2, pos_rows, invf, g_pre, w_al, q_norm, w_qmt, w_qst, kv_norm, w_k, w_vt, batch, seq):
    t, d = x2.shape
    tm = PREP_TM
    tps = seq // tm
    hw = MLA_HEADS * LANES
    vw = MLA_HEADS * MLA_VROWS
    row = lambda i: (i, 0)
    return pl.pallas_call(
        _prep_kernel,
        grid=(t // tm,),
        in_specs=[
            pl.BlockSpec((tm, d), row),
            pl.BlockSpec((1, 1, tm), lambda i: (i, 0, 0)),
            _const_spec(invf.shape),
            _const_spec(g_pre.shape),
            pl.BlockSpec((OFF_KV + 2 * LANES, d), lambda i: (0, 0)),
            _const_spec(q_norm.shape),
            _const_spec(w_qmt.shape),
            _const_spec(w_qst.shape),
            _const_spec(kv_norm.shape),
            _const_spec(w_k.shape),
            _const_spec(w_vt.shape),
        ],
        out_specs=[pl.BlockSpec((tm, d), row),
                   pl.BlockSpec((None, hw, tm), lambda i: (i // tps, 0, i % tps)),
                   pl.BlockSpec((tm, hw), row),
                   pl.BlockSpec((None, 1, vw, tm), lambda i: (i // tps, i % tps, 0, 0))],
        out_shape=[jax.ShapeDtypeStruct((t, d), BF16),
                   jax.ShapeDtypeStruct((batch, hw, seq), BF16),
                   jax.ShapeDtypeStruct((t, hw), BF16),
                   jax.ShapeDtypeStruct((batch, tps, vw, tm), BF16)],
        compiler_params=pltpu.CompilerParams(dimension_semantics=("arbitrary",), vmem_limit_bytes=VMEM_LIMIT),
        name="prep",
    )(x2, pos_rows, invf, g_pre, w_al, q_norm, w_qmt, w_qst, kv_norm, w_k, w_vt)


def _dilproj_kernel(h_ref, wq_ref, wk_ref, wv_ref, o_ref, *acc_refs, dil):
    h = h_ref[...]
    w_refs = (wq_ref, wk_ref, wv_ref)
    tm, tn = h_ref.shape[0], wq_ref.shape[0]

    def projection(j):
        acc = _dot_nt(h, w_refs[j][...])
        return acc * DIL_QSCALE if j == 0 else acc

    if dil == 1:
        for j in range(len(w_refs)):
            o_ref[0, :, j * tn:(j + 1) * tn] = projection(j).astype(BF16)
        return
    chunks = tn // LANES
    rows = tm // dil

    def project(j):
        acc = projection(j)
        for c in range(chunks):
            acc_refs[j][c] = acc[:, c * LANES:(c + 1) * LANES]

    def scatter(j):
        src = acc_refs[j]
        if dil > DILPROJ_MAX_STRIDE:
            s1 = DILPROJ_MAX_STRIDE
            s2 = dil // s1
            tmp = acc_refs[-1]
            for r1 in range(s1):
                for c in range(chunks):
                    tmp[c, r1 * (tm // s1):(r1 + 1) * (tm // s1), :] = src[c, pl.ds(r1, tm // s1, stride=s1), :]
            for r in range(dil):
                r1, r2 = r % s1, r // s1
                for c in range(chunks):
                    cols = slice(j * tn + c * LANES, j * tn + (c + 1) * LANES)
                    o_ref[r, :, cols] = tmp[c, pl.ds(r1 * (tm // s1) + r2, rows, stride=s2), :].astype(BF16)
            return
        for r in range(dil):
            for c in range(chunks):
                cols = slice(j * tn + c * LANES, j * tn + (c + 1) * LANES)
                o_ref[r, :, cols] = src[c, pl.ds(r, rows, stride=dil), :].astype(BF16)

    project(0)
    project(1)
    scatter(0)
    project(2)
    scatter(1)
    scatter(2)


def _dilproj(h2, w_al, batch, seq, dil, group):
    t, d = h2.shape
    n = 3 * DIL_WIDTH
    tm, tn = DILPROJ_TM, DIL_WIDTH
    assert ALIGNED_DIL % tn == 0
    tiles_per_seq = seq // tm
    col0 = ALIGNED_DIL // tn + group
    w_spec = lambda j: pl.BlockSpec((tn, d), lambda i: (col0 + j * DIL_GROUPS, 0))
    n_scratch = 0 if dil == 1 else (3 if dil <= DILPROJ_MAX_STRIDE else 4)
    scratch = [pltpu.VMEM((tn // LANES, tm, LANES), F32)] * n_scratch
    return pl.pallas_call(
        functools.partial(_dilproj_kernel, dil=dil),
        grid=(t // tm,),
        in_specs=[pl.BlockSpec((tm, d), lambda i: (i, 0)), w_spec(0), w_spec(1), w_spec(2)],
        out_specs=pl.BlockSpec((None, dil, tm // dil, n),
                               lambda i: (i // tiles_per_seq, 0, i % tiles_per_seq, 0)),
        out_shape=jax.ShapeDtypeStruct((batch, dil, seq // dil, n), BF16),
        scratch_shapes=scratch,
        compiler_params=pltpu.CompilerParams(dimension_semantics=("arbitrary",), vmem_limit_bytes=VMEM_LIMIT),
        name=f"dilproj{dil}",
    )(h2, w_al, w_al, w_al)


def _mla_kernel(qt_ref, k_ref, vt_ref, qtn_ref, k0n_ref, o_ref, m_ref, acc_ref, sta_ref, stb_ref):
    tq = qt_ref.shape[1]
    tk = MLA_TK
    qi = pl.program_id(2)
    nstrip = tq // MLA_QSTRIP
    m_ref[...] = jnp.full(m_ref.shape, NEG_INF, F32)
    acc_ref[...] = jnp.zeros(acc_ref.shape, F32)
    chains = [(a, hq) for a in range(MLA_HPS) for hq in range(nstrip)]

    def scores(c, i):
        a, hq = chains[i]
        k = k_ref[pl.ds(pl.multiple_of(c * tk, tk), tk), a * LANES:(a + 1) * LANES]
        return _dot(k, qt_ref[a * LANES:(a + 1) * LANES, hq * MLA_QSTRIP:(hq + 1) * MLA_QSTRIP])

    def successor_scores(i):
        a, hq = chains[i]
        return _dot(k0n_ref[:, a * LANES:(a + 1) * LANES],
                    qtn_ref[a * LANES:(a + 1) * LANES, hq * MLA_QSTRIP:(hq + 1) * MLA_QSTRIP])

    def step(c, src_ref, dst_ref, key_off=None, last=False):
        def live(hq, off):
            return off is None or (hq + 1) * MLA_QSTRIP > off

        next_off = None if key_off is None else key_off + tk
        m_all = m_ref[...]
        acc_all = acc_ref[...]
        m_new, acc_new = {}, {}
        for i, (a, hq) in enumerate(chains):
            qs = slice(hq * MLA_QSTRIP, (hq + 1) * MLA_QSTRIP)
            if last:
                dst_ref[i] = successor_scores(i)
            elif live(hq, next_off):
                dst_ref[i] = scores(c + 1, i)
            if not live(hq, key_off):
                m_new[a, hq], acc_new[a, hq] = m_all[a, :, qs], acc_all[a, :, qs]
                continue
            st = src_ref[i]
            if key_off is not None and key_off + tk - 1 > hq * MLA_QSTRIP:
                key = lax.broadcasted_iota(jnp.int32, st.shape, 0) + key_off
                qry = lax.broadcasted_iota(jnp.int32, st.shape, 1) + hq * MLA_QSTRIP
                st = jnp.where(key <= qry, st, NEG_INF)
            m_prev = m_all[a, :, qs]
            m_cur = jnp.maximum(m_prev, jnp.max(st, axis=0, keepdims=True))
            alpha = jnp.exp2(m_prev - m_cur)
            p = jnp.exp2(st - m_cur).astype(BF16)
            vt = vt_ref[c, a * MLA_VROWS:(a + 1) * MLA_VROWS, :]
            acc_new[a, hq] = alpha * acc_all[a, :, qs] + _dot(vt, p)
            m_new[a, hq] = m_cur
        for a in range(MLA_HPS):
            m_ref[a] = jnp.concatenate([m_new[a, hq] for hq in range(nstrip)], axis=-1)
            acc_ref[a] = jnp.concatenate([acc_new[a, hq] for hq in range(nstrip)], axis=-1)

    @pl.when(jnp.logical_and(jnp.logical_and(pl.program_id(0) == 0, pl.program_id(1) == 0), qi == 0))
    def _():
        for i in range(len(chains)):
            sta_ref[i] = scores(0, i)

    cpt = tq // tk
    assert cpt % 2 == 0
    bufs = (sta_ref, stb_ref)

    def trip(j, carry):
        for cc in range(cpt):
            step(cpt * j + cc, bufs[cc % 2], bufs[(cc + 1) % 2])
        return carry

    lax.fori_loop(0, qi, trip, 0)
    for cc in range(cpt):
        step(cpt * qi + cc, bufs[cc % 2], bufs[(cc + 1) % 2], key_off=cc * tk, last=cc == cpt - 1)

    out_t = jnp.concatenate([acc_ref[a, :MLA_V, :] / acc_ref[a, MLA_V:MLA_V + 1, :] for a in range(MLA_HPS)],
                            axis=0)
    o_ref[...] = out_t.T.astype(BF16)


def _mla(qt, k3, vt):
    b, s, _ = k3.shape
    tq, tk = MLA_TQ, MLA_TK
    hps = MLA_HPS
    st_scratch = pltpu.VMEM((hps * (tq // MLA_QSTRIP), tk, MLA_QSTRIP), F32)
    nh, nq = MLA_HEADS // hps, s // tq

    def successor(bi, h, i):
        wrap_i = i == nq - 1
        wrap_h = jnp.logical_and(wrap_i, h == nh - 1)
        i2 = jnp.where(wrap_i, 0, i + 1)
        h2 = jnp.where(wrap_h, 0, jnp.where(wrap_i, h + 1, h))
        b2 = jnp.minimum(bi + wrap_h.astype(jnp.int32), b - 1)
        return b2, h2, i2
    return pl.pallas_call(
        _mla_kernel,
        grid=(b, MLA_HEADS // hps, s // tq),
        in_specs=[
            pl.BlockSpec((None, hps * LANES, tq), lambda bi, h, i: (bi, h, i)),
            pl.BlockSpec((None, s, hps * LANES), lambda bi, h, i: (bi, 0, h)),
            pl.BlockSpec((None, s // tk, hps * MLA_VROWS, tk), lambda bi, h, i: (bi, 0, h, 0)),
            pl.BlockSpec((None, hps * LANES, tq), lambda bi, h, i: successor(bi, h, i)[:3]),
            pl.BlockSpec((None, tk, hps * LANES), lambda bi, h, i: (successor(bi, h, i)[0], 0, successor(bi, h, i)[1])),
        ],
        out_specs=pl.BlockSpec((None, tq, hps * MLA_V), lambda bi, h, i: (bi, i, h)),
        out_shape=jax.ShapeDtypeStruct((b, s, MLA_HEADS * MLA_V), BF16),
        scratch_shapes=[pltpu.VMEM((hps, 1, tq), F32), pltpu.VMEM((hps, MLA_VROWS, tq), F32),
                        st_scratch, st_scratch],
        compiler_params=pltpu.CompilerParams(dimension_semantics=("arbitrary", "arbitrary", "arbitrary"),
                                             vmem_limit_bytes=VMEM_LIMIT),
        name="mla",
    )(qt, k3, vt, qt, k3)


def _dilattn_kernel(q_ref, kc_ref, vc_ref, kp_ref, vp_ref, o_ref, lse_ref, *, dil, group):
    n = pl.program_id(2)
    nres, tb = q_ref.shape[0], q_ref.shape[1]
    nblk = tb // BLOCK

    qry = lax.broadcasted_iota(jnp.int32, (BLOCK, 2 * BLOCK), 0)
    key = lax.broadcasted_iota(jnp.int32, (BLOCK, 2 * BLOCK), 1)
    dist = qry + BLOCK - key
    in_window = jnp.logical_and(dist >= 0, dist <= BLOCK)
    first_ok = jnp.logical_and(in_window, jnp.logical_or(key >= BLOCK, n > 0))
    distf = (dist * dil).astype(F32)
    lane = lax.broadcasted_iota(jnp.int32, (BLOCK, LANES), 1)
    bias, bias_first = [], []
    for hh in range(DIL_HPG):
        slope = float(2.0 ** (-8.0 * (hh * DIL_GROUPS + group + 1) / DIL_HEADS))
        bias.append(jnp.where(in_window, -(slope * LOG2E) * distf, NEG_INF))
        bias_first.append(jnp.where(first_ok, -(slope * LOG2E) * distf, NEG_INF))

    units = [(r, t, hh) for r in range(nres) for t in range(nblk) for hh in range(DIL_HPG)]

    def window(cur_ref, prev_ref, u):
        r, t, hh = units[u]
        cs = slice(hh * DIL_HEAD_DIM, (hh + 1) * DIL_HEAD_DIM)
        if t == 0:
            return jnp.concatenate([prev_ref[r, :, cs], cur_ref[r, :BLOCK, cs]], axis=0)
        return cur_ref[r, (t - 1) * BLOCK:(t + 1) * BLOCK, cs]

    def scores(u):
        r, t, hh = units[u]
        cs = slice(hh * DIL_HEAD_DIM, (hh + 1) * DIL_HEAD_DIM)
        return _dot_nt(q_ref[r, t * BLOCK:(t + 1) * BLOCK, cs], window(kc_ref, kp_ref, u))

    pending = [scores(u) for u in range(min(DIL_AHEAD, len(units)))]
    lse_tile = None
    for u, (r, t, hh) in enumerate(units):
        if u + DIL_AHEAD < len(units):
            pending.append(scores(u + DIL_AHEAD))
        rs = slice(t * BLOCK, (t + 1) * BLOCK)
        cs = slice(hh * DIL_HEAD_DIM, (hh + 1) * DIL_HEAD_DIM)
        s = pending[u] + (bias_first[hh] if t == 0 else bias[hh])
        m = jnp.max(s, axis=-1, keepdims=True)
        e = jnp.exp2(s - m)
        den = jnp.sum(e, axis=-1, keepdims=True)
        o = _dot(e.astype(BF16), window(vc_ref, vp_ref, u)) / den
        o_ref[r, rs, cs] = o.astype(BF16)
        lse = (m + jnp.log2(den)) * LN2
        lse_tile = jnp.where(lane == hh, lse, jnp.zeros((BLOCK, LANES), F32) if hh == 0 else lse_tile)
        if hh == DIL_HPG - 1:
            lse_ref[r, rs, :] = lse_tile


def _dilattn(qkv, dil, group):
    b, d, l, _ = qkv.shape
    tb = min(DIL_TB, l)
    nres = min(d, DIL_TB // tb)
    bpt = tb // BLOCK
    w = DIL_WIDTH
    cur = lambda c: pl.BlockSpec((None, nres, tb, w), lambda bi, r, n: (bi, r, n, c))
    prev = lambda c: pl.BlockSpec((None, nres, BLOCK, w), lambda bi, r, n: (bi, r, jnp.maximum(n * bpt - 1, 0), c))
    return pl.pallas_call(
        functools.partial(_dilattn_kernel, dil=dil, group=group),
        grid=(b, d // nres, l // tb),
        in_specs=[cur(0), cur(1), cur(2), prev(1), prev(2)],
        out_specs=[pl.BlockSpec((None, nres, tb, w), lambda bi, r, n: (bi, r, n, 0)),
                   pl.BlockSpec((None, nres, tb, LANES), lambda bi, r, n: (bi, r, n, 0))],
        out_shape=[jax.ShapeDtypeStruct((b, d, l, w), BF16), jax.ShapeDtypeStruct((b, d, l, LANES), F32)],
        compiler_params=pltpu.CompilerParams(dimension_semantics=("arbitrary", "arbitrary", "arbitrary"),
                                             vmem_limit_bytes=VMEM_LIMIT),
        name=f"dilattn{dil}",
    )(qkv, qkv, qkv, qkv, qkv)


def _memkv_kernel(mem_ref, g_ref, w_ref, o_ref):
    o_ref[...] = _dot(_rms(mem_ref[...], g_ref[...]).astype(BF16), w_ref[...]).astype(BF16)


def _memkv(mem2, g_mem, w):
    return pl.pallas_call(
        _memkv_kernel,
        out_shape=jax.ShapeDtypeStruct((mem2.shape[0], w.shape[1]), BF16),
        compiler_params=pltpu.CompilerParams(vmem_limit_bytes=VMEM_LIMIT),
        name="memkv",
    )(mem2, g_mem, w)


def _sigmoid(z):
    return 1.0 / (1.0 + jnp.exp(-z))


def _merge_kernel(x_ref, h_ref, wmq_ref, wg_ref, bg_ref, kvm_ref, ymla_ref, wbm_ref,
                  o0_ref, o1_ref, o2_ref, l0_ref, l1_ref, l2_ref, wbd_ref, wbmem_ref, wo_ref, gpm_ref, gpf_ref,
                  x1_ref, h2_ref, nat1_ref, nat2_ref, lse1_ref, lse2_ref):
    tm, dm = x_ref.shape
    h = h_ref[...]

    def gate(br):
        cs = slice(br * dm, (br + 1) * dm)
        return _dot_nt(h, wg_ref[cs, :]) + bg_ref[:, cs]

    for src, lsrc, dst, ldst in ((o1_ref, l1_ref, nat1_ref, lse1_ref), (o2_ref, l2_ref, nat2_ref, lse2_ref)):
        d = src.shape[0]
        rows = src.shape[1]
        for r in range(d):
            for hh in range(DIL_HPG):
                cs = slice(hh * DIL_HEAD_DIM, (hh + 1) * DIL_HEAD_DIM)
                dst[hh, pl.ds(r, rows, stride=d), :] = src[r, :, cs].astype(F32)
            ldst[pl.ds(r, rows, stride=d), :] = lsrc[r]
    lg = (l0_ref[0], lse1_ref[...], lse2_ref[...])
    heads = []
    for hh in range(DIL_HPG):
        cs = slice(hh * DIL_HEAD_DIM, (hh + 1) * DIL_HEAD_DIM)
        og = (o0_ref[0, :, cs].astype(F32), nat1_ref[hh], nat2_ref[hh])
        ls = [l[:, hh:hh + 1] for l in lg]
        mx = jnp.maximum(jnp.maximum(ls[0], ls[1]), ls[2])
        ws = [jnp.exp(l - mx) for l in ls]
        num = ws[0] * og[0] + ws[1] * og[1] + ws[2] * og[2]
        heads.append((num / (ws[0] + ws[1] + ws[2])).astype(BF16))
    y_dil = jnp.concatenate(heads, axis=-1)

    memq = _dot_nt(h, wmq_ref[...])
    mheads = []
    for hh in range(MEM_HEADS):
        cs = slice(hh * MEM_HEAD_DIM, (hh + 1) * MEM_HEAD_DIM)
        q = (memq[:, cs] * MEM_HEAD_DIM ** -0.5).astype(BF16)
        s = _dot_nt(q, kvm_ref[:, cs])
        e = jnp.exp(s - jnp.max(s, axis=-1, keepdims=True))
        o = _dot(e.astype(BF16), kvm_ref[:, MEM_WIDTH + hh * MEM_HEAD_DIM:MEM_WIDTH + (hh + 1) * MEM_HEAD_DIM])
        mheads.append((o / jnp.sum(e, axis=-1, keepdims=True)).astype(BF16))
    y_mem = jnp.concatenate(mheads, axis=-1)

    merged = jnp.zeros((tm, dm), F32)
    for br, (y, w_ref) in enumerate(((ymla_ref[...], wbm_ref), (y_dil, wbd_ref), (y_mem, wbmem_ref))):
        merged = merged + _sigmoid(gate(br)) * _dot(y, w_ref[...])
    merged_b = merged.astype(BF16)
    rows = tm // TAIL_SPLIT
    mixed = [_dot(merged_b[g * rows:(g + 1) * rows, :], wo_ref[...]) for g in range(TAIL_SPLIT)]
    for g in range(TAIL_SPLIT):
        rs = slice(g * rows, (g + 1) * rows)
        x1 = x_ref[rs, :] + _rms(mixed[g], gpm_ref[...])
        x1_ref[rs, :] = x1
        h2_ref[rs, :] = _rms(x1, gpf_ref[...]).astype(BF16)


def _merge(x2, h2, w_al, b_g, kvm, y_mla, w_bm, o_dil, lse_dil, w_bd, w_bmem, w_o, g_pm, g_pf, batch, seq):
    t, dm = x2.shape
    tm = MERGE_TM
    tps = seq // tm
    n_mem = kvm.shape[0] // batch
    row = lambda i: (i, 0)
    single = pl.Buffered(1)
    const = lambda a: pl.BlockSpec(a.shape, lambda i: (0,) * a.ndim, pipeline_mode=single)
    gate_w = N_BRANCH * dm
    assert ALIGNED_MEMQ % MEM_WIDTH == 0 and ALIGNED_GATE % gate_w == 0
    w_mq = pl.BlockSpec((MEM_WIDTH, dm), lambda i: (ALIGNED_MEMQ // MEM_WIDTH, 0), pipeline_mode=single)
    w_g = pl.BlockSpec((gate_w, dm), lambda i: (ALIGNED_GATE // gate_w, 0), pipeline_mode=single)

    def dil_spec(a):
        d, width = a.shape[1], a.shape[3]
        return pl.BlockSpec((None, d, tm // d, width), lambda i: (i // tps, 0, i % tps, 0))

    return pl.pallas_call(
        _merge_kernel,
        grid=(t // tm,),
        in_specs=[
            pl.BlockSpec((tm, dm), row), pl.BlockSpec((tm, dm), row), w_mq, w_g, const(b_g),
            pl.BlockSpec((n_mem, kvm.shape[1]), lambda i: (i // tps, 0)),
            pl.BlockSpec((tm, y_mla.shape[1]), row), const(w_bm),
            dil_spec(o_dil[0]), dil_spec(o_dil[1]), dil_spec(o_dil[2]),
            dil_spec(lse_dil[0]), dil_spec(lse_dil[1]), dil_spec(lse_dil[2]),
            const(w_bd), const(w_bmem), const(w_o), const(g_pm), const(g_pf),
        ],
        out_specs=[pl.BlockSpec((tm, dm), row), pl.BlockSpec((tm, dm), row)],
        out_shape=[jax.ShapeDtypeStruct((t, dm), F32), jax.ShapeDtypeStruct((t, dm), BF16)],
        scratch_shapes=[pltpu.VMEM((DIL_HPG, tm, DIL_HEAD_DIM), F32), pltpu.VMEM((DIL_HPG, tm, DIL_HEAD_DIM), F32),
                        pltpu.VMEM((tm, LANES), F32), pltpu.VMEM((tm, LANES), F32)],
        compiler_params=pltpu.CompilerParams(dimension_semantics=("arbitrary",), vmem_limit_bytes=VMEM_LIMIT),
        name="merge",
    )(x2, h2, w_al, w_al, b_g, kvm, y_mla, w_bm, o_dil[0], o_dil[1], o_dil[2], lse_dil[0], lse_dil[1], lse_dil[2],
      w_bd, w_bmem, w_o, g_pm, g_pf)


def _ffn_kernel(x1_ref, h2_ref, halo_ref, wup_ref, cw_ref, cb_ref, wd_ref, gpost_ref,
                out_ref, hcat_ref, ua_ref, ub_ref, acc_ref, *, tiles_per_seq):
    i = pl.program_id(0)
    tm = x1_ref.shape[0]
    halo, tf = FFN_HALO, FFN_TF
    dff = wd_ref.shape[0]
    nchunk = dff // tf
    lanes_per_chunk = tf // LANES

    first = (i % tiles_per_seq) == 0
    hcat_ref[:halo, :] = jnp.where(first, jnp.zeros_like(halo_ref[...]), halo_ref[...])
    hcat_ref[halo:, :] = h2_ref[...]

    def up(c, u_ref):
        hc = hcat_ref[...]
        for part, off in enumerate((c * tf, dff + c * tf)):
            u = _dot(hc, wup_ref[:, off:off + tf])
            for j in range(lanes_per_chunk):
                u_ref[part * lanes_per_chunk + j] = u[:, j * LANES:(j + 1) * LANES]

    def conv(u_ref, slab, col):
        cols = slice(col, col + LANES)
        z = cb_ref[:, cols] + cw_ref[0:1, cols] * u_ref[slab, halo - 2:halo - 2 + tm, :]
        z = z + cw_ref[1:2, cols] * u_ref[slab, halo - 1:halo - 1 + tm, :]
        return z + cw_ref[2:3, cols] * u_ref[slab, halo:halo + tm, :]

    bufs = (ua_ref, ub_ref)
    up(0, bufs[0])
    for c in range(nchunk):
        cur = bufs[c % 2]
        if c + 1 < nchunk:
            up(c + 1, bufs[(c + 1) % 2])
        acts = []
        for j in range(lanes_per_chunk):
            gate = conv(cur, j, c * tf + j * LANES)
            val = conv(cur, lanes_per_chunk + j, dff + c * tf + j * LANES)
            acts.append((gate * _sigmoid(gate) * val).astype(BF16))
        down = _dot(jnp.concatenate(acts, axis=-1), wd_ref[c * tf:(c + 1) * tf, :])
        if c == 0:
            acc_ref[...] = down
        else:
            acc_ref[...] += down

    out_ref[...] = x1_ref[...] + _rms(acc_ref[...], gpost_ref[...])


def _ffn(x1, h2, w_up, conv_w, conv_b, w_down, g_post, seq):
    t, dm = x1.shape
    tm, tf, halo = FFN_TM, FFN_TF, FFN_HALO
    tps = seq // tm
    row = lambda i: (i, 0)
    const = lambda a: pl.BlockSpec(a.shape, lambda i: (0,) * a.ndim, pipeline_mode=pl.Buffered(1))
    u_scratch = pltpu.VMEM((2 * tf // LANES, tm + halo, LANES), F32)
    return pl.pallas_call(
        functools.partial(_ffn_kernel, tiles_per_seq=tps),
        grid=(t // tm,),
        in_specs=[
            pl.BlockSpec((tm, dm), row),
            pl.BlockSpec((tm, dm), row),
            pl.BlockSpec((halo, dm), lambda i: (jnp.maximum(i * (tm // halo) - 1, 0), 0)),
            const(w_up), const(conv_w), const(conv_b), const(w_down), const(g_post),
        ],
        out_specs=pl.BlockSpec((tm, dm), row),
        out_shape=jax.ShapeDtypeStruct((t, dm), F32),
        scratch_shapes=[pltpu.VMEM((tm + halo, dm), BF16), u_scratch, u_scratch, pltpu.VMEM((tm, dm), F32)],
        compiler_params=pltpu.CompilerParams(dimension_semantics=("arbitrary",), vmem_limit_bytes=VMEM_LIMIT),
        name="ffn",
    )(x1, h2, h2, w_up, conv_w, conv_b, w_down, g_post)


def _rot_half_cols(w):
    return jnp.concatenate([-w[..., ROPE_HALF:], w[..., :ROPE_HALF]], axis=-1)


def _align_kernel(wt_ref, o_ref):
    cb = wt_ref.shape[1]
    zeros = lambda n: jnp.zeros((n, cb), BF16)
    kr = wt_ref[OFF_KV:OFF_KR, :]
    o_ref[:OFF_KV, :] = wt_ref[:OFF_KV, :].astype(BF16)
    o_ref[OFF_KV:OFF_KV + MLA_NOPE, :] = zeros(MLA_NOPE)
    o_ref[OFF_KV + MLA_NOPE:OFF_KV + MLA_QK_DIM, :] = kr.astype(BF16)
    o_ref[OFF_KV + MLA_QK_DIM:OFF_KV + LANES + MLA_NOPE, :] = zeros(LANES - MLA_QK_DIM + MLA_NOPE)
    o_ref[OFF_KV + LANES + MLA_NOPE:OFF_KV + LANES + MLA_NOPE + ROPE_HALF, :] = (-kr[ROPE_HALF:, :]).astype(BF16)
    o_ref[OFF_KV + LANES + MLA_NOPE + ROPE_HALF:OFF_KV + LANES + MLA_QK_DIM, :] = kr[:ROPE_HALF, :].astype(BF16)
    o_ref[OFF_KV + LANES + MLA_QK_DIM:ALIGNED_DIL, :] = zeros(ALIGNED_DIL - OFF_KV - LANES - MLA_QK_DIM)
    o_ref[ALIGNED_DIL:, :] = wt_ref[OFF_KR:, :].astype(BF16)


def _align_w_in(w_in_all, layer):
    wt_all = jnp.swapaxes(w_in_all, 1, 2)
    _, d_in, dm = wt_all.shape
    cb = ALIGN_CB
    width = ALIGNED_DIL + d_in - OFF_KR
    return pl.pallas_call(
        _align_kernel,
        grid=(dm // cb,),
        in_specs=[pl.BlockSpec((None, d_in, cb), lambda i: (layer, 0, i))],
        out_specs=pl.BlockSpec((width, cb), lambda i: (0, i)),
        out_shape=jax.ShapeDtypeStruct((width, dm), BF16),
        compiler_params=pltpu.CompilerParams(dimension_semantics=("arbitrary",), vmem_limit_bytes=VMEM_LIMIT),
        name="align",
    )(wt_all)


def _prep_weights(w_uq, w_ukv):
    uq = w_uq.reshape(MLA_Q_RANK, MLA_HEADS, MLA_QK_DIM)
    w_qm = w_uq
    w_qs = _rot_half_cols(uq[..., MLA_NOPE:]).reshape(MLA_Q_RANK, MLA_HEADS * MLA_ROPE)

    ukv = w_ukv.reshape(MLA_KV_RANK, MLA_HEADS, MLA_NOPE + MLA_V)
    zk = jnp.zeros((MLA_KV_RANK, MLA_HEADS, LANES - MLA_NOPE), F32)
    w_k = jnp.concatenate([ukv[..., :MLA_NOPE], zk], axis=-1).reshape(MLA_KV_RANK, MLA_HEADS * LANES)
    w_v = ukv[..., MLA_NOPE:].reshape(MLA_KV_RANK, MLA_HEADS * MLA_V)
    return tuple(a.astype(BF16) for a in (w_qm.T, w_qs.T, w_k, w_v.T))


def _layer(layer, w_in_all, x, mem, positions, g_pre_mix, b_gate, mla_q_norm, w_uq, mla_kv_norm, w_ukv, g_mem, w_mem_kv,
           w_br_mla, w_br_dil, w_br_mem, w_o, g_post_mix, g_pre_ffn, w_ffn_up, conv_w, conv_b, w_ffn_down,
           g_post_ffn):
    batch, seq, dm = x.shape
    t = batch * seq
    x2 = x.reshape(t, dm)
    r2 = lambda v: v.reshape(1, -1)

    w_al = _align_w_in(w_in_all, layer)
    w_qmt, w_qst, w_k, w_vt = _prep_weights(w_uq, w_ukv)
    invf = (ROPE_THETA ** (-jnp.arange(ROPE_HALF, dtype=F32) / ROPE_HALF)).reshape(ROPE_HALF, 1)
    pos_rows = positions.reshape(t // PREP_TM, 1, PREP_TM)

    h2d, qt, k, vt = _prep(x2, pos_rows, invf, r2(g_pre_mix), w_al, r2(mla_q_norm), w_qmt, w_qst,
                           r2(mla_kv_norm), w_k, w_vt, batch, seq)
    y_mla = _mla(qt, k.reshape(batch, seq, MLA_HEADS * LANES), vt).reshape(t, MLA_HEADS * MLA_V)

    o_dil, lse_dil = [], []
    for g, (_, dil) in enumerate(DIL_PAIRS):
        qkv = _dilproj(h2d, w_al, batch, seq, dil, g)
        o, lse = _dilattn(qkv, dil, g)
        o_dil.append(o)
        lse_dil.append(lse)

    kvm = _memkv(mem.reshape(-1, dm), r2(g_mem), w_mem_kv.astype(BF16))

    x1, h2 = _merge(x2, h2d, w_al, r2(b_gate), kvm, y_mla, w_br_mla.astype(BF16), o_dil, lse_dil,
                    w_br_dil.astype(BF16),
                    w_br_mem.astype(BF16), w_o.astype(BF16), r2(g_post_mix), r2(g_pre_ffn), batch, seq)

    out = _ffn(x1, h2, w_ffn_up.astype(BF16), conv_w, r2(conv_b), w_ffn_down.astype(BF16), r2(g_post_ffn), seq)
    return out.reshape(batch, seq, dm)


def kernel(x, mem, positions, g_pre_mix, w_in, b_gate, mla_q_norm, w_uq, mla_kv_norm, w_ukv, g_mem, w_mem_kv,
           w_br_mla, w_br_dil, w_br_mem, w_o, g_post_mix, g_pre_ffn, w_ffn_up, conv_w, conv_b, w_ffn_down,
           g_post_ffn):
    for l in range(w_in.shape[0]):
        x = _layer(l, w_in, x, mem, positions, g_pre_mix[l], b_gate[l], mla_q_norm[l], w_uq[l], mla_kv_norm[l],
                   w_ukv[l], g_mem[l], w_mem_kv[l], w_br_mla[l], w_br_dil[l], w_br_mem[l], w_o[l], g_post_mix[l],
                   g_pre_ffn[l], w_ffn_up[l], conv_w[l], conv_b[l], w_ffn_down[l], g_post_ffn[l])
    return x
```

```python
import functools

import jax
import jax.numpy as jnp
from jax import lax
from jax.experimental import pallas as pl
from jax.experimental.pallas import tpu as pltpu

F32 = jnp.float32
BF16 = jnp.bfloat16

RMS_EPS = 1e-6
LOG2E = 1.4426950408889634
LN2 = 0.6931471805599453
NEG_INF = -1e30
LANES = 128

BLOCK = 128
MLA_HEADS = 8
MLA_NOPE = 64
MLA_ROPE = 32
MLA_V = 64
MLA_QK_DIM = MLA_NOPE + MLA_ROPE
MLA_Q_RANK = 384
MLA_KV_RANK = 256
ROPE_THETA = 10000.0
ROPE_HALF = MLA_ROPE // 2

DIL_PAIRS = ((128, 1), (512, 4), (2048, 16))
DIL_GROUPS = 3
DIL_HPG = 4
DIL_HEADS = DIL_GROUPS * DIL_HPG
DIL_HEAD_DIM = 128
DIL_WIDTH = DIL_HPG * DIL_HEAD_DIM
DIL_QSCALE = DIL_HEAD_DIM ** -0.5 * LOG2E

MEM_HEADS = 4
MEM_HEAD_DIM = 128
MEM_WIDTH = MEM_HEADS * MEM_HEAD_DIM

N_BRANCH = 3
CONV_WIDTH = 3

OFF_Q = MLA_Q_RANK
OFF_KV = OFF_Q + MLA_KV_RANK
OFF_KR = OFF_KV + MLA_ROPE
OFF_DIL = OFF_KR + 3 * DIL_HEADS * DIL_HEAD_DIM
OFF_MEMQ = OFF_DIL + MEM_WIDTH
ALIGNED_DIL = 1024
ALIGNED_MEMQ = ALIGNED_DIL + (OFF_DIL - OFF_KR)
ALIGNED_GATE = ALIGNED_MEMQ + MEM_WIDTH

PREP_TM = 512
DILPROJ_TM = 1024
DILPROJ_MAX_STRIDE = 4
ALIGN_CB = 256
MLA_TQ = 2048
MLA_TK = 512
MLA_HPS = 2
MLA_QSTRIP = 256
MLA_VROWS = MLA_V + 16
assert PREP_TM == MLA_TK
DIL_TB = 1024
DIL_AHEAD = 4
MERGE_TM = 512
FFN_TM = 512
FFN_TF = 256
FFN_HALO = 16

TAIL_SPLIT = 4

VMEM_LIMIT = 56 * 1024 * 1024


def _rms(xf, g):
    return xf * lax.rsqrt(jnp.mean(xf * xf, axis=-1, keepdims=True) + RMS_EPS) * g


def _dot(a, b):
    return jnp.dot(a, b, preferred_element_type=F32)


def _dot_nt(a, b):
    return lax.dot_general(a, b, (((1,), (1,)), ((), ())), preferred_element_type=F32)


def _const_spec(shape):
    nd = len(shape)
    return pl.BlockSpec(shape, lambda *_: (0,) * nd)


def _prep_kernel(x_ref, pos_ref, invf_ref, g_ref, wa_ref, qn_ref, wqmt_ref, wqst_ref, kvn_ref, wk_ref, wvt_ref,
                 h_ref, qt_ref, k_ref, vt_ref):
    tm = x_ref.shape[0]
    h = _rms(x_ref[...], g_ref[...]).astype(BF16)
    h_ref[...] = h
    p = _dot_nt(h, wa_ref[...])
    cq = _rms(p[:, :OFF_Q], qn_ref[...]).astype(BF16)
    ckv = _rms(p[:, OFF_Q:OFF_KV], kvn_ref[...]).astype(BF16)

    ang = invf_ref[...] * pos_ref[0].astype(F32)
    c16 = jnp.cos(ang)
    s16 = jnp.sin(ang)
    ones = jnp.ones((MLA_NOPE, tm), F32)
    zeros_lo = jnp.zeros((MLA_NOPE, tm), F32)
    zeros_hi = jnp.zeros((LANES - MLA_QK_DIM, tm), F32)
    cos_c = jnp.concatenate([ones, c16, c16, zeros_hi], axis=0)
    sin_c = jnp.concatenate([zeros_lo, s16, s16, zeros_hi], axis=0)

    qmt = _dot_nt(wqmt_ref[...], cq)
    qst = _dot_nt(wqst_ref[...], cq)
    qscale = MLA_QK_DIM ** -0.5 * LOG2E
    cos_rope = jnp.concatenate([c16, c16], axis=0)
    sin_rope = jnp.concatenate([s16, s16], axis=0)
    for hd in range(MLA_HEADS):
        base = hd * LANES
        nope = qmt[hd * MLA_QK_DIM:hd * MLA_QK_DIM + MLA_NOPE, :]
        rope = qmt[hd * MLA_QK_DIM + MLA_NOPE:(hd + 1) * MLA_QK_DIM, :]
        rot = qst[hd * MLA_ROPE:(hd + 1) * MLA_ROPE, :]
        qt_ref[base:base + MLA_NOPE, :] = (nope * qscale).astype(BF16)
        qt_ref[base + MLA_NOPE:base + MLA_QK_DIM, :] = ((rope * cos_rope + rot * sin_rope) * qscale).astype(BF16)
        qt_ref[base + MLA_QK_DIM:base + LANES, :] = jnp.zeros((LANES - MLA_QK_DIM, tm), BF16)

    cos_r = cos_c.T
    sin_r = sin_c.T
    kn = _dot(ckv, wk_ref[...])
    kpe = p[:, OFF_KV:OFF_KV + LANES] * cos_r + p[:, OFF_KV + LANES:OFF_KV + 2 * LANES] * sin_r
    for hd in range(MLA_HEADS):
        sl = slice(hd * LANES, (hd + 1) * LANES)
        k_ref[:, sl] = (kn[:, sl] + kpe).astype(BF16)
    vt = _dot_nt(wvt_ref[...], ckv).astype(BF16)
    ones_rows = jnp.ones((MLA_VROWS - MLA_V, tm), BF16)
    for hd in range(MLA_HEADS):
        vt_ref[0, hd * MLA_VROWS:hd * MLA_VROWS + MLA_V, :] = vt[hd * MLA_V:(hd + 1) * MLA_V, :]
        vt_ref[0, hd * MLA_VROWS + MLA_V:(hd + 1) * MLA_VROWS, :] = ones_rows


def _prep(x2, pos_rows, invf, g_pre, w_al, q_norm, w_qmt, w_qst, kv_norm, w_k, w_vt, batch, seq):
    t, d = x2.shape
    tm = PREP_TM
    tps = seq // tm
    hw = MLA_HEADS * LANES
    vw = MLA_HEADS * MLA_VROWS
    row = lambda i: (i, 0)
    return pl.pallas_call(
        _prep_kernel,
        grid=(t // tm,),
        in_specs=[
            pl.BlockSpec((tm, d), row),
            pl.BlockSpec((1, 1, tm), lambda i: (i, 0, 0)),
            _const_spec(invf.shape),
            _const_spec(g_pre.shape),
            pl.BlockSpec((OFF_KV + 2 * LANES, d), lambda i: (0, 0)),
            _const_spec(q_norm.shape),
            _const_spec(w_qmt.shape),
            _const_spec(w_qst.shape),
            _const_spec(kv_norm.shape),
            _const_spec(w_k.shape),
            _const_spec(w_vt.shape),
        ],
        out_specs=[pl.BlockSpec((tm, d), row),
                   pl.BlockSpec((None, hw, tm), lambda i: (i // tps, 0, i % tps)),
                   pl.BlockSpec((tm, hw), row),
                   pl.BlockSpec((None, 1, vw, tm), lambda i: (i // tps, i % tps, 0, 0))],
        out_shape=[jax.ShapeDtypeStruct((t, d), BF16),
                   jax.ShapeDtypeStruct((batch, hw, seq), BF16),
                   jax.ShapeDtypeStruct((t, hw), BF16),
                   jax.ShapeDtypeStruct((batch, tps, vw, tm), BF16)],
        compiler_params=pltpu.CompilerParams(dimension_semantics=("arbitrary",), vmem_limit_bytes=VMEM_LIMIT),
        name="prep",
    )(x2, pos_rows, invf, g_pre, w_al, q_norm, w_qmt, w_qst, kv_norm, w_k, w_vt)


def _dilproj_kernel(h_ref, wq_ref, wk_ref, wv_ref, o_ref, *acc_refs, dil):
    h = h_ref[...]
    w_refs = (wq_ref, wk_ref, wv_ref)
    tm, tn = h_ref.shape[0], wq_ref.shape[0]

    def projection(j):
        acc = _dot_nt(h, w_refs[j][...])
        return acc * DIL_QSCALE if j == 0 else acc

    if dil == 1:
        for j in range(len(w_refs)):
            o_ref[0, :, j * tn:(j + 1) * tn] = projection(j).astype(BF16)
        return
    chunks = tn // LANES
    rows = tm // dil

    def project(j):
        acc = projection(j)
        for c in range(chunks):
            acc_refs[j][c] = acc[:, c * LANES:(c + 1) * LANES]

    def scatter(j):
        src = acc_refs[j]
        if dil > DILPROJ_MAX_STRIDE:
            s1 = DILPROJ_MAX_STRIDE
            s2 = dil // s1
            tmp = acc_refs[-1]
            for r1 in range(s1):
                for c in range(chunks):
                    tmp[c, r1 * (tm // s1):(r1 + 1) * (tm // s1), :] = src[c, pl.ds(r1, tm // s1, stride=s1), :]
            for r in range(dil):
                r1, r2 = r % s1, r // s1
                for c in range(chunks):
                    cols = slice(j * tn + c * LANES, j * tn + (c + 1) * LANES)
                    o_ref[r, :, cols] = tmp[c, pl.ds(r1 * (tm // s1) + r2, rows, stride=s2), :].astype(BF16)
            return
        for r in range(dil):
            for c in range(chunks):
                cols = slice(j * tn + c * LANES, j * tn + (c + 1) * LANES)
                o_ref[r, :, cols] = src[c, pl.ds(r, rows, stride=dil), :].astype(BF16)

    project(0)
    project(1)
    scatter(0)
    project(2)
    scatter(1)
    scatter(2)


def _dilproj(h2, w_al, batch, seq, dil, group):
    t, d = h2.shape
    n = 3 * DIL_WIDTH
    tm, tn = DILPROJ_TM, DIL_WIDTH
    assert ALIGNED_DIL % tn == 0
    tiles_per_seq = seq // tm
    col0 = ALIGNED_DIL // tn + group
    w_spec = lambda j: pl.BlockSpec((tn, d), lambda i: (col0 + j * DIL_GROUPS, 0))
    n_scratch = 0 if dil == 1 else (3 if dil <= DILPROJ_MAX_STRIDE else 4)
    scratch = [pltpu.VMEM((tn // LANES, tm, LANES), F32)] * n_scratch
    return pl.pallas_call(
        functools.partial(_dilproj_kernel, dil=dil),
        grid=(t // tm,),
        in_specs=[pl.BlockSpec((tm, d), lambda i: (i, 0)), w_spec(0), w_spec(1), w_spec(2)],
        out_specs=pl.BlockSpec((None, dil, tm // dil, n),
                               lambda i: (i // tiles_per_seq, 0, i % tiles_per_seq, 0)),
        out_shape=jax.ShapeDtypeStruct((batch, dil, seq // dil, n), BF16),
        scratch_shapes=scratch,
        compiler_params=pltpu.CompilerParams(dimension_semantics=("arbitrary",), vmem_limit_bytes=VMEM_LIMIT),
        name=f"dilproj{dil}",
    )(h2, w_al, w_al, w_al)


def _mla_kernel(qt_ref, k_ref, vt_ref, qtn_ref, k0n_ref, o_ref, m_ref, acc_ref, sta_ref, stb_ref):
    tq = qt_ref.shape[1]
    tk = MLA_TK
    qi = pl.program_id(2)
    nstrip = tq // MLA_QSTRIP
    m_ref[...] = jnp.full(m_ref.shape, NEG_INF, F32)
    acc_ref[...] = jnp.zeros(acc_ref.shape, F32)
    chains = [(a, hq) for a in range(MLA_HPS) for hq in range(nstrip)]

    def scores(c, i, nk=tk):
        a, hq = chains[i]
        k = k_ref[pl.ds(pl.multiple_of(c * tk, tk), nk), a * LANES:(a + 1) * LANES]
        return _dot(k, qt_ref[a * LANES:(a + 1) * LANES, hq * MLA_QSTRIP:(hq + 1) * MLA_QSTRIP])

    def successor_scores(i):
        a, hq = chains[i]
        return _dot(k0n_ref[:, a * LANES:(a + 1) * LANES],
                    qtn_ref[a * LANES:(a + 1) * LANES, hq * MLA_QSTRIP:(hq + 1) * MLA_QSTRIP])

    def step(c, src_ref, dst_ref, key_off=None, last=False):
        def visible(hq, off):
            return tk if off is None else max(0, min(tk, (hq + 1) * MLA_QSTRIP - off))

        next_off = None if key_off is None else key_off + tk
        m_all = m_ref[...]
        acc_all = acc_ref[...]
        m_new, acc_new = {}, {}
        for i, (a, hq) in enumerate(chains):
            qs = slice(hq * MLA_QSTRIP, (hq + 1) * MLA_QSTRIP)
            if last:
                dst_ref[i] = successor_scores(i)
            elif visible(hq, next_off) > 0:
                nk_next = visible(hq, next_off)
                dst_ref[i, :nk_next, :] = scores(c + 1, i, nk_next)
            nk = visible(hq, key_off)
            if nk == 0:
                m_new[a, hq], acc_new[a, hq] = m_all[a, :, qs], acc_all[a, :, qs]
                continue
            st = src_ref[i, :nk, :]
            if key_off is not None and key_off + nk - 1 > hq * MLA_QSTRIP:
                key = lax.broadcasted_iota(jnp.int32, st.shape, 0) + key_off
                qry = lax.broadcasted_iota(jnp.int32, st.shape, 1) + hq * MLA_QSTRIP
                st = jnp.where(key <= qry, st, NEG_INF)
            m_prev = m_all[a, :, qs]
            m_cur = jnp.maximum(m_prev, jnp.max(st, axis=0, keepdims=True))
            alpha = jnp.exp2(m_prev - m_cur)
            p = jnp.exp2(st - m_cur).astype(BF16)
            vt = vt_ref[c, a * MLA_VROWS:(a + 1) * MLA_VROWS, :nk]
            acc_new[a, hq] = alpha * acc_all[a, :, qs] + _dot(vt, p)
            m_new[a, hq] = m_cur
        for a in range(MLA_HPS):
            m_ref[a] = jnp.concatenate([m_new[a, hq] for hq in range(nstrip)], axis=-1)
            acc_ref[a] = jnp.concatenate([acc_new[a, hq] for hq in range(nstrip)], axis=-1)

    @pl.when(jnp.logical_and(jnp.logical_and(pl.program_id(0) == 0, pl.program_id(1) == 0), qi == 0))
    def _():
        for i in range(len(chains)):
            sta_ref[i] = scores(0, i)

    cpt = tq // tk
    assert cpt % 2 == 0
    bufs = (sta_ref, stb_ref)

    def trip(j, carry):
        for cc in range(cpt):
            step(cpt * j + cc, bufs[cc % 2], bufs[(cc + 1) % 2])
        return carry

    lax.fori_loop(0, qi, trip, 0)
    for cc in range(cpt):
        step(cpt * qi + cc, bufs[cc % 2], bufs[(cc + 1) % 2], key_off=cc * tk, last=cc == cpt - 1)

    out_t = jnp.concatenate([acc_ref[a, :MLA_V, :] / acc_ref[a, MLA_V:MLA_V + 1, :] for a in range(MLA_HPS)],
                            axis=0)
    o_ref[...] = out_t.T.astype(BF16)


def _mla(qt, k3, vt):
    b, s, _ = k3.shape
    tq, tk = MLA_TQ, MLA_TK
    hps = MLA_HPS
    st_scratch = pltpu.VMEM((hps * (tq // MLA_QSTRIP), tk, MLA_QSTRIP), F32)
    nh, nq = MLA_HEADS // hps, s // tq

    def successor(bi, h, i):
        wrap_i = i == nq - 1
        wrap_h = jnp.logical_and(wrap_i, h == nh - 1)
        i2 = jnp.where(wrap_i, 0, i + 1)
        h2 = jnp.where(wrap_h, 0, jnp.where(wrap_i, h + 1, h))
        b2 = jnp.minimum(bi + wrap_h.astype(jnp.int32), b - 1)
        return b2, h2, i2
    return pl.pallas_call(
        _mla_kernel,
        grid=(b, MLA_HEADS // hps, s // tq),
        in_specs=[
            pl.BlockSpec((None, hps * LANES, tq), lambda bi, h, i: (bi, h, i)),
            pl.BlockSpec((None, s, hps * LANES), lambda bi, h, i: (bi, 0, h)),
            pl.BlockSpec((None, s // tk, hps * MLA_VROWS, tk), lambda bi, h, i: (bi, 0, h, 0)),
            pl.BlockSpec((None, hps * LANES, tq), lambda bi, h, i: successor(bi, h, i)[:3]),
            pl.BlockSpec((None, tk, hps * LANES), lambda bi, h, i: (successor(bi, h, i)[0], 0, successor(bi, h, i)[1])),
        ],
        out_specs=pl.BlockSpec((None, tq, hps * MLA_V), lambda bi, h, i: (bi, i, h)),
        out_shape=jax.ShapeDtypeStruct((b, s, MLA_HEADS * MLA_V), BF16),
        scratch_shapes=[pltpu.VMEM((hps, 1, tq), F32), pltpu.VMEM((hps, MLA_VROWS, tq), F32),
                        st_scratch, st_scratch],
        compiler_params=pltpu.CompilerParams(dimension_semantics=("arbitrary", "arbitrary", "arbitrary"),
                                             vmem_limit_bytes=VMEM_LIMIT),
        name="mla",
    )(qt, k3, vt, qt, k3)


def _dilattn_kernel(q_ref, kc_ref, vc_ref, kp_ref, vp_ref, o_ref, lse_ref, *, dil, group):
    n = pl.program_id(2)
    nres, tb = q_ref.shape[0], q_ref.shape[1]
    nblk = tb // BLOCK

    qry = lax.broadcasted_iota(jnp.int32, (BLOCK, 2 * BLOCK), 0)
    key = lax.broadcasted_iota(jnp.int32, (BLOCK, 2 * BLOCK), 1)
    dist = qry + BLOCK - key
    in_window = jnp.logical_and(dist >= 0, dist <= BLOCK)
    first_ok = jnp.logical_and(in_window, jnp.logical_or(key >= BLOCK, n > 0))
    distf = (dist * dil).astype(F32)
    lane = lax.broadcasted_iota(jnp.int32, (BLOCK, LANES), 1)
    bias, bias_first = [], []
    for hh in range(DIL_HPG):
        slope = float(2.0 ** (-8.0 * (hh * DIL_GROUPS + group + 1) / DIL_HEADS))
        bias.append(jnp.where(in_window, -(slope * LOG2E) * distf, NEG_INF))
        bias_first.append(jnp.where(first_ok, -(slope * LOG2E) * distf, NEG_INF))

    units = [(r, t, hh) for r in range(nres) for t in range(nblk) for hh in range(DIL_HPG)]

    def window(cur_ref, prev_ref, u):
        r, t, hh = units[u]
        cs = slice(hh * DIL_HEAD_DIM, (hh + 1) * DIL_HEAD_DIM)
        if t == 0:
            return jnp.concatenate([prev_ref[r, :, cs], cur_ref[r, :BLOCK, cs]], axis=0)
        return cur_ref[r, (t - 1) * BLOCK:(t + 1) * BLOCK, cs]

    def scores(u):
        r, t, hh = units[u]
        cs = slice(hh * DIL_HEAD_DIM, (hh + 1) * DIL_HEAD_DIM)
        return _dot_nt(q_ref[r, t * BLOCK:(t + 1) * BLOCK, cs], window(kc_ref, kp_ref, u))

    pending = [scores(u) for u in range(min(DIL_AHEAD, len(units)))]
    lse_tile = None
    for u, (r, t, hh) in enumerate(units):
        if u + DIL_AHEAD < len(units):
            pending.append(scores(u + DIL_AHEAD))
        rs = slice(t * BLOCK, (t + 1) * BLOCK)
        cs = slice(hh * DIL_HEAD_DIM, (hh + 1) * DIL_HEAD_DIM)
        s = pending[u] + (bias_first[hh] if t == 0 else bias[hh])
        m = jnp.max(s, axis=-1, keepdims=True)
        e = jnp.exp2(s - m)
        den = jnp.sum(e, axis=-1, keepdims=True)
        o = _dot(e.astype(BF16), window(vc_ref, vp_ref, u)) / den
        o_ref[r, rs, cs] = o.astype(BF16)
        lse = (m + jnp.log2(den)) * LN2
        lse_tile = jnp.where(lane == hh, lse, jnp.zeros((BLOCK, LANES), F32) if hh == 0 else lse_tile)
        if hh == DIL_HPG - 1:
            lse_ref[r, rs, :] = lse_tile


def _dilattn(qkv, dil, group):
    b, d, l, _ = qkv.shape
    tb = min(DIL_TB, l)
    nres = min(d, DIL_TB // tb)
    bpt = tb // BLOCK
    w = DIL_WIDTH
    cur = lambda c: pl.BlockSpec((None, nres, tb, w), lambda bi, r, n: (bi, r, n, c))
    prev = lambda c: pl.BlockSpec((None, nres, BLOCK, w), lambda bi, r, n: (bi, r, jnp.maximum(n * bpt - 1, 0), c))
    return pl.pallas_call(
        functools.partial(_dilattn_kernel, dil=dil, group=group),
        grid=(b, d // nres, l // tb),
        in_specs=[cur(0), cur(1), cur(2), prev(1), prev(2)],
        out_specs=[pl.BlockSpec((None, nres, tb, w), lambda bi, r, n: (bi, r, n, 0)),
                   pl.BlockSpec((None, nres, tb, LANES), lambda bi, r, n: (bi, r, n, 0))],
        out_shape=[jax.ShapeDtypeStruct((b, d, l, w), BF16), jax.ShapeDtypeStruct((b, d, l, LANES), F32)],
        compiler_params=pltpu.CompilerParams(dimension_semantics=("arbitrary", "arbitrary", "arbitrary"),
                                             vmem_limit_bytes=VMEM_LIMIT),
        name=f"dilattn{dil}",
    )(qkv, qkv, qkv, qkv, qkv)


def _memkv_kernel(mem_ref, g_ref, w_ref, o_ref):
    o_ref[...] = _dot(_rms(mem_ref[...], g_ref[...]).astype(BF16), w_ref[...]).astype(BF16)


def _memkv(mem2, g_mem, w):
    return pl.pallas_call(
        _memkv_kernel,
        out_shape=jax.ShapeDtypeStruct((mem2.shape[0], w.shape[1]), BF16),
        compiler_params=pltpu.CompilerParams(vmem_limit_bytes=VMEM_LIMIT),
        name="memkv",
    )(mem2, g_mem, w)


def _sigmoid(z):
    return 1.0 / (1.0 + jnp.exp(-z))


def _merge_kernel(x_ref, h_ref, wmq_ref, wg_ref, bg_ref, kvm_ref, ymla_ref, wbm_ref,
                  o0_ref, o1_ref, o2_ref, l0_ref, l1_ref, l2_ref, wbd_ref, wbmem_ref, wo_ref, gpm_ref, gpf_ref,
                  x1_ref, h2_ref, nat1_ref, nat2_ref, lse1_ref, lse2_ref):
    tm, dm = x_ref.shape
    h = h_ref[...]

    def gate(br):
        cs = slice(br * dm, (br + 1) * dm)
        return _dot_nt(h, wg_ref[cs, :]) + bg_ref[:, cs]

    for src, lsrc, dst, ldst in ((o1_ref, l1_ref, nat1_ref, lse1_ref), (o2_ref, l2_ref, nat2_ref, lse2_ref)):
        d = src.shape[0]
        rows = src.shape[1]
        for r in range(d):
            for hh in range(DIL_HPG):
                cs = slice(hh * DIL_HEAD_DIM, (hh + 1) * DIL_HEAD_DIM)
                dst[hh, pl.ds(r, rows, stride=d), :] = src[r, :, cs].astype(F32)
            ldst[pl.ds(r, rows, stride=d), :] = lsrc[r]
    lg = (l0_ref[0], lse1_ref[...], lse2_ref[...])
    heads = []
    for hh in range(DIL_HPG):
        cs = slice(hh * DIL_HEAD_DIM, (hh + 1) * DIL_HEAD_DIM)
        og = (o0_ref[0, :, cs].astype(F32), nat1_ref[hh], nat2_ref[hh])
        ls = [l[:, hh:hh + 1] for l in lg]
        mx = jnp.maximum(jnp.maximum(ls[0], ls[1]), ls[2])
        ws = [jnp.exp(l - mx) for l in ls]
        num = ws[0] * og[0] + ws[1] * og[1] + ws[2] * og[2]
        heads.append((num / (ws[0] + ws[1] + ws[2])).astype(BF16))
    y_dil = jnp.concatenate(heads, axis=-1)

    memq = _dot_nt(h, wmq_ref[...])
    mheads = []
    for hh in range(MEM_HEADS):
        cs = slice(hh * MEM_HEAD_DIM, (hh + 1) * MEM_HEAD_DIM)
        q = (memq[:, cs] * MEM_HEAD_DIM ** -0.5).astype(BF16)
        s = _dot_nt(q, kvm_ref[:, cs])
        e = jnp.exp(s - jnp.max(s, axis=-1, keepdims=True))
        o = _dot(e.astype(BF16), kvm_ref[:, MEM_WIDTH + hh * MEM_HEAD_DIM:MEM_WIDTH + (hh + 1) * MEM_HEAD_DIM])
        mheads.append((o / jnp.sum(e, axis=-1, keepdims=True)).astype(BF16))
    y_mem = jnp.concatenate(mheads, axis=-1)

    merged = jnp.zeros((tm, dm), F32)
    for br, (y, w_ref) in enumerate(((ymla_ref[...], wbm_ref), (y_dil, wbd_ref), (y_mem, wbmem_ref))):
        merged = merged + _sigmoid(gate(br)) * _dot(y, w_ref[...])
    merged_b = merged.astype(BF16)
    rows = tm // TAIL_SPLIT
    mixed = [_dot(merged_b[g * rows:(g + 1) * rows, :], wo_ref[...]) for g in range(TAIL_SPLIT)]
    for g in range(TAIL_SPLIT):
        rs = slice(g * rows, (g + 1) * rows)
        x1 = x_ref[rs, :] + _rms(mixed[g], gpm_ref[...])
        x1_ref[rs, :] = x1
        h2_ref[rs, :] = _rms(x1, gpf_ref[...]).astype(BF16)


def _merge(x2, h2, w_al, b_g, kvm, y_mla, w_bm, o_dil, lse_dil, w_bd, w_bmem, w_o, g_pm, g_pf, batch, seq):
    t, dm = x2.shape
    tm = MERGE_TM
    tps = seq // tm
    n_mem = kvm.shape[0] // batch
    row = lambda i: (i, 0)
    single = pl.Buffered(1)
    const = lambda a: pl.BlockSpec(a.shape, lambda i: (0,) * a.ndim, pipeline_mode=single)
    gate_w = N_BRANCH * dm
    assert ALIGNED_MEMQ % MEM_WIDTH == 0 and ALIGNED_GATE % gate_w == 0
    w_mq = pl.BlockSpec((MEM_WIDTH, dm), lambda i: (ALIGNED_MEMQ // MEM_WIDTH, 0), pipeline_mode=single)
    w_g = pl.BlockSpec((gate_w, dm), lambda i: (ALIGNED_GATE // gate_w, 0), pipeline_mode=single)

    def dil_spec(a):
        d, width = a.shape[1], a.shape[3]
        return pl.BlockSpec((None, d, tm // d, width), lambda i: (i // tps, 0, i % tps, 0))

    return pl.pallas_call(
        _merge_kernel,
        grid=(t // tm,),
        in_specs=[
            pl.BlockSpec((tm, dm), row), pl.BlockSpec((tm, dm), row), w_mq, w_g, const(b_g),
            pl.BlockSpec((n_mem, kvm.shape[1]), lambda i: (i // tps, 0)),
            pl.BlockSpec((tm, y_mla.shape[1]), row), const(w_bm),
            dil_spec(o_dil[0]), dil_spec(o_dil[1]), dil_spec(o_dil[2]),
            dil_spec(lse_dil[0]), dil_spec(lse_dil[1]), dil_spec(lse_dil[2]),
            const(w_bd), const(w_bmem), const(w_o), const(g_pm), const(g_pf),
        ],
        out_specs=[pl.BlockSpec((tm, dm), row), pl.BlockSpec((tm, dm), row)],
        out_shape=[jax.ShapeDtypeStruct((t, dm), F32), jax.ShapeDtypeStruct((t, dm), BF16)],
        scratch_shapes=[pltpu.VMEM((DIL_HPG, tm, DIL_HEAD_DIM), F32), pltpu.VMEM((DIL_HPG, tm, DIL_HEAD_DIM), F32),
                        pltpu.VMEM((tm, LANES), F32), pltpu.VMEM((tm, LANES), F32)],
        compiler_params=pltpu.CompilerParams(dimension_semantics=("arbitrary",), vmem_limit_bytes=VMEM_LIMIT),
        name="merge",
    )(x2, h2, w_al, w_al, b_g, kvm, y_mla, w_bm, o_dil[0], o_dil[1], o_dil[2], lse_dil[0], lse_dil[1], lse_dil[2],
      w_bd, w_bmem, w_o, g_pm, g_pf)


def _ffn_kernel(x1_ref, h2_ref, halo_ref, wup_ref, cw_ref, cb_ref, wd_ref, gpost_ref,
                out_ref, hcat_ref, ua_ref, ub_ref, acc_ref, *, tiles_per_seq):
    i = pl.program_id(0)
    tm = x1_ref.shape[0]
    halo, tf = FFN_HALO, FFN_TF
    dff = wd_ref.shape[0]
    nchunk = dff // tf
    lanes_per_chunk = tf // LANES

    first = (i % tiles_per_seq) == 0
    hcat_ref[:halo, :] = jnp.where(first, jnp.zeros_like(halo_ref[...]), halo_ref[...])
    hcat_ref[halo:, :] = h2_ref[...]

    def up(c, u_ref):
        hc = hcat_ref[...]
        for part, off in enumerate((c * tf, dff + c * tf)):
            u = _dot(hc, wup_ref[:, off:off + tf])
            for j in range(lanes_per_chunk):
                u_ref[part * lanes_per_chunk + j] = u[:, j * LANES:(j + 1) * LANES]

    def conv(u_ref, slab, col):
        cols = slice(col, col + LANES)
        z = cb_ref[:, cols] + cw_ref[0:1, cols] * u_ref[slab, halo - 2:halo - 2 + tm, :]
        z = z + cw_ref[1:2, cols] * u_ref[slab, halo - 1:halo - 1 + tm, :]
        return z + cw_ref[2:3, cols] * u_ref[slab, halo:halo + tm, :]

    bufs = (ua_ref, ub_ref)
    up(0, bufs[0])
    for c in range(nchunk):
        cur = bufs[c % 2]
        if c + 1 < nchunk:
            up(c + 1, bufs[(c + 1) % 2])
        acts = []
        for j in range(lanes_per_chunk):
            gate = conv(cur, j, c * tf + j * LANES)
            val = conv(cur, lanes_per_chunk + j, dff + c * tf + j * LANES)
            acts.append((gate * _sigmoid(gate) * val).astype(BF16))
        down = _dot(jnp.concatenate(acts, axis=-1), wd_ref[c * tf:(c + 1) * tf, :])
        if c == 0:
            acc_ref[...] = down
        else:
            acc_ref[...] += down

    out_ref[...] = x1_ref[...] + _rms(acc_ref[...], gpost_ref[...])


def _ffn(x1, h2, w_up, conv_w, conv_b, w_down, g_post, seq):
    t, dm = x1.shape
    tm, tf, halo = FFN_TM, FFN_TF, FFN_HALO
    tps = seq // tm
    row = lambda i: (i, 0)
    const = lambda a: pl.BlockSpec(a.shape, lambda i: (0,) * a.ndim, pipeline_mode=pl.Buffered(1))
    u_scratch = pltpu.VMEM((2 * tf // LANES, tm + halo, LANES), F32)
    return pl.pallas_call(
        functools.partial(_ffn_kernel, tiles_per_seq=tps),
        grid=(t // tm,),
        in_specs=[
            pl.BlockSpec((tm, dm), row),
            pl.BlockSpec((tm, dm), row),
            pl.BlockSpec((halo, dm), lambda i: (jnp.maximum(i * (tm // halo) - 1, 0), 0)),
            const(w_up), const(conv_w), const(conv_b), const(w_down), const(g_post),
        ],
        out_specs=pl.BlockSpec((tm, dm), row),
        out_shape=jax.ShapeDtypeStruct((t, dm), F32),
        scratch_shapes=[pltpu.VMEM((tm + halo, dm), BF16), u_scratch, u_scratch, pltpu.VMEM((tm, dm), F32)],
        compiler_params=pltpu.CompilerParams(dimension_semantics=("arbitrary",), vmem_limit_bytes=VMEM_LIMIT),
        name="ffn",
    )(x1, h2, h2, w_up, conv_w, conv_b, w_down, g_post)


def _rot_half_cols(w):
    return jnp.concatenate([-w[..., ROPE_HALF:], w[..., :ROPE_HALF]], axis=-1)


def _align_kernel(wt_ref, o_ref):
    cb = wt_ref.shape[1]
    zeros = lambda n: jnp.zeros((n, cb), BF16)
    kr = wt_ref[OFF_KV:OFF_KR, :]
    o_ref[:OFF_KV, :] = wt_ref[:OFF_KV, :].astype(BF16)
    o_ref[OFF_KV:OFF_KV + MLA_NOPE, :] = zeros(MLA_NOPE)
    o_ref[OFF_KV + MLA_NOPE:OFF_KV + MLA_QK_DIM, :] = kr.astype(BF16)
    o_ref[OFF_KV + MLA_QK_DIM:OFF_KV + LANES + MLA_NOPE, :] = zeros(LANES - MLA_QK_DIM + MLA_NOPE)
    o_ref[OFF_KV + LANES + MLA_NOPE:OFF_KV + LANES + MLA_NOPE + ROPE_HALF, :] = (-kr[ROPE_HALF:, :]).astype(BF16)
    o_ref[OFF_KV + LANES + MLA_NOPE + ROPE_HALF:OFF_KV + LANES + MLA_QK_DIM, :] = kr[:ROPE_HALF, :].astype(BF16)
    o_ref[OFF_KV + LANES + MLA_QK_DIM:ALIGNED_DIL, :] = zeros(ALIGNED_DIL - OFF_KV - LANES - MLA_QK_DIM)
    o_ref[ALIGNED_DIL:, :] = wt_ref[OFF_KR:, :].astype(BF16)


def _align_w_in(w_in_all, layer):
    wt_all = jnp.swapaxes(w_in_all, 1, 2)
    _, d_in, dm = wt_all.shape
    cb = ALIGN_CB
    width = ALIGNED_DIL + d_in - OFF_KR
    return pl.pallas_call(
        _align_kernel,
        grid=(dm // cb,),
        in_specs=[pl.BlockSpec((None, d_in, cb), lambda i: (layer, 0, i))],
        out_specs=pl.BlockSpec((width, cb), lambda i: (0, i)),
        out_shape=jax.ShapeDtypeStruct((width, dm), BF16),
        compiler_params=pltpu.CompilerParams(dimension_semantics=("arbitrary",), vmem_limit_bytes=VMEM_LIMIT),
        name="align",
    )(wt_all)


def _prep_weights(w_uq, w_ukv):
    uq = w_uq.reshape(MLA_Q_RANK, MLA_HEADS, MLA_QK_DIM)
    w_qm = w_uq
    w_qs = _rot_half_cols(uq[..., MLA_NOPE:]).reshape(MLA_Q_RANK, MLA_HEADS * MLA_ROPE)

    ukv = w_ukv.reshape(MLA_KV_RANK, MLA_HEADS, MLA_NOPE + MLA_V)
    zk = jnp.zeros((MLA_KV_RANK, MLA_HEADS, LANES - MLA_NOPE), F32)
    w_k = jnp.concatenate([ukv[..., :MLA_NOPE], zk], axis=-1).reshape(MLA_KV_RANK, MLA_HEADS * LANES)
    w_v = ukv[..., MLA_NOPE:].reshape(MLA_KV_RANK, MLA_HEADS * MLA_V)
    return tuple(a.astype(BF16) for a in (w_qm.T, w_qs.T, w_k, w_v.T))


def _layer(layer, w_in_all, x, mem, positions, g_pre_mix, b_gate, mla_q_norm, w_uq, mla_kv_norm, w_ukv, g_mem, w_mem_kv,
           w_br_mla, w_br_dil, w_br_mem, w_o, g_post_mix, g_pre_ffn, w_ffn_up, conv_w, conv_b, w_ffn_down,
           g_post_ffn):
    batch, seq, dm = x.shape
    t = batch * seq
    x2 = x.reshape(t, dm)
    r2 = lambda v: v.reshape(1, -1)

    w_al = _align_w_in(w_in_all, layer)
    w_qmt, w_qst, w_k, w_vt = _prep_weights(w_uq, w_ukv)
    invf = (ROPE_THETA ** (-jnp.arange(ROPE_HALF, dtype=F32) / ROPE_HALF)).reshape(ROPE_HALF, 1)
    pos_rows = positions.reshape(t // PREP_TM, 1, PREP_TM)

    h2d, qt, k, vt = _prep(x2, pos_rows, invf, r2(g_pre_mix), w_al, r2(mla_q_norm), w_qmt, w_qst,
                           r2(mla_kv_norm), w_k, w_vt, batch, seq)
    y_mla = _mla(qt, k.reshape(batch, seq, MLA_HEADS * LANES), vt).reshape(t, MLA_HEADS * MLA_V)

    o_dil, lse_dil = [], []
    for g, (_, dil) in enumerate(DIL_PAIRS):
        qkv = _dilproj(h2d, w_al, batch, seq, dil, g)
        o, lse = _dilattn(qkv, dil, g)
        o_dil.append(o)
        lse_dil.append(lse)

    kvm = _memkv(mem.reshape(-1, dm), r2(g_mem), w_mem_kv.astype(BF16))

    x1, h2 = _merge(x2, h2d, w_al, r2(b_gate), kvm, y_mla, w_br_mla.astype(BF16), o_dil, lse_dil,
                    w_br_dil.astype(BF16),
                    w_br_mem.astype(BF16), w_o.astype(BF16), r2(g_post_mix), r2(g_pre_ffn), batch, seq)

    out = _ffn(x1, h2, w_ffn_up.astype(BF16), conv_w, r2(conv_b), w_ffn_down.astype(BF16), r2(g_post_ffn), seq)
    return out.reshape(batch, seq, dm)


def kernel(x, mem, positions, g_pre_mix, w_in, b_gate, mla_q_norm, w_uq, mla_kv_norm, w_ukv, g_mem, w_mem_kv,
           w_br_mla, w_br_dil, w_br_mem, w_o, g_post_mix, g_pre_ffn, w_ffn_up, conv_w, conv_b, w_ffn_down,
           g_post_ffn):
    for l in range(w_in.shape[0]):
        x = _layer(l, w_in, x, mem, positions, g_pre_mix[l], b_gate[l], mla_q_norm[l], w_uq[l], mla_kv_norm[l],
                   w_ukv[l], g_mem[l], w_mem_kv[l], w_br_mla[l], w_br_dil[l], w_br_mem[l], w_o[l], g_post_mix[l],
                   g_pre_ffn[l], w_ffn_up[l], conv_w[l], conv_b[l], w_ffn_down[l], g_post_ffn[l])
    return x
```

```python
import functools

import jax
import jax.numpy as jnp
from jax import lax
from jax.experimental import pallas as pl
from jax.experimental.pallas import tpu as pltpu

F32 = jnp.float32
BF16 = jnp.bfloat16

RMS_EPS = 1e-6
LOG2E = 1.4426950408889634
LN2 = 0.6931471805599453
NEG_INF = -1e30
LANES = 128
BF16_SUBLANES = 16
V7X_VMEM_BYTES = 64 * 1024 * 1024

BLOCK = 128
MLA_HEADS = 8
MLA_NOPE = 64
MLA_ROPE = 32
MLA_V = 64
MLA_QK_DIM = MLA_NOPE + MLA_ROPE
MLA_Q_RANK = 384
MLA_KV_RANK = 256
ROPE_THETA = 10000.0
ROPE_HALF = MLA_ROPE // 2

DIL_PAIRS = ((128, 1), (512, 4), (2048, 16))
DIL_GROUPS = 3
DIL_HPG = 4
DIL_HEADS = DIL_GROUPS * DIL_HPG
DIL_HEAD_DIM = 128
DIL_WIDTH = DIL_HPG * DIL_HEAD_DIM
DIL_QSCALE = DIL_HEAD_DIM ** -0.5 * LOG2E

MEM_HEADS = 4
MEM_HEAD_DIM = 128
MEM_WIDTH = MEM_HEADS * MEM_HEAD_DIM

N_BRANCH = 3
CONV_WIDTH = 3

OFF_Q = MLA_Q_RANK
OFF_KV = OFF_Q + MLA_KV_RANK
OFF_KR = OFF_KV + MLA_ROPE
OFF_DIL = OFF_KR + 3 * DIL_HEADS * DIL_HEAD_DIM
OFF_MEMQ = OFF_DIL + MEM_WIDTH
ALIGNED_DIL = 1024
ALIGNED_MEMQ = ALIGNED_DIL + (OFF_DIL - OFF_KR)
ALIGNED_GATE = ALIGNED_MEMQ + MEM_WIDTH

PREP_TM = 512
DILPROJ_TM = 1024
DILPROJ_MAX_STRIDE = 4
ALIGN_CB = 256
MLA_TQ = 2048
MLA_TK = 512
MLA_HPS = 2
MLA_QSTRIP = 256
MLA_VROWS = MLA_V + BF16_SUBLANES
assert PREP_TM == MLA_TK
DIL_TB = 1024
DIL_AHEAD = 4
MERGE_TM = 512
FFN_TM = 512
FFN_TF = 256
FFN_HALO = BF16_SUBLANES

TAIL_SPLIT = 4

VMEM_LIMIT = V7X_VMEM_BYTES * 7 // 8


def _rms(xf, g):
    return xf * lax.rsqrt(jnp.mean(xf * xf, axis=-1, keepdims=True) + RMS_EPS) * g


def _dot(a, b):
    return jnp.dot(a, b, preferred_element_type=F32)


def _dot_nt(a, b):
    return lax.dot_general(a, b, (((1,), (1,)), ((), ())), preferred_element_type=F32)


def _const_spec(shape):
    nd = len(shape)
    return pl.BlockSpec(shape, lambda *_: (0,) * nd)


def _prep_kernel(x_ref, pos_ref, invf_ref, g_ref, wa_ref, qn_ref, wqmt_ref, wqst_ref, kvn_ref, wk_ref, wvt_ref,
                 h_ref, qt_ref, k_ref, vt_ref):
    tm = x_ref.shape[0]
    h = _rms(x_ref[...], g_ref[...]).astype(BF16)
    h_ref[...] = h
    p = _dot_nt(h, wa_ref[...])
    cq = _rms(p[:, :OFF_Q], qn_ref[...]).astype(BF16)
    ckv = _rms(p[:, OFF_Q:OFF_KV], kvn_ref[...]).astype(BF16)

    ang = invf_ref[...] * pos_ref[0].astype(F32)
    c16 = jnp.cos(ang)
    s16 = jnp.sin(ang)
    ones = jnp.ones((MLA_NOPE, tm), F32)
    zeros_lo = jnp.zeros((MLA_NOPE, tm), F32)
    zeros_hi = jnp.zeros((LANES - MLA_QK_DIM, tm), F32)
    cos_c = jnp.concatenate([ones, c16, c16, zeros_hi], axis=0)
    sin_c = jnp.concatenate([zeros_lo, s16, s16, zeros_hi], axis=0)

    qmt = _dot_nt(wqmt_ref[...], cq)
    qst = _dot_nt(wqst_ref[...], cq)
    qscale = MLA_QK_DIM ** -0.5 * LOG2E
    cos_rope = jnp.concatenate([c16, c16], axis=0)
    sin_rope = jnp.concatenate([s16, s16], axis=0)
    for hd in range(MLA_HEADS):
        base = hd * LANES
        nope = qmt[hd * MLA_QK_DIM:hd * MLA_QK_DIM + MLA_NOPE, :]
        rope = qmt[hd * MLA_QK_DIM + MLA_NOPE:(hd + 1) * MLA_QK_DIM, :]
        rot = qst[hd * MLA_ROPE:(hd + 1) * MLA_ROPE, :]
        qt_ref[base:base + MLA_NOPE, :] = (nope * qscale).astype(BF16)
        qt_ref[base + MLA_NOPE:base + MLA_QK_DIM, :] = ((rope * cos_rope + rot * sin_rope) * qscale).astype(BF16)
        qt_ref[base + MLA_QK_DIM:base + LANES, :] = jnp.zeros((LANES - MLA_QK_DIM, tm), BF16)

    cos_r = cos_c.T
    sin_r = sin_c.T
    kn = _dot(ckv, wk_ref[...])
    kpe = p[:, OFF_KV:OFF_KV + LANES] * cos_r + p[:, OFF_KV + LANES:OFF_KV + 2 * LANES] * sin_r
    for hd in range(MLA_HEADS):
        sl = slice(hd * LANES, (hd + 1) * LANES)
        k_ref[:, sl] = (kn[:, sl] + kpe).astype(BF16)
    vt = _dot_nt(wvt_ref[...], ckv).astype(BF16)
    ones_rows = jnp.ones((MLA_VROWS - MLA_V, tm), BF16)
    for hd in range(MLA_HEADS):
        vt_ref[0, hd * MLA_VROWS:hd * MLA_VROWS + MLA_V, :] = vt[hd * MLA_V:(hd + 1) * MLA_V, :]
        vt_ref[0, hd * MLA_VROWS + MLA_V:(hd + 1) * MLA_VROWS, :] = ones_rows


def _prep(x2, pos_rows, invf, g_pre, w_al, q_norm, w_qmt, w_qst, kv_norm, w_k, w_vt, batch, seq):
    t, d = x2.shape
    tm = PREP_TM
    tps = seq // tm
    hw = MLA_HEADS * LANES
    vw = MLA_HEADS * MLA_VROWS
    row = lambda i: (i, 0)
    return pl.pallas_call(
        _prep_kernel,
        grid=(t // tm,),
        in_specs=[
            pl.BlockSpec((tm, d), row),
            pl.BlockSpec((1, 1, tm), lambda i: (i, 0, 0)),
            _const_spec(invf.shape),
            _const_spec(g_pre.shape),
            pl.BlockSpec((OFF_KV + 2 * LANES, d), lambda i: (0, 0)),
            _const_spec(q_norm.shape),
            _const_spec(w_qmt.shape),
            _const_spec(w_qst.shape),
            _const_spec(kv_norm.shape),
            _const_spec(w_k.shape),
            _const_spec(w_vt.shape),
        ],
        out_specs=[pl.BlockSpec((tm, d), row),
                   pl.BlockSpec((None, hw, tm), lambda i: (i // tps, 0, i % tps)),
                   pl.BlockSpec((tm, hw), row),
                   pl.BlockSpec((None, 1, vw, tm), lambda i: (i // tps, i % tps, 0, 0))],
        out_shape=[jax.ShapeDtypeStruct((t, d), BF16),
                   jax.ShapeDtypeStruct((batch, hw, seq), BF16),
                   jax.ShapeDtypeStruct((t, hw), BF16),
                   jax.ShapeDtypeStruct((batch, tps, vw, tm), BF16)],
        compiler_params=pltpu.CompilerParams(dimension_semantics=("arbitrary",), vmem_limit_bytes=VMEM_LIMIT),
        name="prep",
    )(x2, pos_rows, invf, g_pre, w_al, q_norm, w_qmt, w_qst, kv_norm, w_k, w_vt)


def _dilproj_kernel(h_ref, wq_ref, wk_ref, wv_ref, o_ref, *acc_refs, dil):
    h = h_ref[...]
    w_refs = (wq_ref, wk_ref, wv_ref)
    tm, tn = h_ref.shape[0], wq_ref.shape[0]

    def projection(j):
        acc = _dot_nt(h, w_refs[j][...])
        return acc * DIL_QSCALE if j == 0 else acc

    if dil == 1:
        for j in range(len(w_refs)):
            o_ref[0, :, j * tn:(j + 1) * tn] = projection(j).astype(BF16)
        return
    chunks = tn // LANES
    rows = tm // dil

    def project(j):
        acc = projection(j)
        for c in range(chunks):
            acc_refs[j][c] = acc[:, c * LANES:(c + 1) * LANES]

    def scatter(j):
        src = acc_refs[j]
        if dil > DILPROJ_MAX_STRIDE:
            s1 = DILPROJ_MAX_STRIDE
            s2 = dil // s1
            tmp = acc_refs[-1]
            for r1 in range(s1):
                for c in range(chunks):
                    tmp[c, r1 * (tm // s1):(r1 + 1) * (tm // s1), :] = src[c, pl.ds(r1, tm // s1, stride=s1), :]
            for r in range(dil):
                r1, r2 = r % s1, r // s1
                for c in range(chunks):
                    cols = slice(j * tn + c * LANES, j * tn + (c + 1) * LANES)
                    o_ref[r, :, cols] = tmp[c, pl.ds(r1 * (tm // s1) + r2, rows, stride=s2), :].astype(BF16)
            return
        for r in range(dil):
            for c in range(chunks):
                cols = slice(j * tn + c * LANES, j * tn + (c + 1) * LANES)
                o_ref[r, :, cols] = src[c, pl.ds(r, rows, stride=dil), :].astype(BF16)

    project(0)
    project(1)
    scatter(0)
    project(2)
    scatter(1)
    scatter(2)


def _dilproj(h2, w_al, batch, seq, dil, group):
    t, d = h2.shape
    n = 3 * DIL_WIDTH
    tm, tn = DILPROJ_TM, DIL_WIDTH
    assert ALIGNED_DIL % tn == 0
    tiles_per_seq = seq // tm
    col0 = ALIGNED_DIL // tn + group
    w_spec = lambda j: pl.BlockSpec((tn, d), lambda i: (col0 + j * DIL_GROUPS, 0))
    n_scratch = 0 if dil == 1 else (3 if dil <= DILPROJ_MAX_STRIDE else 4)
    scratch = [pltpu.VMEM((tn // LANES, tm, LANES), F32)] * n_scratch
    return pl.pallas_call(
        functools.partial(_dilproj_kernel, dil=dil),
        grid=(t // tm,),
        in_specs=[pl.BlockSpec((tm, d), lambda i: (i, 0)), w_spec(0), w_spec(1), w_spec(2)],
        out_specs=pl.BlockSpec((None, dil, tm // dil, n),
                               lambda i: (i // tiles_per_seq, 0, i % tiles_per_seq, 0)),
        out_shape=jax.ShapeDtypeStruct((batch, dil, seq // dil, n), BF16),
        scratch_shapes=scratch,
        compiler_params=pltpu.CompilerParams(dimension_semantics=("arbitrary",), vmem_limit_bytes=VMEM_LIMIT),
        name=f"dilproj{dil}",
    )(h2, w_al, w_al, w_al)


def _mla_kernel(qt_ref, k_ref, vt_ref, qtn_ref, k0n_ref, o_ref, m_ref, acc_ref, sta_ref, stb_ref):
    tq = qt_ref.shape[1]
    tk = MLA_TK
    qi = pl.program_id(2)
    nstrip = tq // MLA_QSTRIP
    m_ref[...] = jnp.full(m_ref.shape, NEG_INF, F32)
    acc_ref[...] = jnp.zeros(acc_ref.shape, F32)
    chains = [(a, hq) for a in range(MLA_HPS) for hq in range(nstrip)]

    def scores(c, i, nk=tk):
        a, hq = chains[i]
        k = k_ref[pl.ds(pl.multiple_of(c * tk, tk), nk), a * LANES:(a + 1) * LANES]
        return _dot(k, qt_ref[a * LANES:(a + 1) * LANES, hq * MLA_QSTRIP:(hq + 1) * MLA_QSTRIP])

    def successor_scores(i):
        a, hq = chains[i]
        return _dot(k0n_ref[:, a * LANES:(a + 1) * LANES],
                    qtn_ref[a * LANES:(a + 1) * LANES, hq * MLA_QSTRIP:(hq + 1) * MLA_QSTRIP])

    def step(c, src_ref, dst_ref, key_off=None, last=False):
        def visible(hq, off):
            return tk if off is None else max(0, min(tk, (hq + 1) * MLA_QSTRIP - off))

        next_off = None if key_off is None else key_off + tk
        m_all = m_ref[...]
        acc_all = acc_ref[...]
        m_new, acc_new = {}, {}
        for i, (a, hq) in enumerate(chains):
            qs = slice(hq * MLA_QSTRIP, (hq + 1) * MLA_QSTRIP)
            if last:
                dst_ref[i] = successor_scores(i)
            elif visible(hq, next_off) > 0:
                nk_next = visible(hq, next_off)
                dst_ref[i, :nk_next, :] = scores(c + 1, i, nk_next)
            nk = visible(hq, key_off)
            if nk == 0:
                m_new[a, hq], acc_new[a, hq] = m_all[a, :, qs], acc_all[a, :, qs]
                continue
            st = src_ref[i, :nk, :]
            if key_off is not None and key_off + nk - 1 > hq * MLA_QSTRIP:
                key = lax.broadcasted_iota(jnp.int32, st.shape, 0) + key_off
                qry = lax.broadcasted_iota(jnp.int32, st.shape, 1) + hq * MLA_QSTRIP
                st = jnp.where(key <= qry, st, NEG_INF)
            m_prev = m_all[a, :, qs]
            m_cur = jnp.maximum(m_prev, jnp.max(st, axis=0, keepdims=True))
            alpha = jnp.exp2(m_prev - m_cur)
            p = jnp.exp2(st - m_cur).astype(BF16)
            vt = vt_ref[c, a * MLA_VROWS:(a + 1) * MLA_VROWS, :nk]
            acc_new[a, hq] = alpha * acc_all[a, :, qs] + _dot(vt, p)
            m_new[a, hq] = m_cur
        for a in range(MLA_HPS):
            m_ref[a] = jnp.concatenate([m_new[a, hq] for hq in range(nstrip)], axis=-1)
            acc_ref[a] = jnp.concatenate([acc_new[a, hq] for hq in range(nstrip)], axis=-1)

    @pl.when(jnp.logical_and(jnp.logical_and(pl.program_id(0) == 0, pl.program_id(1) == 0), qi == 0))
    def _():
        for i in range(len(chains)):
            sta_ref[i] = scores(0, i)

    cpt = tq // tk
    assert cpt % 2 == 0
    bufs = (sta_ref, stb_ref)

    def trip(j, carry):
        for cc in range(cpt):
            step(cpt * j + cc, bufs[cc % 2], bufs[(cc + 1) % 2])
        return carry

    lax.fori_loop(0, qi, trip, 0)
    for cc in range(cpt):
        step(cpt * qi + cc, bufs[cc % 2], bufs[(cc + 1) % 2], key_off=cc * tk, last=cc == cpt - 1)

    out_t = jnp.concatenate([acc_ref[a, :MLA_V, :] / acc_ref[a, MLA_V:MLA_V + 1, :] for a in range(MLA_HPS)],
                            axis=0)
    o_ref[...] = out_t.T.astype(BF16)


def _mla(qt, k3, vt):
    b, s, _ = k3.shape
    tq, tk = MLA_TQ, MLA_TK
    hps = MLA_HPS
    st_scratch = pltpu.VMEM((hps * (tq // MLA_QSTRIP), tk, MLA_QSTRIP), F32)
    nh, nq = MLA_HEADS // hps, s // tq

    def successor(bi, h, i):
        wrap_i = i == nq - 1
        wrap_h = jnp.logical_and(wrap_i, h == nh - 1)
        i2 = jnp.where(wrap_i, 0, i + 1)
        h2 = jnp.where(wrap_h, 0, jnp.where(wrap_i, h + 1, h))
        b2 = jnp.minimum(bi + wrap_h.astype(jnp.int32), b - 1)
        return b2, h2, i2
    return pl.pallas_call(
        _mla_kernel,
        grid=(b, MLA_HEADS // hps, s // tq),
        in_specs=[
            pl.BlockSpec((None, hps * LANES, tq), lambda bi, h, i: (bi, h, i)),
            pl.BlockSpec((None, s, hps * LANES), lambda bi, h, i: (bi, 0, h)),
            pl.BlockSpec((None, s // tk, hps * MLA_VROWS, tk), lambda bi, h, i: (bi, 0, h, 0)),
            pl.BlockSpec((None, hps * LANES, tq), lambda bi, h, i: successor(bi, h, i)[:3]),
            pl.BlockSpec((None, tk, hps * LANES), lambda bi, h, i: (successor(bi, h, i)[0], 0, successor(bi, h, i)[1])),
        ],
        out_specs=pl.BlockSpec((None, tq, hps * MLA_V), lambda bi, h, i: (bi, i, h)),
        out_shape=jax.ShapeDtypeStruct((b, s, MLA_HEADS * MLA_V), BF16),
        scratch_shapes=[pltpu.VMEM((hps, 1, tq), F32), pltpu.VMEM((hps, MLA_VROWS, tq), F32),
                        st_scratch, st_scratch],
        compiler_params=pltpu.CompilerParams(dimension_semantics=("arbitrary", "arbitrary", "arbitrary"),
                                             vmem_limit_bytes=VMEM_LIMIT),
        name="mla",
    )(qt, k3, vt, qt, k3)


def _dilattn_kernel(q_ref, kc_ref, vc_ref, kp_ref, vp_ref, o_ref, lse_ref, *, dil, group):
    n = pl.program_id(2)
    nres, tb = q_ref.shape[0], q_ref.shape[1]
    nblk = tb // BLOCK

    qry = lax.broadcasted_iota(jnp.int32, (BLOCK, 2 * BLOCK), 0)
    key = lax.broadcasted_iota(jnp.int32, (BLOCK, 2 * BLOCK), 1)
    dist = qry + BLOCK - key
    in_window = jnp.logical_and(dist >= 0, dist <= BLOCK)
    first_ok = jnp.logical_and(in_window, jnp.logical_or(key >= BLOCK, n > 0))
    distf = (dist * dil).astype(F32)
    lane = lax.broadcasted_iota(jnp.int32, (BLOCK, LANES), 1)
    bias, bias_first = [], []
    for hh in range(DIL_HPG):
        slope = float(2.0 ** (-8.0 * (hh * DIL_GROUPS + group + 1) / DIL_HEADS))
        bias.append(jnp.where(in_window, -(slope * LOG2E) * distf, NEG_INF))
        bias_first.append(jnp.where(first_ok, -(slope * LOG2E) * distf, NEG_INF))

    units = [(r, t, hh) for r in range(nres) for t in range(nblk) for hh in range(DIL_HPG)]

    def window(cur_ref, prev_ref, u):
        r, t, hh = units[u]
        cs = slice(hh * DIL_HEAD_DIM, (hh + 1) * DIL_HEAD_DIM)
        if t == 0:
            return jnp.concatenate([prev_ref[r, :, cs], cur_ref[r, :BLOCK, cs]], axis=0)
        return cur_ref[r, (t - 1) * BLOCK:(t + 1) * BLOCK, cs]

    def scores(u):
        r, t, hh = units[u]
        cs = slice(hh * DIL_HEAD_DIM, (hh + 1) * DIL_HEAD_DIM)
        return _dot_nt(q_ref[r, t * BLOCK:(t + 1) * BLOCK, cs], window(kc_ref, kp_ref, u))

    pending = [scores(u) for u in range(min(DIL_AHEAD, len(units)))]
    lse_tile = None
    for u, (r, t, hh) in enumerate(units):
        if u + DIL_AHEAD < len(units):
            pending.append(scores(u + DIL_AHEAD))
        rs = slice(t * BLOCK, (t + 1) * BLOCK)
        cs = slice(hh * DIL_HEAD_DIM, (hh + 1) * DIL_HEAD_DIM)
        s = pending[u] + (bias_first[hh] if t == 0 else bias[hh])
        m = jnp.max(s, axis=-1, keepdims=True)
        e = jnp.exp2(s - m)
        den = jnp.sum(e, axis=-1, keepdims=True)
        o = _dot(e.astype(BF16), window(vc_ref, vp_ref, u)) / den
        o_ref[r, rs, cs] = o.astype(BF16)
        lse = (m + jnp.log2(den)) * LN2
        lse_tile = jnp.where(lane == hh, lse, jnp.zeros((BLOCK, LANES), F32) if hh == 0 else lse_tile)
        if hh == DIL_HPG - 1:
            lse_ref[r, rs, :] = lse_tile


def _dilattn(qkv, dil, group):
    b, d, l, _ = qkv.shape
    tb = min(DIL_TB, l)
    nres = min(d, DIL_TB // tb)
    bpt = tb // BLOCK
    w = DIL_WIDTH
    cur = lambda c: pl.BlockSpec((None, nres, tb, w), lambda bi, r, n: (bi, r, n, c))
    prev = lambda c: pl.BlockSpec((None, nres, BLOCK, w), lambda bi, r, n: (bi, r, jnp.maximum(n * bpt - 1, 0), c))
    return pl.pallas_call(
        functools.partial(_dilattn_kernel, dil=dil, group=group),
        grid=(b, d // nres, l // tb),
        in_specs=[cur(0), cur(1), cur(2), prev(1), prev(2)],
        out_specs=[pl.BlockSpec((None, nres, tb, w), lambda bi, r, n: (bi, r, n, 0)),
                   pl.BlockSpec((None, nres, tb, LANES), lambda bi, r, n: (bi, r, n, 0))],
        out_shape=[jax.ShapeDtypeStruct((b, d, l, w), BF16), jax.ShapeDtypeStruct((b, d, l, LANES), F32)],
        compiler_params=pltpu.CompilerParams(dimension_semantics=("arbitrary", "arbitrary", "arbitrary"),
                                             vmem_limit_bytes=VMEM_LIMIT),
        name=f"dilattn{dil}",
    )(qkv, qkv, qkv, qkv, qkv)


def _memkv_kernel(mem_ref, g_ref, w_ref, o_ref):
    o_ref[...] = _dot(_rms(mem_ref[...], g_ref[...]).astype(BF16), w_ref[...]).astype(BF16)


def _memkv(mem2, g_mem, w):
    return pl.pallas_call(
        _memkv_kernel,
        out_shape=jax.ShapeDtypeStruct((mem2.shape[0], w.shape[1]), BF16),
        compiler_params=pltpu.CompilerParams(vmem_limit_bytes=VMEM_LIMIT),
        name="memkv",
    )(mem2, g_mem, w)


def _sigmoid(z):
    return 1.0 / (1.0 + jnp.exp(-z))


def _merge_kernel(x_ref, h_ref, wmq_ref, wg_ref, bg_ref, kvm_ref, ymla_ref, wbm_ref,
                  o0_ref, o1_ref, o2_ref, l0_ref, l1_ref, l2_ref, wbd_ref, wbmem_ref, wo_ref, gpm_ref, gpf_ref,
                  x1_ref, h2_ref, nat1_ref, nat2_ref, lse1_ref, lse2_ref):
    tm, dm = x_ref.shape
    h = h_ref[...]

    def gate(br):
        cs = slice(br * dm, (br + 1) * dm)
        return _dot_nt(h, wg_ref[cs, :]) + bg_ref[:, cs]

    for src, lsrc, dst, ldst in ((o1_ref, l1_ref, nat1_ref, lse1_ref), (o2_ref, l2_ref, nat2_ref, lse2_ref)):
        d = src.shape[0]
        rows = src.shape[1]
        for r in range(d):
            for hh in range(DIL_HPG):
                cs = slice(hh * DIL_HEAD_DIM, (hh + 1) * DIL_HEAD_DIM)
                dst[hh, pl.ds(r, rows, stride=d), :] = src[r, :, cs].astype(F32)
            ldst[pl.ds(r, rows, stride=d), :] = lsrc[r]
    lg = (l0_ref[0], lse1_ref[...], lse2_ref[...])
    heads = []
    for hh in range(DIL_HPG):
        cs = slice(hh * DIL_HEAD_DIM, (hh + 1) * DIL_HEAD_DIM)
        og = (o0_ref[0, :, cs].astype(F32), nat1_ref[hh], nat2_ref[hh])
        ls = [l[:, hh:hh + 1] for l in lg]
        mx = jnp.maximum(jnp.maximum(ls[0], ls[1]), ls[2])
        ws = [jnp.exp(l - mx) for l in ls]
        num = ws[0] * og[0] + ws[1] * og[1] + ws[2] * og[2]
        heads.append((num / (ws[0] + ws[1] + ws[2])).astype(BF16))
    y_dil = jnp.concatenate(heads, axis=-1)

    memq = _dot_nt(h, wmq_ref[...])
    mheads = []
    for hh in range(MEM_HEADS):
        cs = slice(hh * MEM_HEAD_DIM, (hh + 1) * MEM_HEAD_DIM)
        q = (memq[:, cs] * MEM_HEAD_DIM ** -0.5).astype(BF16)
        s = _dot_nt(q, kvm_ref[:, cs])
        e = jnp.exp(s - jnp.max(s, axis=-1, keepdims=True))
        o = _dot(e.astype(BF16), kvm_ref[:, MEM_WIDTH + hh * MEM_HEAD_DIM:MEM_WIDTH + (hh + 1) * MEM_HEAD_DIM])
        mheads.append((o / jnp.sum(e, axis=-1, keepdims=True)).astype(BF16))
    y_mem = jnp.concatenate(mheads, axis=-1)

    merged = jnp.zeros((tm, dm), F32)
    for br, (y, w_ref) in enumerate(((ymla_ref[...], wbm_ref), (y_dil, wbd_ref), (y_mem, wbmem_ref))):
        merged = merged + _sigmoid(gate(br)) * _dot(y, w_ref[...])
    merged_b = merged.astype(BF16)
    rows = tm // TAIL_SPLIT
    mixed = [_dot(merged_b[g * rows:(g + 1) * rows, :], wo_ref[...]) for g in range(TAIL_SPLIT)]
    for g in range(TAIL_SPLIT):
        rs = slice(g * rows, (g + 1) * rows)
        x1 = x_ref[rs, :] + _rms(mixed[g], gpm_ref[...])
        x1_ref[rs, :] = x1
        h2_ref[rs, :] = _rms(x1, gpf_ref[...]).astype(BF16)


def _merge(x2, h2, w_al, b_g, kvm, y_mla, w_bm, o_dil, lse_dil, w_bd, w_bmem, w_o, g_pm, g_pf, batch, seq):
    t, dm = x2.shape
    tm = MERGE_TM
    tps = seq // tm
    n_mem = kvm.shape[0] // batch
    row = lambda i: (i, 0)
    single = pl.Buffered(1)
    const = lambda a: pl.BlockSpec(a.shape, lambda i: (0,) * a.ndim, pipeline_mode=single)
    gate_w = N_BRANCH * dm
    assert ALIGNED_MEMQ % MEM_WIDTH == 0 and ALIGNED_GATE % gate_w == 0
    w_mq = pl.BlockSpec((MEM_WIDTH, dm), lambda i: (ALIGNED_MEMQ // MEM_WIDTH, 0), pipeline_mode=single)
    w_g = pl.BlockSpec((gate_w, dm), lambda i: (ALIGNED_GATE // gate_w, 0), pipeline_mode=single)

    def dil_spec(a):
        d, width = a.shape[1], a.shape[3]
        return pl.BlockSpec((None, d, tm // d, width), lambda i: (i // tps, 0, i % tps, 0))

    return pl.pallas_call(
        _merge_kernel,
        grid=(t // tm,),
        in_specs=[
            pl.BlockSpec((tm, dm), row), pl.BlockSpec((tm, dm), row), w_mq, w_g, const(b_g),
            pl.BlockSpec((n_mem, kvm.shape[1]), lambda i: (i // tps, 0)),
            pl.BlockSpec((tm, y_mla.shape[1]), row), const(w_bm),
            dil_spec(o_dil[0]), dil_spec(o_dil[1]), dil_spec(o_dil[2]),
            dil_spec(lse_dil[0]), dil_spec(lse_dil[1]), dil_spec(lse_dil[2]),
            const(w_bd), const(w_bmem), const(w_o), const(g_pm), const(g_pf),
        ],
        out_specs=[pl.BlockSpec((tm, dm), row), pl.BlockSpec((tm, dm), row)],
        out_shape=[jax.ShapeDtypeStruct((t, dm), F32), jax.ShapeDtypeStruct((t, dm), BF16)],
        scratch_shapes=[pltpu.VMEM((DIL_HPG, tm, DIL_HEAD_DIM), F32), pltpu.VMEM((DIL_HPG, tm, DIL_HEAD_DIM), F32),
                        pltpu.VMEM((tm, LANES), F32), pltpu.VMEM((tm, LANES), F32)],
        compiler_params=pltpu.CompilerParams(dimension_semantics=("arbitrary",), vmem_limit_bytes=VMEM_LIMIT),
        name="merge",
    )(x2, h2, w_al, w_al, b_g, kvm, y_mla, w_bm, o_dil[0], o_dil[1], o_dil[2], lse_dil[0], lse_dil[1], lse_dil[2],
      w_bd, w_bmem, w_o, g_pm, g_pf)


def _ffn_kernel(x1_ref, h2_ref, halo_ref, wup_ref, cw_ref, cb_ref, wd_ref, gpost_ref,
                out_ref, hcat_ref, ua_ref, ub_ref, acc_ref, *, tiles_per_seq):
    i = pl.program_id(0)
    tm = x1_ref.shape[0]
    halo, tf = FFN_HALO, FFN_TF
    dff = wd_ref.shape[0]
    nchunk = dff // tf
    lanes_per_chunk = tf // LANES

    first = (i % tiles_per_seq) == 0
    hcat_ref[:halo, :] = jnp.where(first, jnp.zeros_like(halo_ref[...]), halo_ref[...])
    hcat_ref[halo:, :] = h2_ref[...]

    def up(c, u_ref):
        hc = hcat_ref[...]
        for part, off in enumerate((c * tf, dff + c * tf)):
            u = _dot(hc, wup_ref[:, off:off + tf])
            for j in range(lanes_per_chunk):
                u_ref[part * lanes_per_chunk + j] = u[:, j * LANES:(j + 1) * LANES]

    def conv(u_ref, slab, col):
        cols = slice(col, col + LANES)
        z = cb_ref[:, cols]
        for tap in range(CONV_WIDTH):
            back = CONV_WIDTH - 1 - tap
            z = z + cw_ref[tap:tap + 1, cols] * u_ref[slab, halo - back:halo - back + tm, :]
        return z

    bufs = (ua_ref, ub_ref)
    up(0, bufs[0])
    for c in range(nchunk):
        cur = bufs[c % 2]
        if c + 1 < nchunk:
            up(c + 1, bufs[(c + 1) % 2])
        acts = []
        for j in range(lanes_per_chunk):
            gate = conv(cur, j, c * tf + j * LANES)
            val = conv(cur, lanes_per_chunk + j, dff + c * tf + j * LANES)
            acts.append((gate * _sigmoid(gate) * val).astype(BF16))
        down = _dot(jnp.concatenate(acts, axis=-1), wd_ref[c * tf:(c + 1) * tf, :])
        if c == 0:
            acc_ref[...] = down
        else:
            acc_ref[...] += down

    out_ref[...] = x1_ref[...] + _rms(acc_ref[...], gpost_ref[...])


def _ffn(x1, h2, w_up, conv_w, conv_b, w_down, g_post, seq):
    t, dm = x1.shape
    tm, tf, halo = FFN_TM, FFN_TF, FFN_HALO
    tps = seq // tm
    row = lambda i: (i, 0)
    const = lambda a: pl.BlockSpec(a.shape, lambda i: (0,) * a.ndim, pipeline_mode=pl.Buffered(1))
    u_scratch = pltpu.VMEM((2 * tf // LANES, tm + halo, LANES), F32)
    return pl.pallas_call(
        functools.partial(_ffn_kernel, tiles_per_seq=tps),
        grid=(t // tm,),
        in_specs=[
            pl.BlockSpec((tm, dm), row),
            pl.BlockSpec((tm, dm), row),
            pl.BlockSpec((halo, dm), lambda i: (jnp.maximum(i * (tm // halo) - 1, 0), 0)),
            const(w_up), const(conv_w), const(conv_b), const(w_down), const(g_post),
        ],
        out_specs=pl.BlockSpec((tm, dm), row),
        out_shape=jax.ShapeDtypeStruct((t, dm), F32),
        scratch_shapes=[pltpu.VMEM((tm + halo, dm), BF16), u_scratch, u_scratch, pltpu.VMEM((tm, dm), F32)],
        compiler_params=pltpu.CompilerParams(dimension_semantics=("arbitrary",), vmem_limit_bytes=VMEM_LIMIT),
        name="ffn",
    )(x1, h2, h2, w_up, conv_w, conv_b, w_down, g_post)


def _rot_half_cols(w):
    return jnp.concatenate([-w[..., ROPE_HALF:], w[..., :ROPE_HALF]], axis=-1)


def _align_kernel(wt_ref, o_ref):
    cb = wt_ref.shape[1]
    zeros = lambda n: jnp.zeros((n, cb), BF16)
    kr = wt_ref[OFF_KV:OFF_KR, :]
    o_ref[:OFF_KV, :] = wt_ref[:OFF_KV, :].astype(BF16)
    o_ref[OFF_KV:OFF_KV + MLA_NOPE, :] = zeros(MLA_NOPE)
    o_ref[OFF_KV + MLA_NOPE:OFF_KV + MLA_QK_DIM, :] = kr.astype(BF16)
    o_ref[OFF_KV + MLA_QK_DIM:OFF_KV + LANES + MLA_NOPE, :] = zeros(LANES - MLA_QK_DIM + MLA_NOPE)
    o_ref[OFF_KV + LANES + MLA_NOPE:OFF_KV + LANES + MLA_NOPE + ROPE_HALF, :] = (-kr[ROPE_HALF:, :]).astype(BF16)
    o_ref[OFF_KV + LANES + MLA_NOPE + ROPE_HALF:OFF_KV + LANES + MLA_QK_DIM, :] = kr[:ROPE_HALF, :].astype(BF16)
    o_ref[OFF_KV + LANES + MLA_QK_DIM:ALIGNED_DIL, :] = zeros(ALIGNED_DIL - OFF_KV - LANES - MLA_QK_DIM)
    o_ref[ALIGNED_DIL:, :] = wt_ref[OFF_KR:, :].astype(BF16)


def _align_w_in(w_in_all, layer):
    wt_all = jnp.swapaxes(w_in_all, 1, 2)
    _, d_in, dm = wt_all.shape
    cb = ALIGN_CB
    width = ALIGNED_DIL + d_in - OFF_KR
    return pl.pallas_call(
        _align_kernel,
        grid=(dm // cb,),
        in_specs=[pl.BlockSpec((None, d_in, cb), lambda i: (layer, 0, i))],
        out_specs=pl.BlockSpec((width, cb), lambda i: (0, i)),
        out_shape=jax.ShapeDtypeStruct((width, dm), BF16),
        compiler_params=pltpu.CompilerParams(dimension_semantics=("arbitrary",), vmem_limit_bytes=VMEM_LIMIT),
        name="align",
    )(wt_all)


def _prep_weights(w_uq, w_ukv):
    uq = w_uq.reshape(MLA_Q_RANK, MLA_HEADS, MLA_QK_DIM)
    w_qm = w_uq
    w_qs = _rot_half_cols(uq[..., MLA_NOPE:]).reshape(MLA_Q_RANK, MLA_HEADS * MLA_ROPE)

    ukv = w_ukv.reshape(MLA_KV_RANK, MLA_HEADS, MLA_NOPE + MLA_V)
    zk = jnp.zeros((MLA_KV_RANK, MLA_HEADS, LANES - MLA_NOPE), F32)
    w_k = jnp.concatenate([ukv[..., :MLA_NOPE], zk], axis=-1).reshape(MLA_KV_RANK, MLA_HEADS * LANES)
    w_v = ukv[..., MLA_NOPE:].reshape(MLA_KV_RANK, MLA_HEADS * MLA_V)
    return tuple(a.astype(BF16) for a in (w_qm.T, w_qs.T, w_k, w_v.T))


def _layer(layer, w_in_all, x, mem, positions, g_pre_mix, b_gate, mla_q_norm, w_uq, mla_kv_norm, w_ukv, g_mem, w_mem_kv,
           w_br_mla, w_br_dil, w_br_mem, w_o, g_post_mix, g_pre_ffn, w_ffn_up, conv_w, conv_b, w_ffn_down,
           g_post_ffn):
    batch, seq, dm = x.shape
    t = batch * seq
    x2 = x.reshape(t, dm)
    r2 = lambda v: v.reshape(1, -1)

    w_al = _align_w_in(w_in_all, layer)
    w_qmt, w_qst, w_k, w_vt = _prep_weights(w_uq, w_ukv)
    invf = (ROPE_THETA ** (-jnp.arange(ROPE_HALF, dtype=F32) / ROPE_HALF)).reshape(ROPE_HALF, 1)
    pos_rows = positions.reshape(t // PREP_TM, 1, PREP_TM)

    h2d, qt, k, vt = _prep(x2, pos_rows, invf, r2(g_pre_mix), w_al, r2(mla_q_norm), w_qmt, w_qst,
                           r2(mla_kv_norm), w_k, w_vt, batch, seq)
    y_mla = _mla(qt, k.reshape(batch, seq, MLA_HEADS * LANES), vt).reshape(t, MLA_HEADS * MLA_V)

    o_dil, lse_dil = [], []
    for g, (_, dil) in enumerate(DIL_PAIRS):
        qkv = _dilproj(h2d, w_al, batch, seq, dil, g)
        o, lse = _dilattn(qkv, dil, g)
        o_dil.append(o)
        lse_dil.append(lse)

    kvm = _memkv(mem.reshape(-1, dm), r2(g_mem), w_mem_kv.astype(BF16))

    x1, h2 = _merge(x2, h2d, w_al, r2(b_gate), kvm, y_mla, w_br_mla.astype(BF16), o_dil, lse_dil,
                    w_br_dil.astype(BF16),
                    w_br_mem.astype(BF16), w_o.astype(BF16), r2(g_post_mix), r2(g_pre_ffn), batch, seq)

    out = _ffn(x1, h2, w_ffn_up.astype(BF16), conv_w, r2(conv_b), w_ffn_down.astype(BF16), r2(g_post_ffn), seq)
    return out.reshape(batch, seq, dm)


def kernel(x, mem, positions, g_pre_mix, w_in, b_gate, mla_q_norm, w_uq, mla_kv_norm, w_ukv, g_mem, w_mem_kv,
           w_br_mla, w_br_dil, w_br_mem, w_o, g_post_mix, g_pre_ffn, w_ffn_up, conv_w, conv_b, w_ffn_down,
           g_post_ffn):
    for l in range(w_in.shape[0]):
        x = _layer(l, w_in, x, mem, positions, g_pre_mix[l], b_gate[l], mla_q_norm[l], w_uq[l], mla_kv_norm[l],
                   w_ukv[l], g_mem[l], w_mem_kv[l], w_br_mla[l], w_br_dil[l], w_br_mem[l], w_o[l], g_post_mix[l],
                   g_pre_ffn[l], w_ffn_up[l], conv_w[l], conv_b[l], w_ffn_down[l], g_post_ffn[l])
    return x
```

```python
import functools

import jax
import jax.numpy as jnp
from jax import lax
from jax.experimental import pallas as pl
from jax.experimental.pallas import tpu as pltpu

F32 = jnp.float32
BF16 = jnp.bfloat16

RMS_EPS = 1e-6
LOG2E = 1.4426950408889634
LN2 = 0.6931471805599453
NEG_INF = -1e30
LANES = 128
BF16_SUBLANES = 16
V7X_VMEM_BYTES = 64 * 1024 * 1024

BLOCK = 128
MLA_HEADS = 8
MLA_NOPE = 64
MLA_ROPE = 32
MLA_V = 64
MLA_QK_DIM = MLA_NOPE + MLA_ROPE
MLA_Q_RANK = 384
MLA_KV_RANK = 256
ROPE_THETA = 10000.0
ROPE_HALF = MLA_ROPE // 2

DIL_PAIRS = ((128, 1), (512, 4), (2048, 16))
DIL_GROUPS = 3
DIL_HPG = 4
DIL_HEADS = DIL_GROUPS * DIL_HPG
DIL_HEAD_DIM = 128
DIL_WIDTH = DIL_HPG * DIL_HEAD_DIM
DIL_QSCALE = DIL_HEAD_DIM ** -0.5 * LOG2E

MEM_HEADS = 4
MEM_HEAD_DIM = 128
MEM_WIDTH = MEM_HEADS * MEM_HEAD_DIM

N_BRANCH = 3
CONV_WIDTH = 3

OFF_Q = MLA_Q_RANK
OFF_KV = OFF_Q + MLA_KV_RANK
OFF_KR = OFF_KV + MLA_ROPE
OFF_DIL = OFF_KR + 3 * DIL_HEADS * DIL_HEAD_DIM
OFF_MEMQ = OFF_DIL + MEM_WIDTH
ALIGNED_DIL = 1024
ALIGNED_MEMQ = ALIGNED_DIL + (OFF_DIL - OFF_KR)
ALIGNED_GATE = ALIGNED_MEMQ + MEM_WIDTH

PREP_TM = 512
DILPROJ_TM = 1024
DILPROJ_MAX_STRIDE = 4
ALIGN_CB = 256
MLA_TQ = 2048
MLA_TK = 512
MLA_HPS = 2
MLA_QSTRIP = 256
MLA_VROWS = MLA_V + BF16_SUBLANES
assert PREP_TM == MLA_TK
DIL_TB = 2048
DIL_AHEAD = 4
MERGE_TM = 512
FFN_TM = 512
FFN_TF = 256
FFN_HALO = BF16_SUBLANES

TAIL_SPLIT = 4

VMEM_LIMIT = V7X_VMEM_BYTES * 7 // 8


def _rms(xf, g):
    return xf * lax.rsqrt(jnp.mean(xf * xf, axis=-1, keepdims=True) + RMS_EPS) * g


def _dot(a, b):
    return jnp.dot(a, b, preferred_element_type=F32)


def _dot_nt(a, b):
    return lax.dot_general(a, b, (((1,), (1,)), ((), ())), preferred_element_type=F32)


def _const_spec(shape):
    nd = len(shape)
    return pl.BlockSpec(shape, lambda *_: (0,) * nd)


def _prep_kernel(x_ref, pos_ref, invf_ref, g_ref, wa_ref, qn_ref, wqmt_ref, wqst_ref, kvn_ref, wk_ref, wvt_ref,
                 h_ref, qt_ref, k_ref, vt_ref):
    tm = x_ref.shape[0]
    h = _rms(x_ref[...], g_ref[...]).astype(BF16)
    h_ref[...] = h
    p = _dot_nt(h, wa_ref[...])
    cq = _rms(p[:, :OFF_Q], qn_ref[...]).astype(BF16)
    ckv = _rms(p[:, OFF_Q:OFF_KV], kvn_ref[...]).astype(BF16)

    ang = invf_ref[...] * pos_ref[0].astype(F32)
    c16 = jnp.cos(ang)
    s16 = jnp.sin(ang)
    ones = jnp.ones((MLA_NOPE, tm), F32)
    zeros_lo = jnp.zeros((MLA_NOPE, tm), F32)
    zeros_hi = jnp.zeros((LANES - MLA_QK_DIM, tm), F32)
    cos_c = jnp.concatenate([ones, c16, c16, zeros_hi], axis=0)
    sin_c = jnp.concatenate([zeros_lo, s16, s16, zeros_hi], axis=0)

    qmt = _dot_nt(wqmt_ref[...], cq)
    qst = _dot_nt(wqst_ref[...], cq)
    qscale = MLA_QK_DIM ** -0.5 * LOG2E
    cos_rope = jnp.concatenate([c16, c16], axis=0)
    sin_rope = jnp.concatenate([s16, s16], axis=0)
    for hd in range(MLA_HEADS):
        base = hd * LANES
        nope = qmt[hd * MLA_QK_DIM:hd * MLA_QK_DIM + MLA_NOPE, :]
        rope = qmt[hd * MLA_QK_DIM + MLA_NOPE:(hd + 1) * MLA_QK_DIM, :]
        rot = qst[hd * MLA_ROPE:(hd + 1) * MLA_ROPE, :]
        qt_ref[base:base + MLA_NOPE, :] = (nope * qscale).astype(BF16)
        qt_ref[base + MLA_NOPE:base + MLA_QK_DIM, :] = ((rope * cos_rope + rot * sin_rope) * qscale).astype(BF16)
        qt_ref[base + MLA_QK_DIM:base + LANES, :] = jnp.zeros((LANES - MLA_QK_DIM, tm), BF16)

    cos_r = cos_c.T
    sin_r = sin_c.T
    kn = _dot(ckv, wk_ref[...])
    kpe = p[:, OFF_KV:OFF_KV + LANES] * cos_r + p[:, OFF_KV + LANES:OFF_KV + 2 * LANES] * sin_r
    for hd in range(MLA_HEADS):
        sl = slice(hd * LANES, (hd + 1) * LANES)
        k_ref[:, sl] = (kn[:, sl] + kpe).astype(BF16)
    vt = _dot_nt(wvt_ref[...], ckv).astype(BF16)
    ones_rows = jnp.ones((MLA_VROWS - MLA_V, tm), BF16)
    for hd in range(MLA_HEADS):
        vt_ref[0, hd * MLA_VROWS:hd * MLA_VROWS + MLA_V, :] = vt[hd * MLA_V:(hd + 1) * MLA_V, :]
        vt_ref[0, hd * MLA_VROWS + MLA_V:(hd + 1) * MLA_VROWS, :] = ones_rows


def _prep(x2, pos_rows, invf, g_pre, w_al, q_norm, w_qmt, w_qst, kv_norm, w_k, w_vt, batch, seq):
    t, d = x2.shape
    tm = PREP_TM
    tps = seq // tm
    hw = MLA_HEADS * LANES
    vw = MLA_HEADS * MLA_VROWS
    row = lambda i: (i, 0)
    return pl.pallas_call(
        _prep_kernel,
        grid=(t // tm,),
        in_specs=[
            pl.BlockSpec((tm, d), row),
            pl.BlockSpec((1, 1, tm), lambda i: (i, 0, 0)),
            _const_spec(invf.shape),
            _const_spec(g_pre.shape),
            pl.BlockSpec((OFF_KV + 2 * LANES, d), lambda i: (0, 0)),
            _const_spec(q_norm.shape),
            _const_spec(w_qmt.shape),
            _const_spec(w_qst.shape),
            _const_spec(kv_norm.shape),
            _const_spec(w_k.shape),
            _const_spec(w_vt.shape),
        ],
        out_specs=[pl.BlockSpec((tm, d), row),
                   pl.BlockSpec((None, hw, tm), lambda i: (i // tps, 0, i % tps)),
                   pl.BlockSpec((tm, hw), row),
                   pl.BlockSpec((None, 1, vw, tm), lambda i: (i // tps, i % tps, 0, 0))],
        out_shape=[jax.ShapeDtypeStruct((t, d), BF16),
                   jax.ShapeDtypeStruct((batch, hw, seq), BF16),
                   jax.ShapeDtypeStruct((t, hw), BF16),
                   jax.ShapeDtypeStruct((batch, tps, vw, tm), BF16)],
        compiler_params=pltpu.CompilerParams(dimension_semantics=("arbitrary",), vmem_limit_bytes=VMEM_LIMIT),
        name="prep",
    )(x2, pos_rows, invf, g_pre, w_al, q_norm, w_qmt, w_qst, kv_norm, w_k, w_vt)


def _dilproj_kernel(h_ref, wq_ref, wk_ref, wv_ref, o_ref, *acc_refs, dil):
    h = h_ref[...]
    w_refs = (wq_ref, wk_ref, wv_ref)
    tm, tn = h_ref.shape[0], wq_ref.shape[0]

    def projection(j):
        acc = _dot_nt(h, w_refs[j][...])
        return acc * DIL_QSCALE if j == 0 else acc

    if dil == 1:
        for j in range(len(w_refs)):
            o_ref[0, :, j * tn:(j + 1) * tn] = projection(j).astype(BF16)
        return
    chunks = tn // LANES
    rows = tm // dil

    def project(j):
        acc = projection(j)
        for c in range(chunks):
            acc_refs[j][c] = acc[:, c * LANES:(c + 1) * LANES]

    def scatter(j):
        src = acc_refs[j]
        if dil > DILPROJ_MAX_STRIDE:
            s1 = DILPROJ_MAX_STRIDE
            s2 = dil // s1
            tmp = acc_refs[-1]
            for r1 in range(s1):
                for c in range(chunks):
                    tmp[c, r1 * (tm // s1):(r1 + 1) * (tm // s1), :] = src[c, pl.ds(r1, tm // s1, stride=s1), :]
            for r in range(dil):
                r1, r2 = r % s1, r // s1
                for c in range(chunks):
                    cols = slice(j * tn + c * LANES, j * tn + (c + 1) * LANES)
                    o_ref[r, :, cols] = tmp[c, pl.ds(r1 * (tm // s1) + r2, rows, stride=s2), :].astype(BF16)
            return
        for r in range(dil):
            for c in range(chunks):
                cols = slice(j * tn + c * LANES, j * tn + (c + 1) * LANES)
                o_ref[r, :, cols] = src[c, pl.ds(r, rows, stride=dil), :].astype(BF16)

    project(0)
    project(1)
    scatter(0)
    project(2)
    scatter(1)
    scatter(2)


def _dilproj(h2, w_al, batch, seq, dil, group):
    t, d = h2.shape
    n = 3 * DIL_WIDTH
    tm, tn = DILPROJ_TM, DIL_WIDTH
    assert ALIGNED_DIL % tn == 0
    tiles_per_seq = seq // tm
    col0 = ALIGNED_DIL // tn + group
    w_spec = lambda j: pl.BlockSpec((tn, d), lambda i: (col0 + j * DIL_GROUPS, 0))
    n_scratch = 0 if dil == 1 else (3 if dil <= DILPROJ_MAX_STRIDE else 4)
    scratch = [pltpu.VMEM((tn // LANES, tm, LANES), F32)] * n_scratch
    return pl.pallas_call(
        functools.partial(_dilproj_kernel, dil=dil),
        grid=(t // tm,),
        in_specs=[pl.BlockSpec((tm, d), lambda i: (i, 0)), w_spec(0), w_spec(1), w_spec(2)],
        out_specs=pl.BlockSpec((None, dil, tm // dil, n),
                               lambda i: (i // tiles_per_seq, 0, i % tiles_per_seq, 0)),
        out_shape=jax.ShapeDtypeStruct((batch, dil, seq // dil, n), BF16),
        scratch_shapes=scratch,
        compiler_params=pltpu.CompilerParams(dimension_semantics=("arbitrary",), vmem_limit_bytes=VMEM_LIMIT),
        name=f"dilproj{dil}",
    )(h2, w_al, w_al, w_al)


def _mla_kernel(qt_ref, k_ref, vt_ref, qtn_ref, k0n_ref, o_ref, m_ref, acc_ref, sta_ref, stb_ref):
    tq = qt_ref.shape[1]
    tk = MLA_TK
    qi = pl.program_id(2)
    nstrip = tq // MLA_QSTRIP
    m_ref[...] = jnp.full(m_ref.shape, NEG_INF, F32)
    acc_ref[...] = jnp.zeros(acc_ref.shape, F32)
    chains = [(a, hq) for a in range(MLA_HPS) for hq in range(nstrip)]

    def scores(c, i, nk=tk):
        a, hq = chains[i]
        k = k_ref[pl.ds(pl.multiple_of(c * tk, tk), nk), a * LANES:(a + 1) * LANES]
        return _dot(k, qt_ref[a * LANES:(a + 1) * LANES, hq * MLA_QSTRIP:(hq + 1) * MLA_QSTRIP])

    def successor_scores(i):
        a, hq = chains[i]
        return _dot(k0n_ref[:, a * LANES:(a + 1) * LANES],
                    qtn_ref[a * LANES:(a + 1) * LANES, hq * MLA_QSTRIP:(hq + 1) * MLA_QSTRIP])

    def step(c, src_ref, dst_ref, key_off=None, last=False):
        def visible(hq, off):
            return tk if off is None else max(0, min(tk, (hq + 1) * MLA_QSTRIP - off))

        next_off = None if key_off is None else key_off + tk
        m_all = m_ref[...]
        acc_all = acc_ref[...]
        m_new, acc_new = {}, {}
        for i, (a, hq) in enumerate(chains):
            qs = slice(hq * MLA_QSTRIP, (hq + 1) * MLA_QSTRIP)
            if last:
                dst_ref[i] = successor_scores(i)
            elif visible(hq, next_off) > 0:
                nk_next = visible(hq, next_off)
                dst_ref[i, :nk_next, :] = scores(c + 1, i, nk_next)
            nk = visible(hq, key_off)
            if nk == 0:
                m_new[a, hq], acc_new[a, hq] = m_all[a, :, qs], acc_all[a, :, qs]
                continue
            st = src_ref[i, :nk, :]
            if key_off is not None and key_off + nk - 1 > hq * MLA_QSTRIP:
                key = lax.broadcasted_iota(jnp.int32, st.shape, 0) + key_off
                qry = lax.broadcasted_iota(jnp.int32, st.shape, 1) + hq * MLA_QSTRIP
                st = jnp.where(key <= qry, st, NEG_INF)
            m_prev = m_all[a, :, qs]
            m_cur = jnp.maximum(m_prev, jnp.max(st, axis=0, keepdims=True))
            alpha = jnp.exp2(m_prev - m_cur)
            p = jnp.exp2(st - m_cur).astype(BF16)
            vt = vt_ref[c, a * MLA_VROWS:(a + 1) * MLA_VROWS, :nk]
            acc_new[a, hq] = alpha * acc_all[a, :, qs] + _dot(vt, p)
            m_new[a, hq] = m_cur
        for a in range(MLA_HPS):
            m_ref[a] = jnp.concatenate([m_new[a, hq] for hq in range(nstrip)], axis=-1)
            acc_ref[a] = jnp.concatenate([acc_new[a, hq] for hq in range(nstrip)], axis=-1)

    @pl.when(jnp.logical_and(jnp.logical_and(pl.program_id(0) == 0, pl.program_id(1) == 0), qi == 0))
    def _():
        for i in range(len(chains)):
            sta_ref[i] = scores(0, i)

    cpt = tq // tk
    assert cpt % 2 == 0
    bufs = (sta_ref, stb_ref)

    def trip(j, carry):
        for cc in range(cpt):
            step(cpt * j + cc, bufs[cc % 2], bufs[(cc + 1) % 2])
        return carry

    lax.fori_loop(0, qi, trip, 0)
    for cc in range(cpt):
        step(cpt * qi + cc, bufs[cc % 2], bufs[(cc + 1) % 2], key_off=cc * tk, last=cc == cpt - 1)

    out_t = jnp.concatenate([acc_ref[a, :MLA_V, :] / acc_ref[a, MLA_V:MLA_V + 1, :] for a in range(MLA_HPS)],
                            axis=0)
    o_ref[...] = out_t.T.astype(BF16)


def _mla(qt, k3, vt):
    b, s, _ = k3.shape
    tq, tk = MLA_TQ, MLA_TK
    hps = MLA_HPS
    st_scratch = pltpu.VMEM((hps * (tq // MLA_QSTRIP), tk, MLA_QSTRIP), F32)
    nh, nq = MLA_HEADS // hps, s // tq

    def successor(bi, h, i):
        wrap_i = i == nq - 1
        wrap_h = jnp.logical_and(wrap_i, h == nh - 1)
        i2 = jnp.where(wrap_i, 0, i + 1)
        h2 = jnp.where(wrap_h, 0, jnp.where(wrap_i, h + 1, h))
        b2 = jnp.minimum(bi + wrap_h.astype(jnp.int32), b - 1)
        return b2, h2, i2
    return pl.pallas_call(
        _mla_kernel,
        grid=(b, MLA_HEADS // hps, s // tq),
        in_specs=[
            pl.BlockSpec((None, hps * LANES, tq), lambda bi, h, i: (bi, h, i)),
            pl.BlockSpec((None, s, hps * LANES), lambda bi, h, i: (bi, 0, h)),
            pl.BlockSpec((None, s // tk, hps * MLA_VROWS, tk), lambda bi, h, i: (bi, 0, h, 0)),
            pl.BlockSpec((None, hps * LANES, tq), lambda bi, h, i: successor(bi, h, i)[:3]),
            pl.BlockSpec((None, tk, hps * LANES), lambda bi, h, i: (successor(bi, h, i)[0], 0, successor(bi, h, i)[1])),
        ],
        out_specs=pl.BlockSpec((None, tq, hps * MLA_V), lambda bi, h, i: (bi, i, h)),
        out_shape=jax.ShapeDtypeStruct((b, s, MLA_HEADS * MLA_V), BF16),
        scratch_shapes=[pltpu.VMEM((hps, 1, tq), F32), pltpu.VMEM((hps, MLA_VROWS, tq), F32),
                        st_scratch, st_scratch],
        compiler_params=pltpu.CompilerParams(dimension_semantics=("arbitrary", "arbitrary", "arbitrary"),
                                             vmem_limit_bytes=VMEM_LIMIT),
        name="mla",
    )(qt, k3, vt, qt, k3)


def _dilattn_kernel(q_ref, kc_ref, vc_ref, kp_ref, vp_ref, o_ref, lse_ref, *, dil, group):
    n = pl.program_id(2)
    nres, tb = q_ref.shape[0], q_ref.shape[1]
    nblk = tb // BLOCK

    qry = lax.broadcasted_iota(jnp.int32, (BLOCK, 2 * BLOCK), 0)
    key = lax.broadcasted_iota(jnp.int32, (BLOCK, 2 * BLOCK), 1)
    dist = qry + BLOCK - key
    in_window = jnp.logical_and(dist >= 0, dist <= BLOCK)
    first_ok = jnp.logical_and(in_window, jnp.logical_or(key >= BLOCK, n > 0))
    distf = (dist * dil).astype(F32)
    lane = lax.broadcasted_iota(jnp.int32, (BLOCK, LANES), 1)
    bias, bias_first = [], []
    for hh in range(DIL_HPG):
        slope = float(2.0 ** (-8.0 * (hh * DIL_GROUPS + group + 1) / DIL_HEADS))
        bias.append(jnp.where(in_window, -(slope * LOG2E) * distf, NEG_INF))
        bias_first.append(jnp.where(first_ok, -(slope * LOG2E) * distf, NEG_INF))

    units = [(r, t, hh) for r in range(nres) for t in range(nblk) for hh in range(DIL_HPG)]

    def window(cur_ref, prev_ref, u):
        r, t, hh = units[u]
        cs = slice(hh * DIL_HEAD_DIM, (hh + 1) * DIL_HEAD_DIM)
        if t == 0:
            return jnp.concatenate([prev_ref[r, :, cs], cur_ref[r, :BLOCK, cs]], axis=0)
        return cur_ref[r, (t - 1) * BLOCK:(t + 1) * BLOCK, cs]

    def scores(u):
        r, t, hh = units[u]
        cs = slice(hh * DIL_HEAD_DIM, (hh + 1) * DIL_HEAD_DIM)
        return _dot_nt(q_ref[r, t * BLOCK:(t + 1) * BLOCK, cs], window(kc_ref, kp_ref, u))

    pending = [scores(u) for u in range(min(DIL_AHEAD, len(units)))]
    lse_tile = None
    for u, (r, t, hh) in enumerate(units):
        if u + DIL_AHEAD < len(units):
            pending.append(scores(u + DIL_AHEAD))
        rs = slice(t * BLOCK, (t + 1) * BLOCK)
        cs = slice(hh * DIL_HEAD_DIM, (hh + 1) * DIL_HEAD_DIM)
        s = pending[u] + (bias_first[hh] if t == 0 else bias[hh])
        m = jnp.max(s, axis=-1, keepdims=True)
        e = jnp.exp2(s - m)
        den = jnp.sum(e, axis=-1, keepdims=True)
        o = _dot(e.astype(BF16), window(vc_ref, vp_ref, u)) / den
        o_ref[r, rs, cs] = o.astype(BF16)
        lse = (m + jnp.log2(den)) * LN2
        lse_tile = jnp.where(lane == hh, lse, jnp.zeros((BLOCK, LANES), F32) if hh == 0 else lse_tile)
        if hh == DIL_HPG - 1:
            lse_ref[r, rs, :] = lse_tile


def _dilattn(qkv, dil, group):
    b, d, l, _ = qkv.shape
    tb = min(DIL_TB, l)
    nres = min(d, DIL_TB // tb)
    bpt = tb // BLOCK
    w = DIL_WIDTH
    cur = lambda c: pl.BlockSpec((None, nres, tb, w), lambda bi, r, n: (bi, r, n, c))
    prev = lambda c: pl.BlockSpec((None, nres, BLOCK, w), lambda bi, r, n: (bi, r, jnp.maximum(n * bpt - 1, 0), c))
    return pl.pallas_call(
        functools.partial(_dilattn_kernel, dil=dil, group=group),
        grid=(b, d // nres, l // tb),
        in_specs=[cur(0), cur(1), cur(2), prev(1), prev(2)],
        out_specs=[pl.BlockSpec((None, nres, tb, w), lambda bi, r, n: (bi, r, n, 0)),
                   pl.BlockSpec((None, nres, tb, LANES), lambda bi, r, n: (bi, r, n, 0))],
        out_shape=[jax.ShapeDtypeStruct((b, d, l, w), BF16), jax.ShapeDtypeStruct((b, d, l, LANES), F32)],
        compiler_params=pltpu.CompilerParams(dimension_semantics=("arbitrary", "arbitrary", "arbitrary"),
                                             vmem_limit_bytes=VMEM_LIMIT),
        name=f"dilattn{dil}",
    )(qkv, qkv, qkv, qkv, qkv)


def _memkv_kernel(mem_ref, g_ref, w_ref, o_ref):
    o_ref[...] = _dot(_rms(mem_ref[...], g_ref[...]).astype(BF16), w_ref[...]).astype(BF16)


def _memkv(mem2, g_mem, w):
    return pl.pallas_call(
        _memkv_kernel,
        out_shape=jax.ShapeDtypeStruct((mem2.shape[0], w.shape[1]), BF16),
        compiler_params=pltpu.CompilerParams(vmem_limit_bytes=VMEM_LIMIT),
        name="memkv",
    )(mem2, g_mem, w)


def _sigmoid(z):
    return 1.0 / (1.0 + jnp.exp(-z))


def _merge_kernel(x_ref, h_ref, wmq_ref, wg_ref, bg_ref, kvm_ref, ymla_ref, wbm_ref,
                  o0_ref, o1_ref, o2_ref, l0_ref, l1_ref, l2_ref, wbd_ref, wbmem_ref, wo_ref, gpm_ref, gpf_ref,
                  x1_ref, h2_ref, nat1_ref, nat2_ref, lse1_ref, lse2_ref):
    tm, dm = x_ref.shape
    h = h_ref[...]

    def gate(br):
        cs = slice(br * dm, (br + 1) * dm)
        return _dot_nt(h, wg_ref[cs, :]) + bg_ref[:, cs]

    for src, lsrc, dst, ldst in ((o1_ref, l1_ref, nat1_ref, lse1_ref), (o2_ref, l2_ref, nat2_ref, lse2_ref)):
        d = src.shape[0]
        rows = src.shape[1]
        for r in range(d):
            for hh in range(DIL_HPG):
                cs = slice(hh * DIL_HEAD_DIM, (hh + 1) * DIL_HEAD_DIM)
                dst[hh, pl.ds(r, rows, stride=d), :] = src[r, :, cs].astype(F32)
            ldst[pl.ds(r, rows, stride=d), :] = lsrc[r]
    lg = (l0_ref[0], lse1_ref[...], lse2_ref[...])
    heads = []
    for hh in range(DIL_HPG):
        cs = slice(hh * DIL_HEAD_DIM, (hh + 1) * DIL_HEAD_DIM)
        og = (o0_ref[0, :, cs].astype(F32), nat1_ref[hh], nat2_ref[hh])
        ls = [l[:, hh:hh + 1] for l in lg]
        mx = jnp.maximum(jnp.maximum(ls[0], ls[1]), ls[2])
        ws = [jnp.exp(l - mx) for l in ls]
        num = ws[0] * og[0] + ws[1] * og[1] + ws[2] * og[2]
        heads.append((num / (ws[0] + ws[1] + ws[2])).astype(BF16))
    y_dil = jnp.concatenate(heads, axis=-1)

    memq = _dot_nt(h, wmq_ref[...])
    mheads = []
    for hh in range(MEM_HEADS):
        cs = slice(hh * MEM_HEAD_DIM, (hh + 1) * MEM_HEAD_DIM)
        q = (memq[:, cs] * MEM_HEAD_DIM ** -0.5).astype(BF16)
        s = _dot_nt(q, kvm_ref[:, cs])
        e = jnp.exp(s - jnp.max(s, axis=-1, keepdims=True))
        o = _dot(e.astype(BF16), kvm_ref[:, MEM_WIDTH + hh * MEM_HEAD_DIM:MEM_WIDTH + (hh + 1) * MEM_HEAD_DIM])
        mheads.append((o / jnp.sum(e, axis=-1, keepdims=True)).astype(BF16))
    y_mem = jnp.concatenate(mheads, axis=-1)

    merged = jnp.zeros((tm, dm), F32)
    for br, (y, w_ref) in enumerate(((ymla_ref[...], wbm_ref), (y_dil, wbd_ref), (y_mem, wbmem_ref))):
        merged = merged + _sigmoid(gate(br)) * _dot(y, w_ref[...])
    merged_b = merged.astype(BF16)
    rows = tm // TAIL_SPLIT
    mixed = [_dot(merged_b[g * rows:(g + 1) * rows, :], wo_ref[...]) for g in range(TAIL_SPLIT)]
    for g in range(TAIL_SPLIT):
        rs = slice(g * rows, (g + 1) * rows)
        x1 = x_ref[rs, :] + _rms(mixed[g], gpm_ref[...])
        x1_ref[rs, :] = x1
        h2_ref[rs, :] = _rms(x1, gpf_ref[...]).astype(BF16)


def _merge(x2, h2, w_al, b_g, kvm, y_mla, w_bm, o_dil, lse_dil, w_bd, w_bmem, w_o, g_pm, g_pf, batch, seq):
    t, dm = x2.shape
    tm = MERGE_TM
    tps = seq // tm
    n_mem = kvm.shape[0] // batch
    row = lambda i: (i, 0)
    single = pl.Buffered(1)
    const = lambda a: pl.BlockSpec(a.shape, lambda i: (0,) * a.ndim, pipeline_mode=single)
    gate_w = N_BRANCH * dm
    assert ALIGNED_MEMQ % MEM_WIDTH == 0 and ALIGNED_GATE % gate_w == 0
    w_mq = pl.BlockSpec((MEM_WIDTH, dm), lambda i: (ALIGNED_MEMQ // MEM_WIDTH, 0), pipeline_mode=single)
    w_g = pl.BlockSpec((gate_w, dm), lambda i: (ALIGNED_GATE // gate_w, 0), pipeline_mode=single)

    def dil_spec(a):
        d, width = a.shape[1], a.shape[3]
        return pl.BlockSpec((None, d, tm // d, width), lambda i: (i // tps, 0, i % tps, 0))

    return pl.pallas_call(
        _merge_kernel,
        grid=(t // tm,),
        in_specs=[
            pl.BlockSpec((tm, dm), row), pl.BlockSpec((tm, dm), row), w_mq, w_g, const(b_g),
            pl.BlockSpec((n_mem, kvm.shape[1]), lambda i: (i // tps, 0)),
            pl.BlockSpec((tm, y_mla.shape[1]), row), const(w_bm),
            dil_spec(o_dil[0]), dil_spec(o_dil[1]), dil_spec(o_dil[2]),
            dil_spec(lse_dil[0]), dil_spec(lse_dil[1]), dil_spec(lse_dil[2]),
            const(w_bd), const(w_bmem), const(w_o), const(g_pm), const(g_pf),
        ],
        out_specs=[pl.BlockSpec((tm, dm), row), pl.BlockSpec((tm, dm), row)],
        out_shape=[jax.ShapeDtypeStruct((t, dm), F32), jax.ShapeDtypeStruct((t, dm), BF16)],
        scratch_shapes=[pltpu.VMEM((DIL_HPG, tm, DIL_HEAD_DIM), F32), pltpu.VMEM((DIL_HPG, tm, DIL_HEAD_DIM), F32),
                        pltpu.VMEM((tm, LANES), F32), pltpu.VMEM((tm, LANES), F32)],
        compiler_params=pltpu.CompilerParams(dimension_semantics=("arbitrary",), vmem_limit_bytes=VMEM_LIMIT),
        name="merge",
    )(x2, h2, w_al, w_al, b_g, kvm, y_mla, w_bm, o_dil[0], o_dil[1], o_dil[2], lse_dil[0], lse_dil[1], lse_dil[2],
      w_bd, w_bmem, w_o, g_pm, g_pf)


def _ffn_kernel(x1_ref, h2_ref, halo_ref, wup_ref, cw_ref, cb_ref, wd_ref, gpost_ref,
                out_ref, hcat_ref, ua_ref, ub_ref, acc_ref, *, tiles_per_seq):
    i = pl.program_id(0)
    tm = x1_ref.shape[0]
    halo, tf = FFN_HALO, FFN_TF
    dff = wd_ref.shape[0]
    nchunk = dff // tf
    lanes_per_chunk = tf // LANES

    first = (i % tiles_per_seq) == 0
    hcat_ref[:halo, :] = jnp.where(first, jnp.zeros_like(halo_ref[...]), halo_ref[...])
    hcat_ref[halo:, :] = h2_ref[...]

    def up(c, u_ref):
        hc = hcat_ref[...]
        for part, off in enumerate((c * tf, dff + c * tf)):
            u = _dot(hc, wup_ref[:, off:off + tf])
            for j in range(lanes_per_chunk):
                u_ref[part * lanes_per_chunk + j] = u[:, j * LANES:(j + 1) * LANES]

    def conv(u_ref, slab, col):
        cols = slice(col, col + LANES)
        z = cb_ref[:, cols]
        for tap in range(CONV_WIDTH):
            back = CONV_WIDTH - 1 - tap
            z = z + cw_ref[tap:tap + 1, cols] * u_ref[slab, halo - back:halo - back + tm, :]
        return z

    bufs = (ua_ref, ub_ref)
    up(0, bufs[0])
    for c in range(nchunk):
        cur = bufs[c % 2]
        if c + 1 < nchunk:
            up(c + 1, bufs[(c + 1) % 2])
        acts = []
        for j in range(lanes_per_chunk):
            gate = conv(cur, j, c * tf + j * LANES)
            val = conv(cur, lanes_per_chunk + j, dff + c * tf + j * LANES)
            acts.append((gate * _sigmoid(gate) * val).astype(BF16))
        down = _dot(jnp.concatenate(acts, axis=-1), wd_ref[c * tf:(c + 1) * tf, :])
        if c == 0:
            acc_ref[...] = down
        else:
            acc_ref[...] += down

    out_ref[...] = x1_ref[...] + _rms(acc_ref[...], gpost_ref[...])


def _ffn(x1, h2, w_up, conv_w, conv_b, w_down, g_post, seq):
    t, dm = x1.shape
    tm, tf, halo = FFN_TM, FFN_TF, FFN_HALO
    tps = seq // tm
    row = lambda i: (i, 0)
    const = lambda a: pl.BlockSpec(a.shape, lambda i: (0,) * a.ndim, pipeline_mode=pl.Buffered(1))
    u_scratch = pltpu.VMEM((2 * tf // LANES, tm + halo, LANES), F32)
    return pl.pallas_call(
        functools.partial(_ffn_kernel, tiles_per_seq=tps),
        grid=(t // tm,),
        in_specs=[
            pl.BlockSpec((tm, dm), row),
            pl.BlockSpec((tm, dm), row),
            pl.BlockSpec((halo, dm), lambda i: (jnp.maximum(i * (tm // halo) - 1, 0), 0)),
            const(w_up), const(conv_w), const(conv_b), const(w_down), const(g_post),
        ],
        out_specs=pl.BlockSpec((tm, dm), row),
        out_shape=jax.ShapeDtypeStruct((t, dm), F32),
        scratch_shapes=[pltpu.VMEM((tm + halo, dm), BF16), u_scratch, u_scratch, pltpu.VMEM((tm, dm), F32)],
        compiler_params=pltpu.CompilerParams(dimension_semantics=("arbitrary",), vmem_limit_bytes=VMEM_LIMIT),
        name="ffn",
    )(x1, h2, h2, w_up, conv_w, conv_b, w_down, g_post)


def _rot_half_cols(w):
    return jnp.concatenate([-w[..., ROPE_HALF:], w[..., :ROPE_HALF]], axis=-1)


def _align_kernel(wt_ref, o_ref):
    cb = wt_ref.shape[1]
    zeros = lambda n: jnp.zeros((n, cb), BF16)
    kr = wt_ref[OFF_KV:OFF_KR, :]
    o_ref[:OFF_KV, :] = wt_ref[:OFF_KV, :].astype(BF16)
    o_ref[OFF_KV:OFF_KV + MLA_NOPE, :] = zeros(MLA_NOPE)
    o_ref[OFF_KV + MLA_NOPE:OFF_KV + MLA_QK_DIM, :] = kr.astype(BF16)
    o_ref[OFF_KV + MLA_QK_DIM:OFF_KV + LANES + MLA_NOPE, :] = zeros(LANES - MLA_QK_DIM + MLA_NOPE)
    o_ref[OFF_KV + LANES + MLA_NOPE:OFF_KV + LANES + MLA_NOPE + ROPE_HALF, :] = (-kr[ROPE_HALF:, :]).astype(BF16)
    o_ref[OFF_KV + LANES + MLA_NOPE + ROPE_HALF:OFF_KV + LANES + MLA_QK_DIM, :] = kr[:ROPE_HALF, :].astype(BF16)
    o_ref[OFF_KV + LANES + MLA_QK_DIM:ALIGNED_DIL, :] = zeros(ALIGNED_DIL - OFF_KV - LANES - MLA_QK_DIM)
    o_ref[ALIGNED_DIL:, :] = wt_ref[OFF_KR:, :].astype(BF16)


def _align_w_in(w_in_all, layer):
    wt_all = jnp.swapaxes(w_in_all, 1, 2)
    _, d_in, dm = wt_all.shape
    cb = ALIGN_CB
    width = ALIGNED_DIL + d_in - OFF_KR
    return pl.pallas_call(
        _align_kernel,
        grid=(dm // cb,),
        in_specs=[pl.BlockSpec((None, d_in, cb), lambda i: (layer, 0, i))],
        out_specs=pl.BlockSpec((width, cb), lambda i: (0, i)),
        out_shape=jax.ShapeDtypeStruct((width, dm), BF16),
        compiler_params=pltpu.CompilerParams(dimension_semantics=("arbitrary",), vmem_limit_bytes=VMEM_LIMIT),
        name="align",
    )(wt_all)


def _prep_weights(w_uq, w_ukv):
    uq = w_uq.reshape(MLA_Q_RANK, MLA_HEADS, MLA_QK_DIM)
    w_qm = w_uq
    w_qs = _rot_half_cols(uq[..., MLA_NOPE:]).reshape(MLA_Q_RANK, MLA_HEADS * MLA_ROPE)

    ukv = w_ukv.reshape(MLA_KV_RANK, MLA_HEADS, MLA_NOPE + MLA_V)
    zk = jnp.zeros((MLA_KV_RANK, MLA_HEADS, LANES - MLA_NOPE), F32)
    w_k = jnp.concatenate([ukv[..., :MLA_NOPE], zk], axis=-1).reshape(MLA_KV_RANK, MLA_HEADS * LANES)
    w_v = ukv[..., MLA_NOPE:].reshape(MLA_KV_RANK, MLA_HEADS * MLA_V)
    return tuple(a.astype(BF16) for a in (w_qm.T, w_qs.T, w_k, w_v.T))


def _layer(layer, w_in_all, x, mem, positions, g_pre_mix, b_gate, mla_q_norm, w_uq, mla_kv_norm, w_ukv, g_mem, w_mem_kv,
           w_br_mla, w_br_dil, w_br_mem, w_o, g_post_mix, g_pre_ffn, w_ffn_up, conv_w, conv_b, w_ffn_down,
           g_post_ffn):
    batch, seq, dm = x.shape
    t = batch * seq
    x2 = x.reshape(t, dm)
    r2 = lambda v: v.reshape(1, -1)

    w_al = _align_w_in(w_in_all, layer)
    w_qmt, w_qst, w_k, w_vt = _prep_weights(w_uq, w_ukv)
    invf = (ROPE_THETA ** (-jnp.arange(ROPE_HALF, dtype=F32) / ROPE_HALF)).reshape(ROPE_HALF, 1)
    pos_rows = positions.reshape(t // PREP_TM, 1, PREP_TM)

    h2d, qt, k, vt = _prep(x2, pos_rows, invf, r2(g_pre_mix), w_al, r2(mla_q_norm), w_qmt, w_qst,
                           r2(mla_kv_norm), w_k, w_vt, batch, seq)
    y_mla = _mla(qt, k.reshape(batch, seq, MLA_HEADS * LANES), vt).reshape(t, MLA_HEADS * MLA_V)

    o_dil, lse_dil = [], []
    for g, (_, dil) in enumerate(DIL_PAIRS):
        qkv = _dilproj(h2d, w_al, batch, seq, dil, g)
        o, lse = _dilattn(qkv, dil, g)
        o_dil.append(o)
        lse_dil.append(lse)

    kvm = _memkv(mem.reshape(-1, dm), r2(g_mem), w_mem_kv.astype(BF16))

    x1, h2 = _merge(x2, h2d, w_al, r2(b_gate), kvm, y_mla, w_br_mla.astype(BF16), o_dil, lse_dil,
                    w_br_dil.astype(BF16),
                    w_br_mem.astype(BF16), w_o.astype(BF16), r2(g_post_mix), r2(g_pre_ffn), batch, seq)

    out = _ffn(x1, h2, w_ffn_up.astype(BF16), conv_w, r2(conv_b), w_ffn_down.astype(BF16), r2(g_post_ffn), seq)
    return out.reshape(batch, seq, dm)


def kernel(x, mem, positions, g_pre_mix, w_in, b_gate, mla_q_norm, w_uq, mla_kv_norm, w_ukv, g_mem, w_mem_kv,
           w_br_mla, w_br_dil, w_br_mem, w_o, g_post_mix, g_pre_ffn, w_ffn_up, conv_w, conv_b, w_ffn_down,
           g_post_ffn):
    for l in range(w_in.shape[0]):
        x = _layer(l, w_in, x, mem, positions, g_pre_mix[l], b_gate[l], mla_q_norm[l], w_uq[l], mla_kv_norm[l],
                   w_ukv[l], g_mem[l], w_mem_kv[l], w_br_mla[l], w_br_dil[l], w_br_mem[l], w_o[l], g_post_mix[l],
                   g_pre_ffn[l], w_ffn_up[l], conv_w[l], conv_b[l], w_ffn_down[l], g_post_ffn[l])
    return x
```

```python
import functools

import jax
import jax.numpy as jnp
from jax import lax
from jax.experimental import pallas as pl
from jax.experimental.pallas import tpu as pltpu

F32 = jnp.float32
BF16 = jnp.bfloat16

RMS_EPS = 1e-6
LOG2E = 1.4426950408889634
LN2 = 0.6931471805599453
NEG_INF = -1e30
LANES = 128
BF16_SUBLANES = 16
V7X_VMEM_BYTES = 64 * 1024 * 1024

BLOCK = 128
MLA_HEADS = 8
MLA_NOPE = 64
MLA_ROPE = 32
MLA_V = 64
MLA_QK_DIM = MLA_NOPE + MLA_ROPE
MLA_Q_RANK = 384
MLA_KV_RANK = 256
ROPE_THETA = 10000.0
ROPE_HALF = MLA_ROPE // 2

DIL_PAIRS = ((128, 1), (512, 4), (2048, 16))
DIL_GROUPS = 3
DIL_HPG = 4
DIL_HEADS = DIL_GROUPS * DIL_HPG
DIL_HEAD_DIM = 128
DIL_WIDTH = DIL_HPG * DIL_HEAD_DIM
DIL_QSCALE = DIL_HEAD_DIM ** -0.5 * LOG2E

MEM_HEADS = 4
MEM_HEAD_DIM = 128
MEM_WIDTH = MEM_HEADS * MEM_HEAD_DIM

N_BRANCH = 3
CONV_WIDTH = 3

OFF_Q = MLA_Q_RANK
OFF_KV = OFF_Q + MLA_KV_RANK
OFF_KR = OFF_KV + MLA_ROPE
OFF_DIL = OFF_KR + 3 * DIL_HEADS * DIL_HEAD_DIM
OFF_MEMQ = OFF_DIL + MEM_WIDTH
ALIGNED_DIL = 1024
ALIGNED_MEMQ = ALIGNED_DIL + (OFF_DIL - OFF_KR)
ALIGNED_GATE = ALIGNED_MEMQ + MEM_WIDTH

PREP_TM = 1024
DILPROJ_TM = 1024
DILPROJ_MAX_STRIDE = 4
ALIGN_CB = 256
MLA_TQ = 2048
MLA_TK = 512
MLA_HPS = 2
MLA_QSTRIP = 256
MLA_VROWS = MLA_V + BF16_SUBLANES
assert PREP_TM % MLA_TK == 0
DIL_TB = 2048
DIL_AHEAD = 4
MERGE_TM = 512
FFN_TM = 512
FFN_TF = 256
FFN_HALO = BF16_SUBLANES

TAIL_SPLIT = 4

VMEM_LIMIT = V7X_VMEM_BYTES * 7 // 8


def _rms(xf, g):
    return xf * lax.rsqrt(jnp.mean(xf * xf, axis=-1, keepdims=True) + RMS_EPS) * g


def _dot(a, b):
    return jnp.dot(a, b, preferred_element_type=F32)


def _dot_nt(a, b):
    return lax.dot_general(a, b, (((1,), (1,)), ((), ())), preferred_element_type=F32)


def _const_spec(shape):
    nd = len(shape)
    return pl.BlockSpec(shape, lambda *_: (0,) * nd)


def _prep_kernel(x_ref, pos_ref, invf_ref, g_ref, wa_ref, qn_ref, wqmt_ref, wqst_ref, kvn_ref, wk_ref, wvt_ref,
                 h_ref, qt_ref, k_ref, vt_ref):
    tm = x_ref.shape[0]
    h = _rms(x_ref[...], g_ref[...]).astype(BF16)
    h_ref[...] = h
    p = _dot_nt(h, wa_ref[...])
    cq = _rms(p[:, :OFF_Q], qn_ref[...]).astype(BF16)
    ckv = _rms(p[:, OFF_Q:OFF_KV], kvn_ref[...]).astype(BF16)

    ang = invf_ref[...] * pos_ref[0].astype(F32)
    c16 = jnp.cos(ang)
    s16 = jnp.sin(ang)
    ones = jnp.ones((MLA_NOPE, tm), F32)
    zeros_lo = jnp.zeros((MLA_NOPE, tm), F32)
    zeros_hi = jnp.zeros((LANES - MLA_QK_DIM, tm), F32)
    cos_c = jnp.concatenate([ones, c16, c16, zeros_hi], axis=0)
    sin_c = jnp.concatenate([zeros_lo, s16, s16, zeros_hi], axis=0)

    qmt = _dot_nt(wqmt_ref[...], cq)
    qst = _dot_nt(wqst_ref[...], cq)
    qscale = MLA_QK_DIM ** -0.5 * LOG2E
    cos_rope = jnp.concatenate([c16, c16], axis=0)
    sin_rope = jnp.concatenate([s16, s16], axis=0)
    for hd in range(MLA_HEADS):
        base = hd * LANES
        nope = qmt[hd * MLA_QK_DIM:hd * MLA_QK_DIM + MLA_NOPE, :]
        rope = qmt[hd * MLA_QK_DIM + MLA_NOPE:(hd + 1) * MLA_QK_DIM, :]
        rot = qst[hd * MLA_ROPE:(hd + 1) * MLA_ROPE, :]
        qt_ref[base:base + MLA_NOPE, :] = (nope * qscale).astype(BF16)
        qt_ref[base + MLA_NOPE:base + MLA_QK_DIM, :] = ((rope * cos_rope + rot * sin_rope) * qscale).astype(BF16)
        qt_ref[base + MLA_QK_DIM:base + LANES, :] = jnp.zeros((LANES - MLA_QK_DIM, tm), BF16)

    cos_r = cos_c.T
    sin_r = sin_c.T
    kn = _dot(ckv, wk_ref[...])
    kpe = p[:, OFF_KV:OFF_KV + LANES] * cos_r + p[:, OFF_KV + LANES:OFF_KV + 2 * LANES] * sin_r
    for hd in range(MLA_HEADS):
        sl = slice(hd * LANES, (hd + 1) * LANES)
        k_ref[:, sl] = (kn[:, sl] + kpe).astype(BF16)
    vt = _dot_nt(wvt_ref[...], ckv).astype(BF16)
    ones_rows = jnp.ones((MLA_VROWS - MLA_V, MLA_TK), BF16)
    for ci in range(tm // MLA_TK):
        for hd in range(MLA_HEADS):
            keys = slice(ci * MLA_TK, (ci + 1) * MLA_TK)
            vt_ref[ci, hd * MLA_VROWS:hd * MLA_VROWS + MLA_V, :] = vt[hd * MLA_V:(hd + 1) * MLA_V, keys]
            vt_ref[ci, hd * MLA_VROWS + MLA_V:(hd + 1) * MLA_VROWS, :] = ones_rows


def _prep(x2, pos_rows, invf, g_pre, w_al, q_norm, w_qmt, w_qst, kv_norm, w_k, w_vt, batch, seq):
    t, d = x2.shape
    tm = PREP_TM
    tps = seq // tm
    hw = MLA_HEADS * LANES
    vw = MLA_HEADS * MLA_VROWS
    row = lambda i: (i, 0)
    return pl.pallas_call(
        _prep_kernel,
        grid=(t // tm,),
        in_specs=[
            pl.BlockSpec((tm, d), row),
            pl.BlockSpec((1, 1, tm), lambda i: (i, 0, 0)),
            _const_spec(invf.shape),
            _const_spec(g_pre.shape),
            pl.BlockSpec((OFF_KV + 2 * LANES, d), lambda i: (0, 0)),
            _const_spec(q_norm.shape),
            _const_spec(w_qmt.shape),
            _const_spec(w_qst.shape),
            _const_spec(kv_norm.shape),
            _const_spec(w_k.shape),
            _const_spec(w_vt.shape),
        ],
        out_specs=[pl.BlockSpec((tm, d), row),
                   pl.BlockSpec((None, hw, tm), lambda i: (i // tps, 0, i % tps)),
                   pl.BlockSpec((tm, hw), row),
                   pl.BlockSpec((None, tm // MLA_TK, vw, MLA_TK), lambda i: (i // tps, i % tps, 0, 0))],
        out_shape=[jax.ShapeDtypeStruct((t, d), BF16),
                   jax.ShapeDtypeStruct((batch, hw, seq), BF16),
                   jax.ShapeDtypeStruct((t, hw), BF16),
                   jax.ShapeDtypeStruct((batch, seq // MLA_TK, vw, MLA_TK), BF16)],
        compiler_params=pltpu.CompilerParams(dimension_semantics=("arbitrary",), vmem_limit_bytes=VMEM_LIMIT),
        name="prep",
    )(x2, pos_rows, invf, g_pre, w_al, q_norm, w_qmt, w_qst, kv_norm, w_k, w_vt)


def _dilproj_kernel(h_ref, wq_ref, wk_ref, wv_ref, o_ref, *acc_refs, dil):
    h = h_ref[...]
    w_refs = (wq_ref, wk_ref, wv_ref)
    tm, tn = h_ref.shape[0], wq_ref.shape[0]

    def projection(j):
        acc = _dot_nt(h, w_refs[j][...])
        return acc * DIL_QSCALE if j == 0 else acc

    if dil == 1:
        for j in range(len(w_refs)):
            o_ref[0, :, j * tn:(j + 1) * tn] = projection(j).astype(BF16)
        return
    chunks = tn // LANES
    rows = tm // dil

    def project(j):
        acc = projection(j)
        for c in range(chunks):
            acc_refs[j][c] = acc[:, c * LANES:(c + 1) * LANES]

    def scatter(j):
        src = acc_refs[j]
        if dil > DILPROJ_MAX_STRIDE:
            s1 = DILPROJ_MAX_STRIDE
            s2 = dil // s1
            tmp = acc_refs[-1]
            for r1 in range(s1):
                for c in range(chunks):
                    tmp[c, r1 * (tm // s1):(r1 + 1) * (tm // s1), :] = src[c, pl.ds(r1, tm // s1, stride=s1), :]
            for r in range(dil):
                r1, r2 = r % s1, r // s1
                for c in range(chunks):
                    cols = slice(j * tn + c * LANES, j * tn + (c + 1) * LANES)
                    o_ref[r, :, cols] = tmp[c, pl.ds(r1 * (tm // s1) + r2, rows, stride=s2), :].astype(BF16)
            return
        for r in range(dil):
            for c in range(chunks):
                cols = slice(j * tn + c * LANES, j * tn + (c + 1) * LANES)
                o_ref[r, :, cols] = src[c, pl.ds(r, rows, stride=dil), :].astype(BF16)

    project(0)
    project(1)
    scatter(0)
    project(2)
    scatter(1)
    scatter(2)


def _dilproj(h2, w_al, batch, seq, dil, group):
    t, d = h2.shape
    n = 3 * DIL_WIDTH
    tm, tn = DILPROJ_TM, DIL_WIDTH
    assert ALIGNED_DIL % tn == 0
    tiles_per_seq = seq // tm
    col0 = ALIGNED_DIL // tn + group
    w_spec = lambda j: pl.BlockSpec((tn, d), lambda i: (col0 + j * DIL_GROUPS, 0))
    n_scratch = 0 if dil == 1 else (3 if dil <= DILPROJ_MAX_STRIDE else 4)
    scratch = [pltpu.VMEM((tn // LANES, tm, LANES), F32)] * n_scratch
    return pl.pallas_call(
        functools.partial(_dilproj_kernel, dil=dil),
        grid=(t // tm,),
        in_specs=[pl.BlockSpec((tm, d), lambda i: (i, 0)), w_spec(0), w_spec(1), w_spec(2)],
        out_specs=pl.BlockSpec((None, dil, tm // dil, n),
                               lambda i: (i // tiles_per_seq, 0, i % tiles_per_seq, 0)),
        out_shape=jax.ShapeDtypeStruct((batch, dil, seq // dil, n), BF16),
        scratch_shapes=scratch,
        compiler_params=pltpu.CompilerParams(dimension_semantics=("arbitrary",), vmem_limit_bytes=VMEM_LIMIT),
        name=f"dilproj{dil}",
    )(h2, w_al, w_al, w_al)


def _mla_kernel(qt_ref, k_ref, vt_ref, qtn_ref, k0n_ref, o_ref, m_ref, acc_ref, sta_ref, stb_ref):
    tq = qt_ref.shape[1]
    tk = MLA_TK
    qi = pl.program_id(2)
    nstrip = tq // MLA_QSTRIP
    m_ref[...] = jnp.full(m_ref.shape, NEG_INF, F32)
    acc_ref[...] = jnp.zeros(acc_ref.shape, F32)
    chains = [(a, hq) for a in range(MLA_HPS) for hq in range(nstrip)]

    def scores(c, i, nk=tk):
        a, hq = chains[i]
        k = k_ref[pl.ds(pl.multiple_of(c * tk, tk), nk), a * LANES:(a + 1) * LANES]
        return _dot(k, qt_ref[a * LANES:(a + 1) * LANES, hq * MLA_QSTRIP:(hq + 1) * MLA_QSTRIP])

    def successor_scores(i):
        a, hq = chains[i]
        return _dot(k0n_ref[:, a * LANES:(a + 1) * LANES],
                    qtn_ref[a * LANES:(a + 1) * LANES, hq * MLA_QSTRIP:(hq + 1) * MLA_QSTRIP])

    def step(c, src_ref, dst_ref, key_off=None, last=False):
        def visible(hq, off):
            return tk if off is None else max(0, min(tk, (hq + 1) * MLA_QSTRIP - off))

        next_off = None if key_off is None else key_off + tk
        m_all = m_ref[...]
        acc_all = acc_ref[...]
        m_new, acc_new = {}, {}
        for i, (a, hq) in enumerate(chains):
            qs = slice(hq * MLA_QSTRIP, (hq + 1) * MLA_QSTRIP)
            if last:
                dst_ref[i] = successor_scores(i)
            elif visible(hq, next_off) > 0:
                nk_next = visible(hq, next_off)
                dst_ref[i, :nk_next, :] = scores(c + 1, i, nk_next)
            nk = visible(hq, key_off)
            if nk == 0:
                m_new[a, hq], acc_new[a, hq] = m_all[a, :, qs], acc_all[a, :, qs]
                continue
            st = src_ref[i, :nk, :]
            if key_off is not None and key_off + nk - 1 > hq * MLA_QSTRIP:
                key = lax.broadcasted_iota(jnp.int32, st.shape, 0) + key_off
                qry = lax.broadcasted_iota(jnp.int32, st.shape, 1) + hq * MLA_QSTRIP
                st = jnp.where(key <= qry, st, NEG_INF)
            m_prev = m_all[a, :, qs]
            m_cur = jnp.maximum(m_prev, jnp.max(st, axis=0, keepdims=True))
            alpha = jnp.exp2(m_prev - m_cur)
            p = jnp.exp2(st - m_cur).astype(BF16)
            vt = vt_ref[c, a * MLA_VROWS:(a + 1) * MLA_VROWS, :nk]
            acc_new[a, hq] = alpha * acc_all[a, :, qs] + _dot(vt, p)
            m_new[a, hq] = m_cur
        for a in range(MLA_HPS):
            m_ref[a] = jnp.concatenate([m_new[a, hq] for hq in range(nstrip)], axis=-1)
            acc_ref[a] = jnp.concatenate([acc_new[a, hq] for hq in range(nstrip)], axis=-1)

    @pl.when(jnp.logical_and(jnp.logical_and(pl.program_id(0) == 0, pl.program_id(1) == 0), qi == 0))
    def _():
        for i in range(len(chains)):
            sta_ref[i] = scores(0, i)

    cpt = tq // tk
    assert cpt % 2 == 0
    bufs = (sta_ref, stb_ref)

    def trip(j, carry):
        for cc in range(cpt):
            step(cpt * j + cc, bufs[cc % 2], bufs[(cc + 1) % 2])
        return carry

    lax.fori_loop(0, qi, trip, 0)
    for cc in range(cpt):
        step(cpt * qi + cc, bufs[cc % 2], bufs[(cc + 1) % 2], key_off=cc * tk, last=cc == cpt - 1)

    out_t = jnp.concatenate([acc_ref[a, :MLA_V, :] / acc_ref[a, MLA_V:MLA_V + 1, :] for a in range(MLA_HPS)],
                            axis=0)
    o_ref[...] = out_t.T.astype(BF16)


def _mla(qt, k3, vt):
    b, s, _ = k3.shape
    tq, tk = MLA_TQ, MLA_TK
    hps = MLA_HPS
    st_scratch = pltpu.VMEM((hps * (tq // MLA_QSTRIP), tk, MLA_QSTRIP), F32)
    nh, nq = MLA_HEADS // hps, s // tq

    def successor(bi, h, i):
        wrap_i = i == nq - 1
        wrap_h = jnp.logical_and(wrap_i, h == nh - 1)
        i2 = jnp.where(wrap_i, 0, i + 1)
        h2 = jnp.where(wrap_h, 0, jnp.where(wrap_i, h + 1, h))
        b2 = jnp.minimum(bi + wrap_h.astype(jnp.int32), b - 1)
        return b2, h2, i2
    return pl.pallas_call(
        _mla_kernel,
        grid=(b, MLA_HEADS // hps, s // tq),
        in_specs=[
            pl.BlockSpec((None, hps * LANES, tq), lambda bi, h, i: (bi, h, i)),
            pl.BlockSpec((None, s, hps * LANES), lambda bi, h, i: (bi, 0, h)),
            pl.BlockSpec((None, s // tk, hps * MLA_VROWS, tk), lambda bi, h, i: (bi, 0, h, 0)),
            pl.BlockSpec((None, hps * LANES, tq), lambda bi, h, i: successor(bi, h, i)[:3]),
            pl.BlockSpec((None, tk, hps * LANES), lambda bi, h, i: (successor(bi, h, i)[0], 0, successor(bi, h, i)[1])),
        ],
        out_specs=pl.BlockSpec((None, tq, hps * MLA_V), lambda bi, h, i: (bi, i, h)),
        out_shape=jax.ShapeDtypeStruct((b, s, MLA_HEADS * MLA_V), BF16),
        scratch_shapes=[pltpu.VMEM((hps, 1, tq), F32), pltpu.VMEM((hps, MLA_VROWS, tq), F32),
                        st_scratch, st_scratch],
        compiler_params=pltpu.CompilerParams(dimension_semantics=("arbitrary", "arbitrary", "arbitrary"),
                                             vmem_limit_bytes=VMEM_LIMIT),
        name="mla",
    )(qt, k3, vt, qt, k3)


def _dilattn_kernel(q_ref, kc_ref, vc_ref, kp_ref, vp_ref, o_ref, lse_ref, *, dil, group):
    n = pl.program_id(2)
    nres, tb = q_ref.shape[0], q_ref.shape[1]
    nblk = tb // BLOCK

    qry = lax.broadcasted_iota(jnp.int32, (BLOCK, 2 * BLOCK), 0)
    key = lax.broadcasted_iota(jnp.int32, (BLOCK, 2 * BLOCK), 1)
    dist = qry + BLOCK - key
    in_window = jnp.logical_and(dist >= 0, dist <= BLOCK)
    first_ok = jnp.logical_and(in_window, jnp.logical_or(key >= BLOCK, n > 0))
    distf = (dist * dil).astype(F32)
    lane = lax.broadcasted_iota(jnp.int32, (BLOCK, LANES), 1)
    bias, bias_first = [], []
    for hh in range(DIL_HPG):
        slope = float(2.0 ** (-8.0 * (hh * DIL_GROUPS + group + 1) / DIL_HEADS))
        bias.append(jnp.where(in_window, -(slope * LOG2E) * distf, NEG_INF))
        bias_first.append(jnp.where(first_ok, -(slope * LOG2E) * distf, NEG_INF))

    units = [(r, t, hh) for r in range(nres) for t in range(nblk) for hh in range(DIL_HPG)]

    def window(cur_ref, prev_ref, u):
        r, t, hh = units[u]
        cs = slice(hh * DIL_HEAD_DIM, (hh + 1) * DIL_HEAD_DIM)
        if t == 0:
            return jnp.concatenate([prev_ref[r, :, cs], cur_ref[r, :BLOCK, cs]], axis=0)
        return cur_ref[r, (t - 1) * BLOCK:(t + 1) * BLOCK, cs]

    def scores(u):
        r, t, hh = units[u]
        cs = slice(hh * DIL_HEAD_DIM, (hh + 1) * DIL_HEAD_DIM)
        return _dot_nt(q_ref[r, t * BLOCK:(t + 1) * BLOCK, cs], window(kc_ref, kp_ref, u))

    pending = [scores(u) for u in range(min(DIL_AHEAD, len(units)))]
    lse_tile = None
    for u, (r, t, hh) in enumerate(units):
        if u + DIL_AHEAD < len(units):
            pending.append(scores(u + DIL_AHEAD))
        rs = slice(t * BLOCK, (t + 1) * BLOCK)
        cs = slice(hh * DIL_HEAD_DIM, (hh + 1) * DIL_HEAD_DIM)
        s = pending[u] + (bias_first[hh] if t == 0 else bias[hh])
        m = jnp.max(s, axis=-1, keepdims=True)
        e = jnp.exp2(s - m)
        den = jnp.sum(e, axis=-1, keepdims=True)
        o = _dot(e.astype(BF16), window(vc_ref, vp_ref, u)) / den
        o_ref[r, rs, cs] = o.astype(BF16)
        lse = (m + jnp.log2(den)) * LN2
        lse_tile = jnp.where(lane == hh, lse, jnp.zeros((BLOCK, LANES), F32) if hh == 0 else lse_tile)
        if hh == DIL_HPG - 1:
            lse_ref[r, rs, :] = lse_tile


def _dilattn(qkv, dil, group):
    b, d, l, _ = qkv.shape
    tb = min(DIL_TB, l)
    nres = min(d, DIL_TB // tb)
    bpt = tb // BLOCK
    w = DIL_WIDTH
    cur = lambda c: pl.BlockSpec((None, nres, tb, w), lambda bi, r, n: (bi, r, n, c))
    prev = lambda c: pl.BlockSpec((None, nres, BLOCK, w), lambda bi, r, n: (bi, r, jnp.maximum(n * bpt - 1, 0), c))
    return pl.pallas_call(
        functools.partial(_dilattn_kernel, dil=dil, group=group),
        grid=(b, d // nres, l // tb),
        in_specs=[cur(0), cur(1), cur(2), prev(1), prev(2)],
        out_specs=[pl.BlockSpec((None, nres, tb, w), lambda bi, r, n: (bi, r, n, 0)),
                   pl.BlockSpec((None, nres, tb, LANES), lambda bi, r, n: (bi, r, n, 0))],
        out_shape=[jax.ShapeDtypeStruct((b, d, l, w), BF16), jax.ShapeDtypeStruct((b, d, l, LANES), F32)],
        compiler_params=pltpu.CompilerParams(dimension_semantics=("arbitrary", "arbitrary", "arbitrary"),
                                             vmem_limit_bytes=VMEM_LIMIT),
        name=f"dilattn{dil}",
    )(qkv, qkv, qkv, qkv, qkv)


def _memkv_kernel(mem_ref, g_ref, w_ref, o_ref):
    o_ref[...] = _dot(_rms(mem_ref[...], g_ref[...]).astype(BF16), w_ref[...]).astype(BF16)


def _memkv(mem2, g_mem, w):
    return pl.pallas_call(
        _memkv_kernel,
        out_shape=jax.ShapeDtypeStruct((mem2.shape[0], w.shape[1]), BF16),
        compiler_params=pltpu.CompilerParams(vmem_limit_bytes=VMEM_LIMIT),
        name="memkv",
    )(mem2, g_mem, w)


def _sigmoid(z):
    return 1.0 / (1.0 + jnp.exp(-z))


def _merge_kernel(x_ref, h_ref, wmq_ref, wg_ref, bg_ref, kvm_ref, ymla_ref, wbm_ref,
                  o0_ref, o1_ref, o2_ref, l0_ref, l1_ref, l2_ref, wbd_ref, wbmem_ref, wo_ref, gpm_ref, gpf_ref,
                  x1_ref, h2_ref, nat1_ref, nat2_ref, lse1_ref, lse2_ref):
    tm, dm = x_ref.shape
    h = h_ref[...]

    def gate(br):
        cs = slice(br * dm, (br + 1) * dm)
        return _dot_nt(h, wg_ref[cs, :]) + bg_ref[:, cs]

    for src, lsrc, dst, ldst in ((o1_ref, l1_ref, nat1_ref, lse1_ref), (o2_ref, l2_ref, nat2_ref, lse2_ref)):
        d = src.shape[0]
        rows = src.shape[1]
        for r in range(d):
            for hh in range(DIL_HPG):
                cs = slice(hh * DIL_HEAD_DIM, (hh + 1) * DIL_HEAD_DIM)
                dst[hh, pl.ds(r, rows, stride=d), :] = src[r, :, cs].astype(F32)
            ldst[pl.ds(r, rows, stride=d), :] = lsrc[r]
    lg = (l0_ref[0], lse1_ref[...], lse2_ref[...])
    heads = []
    for hh in range(DIL_HPG):
        cs = slice(hh * DIL_HEAD_DIM, (hh + 1) * DIL_HEAD_DIM)
        og = (o0_ref[0, :, cs].astype(F32), nat1_ref[hh], nat2_ref[hh])
        ls = [l[:, hh:hh + 1] for l in lg]
        mx = jnp.maximum(jnp.maximum(ls[0], ls[1]), ls[2])
        ws = [jnp.exp(l - mx) for l in ls]
        num = ws[0] * og[0] + ws[1] * og[1] + ws[2] * og[2]
        heads.append((num / (ws[0] + ws[1] + ws[2])).astype(BF16))
    y_dil = jnp.concatenate(heads, axis=-1)

    memq = _dot_nt(h, wmq_ref[...])
    mheads = []
    for hh in range(MEM_HEADS):
        cs = slice(hh * MEM_HEAD_DIM, (hh + 1) * MEM_HEAD_DIM)
        q = (memq[:, cs] * MEM_HEAD_DIM ** -0.5).astype(BF16)
        s = _dot_nt(q, kvm_ref[:, cs])
        e = jnp.exp(s - jnp.max(s, axis=-1, keepdims=True))
        o = _dot(e.astype(BF16), kvm_ref[:, MEM_WIDTH + hh * MEM_HEAD_DIM:MEM_WIDTH + (hh + 1) * MEM_HEAD_DIM])
        mheads.append((o / jnp.sum(e, axis=-1, keepdims=True)).astype(BF16))
    y_mem = jnp.concatenate(mheads, axis=-1)

    merged = jnp.zeros((tm, dm), F32)
    for br, (y, w_ref) in enumerate(((ymla_ref[...], wbm_ref), (y_dil, wbd_ref), (y_mem, wbmem_ref))):
        merged = merged + _sigmoid(gate(br)) * _dot(y, w_ref[...])
    merged_b = merged.astype(BF16)
    rows = tm // TAIL_SPLIT
    mixed = [_dot(merged_b[g * rows:(g + 1) * rows, :], wo_ref[...]) for g in range(TAIL_SPLIT)]
    for g in range(TAIL_SPLIT):
        rs = slice(g * rows, (g + 1) * rows)
        x1 = x_ref[rs, :] + _rms(mixed[g], gpm_ref[...])
        x1_ref[rs, :] = x1
        h2_ref[rs, :] = _rms(x1, gpf_ref[...]).astype(BF16)


def _merge(x2, h2, w_al, b_g, kvm, y_mla, w_bm, o_dil, lse_dil, w_bd, w_bmem, w_o, g_pm, g_pf, batch, seq):
    t, dm = x2.shape
    tm = MERGE_TM
    tps = seq // tm
    n_mem = kvm.shape[0] // batch
    row = lambda i: (i, 0)
    single = pl.Buffered(1)
    const = lambda a: pl.BlockSpec(a.shape, lambda i: (0,) * a.ndim, pipeline_mode=single)
    gate_w = N_BRANCH * dm
    assert ALIGNED_MEMQ % MEM_WIDTH == 0 and ALIGNED_GATE % gate_w == 0
    w_mq = pl.BlockSpec((MEM_WIDTH, dm), lambda i: (ALIGNED_MEMQ // MEM_WIDTH, 0), pipeline_mode=single)
    w_g = pl.BlockSpec((gate_w, dm), lambda i: (ALIGNED_GATE // gate_w, 0), pipeline_mode=single)

    def dil_spec(a):
        d, width = a.shape[1], a.shape[3]
        return pl.BlockSpec((None, d, tm // d, width), lambda i: (i // tps, 0, i % tps, 0))

    return pl.pallas_call(
        _merge_kernel,
        grid=(t // tm,),
        in_specs=[
            pl.BlockSpec((tm, dm), row), pl.BlockSpec((tm, dm), row), w_mq, w_g, const(b_g),
            pl.BlockSpec((n_mem, kvm.shape[1]), lambda i: (i // tps, 0)),
            pl.BlockSpec((tm, y_mla.shape[1]), row), const(w_bm),
            dil_spec(o_dil[0]), dil_spec(o_dil[1]), dil_spec(o_dil[2]),
            dil_spec(lse_dil[0]), dil_spec(lse_dil[1]), dil_spec(lse_dil[2]),
            const(w_bd), const(w_bmem), const(w_o), const(g_pm), const(g_pf),
        ],
        out_specs=[pl.BlockSpec((tm, dm), row), pl.BlockSpec((tm, dm), row)],
        out_shape=[jax.ShapeDtypeStruct((t, dm), F32), jax.ShapeDtypeStruct((t, dm), BF16)],
        scratch_shapes=[pltpu.VMEM((DIL_HPG, tm, DIL_HEAD_DIM), F32), pltpu.VMEM((DIL_HPG, tm, DIL_HEAD_DIM), F32),
                        pltpu.VMEM((tm, LANES), F32), pltpu.VMEM((tm, LANES), F32)],
        compiler_params=pltpu.CompilerParams(dimension_semantics=("arbitrary",), vmem_limit_bytes=VMEM_LIMIT),
        name="merge",
    )(x2, h2, w_al, w_al, b_g, kvm, y_mla, w_bm, o_dil[0], o_dil[1], o_dil[2], lse_dil[0], lse_dil[1], lse_dil[2],
      w_bd, w_bmem, w_o, g_pm, g_pf)


def _ffn_kernel(x1_ref, h2_ref, halo_ref, wup_ref, cw_ref, cb_ref, wd_ref, gpost_ref,
                out_ref, hcat_ref, ua_ref, ub_ref, acc_ref, *, tiles_per_seq):
    i = pl.program_id(0)
    tm = x1_ref.shape[0]
    halo, tf = FFN_HALO, FFN_TF
    dff = wd_ref.shape[0]
    nchunk = dff // tf
    lanes_per_chunk = tf // LANES

    first = (i % tiles_per_seq) == 0
    hcat_ref[:halo, :] = jnp.where(first, jnp.zeros_like(halo_ref[...]), halo_ref[...])
    hcat_ref[halo:, :] = h2_ref[...]

    def up(c, u_ref):
        hc = hcat_ref[...]
        for part, off in enumerate((c * tf, dff + c * tf)):
            u = _dot(hc, wup_ref[:, off:off + tf])
            for j in range(lanes_per_chunk):
                u_ref[part * lanes_per_chunk + j] = u[:, j * LANES:(j + 1) * LANES]

    def conv(u_ref, slab, col):
        cols = slice(col, col + LANES)
        z = cb_ref[:, cols]
        for tap in range(CONV_WIDTH):
            back = CONV_WIDTH - 1 - tap
            z = z + cw_ref[tap:tap + 1, cols] * u_ref[slab, halo - back:halo - back + tm, :]
        return z

    bufs = (ua_ref, ub_ref)
    up(0, bufs[0])
    for c in range(nchunk):
        cur = bufs[c % 2]
        if c + 1 < nchunk:
            up(c + 1, bufs[(c + 1) % 2])
        acts = []
        for j in range(lanes_per_chunk):
            gate = conv(cur, j, c * tf + j * LANES)
            val = conv(cur, lanes_per_chunk + j, dff + c * tf + j * LANES)
            acts.append((gate * _sigmoid(gate) * val).astype(BF16))
        down = _dot(jnp.concatenate(acts, axis=-1), wd_ref[c * tf:(c + 1) * tf, :])
        if c == 0:
            acc_ref[...] = down
        else:
            acc_ref[...] += down

    out_ref[...] = x1_ref[...] + _rms(acc_ref[...], gpost_ref[...])


def _ffn(x1, h2, w_up, conv_w, conv_b, w_down, g_post, seq):
    t, dm = x1.shape
    tm, tf, halo = FFN_TM, FFN_TF, FFN_HALO
    tps = seq // tm
    row = lambda i: (i, 0)
    const = lambda a: pl.BlockSpec(a.shape, lambda i: (0,) * a.ndim, pipeline_mode=pl.Buffered(1))
    u_scratch = pltpu.VMEM((2 * tf // LANES, tm + halo, LANES), F32)
    return pl.pallas_call(
        functools.partial(_ffn_kernel, tiles_per_seq=tps),
        grid=(t // tm,),
        in_specs=[
            pl.BlockSpec((tm, dm), row),
            pl.BlockSpec((tm, dm), row),
            pl.BlockSpec((halo, dm), lambda i: (jnp.maximum(i * (tm // halo) - 1, 0), 0)),
            const(w_up), const(conv_w), const(conv_b), const(w_down), const(g_post),
        ],
        out_specs=pl.BlockSpec((tm, dm), row),
        out_shape=jax.ShapeDtypeStruct((t, dm), F32),
        scratch_shapes=[pltpu.VMEM((tm + halo, dm), BF16), u_scratch, u_scratch, pltpu.VMEM((tm, dm), F32)],
        compiler_params=pltpu.CompilerParams(dimension_semantics=("arbitrary",), vmem_limit_bytes=VMEM_LIMIT),
        name="ffn",
    )(x1, h2, h2, w_up, conv_w, conv_b, w_down, g_post)


def _rot_half_cols(w):
    return jnp.concatenate([-w[..., ROPE_HALF:], w[..., :ROPE_HALF]], axis=-1)


def _align_kernel(wt_ref, o_ref):
    cb = wt_ref.shape[1]
    zeros = lambda n: jnp.zeros((n, cb), BF16)
    kr = wt_ref[OFF_KV:OFF_KR, :]
    o_ref[:OFF_KV, :] = wt_ref[:OFF_KV, :].astype(BF16)
    o_ref[OFF_KV:OFF_KV + MLA_NOPE, :] = zeros(MLA_NOPE)
    o_ref[OFF_KV + MLA_NOPE:OFF_KV + MLA_QK_DIM, :] = kr.astype(BF16)
    o_ref[OFF_KV + MLA_QK_DIM:OFF_KV + LANES + MLA_NOPE, :] = zeros(LANES - MLA_QK_DIM + MLA_NOPE)
    o_ref[OFF_KV + LANES + MLA_NOPE:OFF_KV + LANES + MLA_NOPE + ROPE_HALF, :] = (-kr[ROPE_HALF:, :]).astype(BF16)
    o_ref[OFF_KV + LANES + MLA_NOPE + ROPE_HALF:OFF_KV + LANES + MLA_QK_DIM, :] = kr[:ROPE_HALF, :].astype(BF16)
    o_ref[OFF_KV + LANES + MLA_QK_DIM:ALIGNED_DIL, :] = zeros(ALIGNED_DIL - OFF_KV - LANES - MLA_QK_DIM)
    o_ref[ALIGNED_DIL:, :] = wt_ref[OFF_KR:, :].astype(BF16)


def _align_w_in(w_in_all, layer):
    wt_all = jnp.swapaxes(w_in_all, 1, 2)
    _, d_in, dm = wt_all.shape
    cb = ALIGN_CB
    width = ALIGNED_DIL + d_in - OFF_KR
    return pl.pallas_call(
        _align_kernel,
        grid=(dm // cb,),
        in_specs=[pl.BlockSpec((None, d_in, cb), lambda i: (layer, 0, i))],
        out_specs=pl.BlockSpec((width, cb), lambda i: (0, i)),
        out_shape=jax.ShapeDtypeStruct((width, dm), BF16),
        compiler_params=pltpu.CompilerParams(dimension_semantics=("arbitrary",), vmem_limit_bytes=VMEM_LIMIT),
        name="align",
    )(wt_all)


def _prep_weights(w_uq, w_ukv):
    uq = w_uq.reshape(MLA_Q_RANK, MLA_HEADS, MLA_QK_DIM)
    w_qm = w_uq
    w_qs = _rot_half_cols(uq[..., MLA_NOPE:]).reshape(MLA_Q_RANK, MLA_HEADS * MLA_ROPE)

    ukv = w_ukv.reshape(MLA_KV_RANK, MLA_HEADS, MLA_NOPE + MLA_V)
    zk = jnp.zeros((MLA_KV_RANK, MLA_HEADS, LANES - MLA_NOPE), F32)
    w_k = jnp.concatenate([ukv[..., :MLA_NOPE], zk], axis=-1).reshape(MLA_KV_RANK, MLA_HEADS * LANES)
    w_v = ukv[..., MLA_NOPE:].reshape(MLA_KV_RANK, MLA_HEADS * MLA_V)
    return tuple(a.astype(BF16) for a in (w_qm.T, w_qs.T, w_k, w_v.T))


def _layer(layer, w_in_all, x, mem, positions, g_pre_mix, b_gate, mla_q_norm, w_uq, mla_kv_norm, w_ukv, g_mem, w_mem_kv,
           w_br_mla, w_br_dil, w_br_mem, w_o, g_post_mix, g_pre_ffn, w_ffn_up, conv_w, conv_b, w_ffn_down,
           g_post_ffn):
    batch, seq, dm = x.shape
    t = batch * seq
    x2 = x.reshape(t, dm)
    r2 = lambda v: v.reshape(1, -1)

    w_al = _align_w_in(w_in_all, layer)
    w_qmt, w_qst, w_k, w_vt = _prep_weights(w_uq, w_ukv)
    invf = (ROPE_THETA ** (-jnp.arange(ROPE_HALF, dtype=F32) / ROPE_HALF)).reshape(ROPE_HALF, 1)
    pos_rows = positions.reshape(t // PREP_TM, 1, PREP_TM)

    h2d, qt, k, vt = _prep(x2, pos_rows, invf, r2(g_pre_mix), w_al, r2(mla_q_norm), w_qmt, w_qst,
                           r2(mla_kv_norm), w_k, w_vt, batch, seq)
    y_mla = _mla(qt, k.reshape(batch, seq, MLA_HEADS * LANES), vt).reshape(t, MLA_HEADS * MLA_V)

    o_dil, lse_dil = [], []
    for g, (_, dil) in enumerate(DIL_PAIRS):
        qkv = _dilproj(h2d, w_al, batch, seq, dil, g)
        o, lse = _dilattn(qkv, dil, g)
        o_dil.append(o)
        lse_dil.append(lse)

    kvm = _memkv(mem.reshape(-1, dm), r2(g_mem), w_mem_kv.astype(BF16))

    x1, h2 = _merge(x2, h2d, w_al, r2(b_gate), kvm, y_mla, w_br_mla.astype(BF16), o_dil, lse_dil,
                    w_br_dil.astype(BF16),
                    w_br_mem.astype(BF16), w_o.astype(BF16), r2(g_post_mix), r2(g_pre_ffn), batch, seq)

    out = _ffn(x1, h2, w_ffn_up.astype(BF16), conv_w, r2(conv_b), w_ffn_down.astype(BF16), r2(g_post_ffn), seq)
    return out.reshape(batch, seq, dm)


def kernel(x, mem, positions, g_pre_mix, w_in, b_gate, mla_q_norm, w_uq, mla_kv_norm, w_ukv, g_mem, w_mem_kv,
           w_br_mla, w_br_dil, w_br_mem, w_o, g_post_mix, g_pre_ffn, w_ffn_up, conv_w, conv_b, w_ffn_down,
           g_post_ffn):
    for l in range(w_in.shape[0]):
        x = _layer(l, w_in, x, mem, positions, g_pre_mix[l], b_gate[l], mla_q_norm[l], w_uq[l], mla_kv_norm[l],
                   w_ukv[l], g_mem[l], w_mem_kv[l], w_br_mla[l], w_br_dil[l], w_br_mem[l], w_o[l], g_post_mix[l],
                   g_pre_ffn[l], w_ffn_up[l], conv_w[l], conv_b[l], w_ffn_down[l], g_post_ffn[l])
    return x
```

```python
import functools

import jax
import jax.numpy as jnp
from jax import lax
from jax.experimental import pallas as pl
from jax.experimental.pallas import tpu as pltpu

F32 = jnp.float32
BF16 = jnp.bfloat16

RMS_EPS = 1e-6
LOG2E = 1.4426950408889634
LN2 = 0.6931471805599453
NEG_INF = -1e30
LANES = 128
BF16_SUBLANES = 16
V7X_VMEM_BYTES = 64 * 1024 * 1024

BLOCK = 128
MLA_HEADS = 8
MLA_NOPE = 64
MLA_ROPE = 32
MLA_V = 64
MLA_QK_DIM = MLA_NOPE + MLA_ROPE
MLA_Q_RANK = 384
MLA_KV_RANK = 256
ROPE_THETA = 10000.0
ROPE_HALF = MLA_ROPE // 2

DIL_PAIRS = ((128, 1), (512, 4), (2048, 16))
DIL_GROUPS = 3
DIL_HPG = 4
DIL_HEADS = DIL_GROUPS * DIL_HPG
DIL_HEAD_DIM = 128
DIL_WIDTH = DIL_HPG * DIL_HEAD_DIM
DIL_QSCALE = DIL_HEAD_DIM ** -0.5 * LOG2E

MEM_HEADS = 4
MEM_HEAD_DIM = 128
MEM_WIDTH = MEM_HEADS * MEM_HEAD_DIM

N_BRANCH = 3
CONV_WIDTH = 3

OFF_Q = MLA_Q_RANK
OFF_KV = OFF_Q + MLA_KV_RANK
OFF_KR = OFF_KV + MLA_ROPE
OFF_DIL = OFF_KR + 3 * DIL_HEADS * DIL_HEAD_DIM
OFF_MEMQ = OFF_DIL + MEM_WIDTH
ALIGNED_DIL = 1024
ALIGNED_MEMQ = ALIGNED_DIL + (OFF_DIL - OFF_KR)
ALIGNED_GATE = ALIGNED_MEMQ + MEM_WIDTH

PREP_TM = 1024
DILPROJ_TM = 1024
DILPROJ_MAX_STRIDE = 4
ALIGN_CB = 256
MLA_TQ = 2048
MLA_TK = 512
MLA_HPS = 2
MLA_QSTRIP = 256
MLA_VROWS = MLA_V + BF16_SUBLANES
assert PREP_TM % MLA_TK == 0
DIL_TB = 2048
DIL_AHEAD = 4
MERGE_TM = 512
FFN_TM = 512
FFN_TF = 256
FFN_HALO = BF16_SUBLANES

TAIL_SPLIT = 4

VMEM_LIMIT = V7X_VMEM_BYTES * 7 // 8


def _rms(xf, g):
    return xf * lax.rsqrt(jnp.mean(xf * xf, axis=-1, keepdims=True) + RMS_EPS) * g


def _dot(a, b):
    return jnp.dot(a, b, preferred_element_type=F32)


def _dot_nt(a, b):
    return lax.dot_general(a, b, (((1,), (1,)), ((), ())), preferred_element_type=F32)


def _const_spec(shape):
    nd = len(shape)
    return pl.BlockSpec(shape, lambda *_: (0,) * nd)


def _prep_kernel(x_ref, pos_ref, invf_ref, g_ref, wa_ref, qn_ref, wqmt_ref, wqst_ref, kvn_ref, wk_ref, wvt_ref,
                 h_ref, qt_ref, k_ref, vt_ref):
    tm = x_ref.shape[0]
    h = _rms(x_ref[...], g_ref[...]).astype(BF16)
    h_ref[...] = h
    p = _dot_nt(h, wa_ref[...])
    cq = _rms(p[:, :OFF_Q], qn_ref[...]).astype(BF16)
    ckv = _rms(p[:, OFF_Q:OFF_KV], kvn_ref[...]).astype(BF16)

    ang = invf_ref[...] * pos_ref[0].astype(F32)
    c16 = jnp.cos(ang)
    s16 = jnp.sin(ang)
    ones = jnp.ones((MLA_NOPE, tm), F32)
    zeros_lo = jnp.zeros((MLA_NOPE, tm), F32)
    zeros_hi = jnp.zeros((LANES - MLA_QK_DIM, tm), F32)
    cos_c = jnp.concatenate([ones, c16, c16, zeros_hi], axis=0)
    sin_c = jnp.concatenate([zeros_lo, s16, s16, zeros_hi], axis=0)

    qmt = _dot_nt(wqmt_ref[...], cq)
    qst = _dot_nt(wqst_ref[...], cq)
    qscale = MLA_QK_DIM ** -0.5 * LOG2E
    cos_rope = jnp.concatenate([c16, c16], axis=0)
    sin_rope = jnp.concatenate([s16, s16], axis=0)
    for hd in range(MLA_HEADS):
        base = hd * LANES
        nope = qmt[hd * MLA_QK_DIM:hd * MLA_QK_DIM + MLA_NOPE, :]
        rope = qmt[hd * MLA_QK_DIM + MLA_NOPE:(hd + 1) * MLA_QK_DIM, :]
        rot = qst[hd * MLA_ROPE:(hd + 1) * MLA_ROPE, :]
        qt_ref[base:base + MLA_NOPE, :] = (nope * qscale).astype(BF16)
        qt_ref[base + MLA_NOPE:base + MLA_QK_DIM, :] = ((rope * cos_rope + rot * sin_rope) * qscale).astype(BF16)
        qt_ref[base + MLA_QK_DIM:base + LANES, :] = jnp.zeros((LANES - MLA_QK_DIM, tm), BF16)

    cos_r = cos_c.T
    sin_r = sin_c.T
    kn = _dot(ckv, wk_ref[...])
    kpe = p[:, OFF_KV:OFF_KV + LANES] * cos_r + p[:, OFF_KV + LANES:OFF_KV + 2 * LANES] * sin_r
    for hd in range(MLA_HEADS):
        sl = slice(hd * LANES, (hd + 1) * LANES)
        k_ref[:, sl] = (kn[:, sl] + kpe).astype(BF16)
    vt = _dot_nt(wvt_ref[...], ckv).astype(BF16)
    ones_rows = jnp.ones((MLA_VROWS - MLA_V, MLA_TK), BF16)
    for ci in range(tm // MLA_TK):
        for hd in range(MLA_HEADS):
            keys = slice(ci * MLA_TK, (ci + 1) * MLA_TK)
            vt_ref[ci, hd * MLA_VROWS:hd * MLA_VROWS + MLA_V, :] = vt[hd * MLA_V:(hd + 1) * MLA_V, keys]
            vt_ref[ci, hd * MLA_VROWS + MLA_V:(hd + 1) * MLA_VROWS, :] = ones_rows


def _prep(x2, pos_rows, invf, g_pre, w_al, q_norm, w_qmt, w_qst, kv_norm, w_k, w_vt, batch, seq):
    t, d = x2.shape
    tm = PREP_TM
    tps = seq // tm
    hw = MLA_HEADS * LANES
    vw = MLA_HEADS * MLA_VROWS
    row = lambda i: (i, 0)
    return pl.pallas_call(
        _prep_kernel,
        grid=(t // tm,),
        in_specs=[
            pl.BlockSpec((tm, d), row),
            pl.BlockSpec((1, 1, tm), lambda i: (i, 0, 0)),
            _const_spec(invf.shape),
            _const_spec(g_pre.shape),
            pl.BlockSpec((OFF_KV + 2 * LANES, d), lambda i: (0, 0)),
            _const_spec(q_norm.shape),
            _const_spec(w_qmt.shape),
            _const_spec(w_qst.shape),
            _const_spec(kv_norm.shape),
            _const_spec(w_k.shape),
            _const_spec(w_vt.shape),
        ],
        out_specs=[pl.BlockSpec((tm, d), row),
                   pl.BlockSpec((None, hw, tm), lambda i: (i // tps, 0, i % tps)),
                   pl.BlockSpec((tm, hw), row),
                   pl.BlockSpec((None, tm // MLA_TK, vw, MLA_TK), lambda i: (i // tps, i % tps, 0, 0))],
        out_shape=[jax.ShapeDtypeStruct((t, d), BF16),
                   jax.ShapeDtypeStruct((batch, hw, seq), BF16),
                   jax.ShapeDtypeStruct((t, hw), BF16),
                   jax.ShapeDtypeStruct((batch, seq // MLA_TK, vw, MLA_TK), BF16)],
        compiler_params=pltpu.CompilerParams(dimension_semantics=("arbitrary",), vmem_limit_bytes=VMEM_LIMIT),
        name="prep",
    )(x2, pos_rows, invf, g_pre, w_al, q_norm, w_qmt, w_qst, kv_norm, w_k, w_vt)


def _dilproj_kernel(h_ref, wq_ref, wk_ref, wv_ref, o_ref, *acc_refs, dil):
    h = h_ref[...]
    w_refs = (wq_ref, wk_ref, wv_ref)
    tm, tn = h_ref.shape[0], wq_ref.shape[0]

    def projection(j):
        acc = _dot_nt(h, w_refs[j][...])
        return acc * DIL_QSCALE if j == 0 else acc

    if dil == 1:
        for j in range(len(w_refs)):
            o_ref[0, :, j * tn:(j + 1) * tn] = projection(j).astype(BF16)
        return
    chunks = tn // LANES
    rows = tm // dil

    def project(j):
        acc = projection(j)
        for c in range(chunks):
            acc_refs[j][c] = acc[:, c * LANES:(c + 1) * LANES]

    def scatter(j):
        src = acc_refs[j]
        if dil > DILPROJ_MAX_STRIDE:
            s1 = DILPROJ_MAX_STRIDE
            s2 = dil // s1
            tmp = acc_refs[-1]
            for r1 in range(s1):
                for c in range(chunks):
                    tmp[c, r1 * (tm // s1):(r1 + 1) * (tm // s1), :] = src[c, pl.ds(r1, tm // s1, stride=s1), :]
            for r in range(dil):
                r1, r2 = r % s1, r // s1
                for c in range(chunks):
                    cols = slice(j * tn + c * LANES, j * tn + (c + 1) * LANES)
                    o_ref[r, :, cols] = tmp[c, pl.ds(r1 * (tm // s1) + r2, rows, stride=s2), :].astype(BF16)
            return
        for r in range(dil):
            for c in range(chunks):
                cols = slice(j * tn + c * LANES, j * tn + (c + 1) * LANES)
                o_ref[r, :, cols] = src[c, pl.ds(r, rows, stride=dil), :].astype(BF16)

    project(0)
    project(1)
    scatter(0)
    project(2)
    scatter(1)
    scatter(2)


def _dilproj(h2, w_al, batch, seq, dil, group):
    t, d = h2.shape
    n = 3 * DIL_WIDTH
    tm, tn = DILPROJ_TM, DIL_WIDTH
    assert ALIGNED_DIL % tn == 0
    tiles_per_seq = seq // tm
    col0 = ALIGNED_DIL // tn + group
    w_spec = lambda j: pl.BlockSpec((tn, d), lambda i: (col0 + j * DIL_GROUPS, 0))
    n_scratch = 0 if dil == 1 else (3 if dil <= DILPROJ_MAX_STRIDE else 4)
    scratch = [pltpu.VMEM((tn // LANES, tm, LANES), F32)] * n_scratch
    return pl.pallas_call(
        functools.partial(_dilproj_kernel, dil=dil),
        grid=(t // tm,),
        in_specs=[pl.BlockSpec((tm, d), lambda i: (i, 0)), w_spec(0), w_spec(1), w_spec(2)],
        out_specs=pl.BlockSpec((None, dil, tm // dil, n),
                               lambda i: (i // tiles_per_seq, 0, i % tiles_per_seq, 0)),
        out_shape=jax.ShapeDtypeStruct((batch, dil, seq // dil, n), BF16),
        scratch_shapes=scratch,
        compiler_params=pltpu.CompilerParams(dimension_semantics=("arbitrary",), vmem_limit_bytes=VMEM_LIMIT),
        name=f"dilproj{dil}",
    )(h2, w_al, w_al, w_al)


def _mla_kernel(qt_ref, k_ref, vt_ref, qtn_ref, k0n_ref, o_ref, m_ref, acc_ref, sta_ref, stb_ref):
    tq = qt_ref.shape[1]
    tk = MLA_TK
    qi = pl.program_id(2)
    nstrip = tq // MLA_QSTRIP
    m_ref[...] = jnp.full(m_ref.shape, NEG_INF, F32)
    acc_ref[...] = jnp.zeros(acc_ref.shape, F32)
    chains = [(a, hq) for a in range(MLA_HPS) for hq in range(nstrip)]

    def scores(c, i, nk=tk):
        a, hq = chains[i]
        k = k_ref[pl.ds(pl.multiple_of(c * tk, tk), nk), a * LANES:(a + 1) * LANES]
        return _dot(k, qt_ref[a * LANES:(a + 1) * LANES, hq * MLA_QSTRIP:(hq + 1) * MLA_QSTRIP])

    def successor_scores(i):
        a, hq = chains[i]
        return _dot(k0n_ref[:, a * LANES:(a + 1) * LANES],
                    qtn_ref[a * LANES:(a + 1) * LANES, hq * MLA_QSTRIP:(hq + 1) * MLA_QSTRIP])

    def step(c, src_ref, dst_ref, key_off=None, last=False):
        def visible(hq, off):
            return tk if off is None else max(0, min(tk, (hq + 1) * MLA_QSTRIP - off))

        next_off = None if key_off is None else key_off + tk
        m_all = m_ref[...]
        acc_all = acc_ref[...]
        m_new, acc_new = {}, {}
        for i, (a, hq) in enumerate(chains):
            qs = slice(hq * MLA_QSTRIP, (hq + 1) * MLA_QSTRIP)
            if last:
                dst_ref[i] = successor_scores(i)
            elif visible(hq, next_off) > 0:
                nk_next = visible(hq, next_off)
                dst_ref[i, :nk_next, :] = scores(c + 1, i, nk_next)
            nk = visible(hq, key_off)
            if nk == 0:
                m_new[a, hq], acc_new[a, hq] = m_all[a, :, qs], acc_all[a, :, qs]
                continue
            st = src_ref[i, :nk, :]
            if key_off is not None and key_off + nk - 1 > hq * MLA_QSTRIP:
                key = lax.broadcasted_iota(jnp.int32, st.shape, 0) + key_off
                qry = lax.broadcasted_iota(jnp.int32, st.shape, 1) + hq * MLA_QSTRIP
                st = jnp.where(key <= qry, st, NEG_INF)
            m_prev = m_all[a, :, qs]
            m_cur = jnp.maximum(m_prev, jnp.max(st, axis=0, keepdims=True))
            alpha = jnp.exp2(m_prev - m_cur)
            p = jnp.exp2(st - m_cur).astype(BF16)
            vt = vt_ref[c, a * MLA_VROWS:(a + 1) * MLA_VROWS, :nk]
            acc_new[a, hq] = alpha * acc_all[a, :, qs] + _dot(vt, p)
            m_new[a, hq] = m_cur
        for a in range(MLA_HPS):
            m_ref[a] = jnp.concatenate([m_new[a, hq] for hq in range(nstrip)], axis=-1)
            acc_ref[a] = jnp.concatenate([acc_new[a, hq] for hq in range(nstrip)], axis=-1)

    @pl.when(jnp.logical_and(jnp.logical_and(pl.program_id(0) == 0, pl.program_id(1) == 0), qi == 0))
    def _():
        for i in range(len(chains)):
            sta_ref[i] = scores(0, i)

    cpt = tq // tk
    assert cpt % 2 == 0
    bufs = (sta_ref, stb_ref)

    def trip(j, carry):
        for cc in range(cpt):
            step(cpt * j + cc, bufs[cc % 2], bufs[(cc + 1) % 2])
        return carry

    lax.fori_loop(0, qi, trip, 0)
    for cc in range(cpt):
        step(cpt * qi + cc, bufs[cc % 2], bufs[(cc + 1) % 2], key_off=cc * tk, last=cc == cpt - 1)

    out_t = jnp.concatenate([acc_ref[a, :MLA_V, :] / acc_ref[a, MLA_V:MLA_V + 1, :] for a in range(MLA_HPS)],
                            axis=0)
    o_ref[...] = out_t.T.astype(BF16)


def _mla(qt, k3, vt):
    b, s, _ = k3.shape
    tq, tk = MLA_TQ, MLA_TK
    hps = MLA_HPS
    st_scratch = pltpu.VMEM((hps * (tq // MLA_QSTRIP), tk, MLA_QSTRIP), F32)
    nh, nq = MLA_HEADS // hps, s // tq

    def successor(bi, h, i):
        wrap_i = i == nq - 1
        wrap_h = jnp.logical_and(wrap_i, h == nh - 1)
        i2 = jnp.where(wrap_i, 0, i + 1)
        h2 = jnp.where(wrap_h, 0, jnp.where(wrap_i, h + 1, h))
        b2 = jnp.minimum(bi + wrap_h.astype(jnp.int32), b - 1)
        return b2, h2, i2
    return pl.pallas_call(
        _mla_kernel,
        grid=(b, MLA_HEADS // hps, s // tq),
        in_specs=[
            pl.BlockSpec((None, hps * LANES, tq), lambda bi, h, i: (bi, h, i)),
            pl.BlockSpec((None, s, hps * LANES), lambda bi, h, i: (bi, 0, h)),
            pl.BlockSpec((None, s // tk, hps * MLA_VROWS, tk), lambda bi, h, i: (bi, 0, h, 0)),
            pl.BlockSpec((None, hps * LANES, tq), lambda bi, h, i: successor(bi, h, i)[:3]),
            pl.BlockSpec((None, tk, hps * LANES), lambda bi, h, i: (successor(bi, h, i)[0], 0, successor(bi, h, i)[1])),
        ],
        out_specs=pl.BlockSpec((None, tq, hps * MLA_V), lambda bi, h, i: (bi, i, h)),
        out_shape=jax.ShapeDtypeStruct((b, s, MLA_HEADS * MLA_V), BF16),
        scratch_shapes=[pltpu.VMEM((hps, 1, tq), F32), pltpu.VMEM((hps, MLA_VROWS, tq), F32),
                        st_scratch, st_scratch],
        compiler_params=pltpu.CompilerParams(dimension_semantics=("arbitrary", "arbitrary", "arbitrary"),
                                             vmem_limit_bytes=VMEM_LIMIT),
        name="mla",
    )(qt, k3, vt, qt, k3)


def _dilattn_kernel(q_ref, kc_ref, vc_ref, kp_ref, vp_ref, o_ref, lse_ref, *, dil, group):
    n = pl.program_id(2)
    nres, tb = q_ref.shape[0], q_ref.shape[1]
    nblk = tb // BLOCK

    qry = lax.broadcasted_iota(jnp.int32, (BLOCK, 2 * BLOCK), 0)
    key = lax.broadcasted_iota(jnp.int32, (BLOCK, 2 * BLOCK), 1)
    dist = qry + BLOCK - key
    in_window = jnp.logical_and(dist >= 0, dist <= BLOCK)
    first_ok = jnp.logical_and(in_window, jnp.logical_or(key >= BLOCK, n > 0))
    distf = (dist * dil).astype(F32)
    lane = lax.broadcasted_iota(jnp.int32, (BLOCK, LANES), 1)
    bias, bias_first = [], []
    for hh in range(DIL_HPG):
        slope = float(2.0 ** (-8.0 * (hh * DIL_GROUPS + group + 1) / DIL_HEADS))
        bias.append(jnp.where(in_window, -(slope * LOG2E) * distf, NEG_INF))
        bias_first.append(jnp.where(first_ok, -(slope * LOG2E) * distf, NEG_INF))

    units = [(r, t, hh) for r in range(nres) for t in range(nblk) for hh in range(DIL_HPG)]

    def window(cur_ref, prev_ref, u):
        r, t, hh = units[u]
        cs = slice(hh * DIL_HEAD_DIM, (hh + 1) * DIL_HEAD_DIM)
        if t == 0:
            return jnp.concatenate([prev_ref[r, :, cs], cur_ref[r, :BLOCK, cs]], axis=0)
        return cur_ref[r, (t - 1) * BLOCK:(t + 1) * BLOCK, cs]

    def scores(u):
        r, t, hh = units[u]
        cs = slice(hh * DIL_HEAD_DIM, (hh + 1) * DIL_HEAD_DIM)
        return _dot_nt(q_ref[r, t * BLOCK:(t + 1) * BLOCK, cs], window(kc_ref, kp_ref, u))

    pending = [scores(u) for u in range(min(DIL_AHEAD, len(units)))]
    lse_tile = None
    for u, (r, t, hh) in enumerate(units):
        if u + DIL_AHEAD < len(units):
            pending.append(scores(u + DIL_AHEAD))
        rs = slice(t * BLOCK, (t + 1) * BLOCK)
        cs = slice(hh * DIL_HEAD_DIM, (hh + 1) * DIL_HEAD_DIM)
        s = pending[u] + (bias_first[hh] if t == 0 else bias[hh])
        m = jnp.max(s, axis=-1, keepdims=True)
        e = jnp.exp2(s - m)
        den = jnp.sum(e, axis=-1, keepdims=True)
        o = _dot(e.astype(BF16), window(vc_ref, vp_ref, u)) / den
        o_ref[r, rs, cs] = o.astype(BF16)
        lse = (m + jnp.log2(den)) * LN2
        lse_tile = jnp.where(lane == hh, lse, jnp.zeros((BLOCK, LANES), F32) if hh == 0 else lse_tile)
        if hh == DIL_HPG - 1:
            lse_ref[r, rs, :] = lse_tile


def _dilattn(qkv, dil, group):
    b, d, l, _ = qkv.shape
    tb = min(DIL_TB, l)
    nres = min(d, DIL_TB // tb)
    bpt = tb // BLOCK
    w = DIL_WIDTH
    cur = lambda c: pl.BlockSpec((None, nres, tb, w), lambda bi, r, n: (bi, r, n, c))
    prev = lambda c: pl.BlockSpec((None, nres, BLOCK, w), lambda bi, r, n: (bi, r, jnp.maximum(n * bpt - 1, 0), c))
    return pl.pallas_call(
        functools.partial(_dilattn_kernel, dil=dil, group=group),
        grid=(b, d // nres, l // tb),
        in_specs=[cur(0), cur(1), cur(2), prev(1), prev(2)],
        out_specs=[pl.BlockSpec((None, nres, tb, w), lambda bi, r, n: (bi, r, n, 0)),
                   pl.BlockSpec((None, nres, tb, LANES), lambda bi, r, n: (bi, r, n, 0))],
        out_shape=[jax.ShapeDtypeStruct((b, d, l, w), BF16), jax.ShapeDtypeStruct((b, d, l, LANES), F32)],
        compiler_params=pltpu.CompilerParams(dimension_semantics=("arbitrary", "arbitrary", "arbitrary"),
                                             vmem_limit_bytes=VMEM_LIMIT),
        name=f"dilattn{dil}",
    )(qkv, qkv, qkv, qkv, qkv)


def _memkv_kernel(mem_ref, g_ref, w_ref, o_ref):
    o_ref[...] = _dot(_rms(mem_ref[...], g_ref[...]).astype(BF16), w_ref[...]).astype(BF16)


def _memkv(mem2, g_mem, w):
    return pl.pallas_call(
        _memkv_kernel,
        out_shape=jax.ShapeDtypeStruct((mem2.shape[0], w.shape[1]), BF16),
        compiler_params=pltpu.CompilerParams(vmem_limit_bytes=VMEM_LIMIT),
        name="memkv",
    )(mem2, g_mem, w)


def _sigmoid(z):
    return 1.0 / (1.0 + jnp.exp(-z))


def _merge_kernel(x_ref, h_ref, wmq_ref, wg_ref, bg_ref, kvm_ref, ymla_ref, wbm_ref,
                  o0_ref, o1_ref, o2_ref, l0_ref, l1_ref, l2_ref, wbd_ref, wbmem_ref, wo_ref, gpm_ref, gpf_ref,
                  x1_ref, h2_ref, nat1_ref, nat2_ref, lse1_ref, lse2_ref):
    tm, dm = x_ref.shape
    h = h_ref[...]

    def gate(br):
        cs = slice(br * dm, (br + 1) * dm)
        return _dot_nt(h, wg_ref[cs, :]) + bg_ref[:, cs]

    for src, lsrc, dst, ldst in ((o1_ref, l1_ref, nat1_ref, lse1_ref), (o2_ref, l2_ref, nat2_ref, lse2_ref)):
        d = src.shape[0]
        rows = src.shape[1]
        for r in range(d):
            for hh in range(DIL_HPG):
                cs = slice(hh * DIL_HEAD_DIM, (hh + 1) * DIL_HEAD_DIM)
                dst[hh, pl.ds(r, rows, stride=d), :] = src[r, :, cs].astype(F32)
            ldst[pl.ds(r, rows, stride=d), :] = lsrc[r]
    lg = (l0_ref[0], lse1_ref[...], lse2_ref[...])
    heads = []
    for hh in range(DIL_HPG):
        cs = slice(hh * DIL_HEAD_DIM, (hh + 1) * DIL_HEAD_DIM)
        og = (o0_ref[0, :, cs].astype(F32), nat1_ref[hh], nat2_ref[hh])
        ls = [l[:, hh:hh + 1] for l in lg]
        mx = jnp.maximum(jnp.maximum(ls[0], ls[1]), ls[2])
        ws = [jnp.exp(l - mx) for l in ls]
        num = ws[0] * og[0] + ws[1] * og[1] + ws[2] * og[2]
        heads.append((num / (ws[0] + ws[1] + ws[2])).astype(BF16))
    y_dil = jnp.concatenate(heads, axis=-1)

    memq = _dot_nt(h, wmq_ref[...])
    mheads = []
    for hh in range(MEM_HEADS):
        cs = slice(hh * MEM_HEAD_DIM, (hh + 1) * MEM_HEAD_DIM)
        q = (memq[:, cs] * MEM_HEAD_DIM ** -0.5).astype(BF16)
        s = _dot_nt(q, kvm_ref[:, cs])
        e = jnp.exp(s - jnp.max(s, axis=-1, keepdims=True))
        o = _dot(e.astype(BF16), kvm_ref[:, MEM_WIDTH + hh * MEM_HEAD_DIM:MEM_WIDTH + (hh + 1) * MEM_HEAD_DIM])
        mheads.append((o / jnp.sum(e, axis=-1, keepdims=True)).astype(BF16))
    y_mem = jnp.concatenate(mheads, axis=-1)

    merged = jnp.zeros((tm, dm), F32)
    for br, (y, w_ref) in enumerate(((ymla_ref[...], wbm_ref), (y_dil, wbd_ref), (y_mem, wbmem_ref))):
        merged = merged + _sigmoid(gate(br)) * _dot(y, w_ref[...])
    merged_b = merged.astype(BF16)
    rows = tm // TAIL_SPLIT
    mixed = [_dot(merged_b[g * rows:(g + 1) * rows, :], wo_ref[...]) for g in range(TAIL_SPLIT)]
    for g in range(TAIL_SPLIT):
        rs = slice(g * rows, (g + 1) * rows)
        x1 = x_ref[rs, :] + _rms(mixed[g], gpm_ref[...])
        x1_ref[rs, :] = x1
        h2_ref[rs, :] = _rms(x1, gpf_ref[...]).astype(BF16)


def _merge(x2, h2, w_al, b_g, kvm, y_mla, w_bm, o_dil, lse_dil, w_bd, w_bmem, w_o, g_pm, g_pf, batch, seq):
    t, dm = x2.shape
    tm = MERGE_TM
    tps = seq // tm
    n_mem = kvm.shape[0] // batch
    row = lambda i: (i, 0)
    single = pl.Buffered(1)
    const = lambda a: pl.BlockSpec(a.shape, lambda i: (0,) * a.ndim, pipeline_mode=single)
    gate_w = N_BRANCH * dm
    assert ALIGNED_MEMQ % MEM_WIDTH == 0 and ALIGNED_GATE % gate_w == 0
    w_mq = pl.BlockSpec((MEM_WIDTH, dm), lambda i: (ALIGNED_MEMQ // MEM_WIDTH, 0), pipeline_mode=single)
    w_g = pl.BlockSpec((gate_w, dm), lambda i: (ALIGNED_GATE // gate_w, 0), pipeline_mode=single)

    def dil_spec(a):
        d, width = a.shape[1], a.shape[3]
        return pl.BlockSpec((None, d, tm // d, width), lambda i: (i // tps, 0, i % tps, 0))

    return pl.pallas_call(
        _merge_kernel,
        grid=(t // tm,),
        in_specs=[
            pl.BlockSpec((tm, dm), row), pl.BlockSpec((tm, dm), row), w_mq, w_g, const(b_g),
            pl.BlockSpec((n_mem, kvm.shape[1]), lambda i: (i // tps, 0)),
            pl.BlockSpec((tm, y_mla.shape[1]), row), const(w_bm),
            dil_spec(o_dil[0]), dil_spec(o_dil[1]), dil_spec(o_dil[2]),
            dil_spec(lse_dil[0]), dil_spec(lse_dil[1]), dil_spec(lse_dil[2]),
            const(w_bd), const(w_bmem), const(w_o), const(g_pm), const(g_pf),
        ],
        out_specs=[pl.BlockSpec((tm, dm), row), pl.BlockSpec((tm, dm), row)],
        out_shape=[jax.ShapeDtypeStruct((t, dm), F32), jax.ShapeDtypeStruct((t, dm), BF16)],
        scratch_shapes=[pltpu.VMEM((DIL_HPG, tm, DIL_HEAD_DIM), F32), pltpu.VMEM((DIL_HPG, tm, DIL_HEAD_DIM), F32),
                        pltpu.VMEM((tm, LANES), F32), pltpu.VMEM((tm, LANES), F32)],
        compiler_params=pltpu.CompilerParams(dimension_semantics=("arbitrary",), vmem_limit_bytes=VMEM_LIMIT),
        name="merge",
    )(x2, h2, w_al, w_al, b_g, kvm, y_mla, w_bm, o_dil[0], o_dil[1], o_dil[2], lse_dil[0], lse_dil[1], lse_dil[2],
      w_bd, w_bmem, w_o, g_pm, g_pf)


def _ffn_kernel(x1_ref, h2_ref, halo_ref, wup_ref, cw_ref, cb_ref, wd_ref, gpost_ref,
                out_ref, hcat_ref, ua_ref, ub_ref, acc_ref, *, tiles_per_seq):
    i = pl.program_id(0)
    tm = x1_ref.shape[0]
    halo, tf = FFN_HALO, FFN_TF
    dff = wd_ref.shape[0]
    nchunk = dff // tf
    lanes_per_chunk = tf // LANES

    first = (i % tiles_per_seq) == 0
    hcat_ref[:halo, :] = jnp.where(first, jnp.zeros_like(halo_ref[...]), halo_ref[...])
    hcat_ref[halo:, :] = h2_ref[...]

    def up(c, u_ref):
        hc = hcat_ref[...]
        for part, off in enumerate((c * tf, dff + c * tf)):
            u = _dot(hc, wup_ref[:, off:off + tf])
            for j in range(lanes_per_chunk):
                u_ref[part * lanes_per_chunk + j] = u[:, j * LANES:(j + 1) * LANES]

    def conv(u_ref, slab, col):
        cols = slice(col, col + LANES)
        z = cb_ref[:, cols]
        for tap in range(CONV_WIDTH):
            back = CONV_WIDTH - 1 - tap
            z = z + cw_ref[tap:tap + 1, cols] * u_ref[slab, halo - back:halo - back + tm, :]
        return z

    bufs = (ua_ref, ub_ref)
    up(0, bufs[0])
    for c in range(nchunk):
        cur = bufs[c % 2]
        if c + 1 < nchunk:
            up(c + 1, bufs[(c + 1) % 2])
        acts = []
        for j in range(lanes_per_chunk):
            gate = conv(cur, j, c * tf + j * LANES)
            val = conv(cur, lanes_per_chunk + j, dff + c * tf + j * LANES)
            acts.append((gate * _sigmoid(gate) * val).astype(BF16))
        down = _dot(jnp.concatenate(acts, axis=-1), wd_ref[c * tf:(c + 1) * tf, :].astype(BF16))
        if c == 0:
            acc_ref[...] = down
        else:
            acc_ref[...] += down

    out_ref[...] = x1_ref[...] + _rms(acc_ref[...], gpost_ref[...])


def _ffn(x1, h2, w_up, conv_w, conv_b, w_down, g_post, seq):
    t, dm = x1.shape
    tm, tf, halo = FFN_TM, FFN_TF, FFN_HALO
    tps = seq // tm
    row = lambda i: (i, 0)
    const = lambda a: pl.BlockSpec(a.shape, lambda i: (0,) * a.ndim, pipeline_mode=pl.Buffered(1))
    u_scratch = pltpu.VMEM((2 * tf // LANES, tm + halo, LANES), F32)
    return pl.pallas_call(
        functools.partial(_ffn_kernel, tiles_per_seq=tps),
        grid=(t // tm,),
        in_specs=[
            pl.BlockSpec((tm, dm), row),
            pl.BlockSpec((tm, dm), row),
            pl.BlockSpec((halo, dm), lambda i: (jnp.maximum(i * (tm // halo) - 1, 0), 0)),
            const(w_up), const(conv_w), const(conv_b), const(w_down), const(g_post),
        ],
        out_specs=pl.BlockSpec((tm, dm), row),
        out_shape=jax.ShapeDtypeStruct((t, dm), F32),
        scratch_shapes=[pltpu.VMEM((tm + halo, dm), BF16), u_scratch, u_scratch, pltpu.VMEM((tm, dm), F32)],
        compiler_params=pltpu.CompilerParams(dimension_semantics=("arbitrary",), vmem_limit_bytes=VMEM_LIMIT),
        name="ffn",
    )(x1, h2, h2, w_up, conv_w, conv_b, w_down, g_post)


def _rot_half_cols(w):
    return jnp.concatenate([-w[..., ROPE_HALF:], w[..., :ROPE_HALF]], axis=-1)


def _align_kernel(wt_ref, o_ref):
    cb = wt_ref.shape[1]
    zeros = lambda n: jnp.zeros((n, cb), BF16)
    kr = wt_ref[OFF_KV:OFF_KR, :]
    o_ref[:OFF_KV, :] = wt_ref[:OFF_KV, :].astype(BF16)
    o_ref[OFF_KV:OFF_KV + MLA_NOPE, :] = zeros(MLA_NOPE)
    o_ref[OFF_KV + MLA_NOPE:OFF_KV + MLA_QK_DIM, :] = kr.astype(BF16)
    o_ref[OFF_KV + MLA_QK_DIM:OFF_KV + LANES + MLA_NOPE, :] = zeros(LANES - MLA_QK_DIM + MLA_NOPE)
    o_ref[OFF_KV + LANES + MLA_NOPE:OFF_KV + LANES + MLA_NOPE + ROPE_HALF, :] = (-kr[ROPE_HALF:, :]).astype(BF16)
    o_ref[OFF_KV + LANES + MLA_NOPE + ROPE_HALF:OFF_KV + LANES + MLA_QK_DIM, :] = kr[:ROPE_HALF, :].astype(BF16)
    o_ref[OFF_KV + LANES + MLA_QK_DIM:ALIGNED_DIL, :] = zeros(ALIGNED_DIL - OFF_KV - LANES - MLA_QK_DIM)
    o_ref[ALIGNED_DIL:, :] = wt_ref[OFF_KR:, :].astype(BF16)


def _align_w_in(w_in_all, layer):
    wt_all = jnp.swapaxes(w_in_all, 1, 2)
    _, d_in, dm = wt_all.shape
    cb = ALIGN_CB
    width = ALIGNED_DIL + d_in - OFF_KR
    return pl.pallas_call(
        _align_kernel,
        grid=(dm // cb,),
        in_specs=[pl.BlockSpec((None, d_in, cb), lambda i: (layer, 0, i))],
        out_specs=pl.BlockSpec((width, cb), lambda i: (0, i)),
        out_shape=jax.ShapeDtypeStruct((width, dm), BF16),
        compiler_params=pltpu.CompilerParams(dimension_semantics=("arbitrary",), vmem_limit_bytes=VMEM_LIMIT),
        name="align",
    )(wt_all)


def _prep_weights(w_uq, w_ukv):
    uq = w_uq.reshape(MLA_Q_RANK, MLA_HEADS, MLA_QK_DIM)
    w_qm = w_uq
    w_qs = _rot_half_cols(uq[..., MLA_NOPE:]).reshape(MLA_Q_RANK, MLA_HEADS * MLA_ROPE)

    ukv = w_ukv.reshape(MLA_KV_RANK, MLA_HEADS, MLA_NOPE + MLA_V)
    zk = jnp.zeros((MLA_KV_RANK, MLA_HEADS, LANES - MLA_NOPE), F32)
    w_k = jnp.concatenate([ukv[..., :MLA_NOPE], zk], axis=-1).reshape(MLA_KV_RANK, MLA_HEADS * LANES)
    w_v = ukv[..., MLA_NOPE:].reshape(MLA_KV_RANK, MLA_HEADS * MLA_V)
    return tuple(a.astype(BF16) for a in (w_qm.T, w_qs.T, w_k, w_v.T))


def _layer(layer, w_in_all, x, mem, positions, g_pre_mix, b_gate, mla_q_norm, w_uq, mla_kv_norm, w_ukv, g_mem, w_mem_kv,
           w_br_mla, w_br_dil, w_br_mem, w_o, g_post_mix, g_pre_ffn, w_ffn_up, conv_w, conv_b, w_ffn_down,
           g_post_ffn):
    batch, seq, dm = x.shape
    t = batch * seq
    x2 = x.reshape(t, dm)
    r2 = lambda v: v.reshape(1, -1)

    w_al = _align_w_in(w_in_all, layer)
    w_qmt, w_qst, w_k, w_vt = _prep_weights(w_uq, w_ukv)
    invf = (ROPE_THETA ** (-jnp.arange(ROPE_HALF, dtype=F32) / ROPE_HALF)).reshape(ROPE_HALF, 1)
    pos_rows = positions.reshape(t // PREP_TM, 1, PREP_TM)

    h2d, qt, k, vt = _prep(x2, pos_rows, invf, r2(g_pre_mix), w_al, r2(mla_q_norm), w_qmt, w_qst,
                           r2(mla_kv_norm), w_k, w_vt, batch, seq)
    y_mla = _mla(qt, k.reshape(batch, seq, MLA_HEADS * LANES), vt).reshape(t, MLA_HEADS * MLA_V)

    o_dil, lse_dil = [], []
    for g, (_, dil) in enumerate(DIL_PAIRS):
        qkv = _dilproj(h2d, w_al, batch, seq, dil, g)
        o, lse = _dilattn(qkv, dil, g)
        o_dil.append(o)
        lse_dil.append(lse)

    kvm = _memkv(mem.reshape(-1, dm), r2(g_mem), w_mem_kv.astype(BF16))

    x1, h2 = _merge(x2, h2d, w_al, r2(b_gate), kvm, y_mla, w_br_mla.astype(BF16), o_dil, lse_dil,
                    w_br_dil.astype(BF16),
                    w_br_mem.astype(BF16), w_o.astype(BF16), r2(g_post_mix), r2(g_pre_ffn), batch, seq)

    out = _ffn(x1, h2, w_ffn_up.astype(BF16), conv_w, r2(conv_b), w_ffn_down, r2(g_post_ffn), seq)
    return out.reshape(batch, seq, dm)


def kernel(x, mem, positions, g_pre_mix, w_in, b_gate, mla_q_norm, w_uq, mla_kv_norm, w_ukv, g_mem, w_mem_kv,
           w_br_mla, w_br_dil, w_br_mem, w_o, g_post_mix, g_pre_ffn, w_ffn_up, conv_w, conv_b, w_ffn_down,
           g_post_ffn):
    for l in range(w_in.shape[0]):
        x = _layer(l, w_in, x, mem, positions, g_pre_mix[l], b_gate[l], mla_q_norm[l], w_uq[l], mla_kv_norm[l],
                   w_ukv[l], g_mem[l], w_mem_kv[l], w_br_mla[l], w_br_dil[l], w_br_mem[l], w_o[l], g_post_mix[l],
                   g_pre_ffn[l], w_ffn_up[l], conv_w[l], conv_b[l], w_ffn_down[l], g_post_ffn[l])
    return x
```

```python
import functools

import jax
import jax.numpy as jnp
from jax import lax
from jax.experimental import pallas as pl
from jax.experimental.pallas import tpu as pltpu

F32 = jnp.float32
BF16 = jnp.bfloat16

RMS_EPS = 1e-6
LOG2E = 1.4426950408889634
LN2 = 0.6931471805599453
NEG_INF = -1e30
LANES = 128
BF16_SUBLANES = 16
V7X_VMEM_BYTES = 64 * 1024 * 1024

BLOCK = 128
MLA_HEADS = 8
MLA_NOPE = 64
MLA_ROPE = 32
MLA_V = 64
MLA_QK_DIM = MLA_NOPE + MLA_ROPE
MLA_Q_RANK = 384
MLA_KV_RANK = 256
ROPE_THETA = 10000.0
ROPE_HALF = MLA_ROPE // 2

DIL_PAIRS = ((128, 1), (512, 4), (2048, 16))
DIL_GROUPS = 3
DIL_HPG = 4
DIL_HEADS = DIL_GROUPS * DIL_HPG
DIL_HEAD_DIM = 128
DIL_WIDTH = DIL_HPG * DIL_HEAD_DIM
DIL_QSCALE = DIL_HEAD_DIM ** -0.5 * LOG2E

MEM_HEADS = 4
MEM_HEAD_DIM = 128
MEM_WIDTH = MEM_HEADS * MEM_HEAD_DIM

N_BRANCH = 3
CONV_WIDTH = 3

OFF_Q = MLA_Q_RANK
OFF_KV = OFF_Q + MLA_KV_RANK
OFF_KR = OFF_KV + MLA_ROPE
OFF_DIL = OFF_KR + 3 * DIL_HEADS * DIL_HEAD_DIM
OFF_MEMQ = OFF_DIL + MEM_WIDTH
ALIGNED_DIL = 1024
ALIGNED_MEMQ = ALIGNED_DIL + (OFF_DIL - OFF_KR)
ALIGNED_GATE = ALIGNED_MEMQ + MEM_WIDTH

PREP_TM = 1024
DILPROJ_TM = 2048
DILPROJ_MAX_STRIDE = 4
ALIGN_CB = 256
MLA_TQ = 2048
MLA_TK = 512
MLA_HPS = 2
MLA_QSTRIP = 256
MLA_VROWS = MLA_V + BF16_SUBLANES
assert PREP_TM % MLA_TK == 0
DIL_TB = 2048
DIL_AHEAD = 4
MERGE_TM = 512
FFN_TM = 512
FFN_TF = 256
FFN_HALO = BF16_SUBLANES

TAIL_SPLIT = 4

VMEM_LIMIT = V7X_VMEM_BYTES * 7 // 8


def _rms(xf, g):
    return xf * lax.rsqrt(jnp.mean(xf * xf, axis=-1, keepdims=True) + RMS_EPS) * g


def _dot(a, b):
    return jnp.dot(a, b, preferred_element_type=F32)


def _dot_nt(a, b):
    return lax.dot_general(a, b, (((1,), (1,)), ((), ())), preferred_element_type=F32)


def _const_spec(shape):
    nd = len(shape)
    return pl.BlockSpec(shape, lambda *_: (0,) * nd)


def _prep_kernel(x_ref, pos_ref, invf_ref, g_ref, wa_ref, qn_ref, wqmt_ref, wqst_ref, kvn_ref, wk_ref, wvt_ref,
                 h_ref, qt_ref, k_ref, vt_ref):
    tm = x_ref.shape[0]
    h = _rms(x_ref[...], g_ref[...]).astype(BF16)
    h_ref[...] = h
    p = _dot_nt(h, wa_ref[...])
    cq = _rms(p[:, :OFF_Q], qn_ref[...]).astype(BF16)
    ckv = _rms(p[:, OFF_Q:OFF_KV], kvn_ref[...]).astype(BF16)

    ang = invf_ref[...] * pos_ref[0].astype(F32)
    c16 = jnp.cos(ang)
    s16 = jnp.sin(ang)
    ones = jnp.ones((MLA_NOPE, tm), F32)
    zeros_lo = jnp.zeros((MLA_NOPE, tm), F32)
    zeros_hi = jnp.zeros((LANES - MLA_QK_DIM, tm), F32)
    cos_c = jnp.concatenate([ones, c16, c16, zeros_hi], axis=0)
    sin_c = jnp.concatenate([zeros_lo, s16, s16, zeros_hi], axis=0)

    qmt = _dot_nt(wqmt_ref[...], cq)
    qst = _dot_nt(wqst_ref[...], cq)
    qscale = MLA_QK_DIM ** -0.5 * LOG2E
    cos_rope = jnp.concatenate([c16, c16], axis=0)
    sin_rope = jnp.concatenate([s16, s16], axis=0)
    for hd in range(MLA_HEADS):
        base = hd * LANES
        nope = qmt[hd * MLA_QK_DIM:hd * MLA_QK_DIM + MLA_NOPE, :]
        rope = qmt[hd * MLA_QK_DIM + MLA_NOPE:(hd + 1) * MLA_QK_DIM, :]
        rot = qst[hd * MLA_ROPE:(hd + 1) * MLA_ROPE, :]
        qt_ref[base:base + MLA_NOPE, :] = (nope * qscale).astype(BF16)
        qt_ref[base + MLA_NOPE:base + MLA_QK_DIM, :] = ((rope * cos_rope + rot * sin_rope) * qscale).astype(BF16)
        qt_ref[base + MLA_QK_DIM:base + LANES, :] = jnp.zeros((LANES - MLA_QK_DIM, tm), BF16)

    cos_r = cos_c.T
    sin_r = sin_c.T
    kn = _dot(ckv, wk_ref[...])
    kpe = p[:, OFF_KV:OFF_KV + LANES] * cos_r + p[:, OFF_KV + LANES:OFF_KV + 2 * LANES] * sin_r
    for hd in range(MLA_HEADS):
        sl = slice(hd * LANES, (hd + 1) * LANES)
        k_ref[:, sl] = (kn[:, sl] + kpe).astype(BF16)
    vt = _dot_nt(wvt_ref[...], ckv).astype(BF16)
    ones_rows = jnp.ones((MLA_VROWS - MLA_V, MLA_TK), BF16)
    for ci in range(tm // MLA_TK):
        for hd in range(MLA_HEADS):
            keys = slice(ci * MLA_TK, (ci + 1) * MLA_TK)
            vt_ref[ci, hd * MLA_VROWS:hd * MLA_VROWS + MLA_V, :] = vt[hd * MLA_V:(hd + 1) * MLA_V, keys]
            vt_ref[ci, hd * MLA_VROWS + MLA_V:(hd + 1) * MLA_VROWS, :] = ones_rows


def _prep(x2, pos_rows, invf, g_pre, w_al, q_norm, w_qmt, w_qst, kv_norm, w_k, w_vt, batch, seq):
    t, d = x2.shape
    tm = PREP_TM
    tps = seq // tm
    hw = MLA_HEADS * LANES
    vw = MLA_HEADS * MLA_VROWS
    row = lambda i: (i, 0)
    return pl.pallas_call(
        _prep_kernel,
        grid=(t // tm,),
        in_specs=[
            pl.BlockSpec((tm, d), row),
            pl.BlockSpec((1, 1, tm), lambda i: (i, 0, 0)),
            _const_spec(invf.shape),
            _const_spec(g_pre.shape),
            pl.BlockSpec((OFF_KV + 2 * LANES, d), lambda i: (0, 0)),
            _const_spec(q_norm.shape),
            _const_spec(w_qmt.shape),
            _const_spec(w_qst.shape),
            _const_spec(kv_norm.shape),
            _const_spec(w_k.shape),
            _const_spec(w_vt.shape),
        ],
        out_specs=[pl.BlockSpec((tm, d), row),
                   pl.BlockSpec((None, hw, tm), lambda i: (i // tps, 0, i % tps)),
                   pl.BlockSpec((tm, hw), row),
                   pl.BlockSpec((None, tm // MLA_TK, vw, MLA_TK), lambda i: (i // tps, i % tps, 0, 0))],
        out_shape=[jax.ShapeDtypeStruct((t, d), BF16),
                   jax.ShapeDtypeStruct((batch, hw, seq), BF16),
                   jax.ShapeDtypeStruct((t, hw), BF16),
                   jax.ShapeDtypeStruct((batch, seq // MLA_TK, vw, MLA_TK), BF16)],
        compiler_params=pltpu.CompilerParams(dimension_semantics=("arbitrary",), vmem_limit_bytes=VMEM_LIMIT),
        name="prep",
    )(x2, pos_rows, invf, g_pre, w_al, q_norm, w_qmt, w_qst, kv_norm, w_k, w_vt)


def _dilproj_kernel(h_ref, wq_ref, wk_ref, wv_ref, o_ref, *acc_refs, dil):
    h = h_ref[...]
    w_refs = (wq_ref, wk_ref, wv_ref)
    tm, tn = h_ref.shape[0], wq_ref.shape[0]

    def projection(j):
        acc = _dot_nt(h, w_refs[j][...])
        return acc * DIL_QSCALE if j == 0 else acc

    if dil == 1:
        for j in range(len(w_refs)):
            o_ref[0, :, j * tn:(j + 1) * tn] = projection(j).astype(BF16)
        return
    chunks = tn // LANES
    rows = tm // dil

    def project(j):
        acc = projection(j)
        for c in range(chunks):
            acc_refs[j][c] = acc[:, c * LANES:(c + 1) * LANES]

    def scatter(j):
        src = acc_refs[j]
        if dil > DILPROJ_MAX_STRIDE:
            s1 = DILPROJ_MAX_STRIDE
            s2 = dil // s1
            tmp = acc_refs[-1]
            for r1 in range(s1):
                for c in range(chunks):
                    tmp[c, r1 * (tm // s1):(r1 + 1) * (tm // s1), :] = src[c, pl.ds(r1, tm // s1, stride=s1), :]
            for r in range(dil):
                r1, r2 = r % s1, r // s1
                for c in range(chunks):
                    cols = slice(j * tn + c * LANES, j * tn + (c + 1) * LANES)
                    o_ref[r, :, cols] = tmp[c, pl.ds(r1 * (tm // s1) + r2, rows, stride=s2), :].astype(BF16)
            return
        for r in range(dil):
            for c in range(chunks):
                cols = slice(j * tn + c * LANES, j * tn + (c + 1) * LANES)
                o_ref[r, :, cols] = src[c, pl.ds(r, rows, stride=dil), :].astype(BF16)

    project(0)
    project(1)
    scatter(0)
    project(2)
    scatter(1)
    scatter(2)


def _dilproj(h2, w_al, batch, seq, dil, group):
    t, d = h2.shape
    n = 3 * DIL_WIDTH
    tm, tn = DILPROJ_TM, DIL_WIDTH
    assert ALIGNED_DIL % tn == 0
    tiles_per_seq = seq // tm
    col0 = ALIGNED_DIL // tn + group
    w_spec = lambda j: pl.BlockSpec((tn, d), lambda i: (col0 + j * DIL_GROUPS, 0))
    n_scratch = 0 if dil == 1 else (3 if dil <= DILPROJ_MAX_STRIDE else 4)
    scratch = [pltpu.VMEM((tn // LANES, tm, LANES), F32)] * n_scratch
    return pl.pallas_call(
        functools.partial(_dilproj_kernel, dil=dil),
        grid=(t // tm,),
        in_specs=[pl.BlockSpec((tm, d), lambda i: (i, 0)), w_spec(0), w_spec(1), w_spec(2)],
        out_specs=pl.BlockSpec((None, dil, tm // dil, n),
                               lambda i: (i // tiles_per_seq, 0, i % tiles_per_seq, 0)),
        out_shape=jax.ShapeDtypeStruct((batch, dil, seq // dil, n), BF16),
        scratch_shapes=scratch,
        compiler_params=pltpu.CompilerParams(dimension_semantics=("arbitrary",), vmem_limit_bytes=VMEM_LIMIT),
        name=f"dilproj{dil}",
    )(h2, w_al, w_al, w_al)


def _mla_kernel(qt_ref, k_ref, vt_ref, qtn_ref, k0n_ref, o_ref, m_ref, acc_ref, sta_ref, stb_ref):
    tq = qt_ref.shape[1]
    tk = MLA_TK
    qi = pl.program_id(2)
    nstrip = tq // MLA_QSTRIP
    m_ref[...] = jnp.full(m_ref.shape, NEG_INF, F32)
    acc_ref[...] = jnp.zeros(acc_ref.shape, F32)
    chains = [(a, hq) for a in range(MLA_HPS) for hq in range(nstrip)]

    def scores(c, i, nk=tk):
        a, hq = chains[i]
        k = k_ref[pl.ds(pl.multiple_of(c * tk, tk), nk), a * LANES:(a + 1) * LANES]
        return _dot(k, qt_ref[a * LANES:(a + 1) * LANES, hq * MLA_QSTRIP:(hq + 1) * MLA_QSTRIP])

    def successor_scores(i):
        a, hq = chains[i]
        return _dot(k0n_ref[:, a * LANES:(a + 1) * LANES],
                    qtn_ref[a * LANES:(a + 1) * LANES, hq * MLA_QSTRIP:(hq + 1) * MLA_QSTRIP])

    def step(c, src_ref, dst_ref, key_off=None, last=False):
        def visible(hq, off):
            return tk if off is None else max(0, min(tk, (hq + 1) * MLA_QSTRIP - off))

        next_off = None if key_off is None else key_off + tk
        m_all = m_ref[...]
        acc_all = acc_ref[...]
        m_new, acc_new = {}, {}
        for i, (a, hq) in enumerate(chains):
            qs = slice(hq * MLA_QSTRIP, (hq + 1) * MLA_QSTRIP)
            if last:
                dst_ref[i] = successor_scores(i)
            elif visible(hq, next_off) > 0:
                nk_next = visible(hq, next_off)
                dst_ref[i, :nk_next, :] = scores(c + 1, i, nk_next)
            nk = visible(hq, key_off)
            if nk == 0:
                m_new[a, hq], acc_new[a, hq] = m_all[a, :, qs], acc_all[a, :, qs]
                continue
            st = src_ref[i, :nk, :]
            if key_off is not None and key_off + nk - 1 > hq * MLA_QSTRIP:
                key = lax.broadcasted_iota(jnp.int32, st.shape, 0) + key_off
                qry = lax.broadcasted_iota(jnp.int32, st.shape, 1) + hq * MLA_QSTRIP
                st = jnp.where(key <= qry, st, NEG_INF)
            m_prev = m_all[a, :, qs]
            m_cur = jnp.maximum(m_prev, jnp.max(st, axis=0, keepdims=True))
            alpha = jnp.exp2(m_prev - m_cur)
            p = jnp.exp2(st - m_cur).astype(BF16)
            vt = vt_ref[c, a * MLA_VROWS:(a + 1) * MLA_VROWS, :nk]
            acc_new[a, hq] = alpha * acc_all[a, :, qs] + _dot(vt, p)
            m_new[a, hq] = m_cur
        for a in range(MLA_HPS):
            m_ref[a] = jnp.concatenate([m_new[a, hq] for hq in range(nstrip)], axis=-1)
            acc_ref[a] = jnp.concatenate([acc_new[a, hq] for hq in range(nstrip)], axis=-1)

    @pl.when(jnp.logical_and(jnp.logical_and(pl.program_id(0) == 0, pl.program_id(1) == 0), qi == 0))
    def _():
        for i in range(len(chains)):
            sta_ref[i] = scores(0, i)

    cpt = tq // tk
    assert cpt % 2 == 0
    bufs = (sta_ref, stb_ref)

    def trip(j, carry):
        for cc in range(cpt):
            step(cpt * j + cc, bufs[cc % 2], bufs[(cc + 1) % 2])
        return carry

    lax.fori_loop(0, qi, trip, 0)
    for cc in range(cpt):
        step(cpt * qi + cc, bufs[cc % 2], bufs[(cc + 1) % 2], key_off=cc * tk, last=cc == cpt - 1)

    out_t = jnp.concatenate([acc_ref[a, :MLA_V, :] / acc_ref[a, MLA_V:MLA_V + 1, :] for a in range(MLA_HPS)],
                            axis=0)
    o_ref[...] = out_t.T.astype(BF16)


def _mla(qt, k3, vt):
    b, s, _ = k3.shape
    tq, tk = MLA_TQ, MLA_TK
    hps = MLA_HPS
    st_scratch = pltpu.VMEM((hps * (tq // MLA_QSTRIP), tk, MLA_QSTRIP), F32)
    nh, nq = MLA_HEADS // hps, s // tq

    def successor(bi, h, i):
        wrap_i = i == nq - 1
        wrap_h = jnp.logical_and(wrap_i, h == nh - 1)
        i2 = jnp.where(wrap_i, 0, i + 1)
        h2 = jnp.where(wrap_h, 0, jnp.where(wrap_i, h + 1, h))
        b2 = jnp.minimum(bi + wrap_h.astype(jnp.int32), b - 1)
        return b2, h2, i2
    return pl.pallas_call(
        _mla_kernel,
        grid=(b, MLA_HEADS // hps, s // tq),
        in_specs=[
            pl.BlockSpec((None, hps * LANES, tq), lambda bi, h, i: (bi, h, i)),
            pl.BlockSpec((None, s, hps * LANES), lambda bi, h, i: (bi, 0, h)),
            pl.BlockSpec((None, s // tk, hps * MLA_VROWS, tk), lambda bi, h, i: (bi, 0, h, 0)),
            pl.BlockSpec((None, hps * LANES, tq), lambda bi, h, i: successor(bi, h, i)[:3]),
            pl.BlockSpec((None, tk, hps * LANES), lambda bi, h, i: (successor(bi, h, i)[0], 0, successor(bi, h, i)[1])),
        ],
        out_specs=pl.BlockSpec((None, tq, hps * MLA_V), lambda bi, h, i: (bi, i, h)),
        out_shape=jax.ShapeDtypeStruct((b, s, MLA_HEADS * MLA_V), BF16),
        scratch_shapes=[pltpu.VMEM((hps, 1, tq), F32), pltpu.VMEM((hps, MLA_VROWS, tq), F32),
                        st_scratch, st_scratch],
        compiler_params=pltpu.CompilerParams(dimension_semantics=("arbitrary", "arbitrary", "arbitrary"),
                                             vmem_limit_bytes=VMEM_LIMIT),
        name="mla",
    )(qt, k3, vt, qt, k3)


def _dilattn_kernel(q_ref, kc_ref, vc_ref, kp_ref, vp_ref, o_ref, lse_ref, *, dil, group):
    n = pl.program_id(2)
    nres, tb = q_ref.shape[0], q_ref.shape[1]
    nblk = tb // BLOCK

    qry = lax.broadcasted_iota(jnp.int32, (BLOCK, 2 * BLOCK), 0)
    key = lax.broadcasted_iota(jnp.int32, (BLOCK, 2 * BLOCK), 1)
    dist = qry + BLOCK - key
    in_window = jnp.logical_and(dist >= 0, dist <= BLOCK)
    first_ok = jnp.logical_and(in_window, jnp.logical_or(key >= BLOCK, n > 0))
    distf = (dist * dil).astype(F32)
    lane = lax.broadcasted_iota(jnp.int32, (BLOCK, LANES), 1)
    bias, bias_first = [], []
    for hh in range(DIL_HPG):
        slope = float(2.0 ** (-8.0 * (hh * DIL_GROUPS + group + 1) / DIL_HEADS))
        bias.append(jnp.where(in_window, -(slope * LOG2E) * distf, NEG_INF))
        bias_first.append(jnp.where(first_ok, -(slope * LOG2E) * distf, NEG_INF))

    units = [(r, t, hh) for r in range(nres) for t in range(nblk) for hh in range(DIL_HPG)]

    def window(cur_ref, prev_ref, u):
        r, t, hh = units[u]
        cs = slice(hh * DIL_HEAD_DIM, (hh + 1) * DIL_HEAD_DIM)
        if t == 0:
            return jnp.concatenate([prev_ref[r, :, cs], cur_ref[r, :BLOCK, cs]], axis=0)
        return cur_ref[r, (t - 1) * BLOCK:(t + 1) * BLOCK, cs]

    def scores(u):
        r, t, hh = units[u]
        cs = slice(hh * DIL_HEAD_DIM, (hh + 1) * DIL_HEAD_DIM)
        return _dot_nt(q_ref[r, t * BLOCK:(t + 1) * BLOCK, cs], window(kc_ref, kp_ref, u))

    pending = [scores(u) for u in range(min(DIL_AHEAD, len(units)))]
    lse_tile = None
    for u, (r, t, hh) in enumerate(units):
        if u + DIL_AHEAD < len(units):
            pending.append(scores(u + DIL_AHEAD))
        rs = slice(t * BLOCK, (t + 1) * BLOCK)
        cs = slice(hh * DIL_HEAD_DIM, (hh + 1) * DIL_HEAD_DIM)
        s = pending[u] + (bias_first[hh] if t == 0 else bias[hh])
        m = jnp.max(s, axis=-1, keepdims=True)
        e = jnp.exp2(s - m)
        den = jnp.sum(e, axis=-1, keepdims=True)
        o = _dot(e.astype(BF16), window(vc_ref, vp_ref, u)) / den
        o_ref[r, rs, cs] = o.astype(BF16)
        lse = (m + jnp.log2(den)) * LN2
        lse_tile = jnp.where(lane == hh, lse, jnp.zeros((BLOCK, LANES), F32) if hh == 0 else lse_tile)
        if hh == DIL_HPG - 1:
            lse_ref[r, rs, :] = lse_tile


def _dilattn(qkv, dil, group):
    b, d, l, _ = qkv.shape
    tb = min(DIL_TB, l)
    nres = min(d, DIL_TB // tb)
    bpt = tb // BLOCK
    w = DIL_WIDTH
    cur = lambda c: pl.BlockSpec((None, nres, tb, w), lambda bi, r, n: (bi, r, n, c))
    prev = lambda c: pl.BlockSpec((None, nres, BLOCK, w), lambda bi, r, n: (bi, r, jnp.maximum(n * bpt - 1, 0), c))
    return pl.pallas_call(
        functools.partial(_dilattn_kernel, dil=dil, group=group),
        grid=(b, d // nres, l // tb),
        in_specs=[cur(0), cur(1), cur(2), prev(1), prev(2)],
        out_specs=[pl.BlockSpec((None, nres, tb, w), lambda bi, r, n: (bi, r, n, 0)),
                   pl.BlockSpec((None, nres, tb, LANES), lambda bi, r, n: (bi, r, n, 0))],
        out_shape=[jax.ShapeDtypeStruct((b, d, l, w), BF16), jax.ShapeDtypeStruct((b, d, l, LANES), F32)],
        compiler_params=pltpu.CompilerParams(dimension_semantics=("arbitrary", "arbitrary", "arbitrary"),
                                             vmem_limit_bytes=VMEM_LIMIT),
        name=f"dilattn{dil}",
    )(qkv, qkv, qkv, qkv, qkv)


def _memkv_kernel(mem_ref, g_ref, w_ref, o_ref):
    o_ref[...] = _dot(_rms(mem_ref[...], g_ref[...]).astype(BF16), w_ref[...]).astype(BF16)


def _memkv(mem2, g_mem, w):
    return pl.pallas_call(
        _memkv_kernel,
        out_shape=jax.ShapeDtypeStruct((mem2.shape[0], w.shape[1]), BF16),
        compiler_params=pltpu.CompilerParams(vmem_limit_bytes=VMEM_LIMIT),
        name="memkv",
    )(mem2, g_mem, w)


def _sigmoid(z):
    return 1.0 / (1.0 + jnp.exp(-z))


def _merge_kernel(x_ref, h_ref, wmq_ref, wg_ref, bg_ref, kvm_ref, ymla_ref, wbm_ref,
                  o0_ref, o1_ref, o2_ref, l0_ref, l1_ref, l2_ref, wbd_ref, wbmem_ref, wo_ref, gpm_ref, gpf_ref,
                  x1_ref, h2_ref, nat1_ref, nat2_ref, lse1_ref, lse2_ref):
    tm, dm = x_ref.shape
    h = h_ref[...]

    def gate(br):
        cs = slice(br * dm, (br + 1) * dm)
        return _dot_nt(h, wg_ref[cs, :]) + bg_ref[:, cs]

    for src, lsrc, dst, ldst in ((o1_ref, l1_ref, nat1_ref, lse1_ref), (o2_ref, l2_ref, nat2_ref, lse2_ref)):
        d = src.shape[0]
        rows = src.shape[1]
        for r in range(d):
            for hh in range(DIL_HPG):
                cs = slice(hh * DIL_HEAD_DIM, (hh + 1) * DIL_HEAD_DIM)
                dst[hh, pl.ds(r, rows, stride=d), :] = src[r, :, cs].astype(F32)
            ldst[pl.ds(r, rows, stride=d), :] = lsrc[r]
    lg = (l0_ref[0], lse1_ref[...], lse2_ref[...])
    heads = []
    for hh in range(DIL_HPG):
        cs = slice(hh * DIL_HEAD_DIM, (hh + 1) * DIL_HEAD_DIM)
        og = (o0_ref[0, :, cs].astype(F32), nat1_ref[hh], nat2_ref[hh])
        ls = [l[:, hh:hh + 1] for l in lg]
        mx = jnp.maximum(jnp.maximum(ls[0], ls[1]), ls[2])
        ws = [jnp.exp(l - mx) for l in ls]
        num = ws[0] * og[0] + ws[1] * og[1] + ws[2] * og[2]
        heads.append((num / (ws[0] + ws[1] + ws[2])).astype(BF16))
    y_dil = jnp.concatenate(heads, axis=-1)

    memq = _dot_nt(h, wmq_ref[...])
    mheads = []
    for hh in range(MEM_HEADS):
        cs = slice(hh * MEM_HEAD_DIM, (hh + 1) * MEM_HEAD_DIM)
        q = (memq[:, cs] * MEM_HEAD_DIM ** -0.5).astype(BF16)
        s = _dot_nt(q, kvm_ref[:, cs])
        e = jnp.exp(s - jnp.max(s, axis=-1, keepdims=True))
        o = _dot(e.astype(BF16), kvm_ref[:, MEM_WIDTH + hh * MEM_HEAD_DIM:MEM_WIDTH + (hh + 1) * MEM_HEAD_DIM])
        mheads.append((o / jnp.sum(e, axis=-1, keepdims=True)).astype(BF16))
    y_mem = jnp.concatenate(mheads, axis=-1)

    merged = jnp.zeros((tm, dm), F32)
    for br, (y, w_ref) in enumerate(((ymla_ref[...], wbm_ref), (y_dil, wbd_ref), (y_mem, wbmem_ref))):
        merged = merged + _sigmoid(gate(br)) * _dot(y, w_ref[...])
    merged_b = merged.astype(BF16)
    rows = tm // TAIL_SPLIT
    mixed = [_dot(merged_b[g * rows:(g + 1) * rows, :], wo_ref[...]) for g in range(TAIL_SPLIT)]
    for g in range(TAIL_SPLIT):
        rs = slice(g * rows, (g + 1) * rows)
        x1 = x_ref[rs, :] + _rms(mixed[g], gpm_ref[...])
        x1_ref[rs, :] = x1
        h2_ref[rs, :] = _rms(x1, gpf_ref[...]).astype(BF16)


def _merge(x2, h2, w_al, b_g, kvm, y_mla, w_bm, o_dil, lse_dil, w_bd, w_bmem, w_o, g_pm, g_pf, batch, seq):
    t, dm = x2.shape
    tm = MERGE_TM
    tps = seq // tm
    n_mem = kvm.shape[0] // batch
    row = lambda i: (i, 0)
    single = pl.Buffered(1)
    const = lambda a: pl.BlockSpec(a.shape, lambda i: (0,) * a.ndim, pipeline_mode=single)
    gate_w = N_BRANCH * dm
    assert ALIGNED_MEMQ % MEM_WIDTH == 0 and ALIGNED_GATE % gate_w == 0
    w_mq = pl.BlockSpec((MEM_WIDTH, dm), lambda i: (ALIGNED_MEMQ // MEM_WIDTH, 0), pipeline_mode=single)
    w_g = pl.BlockSpec((gate_w, dm), lambda i: (ALIGNED_GATE // gate_w, 0), pipeline_mode=single)

    def dil_spec(a):
        d, width = a.shape[1], a.shape[3]
        return pl.BlockSpec((None, d, tm // d, width), lambda i: (i // tps, 0, i % tps, 0))

    return pl.pallas_call(
        _merge_kernel,
        grid=(t // tm,),
        in_specs=[
            pl.BlockSpec((tm, dm), row), pl.BlockSpec((tm, dm), row), w_mq, w_g, const(b_g),
            pl.BlockSpec((n_mem, kvm.shape[1]), lambda i: (i // tps, 0)),
            pl.BlockSpec((tm, y_mla.shape[1]), row), const(w_bm),
            dil_spec(o_dil[0]), dil_spec(o_dil[1]), dil_spec(o_dil[2]),
            dil_spec(lse_dil[0]), dil_spec(lse_dil[1]), dil_spec(lse_dil[2]),
            const(w_bd), const(w_bmem), const(w_o), const(g_pm), const(g_pf),
        ],
        out_specs=[pl.BlockSpec((tm, dm), row), pl.BlockSpec((tm, dm), row)],
        out_shape=[jax.ShapeDtypeStruct((t, dm), F32), jax.ShapeDtypeStruct((t, dm), BF16)],
        scratch_shapes=[pltpu.VMEM((DIL_HPG, tm, DIL_HEAD_DIM), F32), pltpu.VMEM((DIL_HPG, tm, DIL_HEAD_DIM), F32),
                        pltpu.VMEM((tm, LANES), F32), pltpu.VMEM((tm, LANES), F32)],
        compiler_params=pltpu.CompilerParams(dimension_semantics=("arbitrary",), vmem_limit_bytes=VMEM_LIMIT),
        name="merge",
    )(x2, h2, w_al, w_al, b_g, kvm, y_mla, w_bm, o_dil[0], o_dil[1], o_dil[2], lse_dil[0], lse_dil[1], lse_dil[2],
      w_bd, w_bmem, w_o, g_pm, g_pf)


def _ffn_kernel(x1_ref, h2_ref, halo_ref, wup_ref, cw_ref, cb_ref, wd_ref, gpost_ref,
                out_ref, hcat_ref, ua_ref, ub_ref, acc_ref, *, tiles_per_seq):
    i = pl.program_id(0)
    tm = x1_ref.shape[0]
    halo, tf = FFN_HALO, FFN_TF
    dff = wd_ref.shape[0]
    nchunk = dff // tf
    lanes_per_chunk = tf // LANES

    first = (i % tiles_per_seq) == 0
    hcat_ref[:halo, :] = jnp.where(first, jnp.zeros_like(halo_ref[...]), halo_ref[...])
    hcat_ref[halo:, :] = h2_ref[...]

    def up(c, u_ref):
        hc = hcat_ref[...]
        for part, off in enumerate((c * tf, dff + c * tf)):
            u = _dot(hc, wup_ref[:, off:off + tf])
            for j in range(lanes_per_chunk):
                u_ref[part * lanes_per_chunk + j] = u[:, j * LANES:(j + 1) * LANES]

    def conv(u_ref, slab, col):
        cols = slice(col, col + LANES)
        z = cb_ref[:, cols]
        for tap in range(CONV_WIDTH):
            back = CONV_WIDTH - 1 - tap
            z = z + cw_ref[tap:tap + 1, cols] * u_ref[slab, halo - back:halo - back + tm, :]
        return z

    bufs = (ua_ref, ub_ref)
    up(0, bufs[0])
    for c in range(nchunk):
        cur = bufs[c % 2]
        if c + 1 < nchunk:
            up(c + 1, bufs[(c + 1) % 2])
        acts = []
        for j in range(lanes_per_chunk):
            gate = conv(cur, j, c * tf + j * LANES)
            val = conv(cur, lanes_per_chunk + j, dff + c * tf + j * LANES)
            acts.append((gate * _sigmoid(gate) * val).astype(BF16))
        down = _dot(jnp.concatenate(acts, axis=-1), wd_ref[c * tf:(c + 1) * tf, :].astype(BF16))
        if c == 0:
            acc_ref[...] = down
        else:
            acc_ref[...] += down

    out_ref[...] = x1_ref[...] + _rms(acc_ref[...], gpost_ref[...])


def _ffn(x1, h2, w_up, conv_w, conv_b, w_down, g_post, seq):
    t, dm = x1.shape
    tm, tf, halo = FFN_TM, FFN_TF, FFN_HALO
    tps = seq // tm
    row = lambda i: (i, 0)
    const = lambda a: pl.BlockSpec(a.shape, lambda i: (0,) * a.ndim, pipeline_mode=pl.Buffered(1))
    u_scratch = pltpu.VMEM((2 * tf // LANES, tm + halo, LANES), F32)
    return pl.pallas_call(
        functools.partial(_ffn_kernel, tiles_per_seq=tps),
        grid=(t // tm,),
        in_specs=[
            pl.BlockSpec((tm, dm), row),
            pl.BlockSpec((tm, dm), row),
            pl.BlockSpec((halo, dm), lambda i: (jnp.maximum(i * (tm // halo) - 1, 0), 0)),
            const(w_up), const(conv_w), const(conv_b), const(w_down), const(g_post),
        ],
        out_specs=pl.BlockSpec((tm, dm), row),
        out_shape=jax.ShapeDtypeStruct((t, dm), F32),
        scratch_shapes=[pltpu.VMEM((tm + halo, dm), BF16), u_scratch, u_scratch, pltpu.VMEM((tm, dm), F32)],
        compiler_params=pltpu.CompilerParams(dimension_semantics=("arbitrary",), vmem_limit_bytes=VMEM_LIMIT),
        name="ffn",
    )(x1, h2, h2, w_up, conv_w, conv_b, w_down, g_post)


def _rot_half_cols(w):
    return jnp.concatenate([-w[..., ROPE_HALF:], w[..., :ROPE_HALF]], axis=-1)


def _align_kernel(wt_ref, o_ref):
    cb = wt_ref.shape[1]
    zeros = lambda n: jnp.zeros((n, cb), BF16)
    kr = wt_ref[OFF_KV:OFF_KR, :]
    o_ref[:OFF_KV, :] = wt_ref[:OFF_KV, :].astype(BF16)
    o_ref[OFF_KV:OFF_KV + MLA_NOPE, :] = zeros(MLA_NOPE)
    o_ref[OFF_KV + MLA_NOPE:OFF_KV + MLA_QK_DIM, :] = kr.astype(BF16)
    o_ref[OFF_KV + MLA_QK_DIM:OFF_KV + LANES + MLA_NOPE, :] = zeros(LANES - MLA_QK_DIM + MLA_NOPE)
    o_ref[OFF_KV + LANES + MLA_NOPE:OFF_KV + LANES + MLA_NOPE + ROPE_HALF, :] = (-kr[ROPE_HALF:, :]).astype(BF16)
    o_ref[OFF_KV + LANES + MLA_NOPE + ROPE_HALF:OFF_KV + LANES + MLA_QK_DIM, :] = kr[:ROPE_HALF, :].astype(BF16)
    o_ref[OFF_KV + LANES + MLA_QK_DIM:ALIGNED_DIL, :] = zeros(ALIGNED_DIL - OFF_KV - LANES - MLA_QK_DIM)
    o_ref[ALIGNED_DIL:, :] = wt_ref[OFF_KR:, :].astype(BF16)


def _align_w_in(w_in_all, layer):
    wt_all = jnp.swapaxes(w_in_all, 1, 2)
    _, d_in, dm = wt_all.shape
    cb = ALIGN_CB
    width = ALIGNED_DIL + d_in - OFF_KR
    return pl.pallas_call(
        _align_kernel,
        grid=(dm // cb,),
        in_specs=[pl.BlockSpec((None, d_in, cb), lambda i: (layer, 0, i))],
        out_specs=pl.BlockSpec((width, cb), lambda i: (0, i)),
        out_shape=jax.ShapeDtypeStruct((width, dm), BF16),
        compiler_params=pltpu.CompilerParams(dimension_semantics=("arbitrary",), vmem_limit_bytes=VMEM_LIMIT),
        name="align",
    )(wt_all)


def _prep_weights(w_uq, w_ukv):
    uq = w_uq.reshape(MLA_Q_RANK, MLA_HEADS, MLA_QK_DIM)
    w_qm = w_uq
    w_qs = _rot_half_cols(uq[..., MLA_NOPE:]).reshape(MLA_Q_RANK, MLA_HEADS * MLA_ROPE)

    ukv = w_ukv.reshape(MLA_KV_RANK, MLA_HEADS, MLA_NOPE + MLA_V)
    zk = jnp.zeros((MLA_KV_RANK, MLA_HEADS, LANES - MLA_NOPE), F32)
    w_k = jnp.concatenate([ukv[..., :MLA_NOPE], zk], axis=-1).reshape(MLA_KV_RANK, MLA_HEADS * LANES)
    w_v = ukv[..., MLA_NOPE:].reshape(MLA_KV_RANK, MLA_HEADS * MLA_V)
    return tuple(a.astype(BF16) for a in (w_qm.T, w_qs.T, w_k, w_v.T))


def _layer(layer, w_in_all, x, mem, positions, g_pre_mix, b_gate, mla_q_norm, w_uq, mla_kv_norm, w_ukv, g_mem, w_mem_kv,
           w_br_mla, w_br_dil, w_br_mem, w_o, g_post_mix, g_pre_ffn, w_ffn_up, conv_w, conv_b, w_ffn_down,
           g_post_ffn):
    batch, seq, dm = x.shape
    t = batch * seq
    x2 = x.reshape(t, dm)
    r2 = lambda v: v.reshape(1, -1)

    w_al = _align_w_in(w_in_all, layer)
    w_qmt, w_qst, w_k, w_vt = _prep_weights(w_uq, w_ukv)
    invf = (ROPE_THETA ** (-jnp.arange(ROPE_HALF, dtype=F32) / ROPE_HALF)).reshape(ROPE_HALF, 1)
    pos_rows = positions.reshape(t // PREP_TM, 1, PREP_TM)

    h2d, qt, k, vt = _prep(x2, pos_rows, invf, r2(g_pre_mix), w_al, r2(mla_q_norm), w_qmt, w_qst,
                           r2(mla_kv_norm), w_k, w_vt, batch, seq)
    y_mla = _mla(qt, k.reshape(batch, seq, MLA_HEADS * LANES), vt).reshape(t, MLA_HEADS * MLA_V)

    o_dil, lse_dil = [], []
    for g, (_, dil) in enumerate(DIL_PAIRS):
        qkv = _dilproj(h2d, w_al, batch, seq, dil, g)
        o, lse = _dilattn(qkv, dil, g)
        o_dil.append(o)
        lse_dil.append(lse)

    kvm = _memkv(mem.reshape(-1, dm), r2(g_mem), w_mem_kv.astype(BF16))

    x1, h2 = _merge(x2, h2d, w_al, r2(b_gate), kvm, y_mla, w_br_mla.astype(BF16), o_dil, lse_dil,
                    w_br_dil.astype(BF16),
                    w_br_mem.astype(BF16), w_o.astype(BF16), r2(g_post_mix), r2(g_pre_ffn), batch, seq)

    out = _ffn(x1, h2, w_ffn_up.astype(BF16), conv_w, r2(conv_b), w_ffn_down, r2(g_post_ffn), seq)
    return out.reshape(batch, seq, dm)


def kernel(x, mem, positions, g_pre_mix, w_in, b_gate, mla_q_norm, w_uq, mla_kv_norm, w_ukv, g_mem, w_mem_kv,
           w_br_mla, w_br_dil, w_br_mem, w_o, g_post_mix, g_pre_ffn, w_ffn_up, conv_w, conv_b, w_ffn_down,
           g_post_ffn):
    for l in range(w_in.shape[0]):
        x = _layer(l, w_in, x, mem, positions, g_pre_mix[l], b_gate[l], mla_q_norm[l], w_uq[l], mla_kv_norm[l],
                   w_ukv[l], g_mem[l], w_mem_kv[l], w_br_mla[l], w_br_dil[l], w_br_mem[l], w_o[l], g_post_mix[l],
                   g_pre_ffn[l], w_ffn_up[l], conv_w[l], conv_b[l], w_ffn_down[l], g_post_ffn[l])
    return x
```
